```python
import math
import jax
import jax.numpy as jnp
from jax import lax
import numpy as np

D_MODEL = 1024
BATCH = 4
SEQ = 4096
DEPTH = 2

RET_HEADS = 8
RET_DK = 64
RET_DV = 128
RET_CHUNK = 128
ROPE_BASE = 10000.0
DIL_HEADS = 8
DIL_DH = 64
DIL_BRANCHES = ((128, 1), (512, 4), (2048, 16))
DIL_BLOCK = 128
HGRN_HEADS = 8
HGRN_DK = 128
HGRN_DV = 128
HGRN_CHUNK = 32
REL_BUCKETS = 32
REL_MAX_DIST = 2048
D_FF = 2816
CONV_WIDTH = 3
EPS = 1e-6

A_QK = RET_HEADS * RET_DK
A_V = RET_HEADS * RET_DV
B_W = DIL_HEADS * DIL_DH
EVEN_IN = 2 * A_QK + 2 * A_V + 3 * B_W
EVEN_OUT = A_V + B_W
EVEN_SPLITS = (A_QK, 2 * A_QK, 2 * A_QK + A_V, 2 * A_QK + 2 * A_V,
               2 * A_QK + 2 * A_V + B_W, 2 * A_QK + 2 * A_V + 2 * B_W)
C_K = HGRN_HEADS * HGRN_DK
C_V = HGRN_HEADS * HGRN_DV
ODD_IN = 2 * C_K + 2 * C_V
ODD_SPLITS = (C_K, 2 * C_K, 2 * C_K + C_V)

kernel_name = "hybrid_retention_dilated_hgrn2_trunk"


def rmsnorm(x, g):
    xf = x.astype(jnp.float32)
    y = xf * lax.rsqrt(jnp.mean(xf * xf, axis=-1, keepdims=True) + EPS)
    return (y * g.astype(jnp.float32)).astype(x.dtype)


def head_layernorm(y):
    mu = jnp.mean(y, axis=-1, keepdims=True)
    yc = y - mu
    return yc * lax.rsqrt(jnp.mean(yc * yc, axis=-1, keepdims=True) + EPS)


def head_rmsnorm(y):
    return y * lax.rsqrt(jnp.mean(y * y, axis=-1, keepdims=True) + EPS)


def rotary(x):
    S, d = x.shape[1], x.shape[-1]
    inv = ROPE_BASE ** (-jnp.arange(0, d, 2, dtype=jnp.float32) / d)
    ang = jnp.arange(S, dtype=jnp.float32)[:, None] * inv[None, :]
    cos = jnp.cos(ang)[None, :, None, :]
    sin = jnp.sin(ang)[None, :, None, :]
    x1, x2 = x[..., : d // 2], x[..., d // 2:]
    return jnp.concatenate([x1 * cos - x2 * sin, x1 * sin + x2 * cos], axis=-1)


def rel_bucket(dist):
    max_exact = REL_BUCKETS // 2
    d = dist.astype(jnp.float32)
    large = max_exact + (jnp.log(jnp.maximum(d, 1.0) / max_exact)
                         / math.log(REL_MAX_DIST / max_exact)
                         * (REL_BUCKETS - max_exact)).astype(jnp.int32)
    large = jnp.minimum(large, REL_BUCKETS - 1)
    return jnp.where(dist < max_exact, dist, large)


def retention(q, k, v):
    Bn, S, H, dk = q.shape
    dv = v.shape[-1]
    C = RET_CHUNK
    N = S // C
    log_g = jnp.log1p(-jnp.exp2(-5.0 - jnp.arange(H, dtype=jnp.float32)))
    pos = jnp.arange(C, dtype=jnp.float32)
    diff = pos[:, None] - pos[None, :]
    decay = jnp.where(diff >= 0, jnp.exp(jnp.maximum(diff, 0.0)[None] * log_g[:, None, None]), 0.0)
    qc = q.reshape(Bn, N, C, H, dk)
    kc = k.reshape(Bn, N, C, H, dk)
    vc = v.reshape(Bn, N, C, H, dv)
    scores = jnp.einsum('bnihd,bnjhd->bnhij', qc, kc) * decay
    intra = jnp.einsum('bnhij,bnjhe->bnihe', scores, vc)
    q_in = qc * jnp.exp((pos + 1.0)[:, None] * log_g[None, :])[:, :, None]
    k_out = kc * jnp.exp((C - 1.0 - pos)[:, None] * log_g[None, :])[:, :, None]
    chunk_decay = jnp.exp(C * log_g)[None, :, None, None]

    def step(state, inp):
        q_i, k_i, v_i = inp
        out = jnp.einsum('bihd,bhde->bihe', q_i, state)
        state = state * chunk_decay + jnp.einsum('bjhd,bjhe->bhde', k_i, v_i)
        return state, out

    xs = (jnp.moveaxis(q_in, 1, 0), jnp.moveaxis(k_out, 1, 0), jnp.moveaxis(vc, 1, 0))
    _, inter = lax.scan(step, jnp.zeros((Bn, H, dk, dv), jnp.float32), xs)
    return (intra + jnp.moveaxis(inter, 0, 1)).reshape(Bn, S, H, dv)


def dilated_branch(q, k, v, rel_bias, window, dilation):
    Bn, S, H, dh = q.shape
    W = window // dilation
    Lb = DIL_BLOCK
    n_prev = -(-W // Lb)
    seg = dilation * Lb
    Sp = -(-S // seg) * seg
    L = Sp // dilation
    nb = L // Lb

    def to_blocks(a):
        a = jnp.pad(a, ((0, 0), (0, Sp - S), (0, 0), (0, 0)))
        a = a.reshape(Bn, L, dilation, H, dh).transpose(0, 2, 1, 3, 4)
        return a.reshape(Bn, dilation, nb, Lb, H, dh)

    def with_history(a):
        ap = jnp.pad(a, ((0, 0), (0, 0), (n_prev, 0), (0, 0), (0, 0), (0, 0)))
        return jnp.concatenate([ap[:, :, j:j + nb] for j in range(n_prev + 1)], axis=3)

    qb = to_blocks(q)
    kh = with_history(to_blocks(k))
    vh = with_history(to_blocks(v))
    Kc = (n_prev + 1) * Lb
    a_idx = jnp.arange(Lb)[:, None]
    c_idx = jnp.arange(Kc)[None, :]
    dist = n_prev * Lb + a_idx - c_idx
    key_pos = (jnp.arange(nb)[:, None, None] - n_prev) * Lb + c_idx[None]
    valid = (dist >= 0)[None] & (dist <= W)[None] & (key_pos >= 0)
    bias = rel_bias[rel_bucket(jnp.maximum(dist, 0) * dilation)].transpose(2, 0, 1)
    s = jnp.einsum('brnahd,brnchd->brnhac', qb, kh) + bias.astype(jnp.float32)
    s = jnp.where(valid[None, None, :, None], s, -jnp.inf)
    m = jnp.max(s, axis=-1, keepdims=True)
    p = jnp.exp(s - m)
    den = jnp.sum(p, axis=-1)
    o = jnp.einsum('brnhac,brnchd->brnahd', p, vh) / jnp.swapaxes(den, 3, 4)[..., None]
    lse = jnp.swapaxes(m[..., 0] + jnp.log(den), 3, 4)

    def from_blocks(a):
        a = a.reshape(Bn, dilation, L, *a.shape[4:])
        a = jnp.moveaxis(a, 1, 2).reshape(Bn, Sp, *a.shape[3:])
        return a[:, :S]

    return from_blocks(o), from_blocks(lse)


def dilated_attention(q, k, v, rel_bias):
    outs, lses = [], []
    for window, dilation in DIL_BRANCHES:
        o, lse = dilated_branch(q, k, v, rel_bias, window, dilation)
        outs.append(o)
        lses.append(lse)
    w = jax.nn.softmax(jnp.stack(lses, axis=0), axis=0)
    return jnp.sum(w[..., None] * jnp.stack(outs, axis=0), axis=0)


def even_mixer(hn, w_in, w_out, ret_norm, rel_bias):
    Bn, S, _ = hn.shape
    proj = (hn @ w_in).astype(jnp.float32)
    qa, ka, va, ga, qb, kb, vb = jnp.split(proj, EVEN_SPLITS, axis=-1)
    qa = rotary(qa.reshape(Bn, S, RET_HEADS, RET_DK))
    ka = rotary(ka.reshape(Bn, S, RET_HEADS, RET_DK)) * (RET_DK ** -0.5)
    ya = retention(qa, ka, va.reshape(Bn, S, RET_HEADS, RET_DV))
    ya = head_layernorm(ya).reshape(Bn, S, A_V) * ret_norm.astype(jnp.float32) * jax.nn.silu(ga)
    yb = dilated_attention(qb.reshape(Bn, S, DIL_HEADS, DIL_DH) * (DIL_DH ** -0.5),
                           kb.reshape(Bn, S, DIL_HEADS, DIL_DH),
                           vb.reshape(Bn, S, DIL_HEADS, DIL_DH), rel_bias).reshape(Bn, S, B_W)
    y = jnp.concatenate([ya, yb], axis=-1).astype(hn.dtype)
    return y @ w_out


def hgrn2(q, f_gate, i, Bn, S):
    H, dk, dv = HGRN_HEADS, HGRN_DK, HGRN_DV
    C = HGRN_CHUNK
    N = S // C
    k = 1.0 - f_gate
    log_f = jnp.log(f_gate)
    qc = q.reshape(Bn, N, C, H, dk)
    kc = k.reshape(Bn, N, C, H, dk)
    vc = i.reshape(Bn, N, C, H, dv)
    b = jnp.cumsum(log_f.reshape(Bn, N, C, H, dk), axis=2)
    b_last = b[:, :, -1:]
    q_t = qc * jnp.exp(b)
    k_t = kc * jnp.exp(-b)
    mask = jnp.tril(jnp.ones((C, C), dtype=bool))
    scores = jnp.where(mask, jnp.einsum('bnihd,bnjhd->bnhij', q_t, k_t), 0.0)
    intra = jnp.einsum('bnhij,bnjhe->bnihe', scores, vc)
    k_end = kc * jnp.exp(b_last - b)
    chunk_decay = jnp.exp(b_last[:, :, 0])

    def step(state, inp):
        q_i, k_i, v_i, d_i = inp
        out = jnp.einsum('bihd,bhde->bihe', q_i, state)
        state = state * d_i[..., None] + jnp.einsum('bjhd,bjhe->bhde', k_i, v_i)
        return state, out

    xs = (jnp.moveaxis(q_t, 1, 0), jnp.moveaxis(k_end, 1, 0), jnp.moveaxis(vc, 1, 0),
          jnp.moveaxis(chunk_decay, 1, 0))
    _, inter = lax.scan(step, jnp.zeros((Bn, H, dk, dv), jnp.float32), xs)
    return (intra + jnp.moveaxis(inter, 0, 1)).reshape(Bn, S, H, dv)


def odd_mixer(hn, w_in, w_out, out_norm, lb):
    Bn, S, _ = hn.shape
    proj = (hn @ w_in).astype(jnp.float32)
    q, f, i, g = jnp.split(proj, ODD_SPLITS, axis=-1)
    q = jax.nn.silu(q)
    f_gate = lb + (1.0 - lb) * jax.nn.sigmoid(f)
    shp_k = (Bn, S, HGRN_HEADS, HGRN_DK)
    o = hgrn2(q.reshape(shp_k), f_gate.reshape(shp_k), i.reshape(Bn, S, HGRN_HEADS, HGRN_DV), Bn, S)
    o = head_rmsnorm(o).reshape(Bn, S, C_V) * out_norm.astype(jnp.float32) * jax.nn.silu(g)
    return o.astype(hn.dtype) @ w_out


def conv_ffn(hn, w_up, conv_w, conv_b, w_down):
    u = hn @ w_up
    ch = u.shape[-1]
    u = lax.conv_general_dilated(u, conv_w[:, None, :].astype(u.dtype), window_strides=(1,),
                                 padding=[(CONV_WIDTH - 1, 0)],
                                 dimension_numbers=('NWC', 'WIO', 'NWC'),
                                 feature_group_count=ch) + conv_b
    a, b = jnp.split(u, 2, axis=-1)
    return (jax.nn.silu(a) * b) @ w_down


def setup_inputs(seed: int = 0) -> dict:
    key = jax.random.key(seed)
    ks = jax.random.split(key, 17)
    f32 = jnp.float32
    n_even = (DEPTH + 1) // 2
    n_odd = DEPTH // 2

    def nrm(k, shape, scale):
        return jax.random.normal(k, shape, f32) * scale

    return {
        "x": nrm(ks[0], (BATCH, SEQ, D_MODEL), 1.0),
        "even_w_in": nrm(ks[1], (n_even, D_MODEL, EVEN_IN), D_MODEL ** -0.5),
        "even_w_out": nrm(ks[2], (n_even, EVEN_OUT, D_MODEL), EVEN_OUT ** -0.5),
        "ret_norm": 1.0 + nrm(ks[3], (n_even, A_V), 0.02),
        "rel_bias": nrm(ks[4], (REL_BUCKETS, DIL_HEADS), 0.5),
        "odd_w_in": nrm(ks[5], (n_odd, D_MODEL, ODD_IN), D_MODEL ** -0.5),
        "odd_w_out": nrm(ks[6], (n_odd, C_V, D_MODEL), C_V ** -0.5),
        "hgrn_lb": nrm(ks[7], (DEPTH, C_K), 0.1),
        "hgrn_norm": 1.0 + nrm(ks[8], (n_odd, C_V), 0.02),
        "mix_norm": 1.0 + nrm(ks[9], (DEPTH, D_MODEL), 0.02),
        "ffn_norm": 1.0 + nrm(ks[10], (DEPTH, D_MODEL), 0.02),
        "ffn_w_up": nrm(ks[11], (DEPTH, D_MODEL, 2 * D_FF), D_MODEL ** -0.5),
        "ffn_conv_w": nrm(ks[12], (DEPTH, CONV_WIDTH, 2 * D_FF), CONV_WIDTH ** -0.5),
        "ffn_conv_b": nrm(ks[13], (DEPTH, 2 * D_FF), 0.01),
        "ffn_w_down": nrm(ks[14], (DEPTH, D_FF, D_MODEL), D_FF ** -0.5),
        "final_norm": 1.0 + nrm(ks[15], (D_MODEL,), 0.02),
    }


def reference(x, even_w_in, even_w_out, ret_norm, rel_bias, odd_w_in, odd_w_out, hgrn_lb,
              hgrn_norm, mix_norm, ffn_norm, ffn_w_up, ffn_conv_w, ffn_conv_b, ffn_w_down,
              final_norm):
    sm = jax.nn.softmax(hgrn_lb.astype(jnp.float32), axis=0)
    lower_bounds = jnp.cumsum(sm, axis=0) - sm[0]
    h = x
    for l in range(DEPTH):
        hn = rmsnorm(h, mix_norm[l])
        if l % 2 == 0:
            e = l // 2
            h = h + even_mixer(hn, even_w_in[e], even_w_out[e], ret_norm[e], rel_bias)
        else:
            o = l // 2
            h = h + odd_mixer(hn, odd_w_in[o], odd_w_out[o], hgrn_norm[o], lower_bounds[l])
        hn = rmsnorm(h, ffn_norm[l])
        h = h + conv_ffn(hn, ffn_w_up[l], ffn_conv_w[l], ffn_conv_b[l], ffn_w_down[l])
    return rmsnorm(h, final_norm)
```

```python
import functools
import math

import numpy as np
import jax
import jax.numpy as jnp
from jax import lax
from jax.experimental import pallas as pl
from jax.experimental.pallas import tpu as pltpu

F32 = jnp.float32
BF16 = jnp.bfloat16

D_MODEL = 1024
RET_HEADS = 8
RET_DK = 64
RET_DV = 128
RET_CHUNK = 128
ROPE_BASE = 10000.0
DIL_HEADS = 8
DIL_DH = 64
DIL_BRANCHES = ((128, 1), (512, 4), (2048, 16))
DIL_BLOCK = 128
HGRN_HEADS = 8
HGRN_DK = 128
HGRN_DV = 128
HGRN_CHUNK = 32
REL_BUCKETS = 32
REL_MAX_DIST = 2048
D_FF = 2816
CONV_WIDTH = 3
EPS = 1e-6

A_QK = RET_HEADS * RET_DK
A_V = RET_HEADS * RET_DV
B_W = DIL_HEADS * DIL_DH
EVEN_IN = 2 * A_QK + 2 * A_V + 3 * B_W
C_K = HGRN_HEADS * HGRN_DK
C_V = HGRN_HEADS * HGRN_DV
ODD_IN = 2 * C_K + 2 * C_V

LANES = 128
SUBLANES = 8
VMEM_LIMIT = 48 * 1024 * 1024
MASK_VALUE = -1e30

NT_DIMS = (((1,), (1,)), ((), ()))
TN_DIMS = (((0,), (0,)), ((), ()))


def _params(*semantics):
    return pltpu.CompilerParams(dimension_semantics=semantics, vmem_limit_bytes=VMEM_LIMIT)


def _silu(x):
    return x * jax.nn.sigmoid(x)


def _norm_matmul_kernel(x_ref, g_ref, w_ref, o_ref, hn_ref):
    @pl.when(pl.program_id(1) == 0)
    def _():
        x = x_ref[...]
        ms = jnp.mean(x * x, axis=-1, keepdims=True)
        hn_ref[...] = (x * lax.rsqrt(ms + EPS) * g_ref[...]).astype(BF16)

    o_ref[...] = jnp.dot(hn_ref[...], w_ref[...], preferred_element_type=F32).astype(o_ref.dtype)


def _norm_matmul(x, g, w, tm, tn):
    m, k = x.shape
    n = w.shape[1]
    return pl.pallas_call(
        _norm_matmul_kernel,
        grid=(m // tm, n // tn),
        in_specs=[
            pl.BlockSpec((tm, k), lambda i, j: (i, 0)),
            pl.BlockSpec((1, k), lambda i, j: (0, 0)),
            pl.BlockSpec((k, tn), lambda i, j: (0, j)),
        ],
        out_specs=pl.BlockSpec((tm, tn), lambda i, j: (i, j)),
        out_shape=jax.ShapeDtypeStruct((m, n), F32),
        scratch_shapes=[pltpu.VMEM((tm, k), BF16)],
        compiler_params=_params("arbitrary", "arbitrary"),
        name="norm_matmul",
    )(x, g, w)


def _retention_tables(seq):
    h = jnp.arange(RET_HEADS, dtype=F32)
    log_g = jnp.log1p(-jnp.exp2(-5.0 - h))
    pos = jnp.arange(RET_CHUNK, dtype=F32)
    diff = pos[:, None] - pos[None, :]
    decay = jnp.where(diff >= 0, jnp.exp(jnp.maximum(diff, 0.0)[None] * log_g[:, None, None]), 0.0)
    q_scale = jnp.repeat(jnp.exp((pos + 1.0)[:, None] * log_g[None, :]), RET_DK, axis=1)
    k_scale = jnp.repeat(jnp.exp((RET_CHUNK - 1.0 - pos)[:, None] * log_g[None, :]), RET_DK, axis=1)
    chunk_decay = jnp.broadcast_to(jnp.exp(RET_CHUNK * log_g)[:, None, None], (RET_HEADS, 1, RET_DV))
    inv = ROPE_BASE ** (-jnp.arange(0, RET_DK, 2, dtype=F32) / RET_DK)
    ang = jnp.arange(seq, dtype=F32)[:, None] * inv[None, :]
    cos, sin = jnp.cos(ang), jnp.sin(ang)
    reps = LANES // RET_DK
    cos_t = jnp.tile(jnp.concatenate([cos, cos], axis=1), (1, reps))
    sin_t = jnp.tile(jnp.concatenate([-sin, sin], axis=1), (1, reps))
    return decay, q_scale, k_scale, chunk_decay, cos_t, sin_t


def _retention_kernel(q_ref, k_ref, v_ref, g_ref, cos_ref, sin_ref, qs_ref, ks_ref, dec_ref, cd_ref,
                      rn_ref, o_ref, st_ref):
    @pl.when(pl.program_id(1) == 0)
    def _():
        st_ref[...] = jnp.zeros_like(st_ref)

    cos = cos_ref[...]
    sin = sin_ref[...]
    lane = lax.broadcasted_iota(jnp.int32, (RET_CHUNK, LANES), 1)
    first_half = (lane % RET_DK) < (RET_DK // 2)
    head0 = lane < RET_DK
    half = RET_DK // 2

    def rot(x):
        partner = jnp.where(first_half, pltpu.roll(x, LANES - half, 1), pltpu.roll(x, half, 1))
        return x * cos + partner * sin

    for p in range(RET_HEADS // 2):
        sl = slice(p * LANES, (p + 1) * LANES)
        q = rot(q_ref[:, sl])
        k = rot(k_ref[:, sl]) * (RET_DK ** -0.5)
        kb = k.astype(BF16)
        q_in = q * qs_ref[:, sl]
        k_out = (k * ks_ref[:, sl]).astype(BF16)
        for a in range(2):
            h = 2 * p + a
            hs = slice(h * RET_DV, (h + 1) * RET_DV)
            mask = head0 if a == 0 else jnp.logical_not(head0)
            qm = jnp.where(mask, q, 0.0).astype(BF16)
            qim = jnp.where(mask, q_in, 0.0).astype(BF16)
            v = v_ref[:, hs].astype(BF16)
            s = lax.dot_general(qm, kb, NT_DIMS, preferred_element_type=F32) * dec_ref[h]
            intra = jnp.dot(s.astype(BF16), v, preferred_element_type=F32)
            st = st_ref[h]
            inter = jnp.dot(qim, st.astype(BF16), preferred_element_type=F32)
            st_ref[h] = st * cd_ref[h] + lax.dot_general(k_out, v, TN_DIMS,
                                                         preferred_element_type=F32)
            y = intra + inter
            yc = y - jnp.mean(y, axis=-1, keepdims=True)
            yn = yc * lax.rsqrt(jnp.mean(yc * yc, axis=-1, keepdims=True) + EPS)
            o_ref[:, hs] = (yn * rn_ref[:, hs] * _silu(g_ref[:, hs])).astype(o_ref.dtype)


def _retention(proj, ret_norm, seq):
    bn = proj.shape[0]
    c = RET_CHUNK
    decay, q_scale, k_scale, chunk_decay, cos_t, sin_t = _retention_tables(seq)
    blk = lambda w, col: pl.BlockSpec((None, c, w), lambda b, n: (b, n, col))
    return pl.pallas_call(
        _retention_kernel,
        grid=(bn, seq // c),
        in_specs=[
            blk(A_QK, 0),
            blk(A_QK, 1),
            blk(A_V, 1),
            blk(A_V, 2),
            pl.BlockSpec((c, LANES), lambda b, n: (n, 0)),
            pl.BlockSpec((c, LANES), lambda b, n: (n, 0)),
            pl.BlockSpec((c, A_QK), lambda b, n: (0, 0)),
            pl.BlockSpec((c, A_QK), lambda b, n: (0, 0)),
            pl.BlockSpec((RET_HEADS, c, c), lambda b, n: (0, 0, 0)),
            pl.BlockSpec((RET_HEADS, 1, RET_DV), lambda b, n: (0, 0, 0)),
            pl.BlockSpec((1, A_V), lambda b, n: (0, 0)),
        ],
        out_specs=pl.BlockSpec((None, c, A_V), lambda b, n: (b, n, 0)),
        out_shape=jax.ShapeDtypeStruct((bn, seq, A_V), F32),
        scratch_shapes=[pltpu.VMEM((RET_HEADS, 2 * RET_DK, RET_DV), F32)],
        compiler_params=_params("arbitrary", "arbitrary"),
        name="retention",
    )(proj, proj, proj, proj, cos_t, sin_t, q_scale, k_scale, decay, chunk_decay, ret_norm)


def _rel_bucket(dist):
    max_exact = REL_BUCKETS // 2
    d = dist.astype(F32)
    large = max_exact + (jnp.log(jnp.maximum(d, 1.0) / max_exact) / math.log(REL_MAX_DIST / max_exact)
                         * (REL_BUCKETS - max_exact)).astype(jnp.int32)
    large = jnp.minimum(large, REL_BUCKETS - 1)
    return jnp.where(dist < max_exact, dist, large)


def _dilated_bias(rel_bias, window, dilation):
    lb = DIL_BLOCK
    w = window // dilation
    a_idx = jnp.arange(lb)[:, None]
    c_idx = jnp.arange(2 * lb)[None, :]
    dist = lb + a_idx - c_idx
    valid = (dist >= 0) & (dist <= w)
    bucket = _rel_bucket(jnp.maximum(dist, 0) * dilation)
    bias = jnp.transpose(rel_bias.astype(F32)[bucket], (2, 0, 1))
    return jnp.where(valid[None], bias, MASK_VALUE)


def _dilated_kernel(q_ref, kc_ref, kp_ref, vc_ref, vp_ref, bias_ref, o_ref, lse_ref):
    lb = DIL_BLOCK
    first_block = pl.program_id(2) == 0
    lane = lax.broadcasted_iota(jnp.int32, (lb, LANES), 1)
    head0 = lane < DIL_DH
    col = lax.broadcasted_iota(jnp.int32, (lb, 2 * lb), 1)
    no_prev = jnp.where(jnp.logical_and(col < lb, first_block), MASK_VALUE, 0.0)
    for p in range(DIL_HEADS // 2):
        sl = slice(p * LANES, (p + 1) * LANES)
        q = q_ref[:, sl] * (DIL_DH ** -0.5)
        k = jnp.concatenate([kp_ref[:, sl], kc_ref[:, sl]], axis=0).astype(BF16)
        v = jnp.concatenate([vp_ref[:, sl], vc_ref[:, sl]], axis=0).astype(BF16)
        outs, lses = [], []
        for a in range(2):
            mask = head0 if a == 0 else jnp.logical_not(head0)
            qm = jnp.where(mask, q, 0.0).astype(BF16)
            s = lax.dot_general(qm, k, NT_DIMS, preferred_element_type=F32)
            s = s + bias_ref[2 * p + a] + no_prev
            m = jnp.max(s, axis=-1, keepdims=True)
            e = jnp.exp(s - m)
            den = jnp.sum(e, axis=-1, keepdims=True)
            outs.append(jnp.dot(e.astype(BF16), v, preferred_element_type=F32) / den)
            lses.append(jnp.broadcast_to(m + jnp.log(den), (lb, LANES)))
        o_ref[:, sl] = jnp.where(head0, outs[0], outs[1])
        lse_ref[:, sl] = jnp.where(head0, lses[0], lses[1])


def _dilated_branch(proj, rel_bias, window, dilation, seq):
    bn = proj.shape[0]
    r = dilation
    lb = DIL_BLOCK
    length = seq // r
    nb = length // lb
    cols = EVEN_IN // B_W
    q_col, k_col, v_col = (2 * A_QK + 2 * A_V) // B_W, (2 * A_QK + 2 * A_V) // B_W + 1, \
        (2 * A_QK + 2 * A_V) // B_W + 2
    view = proj.reshape(bn, length, r * EVEN_IN)
    bias = _dilated_bias(rel_bias, window, dilation)
    cur = lambda c: pl.BlockSpec((None, lb, B_W), lambda b, rho, n: (b, n, rho * cols + c))
    prev = lambda c: pl.BlockSpec((None, lb, B_W),
                                  lambda b, rho, n: (b, jnp.maximum(n - 1, 0), rho * cols + c))
    out_spec = pl.BlockSpec((None, lb, B_W), lambda b, rho, n: (b, n, rho))
    o, lse = pl.pallas_call(
        _dilated_kernel,
        grid=(bn, r, nb),
        in_specs=[cur(q_col), cur(k_col), prev(k_col), cur(v_col), prev(v_col),
                  pl.BlockSpec((DIL_HEADS, lb, 2 * lb), lambda b, rho, n: (0, 0, 0))],
        out_specs=[out_spec, out_spec],
        out_shape=[jax.ShapeDtypeStruct((bn, length, r * B_W), F32)] * 2,
        compiler_params=_params("arbitrary", "arbitrary", "arbitrary"),
        name=f"dilated_r{r}",
    )(view, view, view, view, view, bias)
    return o.reshape(bn * seq, B_W), lse.reshape(bn * seq, B_W)


def _even_out_kernel(x_ref, ya_ref, o1_ref, o2_ref, o3_ref, l1_ref, l2_ref, l3_ref, wa_ref, wb_ref,
                     h_ref):
    l1, l2, l3 = l1_ref[...], l2_ref[...], l3_ref[...]
    m = jnp.maximum(jnp.maximum(l1, l2), l3)
    e1, e2, e3 = jnp.exp(l1 - m), jnp.exp(l2 - m), jnp.exp(l3 - m)
    tot = e1 + e2 + e3
    yb = (e1 / tot) * o1_ref[...] + (e2 / tot) * o2_ref[...] + (e3 / tot) * o3_ref[...]
    acc = jnp.dot(ya_ref[...].astype(BF16), wa_ref[...], preferred_element_type=F32)
    acc = acc + jnp.dot(yb.astype(BF16), wb_ref[...], preferred_element_type=F32)
    h_ref[...] = x_ref[...] + acc


def _even_out(x, ya, branches, w_out, tm):
    m = x.shape[0]
    row = lambda w: pl.BlockSpec((tm, w), lambda i: (i, 0))
    (o1, l1), (o2, l2), (o3, l3) = branches
    return pl.pallas_call(
        _even_out_kernel,
        grid=(m // tm,),
        in_specs=[row(D_MODEL), row(A_V)] + [row(B_W)] * 6 + [
            pl.BlockSpec((A_V, D_MODEL), lambda i: (0, 0)),
            pl.BlockSpec((B_W, D_MODEL), lambda i: (A_V // B_W, 0)),
        ],
        out_specs=row(D_MODEL),
        out_shape=jax.ShapeDtypeStruct((m, D_MODEL), F32),
        compiler_params=_params("arbitrary"),
        name="even_out",
    )(x, ya, o1, o2, o3, l1, l2, l3, w_out, w_out)


def _ffn_kernel(tiles_per_seq, final, h_ref, g_ref, wa_ref, wb_ref, cwa_ref, cwb_ref, cba_ref,
                cbb_ref, wd_ref, fg_ref, o_ref, hn_ref, acc_ref, ua_ref, ub_ref, halo_a_ref,
                halo_b_ref):
    i = pl.program_id(0)
    j = pl.program_id(1)
    tm = h_ref.shape[0]
    pad = SUBLANES

    @pl.when(j == 0)
    def _():
        x = h_ref[...]
        ms = jnp.mean(x * x, axis=-1, keepdims=True)
        hn_ref[...] = (x * lax.rsqrt(ms + EPS) * g_ref[...]).astype(BF16)
        acc_ref[...] = jnp.zeros_like(acc_ref)

    @pl.when((i % tiles_per_seq) == 0)
    def _():
        halo_a_ref[j] = jnp.zeros(halo_a_ref.shape[1:], F32)
        halo_b_ref[j] = jnp.zeros(halo_b_ref.shape[1:], F32)

    def conv_half(w_ref, cw_ref, cb_ref, u_ref, halo_ref):
        u = jnp.dot(hn_ref[...], w_ref[...], preferred_element_type=F32)
        u_ref[0:pad, :] = halo_ref[j]
        u_ref[pad:, :] = u
        halo_ref[j] = u[tm - pad:, :]
        cw = cw_ref[...]
        return (cw[0:1, :] * u_ref[pad - 2:pad - 2 + tm, :] + cw[1:2, :] * u_ref[pad - 1:pad - 1 + tm, :]
                + cw[2:3, :] * u + cb_ref[...])

    ca = conv_half(wa_ref, cwa_ref, cba_ref, ua_ref, halo_a_ref)
    cb = conv_half(wb_ref, cwb_ref, cbb_ref, ub_ref, halo_b_ref)
    act = (_silu(ca) * cb).astype(BF16)
    acc_ref[...] += jnp.dot(act, wd_ref[...], preferred_element_type=F32)

    @pl.when(j == pl.num_programs(1) - 1)
    def _():
        y = h_ref[...] + acc_ref[...]
        if final:
            ms = jnp.mean(y * y, axis=-1, keepdims=True)
            y = y * lax.rsqrt(ms + EPS) * fg_ref[...]
        o_ref[...] = y


def _ffn(h, g, w_up, conv_w, conv_b, w_down, final_g, seq, tm, tn, final):
    m = h.shape[0]
    nj = D_FF // tn
    col = lambda rows, off: pl.BlockSpec((rows, tn), lambda i, j: (0, j + off))
    return pl.pallas_call(
        functools.partial(_ffn_kernel, seq // tm, final),
        grid=(m // tm, nj),
        in_specs=[
            pl.BlockSpec((tm, D_MODEL), lambda i, j: (i, 0)),
            pl.BlockSpec((1, D_MODEL), lambda i, j: (0, 0)),
            col(D_MODEL, 0), col(D_MODEL, nj),
            col(CONV_WIDTH, 0), col(CONV_WIDTH, nj),
            col(1, 0), col(1, nj),
            pl.BlockSpec((tn, D_MODEL), lambda i, j: (j, 0)),
            pl.BlockSpec((1, D_MODEL), lambda i, j: (0, 0)),
        ],
        out_specs=pl.BlockSpec((tm, D_MODEL), lambda i, j: (i, 0)),
        out_shape=jax.ShapeDtypeStruct((m, D_MODEL), F32),
        scratch_shapes=[
            pltpu.VMEM((tm, D_MODEL), BF16),
            pltpu.VMEM((tm, D_MODEL), F32),
            pltpu.VMEM((tm + SUBLANES, tn), F32),
            pltpu.VMEM((tm + SUBLANES, tn), F32),
            pltpu.VMEM((nj, SUBLANES, tn), F32),
            pltpu.VMEM((nj, SUBLANES, tn), F32),
        ],
        compiler_params=_params("arbitrary", "arbitrary"),
        name="conv_ffn",
    )(h, g, w_up, w_up, conv_w, conv_w, conv_b, conv_b, w_down, final_g)


HGRN_ROWS = 256
HGRN_GROUP = 128


def _hgrn_kernel(layer, q_ref, f_ref, i_ref, g_ref, lbp_ref, on_ref, o_ref, st_ref):
    @pl.when(pl.program_id(2) == 0)
    def _():
        st_ref[...] = jnp.zeros_like(st_ref)

    c = HGRN_CHUNK
    grp = HGRN_GROUP
    lbp = lbp_ref[...]
    ex = jnp.exp(lbp - jnp.max(lbp, axis=0, keepdims=True))
    sm = ex / jnp.sum(ex, axis=0, keepdims=True)
    lower = jnp.sum(sm[0:layer + 1, :], axis=0, keepdims=True) - sm[0:1, :]

    row = lax.broadcasted_iota(jnp.int32, (grp, grp), 0)
    colm = lax.broadcasted_iota(jnp.int32, (grp, grp), 1)
    causal = jnp.logical_and(row // c == colm // c, row >= colm)
    tri = jnp.where(causal, 1.0, 0.0).astype(BF16)

    st = st_ref[...]
    for r0 in range(0, q_ref.shape[0], grp):
        rows = slice(r0, r0 + grp)
        q = _silu(q_ref[rows, :])
        fg = lower + (1.0 - lower) * jax.nn.sigmoid(f_ref[rows, :])
        k = 1.0 - fg
        log_f = jnp.log(fg)
        hi = log_f.astype(BF16)
        rem = log_f - hi.astype(F32)
        mid = rem.astype(BF16)
        lo = (rem - mid.astype(F32)).astype(BF16)
        b = (jnp.dot(tri, hi, preferred_element_type=F32) + jnp.dot(tri, mid, preferred_element_type=F32)
             + jnp.dot(tri, lo, preferred_element_type=F32))
        v = i_ref[rows, :].astype(BF16)
        q_t = (q * jnp.exp(b)).astype(BF16)
        k_t = (k * jnp.exp(-b)).astype(BF16)
        s = lax.dot_general(q_t, k_t, NT_DIMS, preferred_element_type=F32)
        s = jnp.where(causal, s, 0.0).astype(BF16)
        intra = jnp.dot(s, v, preferred_element_type=F32)
        inters = []
        for n in range(grp // c):
            cs = slice(n * c, (n + 1) * c)
            b_last = b[(n + 1) * c - 1:(n + 1) * c, :]
            k_end = (k[cs, :] * jnp.exp(b_last - b[cs, :])).astype(BF16)
            inters.append(lax.dot_general(q_t[cs, :], st.astype(BF16), NT_DIMS,
                                          preferred_element_type=F32))
            st = st * jnp.exp(b_last) + lax.dot_general(v[cs, :], k_end, TN_DIMS,
                                                        preferred_element_type=F32)
        y = intra + jnp.concatenate(inters, axis=0)
        yn = y * lax.rsqrt(jnp.mean(y * y, axis=-1, keepdims=True) + EPS)
        o_ref[rows, :] = (yn * on_ref[...] * _silu(g_ref[rows, :])).astype(o_ref.dtype)
    st_ref[...] = st


def _hgrn(proj, hgrn_lb, out_norm, layer, seq):
    bn = proj.shape[0]
    rows = HGRN_ROWS
    nh = HGRN_HEADS
    blk = lambda off: pl.BlockSpec((None, rows, HGRN_DK), lambda b, h, n: (b, n, h + off))
    return pl.pallas_call(
        functools.partial(_hgrn_kernel, layer),
        grid=(bn, nh, seq // rows),
        in_specs=[
            blk(0), blk(nh), blk(2 * nh), blk(3 * nh),
            pl.BlockSpec((hgrn_lb.shape[0], HGRN_DK), lambda b, h, n: (0, h)),
            pl.BlockSpec((1, HGRN_DV), lambda b, h, n: (0, h)),
        ],
        out_specs=pl.BlockSpec((None, rows, HGRN_DV), lambda b, h, n: (b, n, h)),
        out_shape=jax.ShapeDtypeStruct((bn, seq, C_V), F32),
        scratch_shapes=[pltpu.VMEM((HGRN_DV, HGRN_DK), F32)],
        compiler_params=_params("arbitrary", "arbitrary", "arbitrary"),
        name="hgrn2",
    )(proj, proj, proj, proj, hgrn_lb, out_norm)


def _matmul_residual_kernel(h_ref, y_ref, w_ref, o_ref):
    o_ref[...] = h_ref[...] + jnp.dot(y_ref[...].astype(BF16), w_ref[...],
                                      preferred_element_type=F32)


def _matmul_residual(h, y, w, tm):
    m, n = h.shape
    k = y.shape[1]
    return pl.pallas_call(
        _matmul_residual_kernel,
        grid=(m // tm,),
        in_specs=[
            pl.BlockSpec((tm, n), lambda i: (i, 0)),
            pl.BlockSpec((tm, k), lambda i: (i, 0)),
            pl.BlockSpec((k, n), lambda i: (0, 0)),
        ],
        out_specs=pl.BlockSpec((tm, n), lambda i: (i, 0)),
        out_shape=jax.ShapeDtypeStruct((m, n), F32),
        compiler_params=_params("arbitrary"),
        name="matmul_residual",
    )(h, y, w)


def kernel(x, even_w_in, even_w_out, ret_norm, rel_bias, odd_w_in, odd_w_out, hgrn_lb, hgrn_norm,
           mix_norm, ffn_norm, ffn_w_up, ffn_conv_w, ffn_conv_b, ffn_w_down, final_norm):
    bn, seq, d = x.shape
    tokens = bn * seq
    row = lambda a: a.reshape(1, -1).astype(F32)
    h = x.reshape(tokens, d)
    final_g = row(final_norm)

    proj = _norm_matmul(h, row(mix_norm[0]), even_w_in[0].astype(BF16), tm=1024, tn=1152)
    proj = proj.reshape(bn, seq, EVEN_IN)
    ya = _retention(proj, row(ret_norm[0]), seq).reshape(tokens, A_V)
    branches = [_dilated_branch(proj, rel_bias, w, r, seq) for w, r in DIL_BRANCHES]
    h = _even_out(h, ya, branches, even_w_out[0].astype(BF16), tm=512)
    h = _ffn(h, row(ffn_norm[0]), ffn_w_up[0].astype(BF16), ffn_conv_w[0].astype(F32),
             row(ffn_conv_b[0]), ffn_w_down[0].astype(BF16), final_g, seq, tm=1024, tn=256,
             final=False)

    proj = _norm_matmul(h, row(mix_norm[1]), odd_w_in[0].astype(BF16), tm=1024, tn=1024)
    y = _hgrn(proj.reshape(bn, seq, ODD_IN), hgrn_lb.astype(F32), row(hgrn_norm[0]), 1, seq)
    h = _matmul_residual(h, y.reshape(tokens, C_V), odd_w_out[0].astype(BF16), tm=1024)
    h = _ffn(h, row(ffn_norm[1]), ffn_w_up[1].astype(BF16), ffn_conv_w[1].astype(F32),
             row(ffn_conv_b[1]), ffn_w_down[1].astype(BF16), final_g, seq, tm=1024, tn=256,
             final=True)
    return h.reshape(bn, seq, d)
```

```python
import functools
import math

import numpy as np
import jax
import jax.numpy as jnp
from jax import lax
from jax.experimental import pallas as pl
from jax.experimental.pallas import tpu as pltpu

F32 = jnp.float32
BF16 = jnp.bfloat16

D_MODEL = 1024
RET_HEADS = 8
RET_DK = 64
RET_DV = 128
RET_CHUNK = 128
ROPE_BASE = 10000.0
DIL_HEADS = 8
DIL_DH = 64
DIL_BRANCHES = ((128, 1), (512, 4), (2048, 16))
DIL_BLOCK = 128
HGRN_HEADS = 8
HGRN_DK = 128
HGRN_DV = 128
HGRN_CHUNK = 32
REL_BUCKETS = 32
REL_MAX_DIST = 2048
D_FF = 2816
CONV_WIDTH = 3
EPS = 1e-6

A_QK = RET_HEADS * RET_DK
A_V = RET_HEADS * RET_DV
B_W = DIL_HEADS * DIL_DH
EVEN_IN = 2 * A_QK + 2 * A_V + 3 * B_W
C_K = HGRN_HEADS * HGRN_DK
C_V = HGRN_HEADS * HGRN_DV
ODD_IN = 2 * C_K + 2 * C_V

LANES = 128
SUBLANES = 8
VMEM_LIMIT = 48 * 1024 * 1024
MASK_VALUE = -1e30

NT_DIMS = (((1,), (1,)), ((), ()))
TN_DIMS = (((0,), (0,)), ((), ()))


def _params(*semantics):
    return pltpu.CompilerParams(dimension_semantics=semantics, vmem_limit_bytes=VMEM_LIMIT)


def _silu(x):
    return x * jax.nn.sigmoid(x)


def _norm_matmul_kernel(x_ref, g_ref, w_ref, o_ref, hn_ref):
    @pl.when(pl.program_id(1) == 0)
    def _():
        x = x_ref[...]
        ms = jnp.mean(x * x, axis=-1, keepdims=True)
        hn_ref[...] = (x * lax.rsqrt(ms + EPS) * g_ref[...]).astype(BF16)

    o_ref[...] = jnp.dot(hn_ref[...], w_ref[...], preferred_element_type=F32).astype(o_ref.dtype)


def _norm_matmul(x, g, w, tm, tn):
    m, k = x.shape
    n = w.shape[1]
    return pl.pallas_call(
        _norm_matmul_kernel,
        grid=(m // tm, n // tn),
        in_specs=[
            pl.BlockSpec((tm, k), lambda i, j: (i, 0)),
            pl.BlockSpec((1, k), lambda i, j: (0, 0)),
            pl.BlockSpec((k, tn), lambda i, j: (0, j)),
        ],
        out_specs=pl.BlockSpec((tm, tn), lambda i, j: (i, j)),
        out_shape=jax.ShapeDtypeStruct((m, n), F32),
        scratch_shapes=[pltpu.VMEM((tm, k), BF16)],
        compiler_params=_params("arbitrary", "arbitrary"),
        name="norm_matmul",
    )(x, g, w)


def _retention_tables(seq):
    h = jnp.arange(RET_HEADS, dtype=F32)
    log_g = jnp.log1p(-jnp.exp2(-5.0 - h))
    pos = jnp.arange(RET_CHUNK, dtype=F32)
    diff = pos[:, None] - pos[None, :]
    decay = jnp.where(diff >= 0, jnp.exp(jnp.maximum(diff, 0.0)[None] * log_g[:, None, None]), 0.0)
    q_scale = jnp.repeat(jnp.exp((pos + 1.0)[:, None] * log_g[None, :]), RET_DK, axis=1)
    k_scale = jnp.repeat(jnp.exp((RET_CHUNK - 1.0 - pos)[:, None] * log_g[None, :]), RET_DK, axis=1)
    chunk_decay = jnp.broadcast_to(jnp.exp(RET_CHUNK * log_g)[:, None, None], (RET_HEADS, 1, RET_DV))
    inv = ROPE_BASE ** (-jnp.arange(0, RET_DK, 2, dtype=F32) / RET_DK)
    ang = jnp.arange(seq, dtype=F32)[:, None] * inv[None, :]
    cos, sin = jnp.cos(ang), jnp.sin(ang)
    reps = LANES // RET_DK
    cos_t = jnp.tile(jnp.concatenate([cos, cos], axis=1), (1, reps))
    sin_t = jnp.tile(jnp.concatenate([-sin, sin], axis=1), (1, reps))
    return decay, q_scale, k_scale, chunk_decay, cos_t, sin_t


EVEN_TM = 512


def _even_in_kernel(x_ref, g_ref, w_ref, cos_ref, sin_ref, qs_ref, ks_ref,
                    qa_ref, qin_ref, ka_ref, kout_ref, va_ref, ga_ref,
                    qn_ref, q4_ref, q16_ref, kn_ref, k4_ref, k16_ref, vn_ref, v4_ref, v16_ref,
                    slab_ref):
    tm = x_ref.shape[0]
    x = x_ref[...]
    ms = jnp.mean(x * x, axis=-1, keepdims=True)
    hn = (x * lax.rsqrt(ms + EPS) * g_ref[...]).astype(BF16)

    def proj(c):
        return jnp.dot(hn, w_ref[:, c * B_W:(c + 1) * B_W], preferred_element_type=F32)

    cos = cos_ref[...]
    sin = sin_ref[...]
    lane = lax.broadcasted_iota(jnp.int32, (tm, LANES), 1)
    half = RET_DK // 2
    first_half = (lane % RET_DK) < half

    def rot(t):
        partner = jnp.where(first_half, pltpu.roll(t, LANES - half, 1), pltpu.roll(t, half, 1))
        return t * cos + partner * sin

    groups = [slice(s * LANES, (s + 1) * LANES) for s in range(B_W // LANES)]
    r = proj(0)
    for sl in groups:
        q = rot(r[:, sl])
        qa_ref[:, sl] = q.astype(BF16)
        qin_ref[:, sl] = (q * qs_ref[:, sl]).astype(BF16)
    r = proj(1)
    for sl in groups:
        k = rot(r[:, sl]) * (RET_DK ** -0.5)
        ka_ref[:, sl] = k.astype(BF16)
        kout_ref[:, sl] = (k * ks_ref[:, sl]).astype(BF16)
    va_ref[:, 0:B_W] = proj(2).astype(BF16)
    va_ref[:, B_W:2 * B_W] = proj(3).astype(BF16)
    ga_ref[:, 0:B_W] = proj(4)
    ga_ref[:, B_W:2 * B_W] = proj(5)
    for c, nat_ref, r4_ref, r16_ref, scale in ((6, qn_ref, q4_ref, q16_ref, DIL_DH ** -0.5),
                                               (7, kn_ref, k4_ref, k16_ref, None),
                                               (8, vn_ref, v4_ref, v16_ref, None)):
        r = proj(c)
        if scale is not None:
            r = r * scale
        nat_ref[...] = r.astype(BF16)
        for s, sl in enumerate(groups):
            slab_ref[s] = r[:, sl]
        for s, sl in enumerate(groups):
            for rho in range(4):
                r4_ref[rho, :, sl] = slab_ref[s, pl.ds(rho, tm // 4, stride=4), :].astype(BF16)
            for rho in range(16):
                r16_ref[rho, :, sl] = slab_ref[s, pl.ds(rho, tm // 16, stride=16), :].astype(BF16)


def _even_in(x, g, w, seq):
    tokens, d = x.shape
    bn = tokens // seq
    tm = EVEN_TM
    tps = seq // tm
    _, q_scale, k_scale, _, cos_t, sin_t = _retention_tables(seq)
    q_scale = jnp.tile(q_scale, (tm // RET_CHUNK, 1))
    k_scale = jnp.tile(k_scale, (tm // RET_CHUNK, 1))
    const = lambda shape: pl.BlockSpec(shape, lambda i: (0,) * len(shape))
    flat = lambda w_: pl.BlockSpec((tm, w_), lambda i: (i, 0))
    nat = pl.BlockSpec((None, tm, B_W), lambda i: (i // tps, i % tps, 0))
    by = lambda r: pl.BlockSpec((None, r, tm // r, B_W), lambda i: (i // tps, 0, i % tps, 0))
    sds = jax.ShapeDtypeStruct
    lay = [sds((bn, seq, B_W), BF16), sds((bn, 4, seq // 4, B_W), BF16),
           sds((bn, 16, seq // 16, B_W), BF16)]
    return pl.pallas_call(
        _even_in_kernel,
        grid=(tokens // tm,),
        in_specs=[
            flat(d), const((1, d)),
            pl.BlockSpec(w.shape, lambda i: (0, 0), pipeline_mode=pl.Buffered(1)),
            pl.BlockSpec((tm, LANES), lambda i: (i % tps, 0)),
            pl.BlockSpec((tm, LANES), lambda i: (i % tps, 0)),
            const((tm, A_QK)), const((tm, A_QK)),
        ],
        out_specs=[flat(A_QK)] * 4 + [flat(A_V), flat(A_V)] + [nat, by(4), by(16)] * 3,
        out_shape=[sds((tokens, A_QK), BF16)] * 4 + [sds((tokens, A_V), BF16), sds((tokens, A_V), F32)]
        + lay * 3,
        scratch_shapes=[pltpu.VMEM((B_W // LANES, tm, LANES), F32)],
        compiler_params=_params("arbitrary"),
        name="even_in",
    )(x, g, w, cos_t, sin_t, q_scale, k_scale)


def _retention_kernel(q_ref, qin_ref, k_ref, kout_ref, v_ref, g_ref, dec_ref, cd_ref, rn_ref, o_ref,
                      st_ref):
    @pl.when(pl.program_id(1) == 0)
    def _():
        st_ref[...] = jnp.zeros_like(st_ref)

    lane = lax.broadcasted_iota(jnp.int32, (RET_CHUNK, LANES), 1)
    head0 = lane < RET_DK
    for p in range(RET_HEADS // 2):
        sl = slice(p * LANES, (p + 1) * LANES)
        q, q_in, k, k_out = q_ref[:, sl], qin_ref[:, sl], k_ref[:, sl], kout_ref[:, sl]
        for a in range(2):
            h = 2 * p + a
            hs = slice(h * RET_DV, (h + 1) * RET_DV)
            mask = head0 if a == 0 else jnp.logical_not(head0)
            qm = jnp.where(mask, q, 0)
            qim = jnp.where(mask, q_in, 0)
            v = v_ref[:, hs]
            s = lax.dot_general(qm, k, NT_DIMS, preferred_element_type=F32) * dec_ref[h]
            intra = jnp.dot(s.astype(BF16), v, preferred_element_type=F32)
            st = st_ref[h]
            inter = jnp.dot(qim, st.astype(BF16), preferred_element_type=F32)
            st_ref[h] = st * cd_ref[h] + lax.dot_general(k_out, v, TN_DIMS,
                                                         preferred_element_type=F32)
            y = intra + inter
            yc = y - jnp.mean(y, axis=-1, keepdims=True)
            yn = yc * lax.rsqrt(jnp.mean(yc * yc, axis=-1, keepdims=True) + EPS)
            o_ref[:, hs] = (yn * rn_ref[:, hs] * _silu(g_ref[:, hs])).astype(o_ref.dtype)


def _retention(qa, qin, ka, kout, va, ga, ret_norm, bn, seq):
    c = RET_CHUNK
    decay, _, _, chunk_decay, _, _ = _retention_tables(seq)
    nchunk = seq // c
    blk = lambda w: pl.BlockSpec((c, w), lambda b, n: (b * nchunk + n, 0))
    return pl.pallas_call(
        _retention_kernel,
        grid=(bn, nchunk),
        in_specs=[
            blk(A_QK), blk(A_QK), blk(A_QK), blk(A_QK), blk(A_V), blk(A_V),
            pl.BlockSpec((RET_HEADS, c, c), lambda b, n: (0, 0, 0)),
            pl.BlockSpec((RET_HEADS, 1, RET_DV), lambda b, n: (0, 0, 0)),
            pl.BlockSpec((1, A_V), lambda b, n: (0, 0)),
        ],
        out_specs=blk(A_V),
        out_shape=jax.ShapeDtypeStruct((bn * seq, A_V), BF16),
        scratch_shapes=[pltpu.VMEM((RET_HEADS, 2 * RET_DK, RET_DV), F32)],
        compiler_params=_params("arbitrary", "arbitrary"),
        name="retention",
    )(qa, qin, ka, kout, va, ga, decay, chunk_decay, ret_norm)


def _rel_bucket(dist):
    max_exact = REL_BUCKETS // 2
    d = dist.astype(F32)
    large = max_exact + (jnp.log(jnp.maximum(d, 1.0) / max_exact) / math.log(REL_MAX_DIST / max_exact)
                         * (REL_BUCKETS - max_exact)).astype(jnp.int32)
    large = jnp.minimum(large, REL_BUCKETS - 1)
    return jnp.where(dist < max_exact, dist, large)


def _dilated_buckets():
    lb = DIL_BLOCK
    a_idx = jnp.arange(lb)[:, None]
    c_idx = jnp.arange(2 * lb)[None, :]
    dist = lb + a_idx - c_idx
    tables = []
    for window, dilation in DIL_BRANCHES:
        valid = (dist >= 0) & (dist <= window // dilation)
        tables.append(jnp.where(valid, _rel_bucket(jnp.maximum(dist, 0) * dilation), -1))
    return jnp.stack(tables).astype(jnp.int32)


def _dilated_kernel(rb_ref, bucket_ref, qn_ref, kn_ref, vn_ref, q4_ref, k4_ref, v4_ref, q16_ref,
                    k16_ref, v16_ref, y_ref, bias_ref, acc_ref, m_ref, l_ref):
    lb = DIL_BLOCK
    seq = qn_ref.shape[0]
    pair = pl.program_id(1)

    @pl.when(pl.program_id(0) == 0)
    def _():
        for br in range(len(DIL_BRANCHES)):
            bucket = bucket_ref[br]
            for a in range(2):
                h = 2 * pair + a
                t = jnp.full((lb, 2 * lb), MASK_VALUE, F32)
                for j in range(REL_BUCKETS):
                    t = jnp.where(bucket == j, rb_ref[j, h], t)
                bias_ref[br * DIL_HEADS + h] = t

    lane = lax.broadcasted_iota(jnp.int32, (lb, LANES), 1)
    head0 = lane < DIL_DH
    prev_cols = lax.broadcasted_iota(jnp.int32, (lb, 2 * lb), 1) < lb

    def attend(br, first, q, kp, kc, vp, vc):
        k = jnp.concatenate([kp, kc], axis=0)
        v = jnp.concatenate([vp, vc], axis=0)
        no_prev = jnp.where(jnp.logical_and(prev_cols, first), MASK_VALUE, 0.0)
        parts = []
        for a in range(2):
            mask = head0 if a == 0 else jnp.logical_not(head0)
            s = lax.dot_general(jnp.where(mask, q, 0), k, NT_DIMS, preferred_element_type=F32)
            s = s + bias_ref[br * DIL_HEADS + 2 * pair + a] + no_prev
            m = jnp.max(s, axis=-1, keepdims=True)
            e = jnp.exp(s - m)
            parts.append((jnp.dot(e.astype(BF16), v, preferred_element_type=F32), m,
                          jnp.sum(e, axis=-1, keepdims=True)))
        return tuple(jnp.where(head0, x0, x1) for x0, x1 in zip(*parts))

    def natural_block(n, carry):
        r0 = pl.multiple_of(n * lb, lb)
        rp = pl.multiple_of(jnp.maximum(n - 1, 0) * lb, lb)
        cur, prev = pl.ds(r0, lb), pl.ds(rp, lb)
        acc, m, l = attend(0, n == 0, qn_ref[cur, :], kn_ref[prev, :], kn_ref[cur, :],
                           vn_ref[prev, :], vn_ref[cur, :])
        acc_ref[cur, :] = acc
        m_ref[cur, :] = m
        l_ref[cur, :] = l
        return carry

    lax.fori_loop(0, seq // lb, natural_block, 0)

    def dilated_blocks(br, r, q_ref, k_ref, v_ref):
        nb = seq // (r * lb)

        def body(idx, carry):
            rho, n = idx // nb, idx % nb
            r0 = pl.multiple_of(n * lb, lb)
            rp = pl.multiple_of(jnp.maximum(n - 1, 0) * lb, lb)
            cur, prev = pl.ds(r0, lb), pl.ds(rp, lb)
            acc, m, l = attend(br, n == 0, q_ref[rho, cur, :], k_ref[rho, prev, :], k_ref[rho, cur, :],
                               v_ref[rho, prev, :], v_ref[rho, cur, :])
            rows = pl.ds(n * (lb * r) + rho, lb, stride=r)
            m_old = m_ref[rows, :]
            m_new = jnp.maximum(m_old, m)
            c_old = jnp.exp(m_old - m_new)
            c_cur = jnp.exp(m - m_new)
            acc_ref[rows, :] = acc_ref[rows, :] * c_old + acc * c_cur
            l_ref[rows, :] = l_ref[rows, :] * c_old + l * c_cur
            m_ref[rows, :] = m_new
            return carry

        lax.fori_loop(0, r * nb, body, 0)

    dilated_blocks(1, DIL_BRANCHES[1][1], q4_ref, k4_ref, v4_ref)
    dilated_blocks(2, DIL_BRANCHES[2][1], q16_ref, k16_ref, v16_ref)

    def finish(n, carry):
        rows = pl.ds(pl.multiple_of(n * lb, lb), lb)
        y_ref[rows, :] = (acc_ref[rows, :] / l_ref[rows, :]).astype(y_ref.dtype)
        return carry

    lax.fori_loop(0, seq // lb, finish, 0)


def _dilated(layouts, rel_bias, bn, seq):
    lb = DIL_BLOCK
    assert all(w // r <= lb and seq % (r * lb) == 0 for w, r in DIL_BRANCHES)
    assert tuple(r for _, r in DIL_BRANCHES) == (1, 4, 16)
    nat = pl.BlockSpec((None, seq, LANES), lambda b, p: (b, 0, p))
    by = lambda r: pl.BlockSpec((None, r, seq // r, LANES), lambda b, p: (b, 0, 0, p))
    nbr = len(DIL_BRANCHES)
    return pl.pallas_call(
        _dilated_kernel,
        grid=(bn, DIL_HEADS // 2),
        in_specs=[pl.BlockSpec(memory_space=pltpu.SMEM),
                  pl.BlockSpec((nbr, lb, 2 * lb), lambda b, p: (0, 0, 0))]
        + [nat] * 3 + [by(4)] * 3 + [by(16)] * 3,
        out_specs=nat,
        out_shape=jax.ShapeDtypeStruct((bn, seq, B_W), BF16),
        scratch_shapes=[pltpu.VMEM((nbr * DIL_HEADS, lb, 2 * lb), F32),
                        pltpu.VMEM((seq, LANES), F32), pltpu.VMEM((seq, LANES), F32),
                        pltpu.VMEM((seq, LANES), F32)],
        compiler_params=_params("arbitrary", "arbitrary"),
        name="dilated",
    )(rel_bias.astype(F32), _dilated_buckets(), *layouts)


def _even_out_kernel(x_ref, ya_ref, yb_ref, wa_ref, wb_ref, h_ref):
    acc = jnp.dot(ya_ref[...], wa_ref[...], preferred_element_type=F32)
    acc = acc + jnp.dot(yb_ref[...], wb_ref[...], preferred_element_type=F32)
    h_ref[...] = x_ref[...] + acc


def _even_out(x, ya, yb, w_out, tm):
    m = x.shape[0]
    row = lambda w: pl.BlockSpec((tm, w), lambda i: (i, 0))
    return pl.pallas_call(
        _even_out_kernel,
        grid=(m // tm,),
        in_specs=[row(D_MODEL), row(A_V), row(B_W),
                  pl.BlockSpec((A_V, D_MODEL), lambda i: (0, 0)),
                  pl.BlockSpec((B_W, D_MODEL), lambda i: (A_V // B_W, 0))],
        out_specs=row(D_MODEL),
        out_shape=jax.ShapeDtypeStruct((m, D_MODEL), F32),
        compiler_params=_params("arbitrary"),
        name="even_out",
    )(x, ya, yb, w_out, w_out)


def _ffn_kernel(tiles_per_seq, final, h_ref, g_ref, wa_ref, wb_ref, cwa_ref, cwb_ref, cba_ref,
                cbb_ref, wd_ref, fg_ref, o_ref, hn_ref, acc_ref, ua_ref, ub_ref, halo_a_ref,
                halo_b_ref):
    i = pl.program_id(0)
    j = pl.program_id(1)
    tm = h_ref.shape[0]
    pad = SUBLANES

    @pl.when(j == 0)
    def _():
        x = h_ref[...]
        ms = jnp.mean(x * x, axis=-1, keepdims=True)
        hn_ref[...] = (x * lax.rsqrt(ms + EPS) * g_ref[...]).astype(BF16)
        acc_ref[...] = jnp.zeros_like(acc_ref)

    @pl.when((i % tiles_per_seq) == 0)
    def _():
        halo_a_ref[j] = jnp.zeros(halo_a_ref.shape[1:], F32)
        halo_b_ref[j] = jnp.zeros(halo_b_ref.shape[1:], F32)

    def conv_half(w_ref, cw_ref, cb_ref, u_ref, halo_ref):
        u = jnp.dot(hn_ref[...], w_ref[...], preferred_element_type=F32)
        u_ref[0:pad, :] = halo_ref[j]
        u_ref[pad:, :] = u
        halo_ref[j] = u[tm - pad:, :]
        cw = cw_ref[...]
        return (cw[0:1, :] * u_ref[pad - 2:pad - 2 + tm, :] + cw[1:2, :] * u_ref[pad - 1:pad - 1 + tm, :]
                + cw[2:3, :] * u + cb_ref[...])

    ca = conv_half(wa_ref, cwa_ref, cba_ref, ua_ref, halo_a_ref)
    cb = conv_half(wb_ref, cwb_ref, cbb_ref, ub_ref, halo_b_ref)
    act = (_silu(ca) * cb).astype(BF16)
    acc_ref[...] += jnp.dot(act, wd_ref[...], preferred_element_type=F32)

    @pl.when(j == pl.num_programs(1) - 1)
    def _():
        y = h_ref[...] + acc_ref[...]
        if final:
            ms = jnp.mean(y * y, axis=-1, keepdims=True)
            y = y * lax.rsqrt(ms + EPS) * fg_ref[...]
        o_ref[...] = y


def _ffn(h, g, w_up, conv_w, conv_b, w_down, final_g, seq, tm, tn, final):
    m = h.shape[0]
    nj = D_FF // tn
    col = lambda rows, off: pl.BlockSpec((rows, tn), lambda i, j: (0, j + off))
    return pl.pallas_call(
        functools.partial(_ffn_kernel, seq // tm, final),
        grid=(m // tm, nj),
        in_specs=[
            pl.BlockSpec((tm, D_MODEL), lambda i, j: (i, 0)),
            pl.BlockSpec((1, D_MODEL), lambda i, j: (0, 0)),
            col(D_MODEL, 0), col(D_MODEL, nj),
            col(CONV_WIDTH, 0), col(CONV_WIDTH, nj),
            col(1, 0), col(1, nj),
            pl.BlockSpec((tn, D_MODEL), lambda i, j: (j, 0)),
            pl.BlockSpec((1, D_MODEL), lambda i, j: (0, 0)),
        ],
        out_specs=pl.BlockSpec((tm, D_MODEL), lambda i, j: (i, 0)),
        out_shape=jax.ShapeDtypeStruct((m, D_MODEL), F32),
        scratch_shapes=[
            pltpu.VMEM((tm, D_MODEL), BF16),
            pltpu.VMEM((tm, D_MODEL), F32),
            pltpu.VMEM((tm + SUBLANES, tn), F32),
            pltpu.VMEM((tm + SUBLANES, tn), F32),
            pltpu.VMEM((nj, SUBLANES, tn), F32),
            pltpu.VMEM((nj, SUBLANES, tn), F32),
        ],
        compiler_params=_params("arbitrary", "arbitrary"),
        name="conv_ffn",
    )(h, g, w_up, w_up, conv_w, conv_w, conv_b, conv_b, w_down, final_g)


HGRN_ROWS = 256
HGRN_GROUP = 128


def _hgrn_kernel(layer, q_ref, f_ref, i_ref, g_ref, lbp_ref, on_ref, o_ref, st_ref):
    @pl.when(pl.program_id(2) == 0)
    def _():
        st_ref[...] = jnp.zeros_like(st_ref)

    c = HGRN_CHUNK
    grp = HGRN_GROUP
    lbp = lbp_ref[...]
    ex = jnp.exp(lbp - jnp.max(lbp, axis=0, keepdims=True))
    sm = ex / jnp.sum(ex, axis=0, keepdims=True)
    lower = jnp.sum(sm[0:layer + 1, :], axis=0, keepdims=True) - sm[0:1, :]

    row = lax.broadcasted_iota(jnp.int32, (grp, grp), 0)
    colm = lax.broadcasted_iota(jnp.int32, (grp, grp), 1)
    causal = jnp.logical_and(row // c == colm // c, row >= colm)
    tri = jnp.where(causal, 1.0, 0.0).astype(BF16)

    st = st_ref[...]
    for r0 in range(0, q_ref.shape[0], grp):
        rows = slice(r0, r0 + grp)
        q = _silu(q_ref[rows, :])
        fg = lower + (1.0 - lower) * jax.nn.sigmoid(f_ref[rows, :])
        k = 1.0 - fg
        log_f = jnp.log(fg)
        hi = log_f.astype(BF16)
        rem = log_f - hi.astype(F32)
        mid = rem.astype(BF16)
        lo = (rem - mid.astype(F32)).astype(BF16)
        b = (jnp.dot(tri, hi, preferred_element_type=F32) + jnp.dot(tri, mid, preferred_element_type=F32)
             + jnp.dot(tri, lo, preferred_element_type=F32))
        v = i_ref[rows, :].astype(BF16)
        q_t = (q * jnp.exp(b)).astype(BF16)
        k_t = (k * jnp.exp(-b)).astype(BF16)
        s = lax.dot_general(q_t, k_t, NT_DIMS, preferred_element_type=F32)
        s = jnp.where(causal, s, 0.0).astype(BF16)
        intra = jnp.dot(s, v, preferred_element_type=F32)
        inters = []
        for n in range(grp // c):
            cs = slice(n * c, (n + 1) * c)
            b_last = b[(n + 1) * c - 1:(n + 1) * c, :]
            k_end = (k[cs, :] * jnp.exp(b_last - b[cs, :])).astype(BF16)
            inters.append(lax.dot_general(q_t[cs, :], st.astype(BF16), NT_DIMS,
                                          preferred_element_type=F32))
            st = st * jnp.exp(b_last) + lax.dot_general(v[cs, :], k_end, TN_DIMS,
                                                        preferred_element_type=F32)
        y = intra + jnp.concatenate(inters, axis=0)
        yn = y * lax.rsqrt(jnp.mean(y * y, axis=-1, keepdims=True) + EPS)
        o_ref[rows, :] = (yn * on_ref[...] * _silu(g_ref[rows, :])).astype(o_ref.dtype)
    st_ref[...] = st


def _hgrn(proj, hgrn_lb, out_norm, layer, seq):
    bn = proj.shape[0]
    rows = HGRN_ROWS
    nh = HGRN_HEADS
    blk = lambda off: pl.BlockSpec((None, rows, HGRN_DK), lambda b, h, n: (b, n, h + off))
    return pl.pallas_call(
        functools.partial(_hgrn_kernel, layer),
        grid=(bn, nh, seq // rows),
        in_specs=[
            blk(0), blk(nh), blk(2 * nh), blk(3 * nh),
            pl.BlockSpec((hgrn_lb.shape[0], HGRN_DK), lambda b, h, n: (0, h)),
            pl.BlockSpec((1, HGRN_DV), lambda b, h, n: (0, h)),
        ],
        out_specs=pl.BlockSpec((None, rows, HGRN_DV), lambda b, h, n: (b, n, h)),
        out_shape=jax.ShapeDtypeStruct((bn, seq, C_V), F32),
        scratch_shapes=[pltpu.VMEM((HGRN_DV, HGRN_DK), F32)],
        compiler_params=_params("arbitrary", "arbitrary", "arbitrary"),
        name="hgrn2",
    )(proj, proj, proj, proj, hgrn_lb, out_norm)


def _matmul_residual_kernel(h_ref, y_ref, w_ref, o_ref):
    o_ref[...] = h_ref[...] + jnp.dot(y_ref[...].astype(BF16), w_ref[...],
                                      preferred_element_type=F32)


def _matmul_residual(h, y, w, tm):
    m, n = h.shape
    k = y.shape[1]
    return pl.pallas_call(
        _matmul_residual_kernel,
        grid=(m // tm,),
        in_specs=[
            pl.BlockSpec((tm, n), lambda i: (i, 0)),
            pl.BlockSpec((tm, k), lambda i: (i, 0)),
            pl.BlockSpec((k, n), lambda i: (0, 0)),
        ],
        out_specs=pl.BlockSpec((tm, n), lambda i: (i, 0)),
        out_shape=jax.ShapeDtypeStruct((m, n), F32),
        compiler_params=_params("arbitrary"),
        name="matmul_residual",
    )(h, y, w)


def kernel(x, even_w_in, even_w_out, ret_norm, rel_bias, odd_w_in, odd_w_out, hgrn_lb, hgrn_norm,
           mix_norm, ffn_norm, ffn_w_up, ffn_conv_w, ffn_conv_b, ffn_w_down, final_norm):
    bn, seq, d = x.shape
    tokens = bn * seq
    row = lambda a: a.reshape(1, -1).astype(F32)
    h = x.reshape(tokens, d)
    final_g = row(final_norm)

    outs = _even_in(h, row(mix_norm[0]), even_w_in[0].astype(BF16), seq)
    qa, qin, ka, kout, va, ga = outs[:6]
    qn, q4, q16, kn, k4, k16, vn, v4, v16 = outs[6:]
    ya = _retention(qa, qin, ka, kout, va, ga, row(ret_norm[0]), bn, seq)
    yb = _dilated((qn, kn, vn, q4, k4, v4, q16, k16, v16), rel_bias, bn, seq)
    h = _even_out(h, ya, yb.reshape(tokens, B_W), even_w_out[0].astype(BF16), tm=1024)
    h = _ffn(h, row(ffn_norm[0]), ffn_w_up[0].astype(BF16), ffn_conv_w[0].astype(F32),
             row(ffn_conv_b[0]), ffn_w_down[0].astype(BF16), final_g, seq, tm=1024, tn=256,
             final=False)

    proj = _norm_matmul(h, row(mix_norm[1]), odd_w_in[0].astype(BF16), tm=1024, tn=1024)
    y = _hgrn(proj.reshape(bn, seq, ODD_IN), hgrn_lb.astype(F32), row(hgrn_norm[0]), 1, seq)
    h = _matmul_residual(h, y.reshape(tokens, C_V), odd_w_out[0].astype(BF16), tm=1024)
    h = _ffn(h, row(ffn_norm[1]), ffn_w_up[1].astype(BF16), ffn_conv_w[1].astype(F32),
             row(ffn_conv_b[1]), ffn_w_down[1].astype(BF16), final_g, seq, tm=1024, tn=256,
             final=True)
    return h.reshape(bn, seq, d)
```

```python
import functools
import math

import numpy as np
import jax
import jax.numpy as jnp
from jax import lax
from jax.experimental import pallas as pl
from jax.experimental.pallas import tpu as pltpu

F32 = jnp.float32
BF16 = jnp.bfloat16

D_MODEL = 1024
RET_HEADS = 8
RET_DK = 64
RET_DV = 128
RET_CHUNK = 128
ROPE_BASE = 10000.0
DIL_HEADS = 8
DIL_DH = 64
DIL_BRANCHES = ((128, 1), (512, 4), (2048, 16))
DIL_BLOCK = 128
HGRN_HEADS = 8
HGRN_DK = 128
HGRN_DV = 128
HGRN_CHUNK = 32
REL_BUCKETS = 32
REL_MAX_DIST = 2048
D_FF = 2816
CONV_WIDTH = 3
EPS = 1e-6

A_QK = RET_HEADS * RET_DK
A_V = RET_HEADS * RET_DV
B_W = DIL_HEADS * DIL_DH
EVEN_IN = 2 * A_QK + 2 * A_V + 3 * B_W
C_K = HGRN_HEADS * HGRN_DK
C_V = HGRN_HEADS * HGRN_DV
ODD_IN = 2 * C_K + 2 * C_V

LANES = 128
SUBLANES = 8
VMEM_LIMIT = 48 * 1024 * 1024
MASK_VALUE = -1e30

NT_DIMS = (((1,), (1,)), ((), ()))
TN_DIMS = (((0,), (0,)), ((), ()))


def _params(*semantics):
    return pltpu.CompilerParams(dimension_semantics=semantics, vmem_limit_bytes=VMEM_LIMIT)


def _silu(x):
    return x * jax.nn.sigmoid(x)


def _norm_matmul_kernel(x_ref, g_ref, w_ref, o_ref, hn_ref):
    @pl.when(pl.program_id(1) == 0)
    def _():
        x = x_ref[...]
        ms = jnp.mean(x * x, axis=-1, keepdims=True)
        hn_ref[...] = (x * lax.rsqrt(ms + EPS) * g_ref[...]).astype(BF16)

    o_ref[...] = jnp.dot(hn_ref[...], w_ref[...], preferred_element_type=F32).astype(o_ref.dtype)


def _norm_matmul(x, g, w, tm, tn):
    m, k = x.shape
    n = w.shape[1]
    return pl.pallas_call(
        _norm_matmul_kernel,
        grid=(m // tm, n // tn),
        in_specs=[
            pl.BlockSpec((tm, k), lambda i, j: (i, 0)),
            pl.BlockSpec((1, k), lambda i, j: (0, 0)),
            pl.BlockSpec((k, tn), lambda i, j: (0, j)),
        ],
        out_specs=pl.BlockSpec((tm, tn), lambda i, j: (i, j)),
        out_shape=jax.ShapeDtypeStruct((m, n), F32),
        scratch_shapes=[pltpu.VMEM((tm, k), BF16)],
        compiler_params=_params("arbitrary", "arbitrary"),
        name="norm_matmul",
    )(x, g, w)


def _retention_tables(seq):
    h = jnp.arange(RET_HEADS, dtype=F32)
    log_g = jnp.log1p(-jnp.exp2(-5.0 - h))
    pos = jnp.arange(RET_CHUNK, dtype=F32)
    diff = pos[:, None] - pos[None, :]
    decay = jnp.where(diff >= 0, jnp.exp(jnp.maximum(diff, 0.0)[None] * log_g[:, None, None]), 0.0)
    q_scale = jnp.repeat(jnp.exp((pos + 1.0)[:, None] * log_g[None, :]), RET_DK, axis=1)
    k_scale = jnp.repeat(jnp.exp((RET_CHUNK - 1.0 - pos)[:, None] * log_g[None, :]), RET_DK, axis=1)
    chunk_decay = jnp.broadcast_to(jnp.exp(RET_CHUNK * log_g)[:, None, None], (RET_HEADS, 1, RET_DV))
    inv = ROPE_BASE ** (-jnp.arange(0, RET_DK, 2, dtype=F32) / RET_DK)
    ang = jnp.arange(seq, dtype=F32)[:, None] * inv[None, :]
    cos, sin = jnp.cos(ang), jnp.sin(ang)
    reps = LANES // RET_DK
    cos_t = jnp.tile(jnp.concatenate([cos, cos], axis=1), (1, reps))
    sin_t = jnp.tile(jnp.concatenate([-sin, sin], axis=1), (1, reps))
    return decay, q_scale, k_scale, chunk_decay, cos_t, sin_t


EVEN_TM = 512


def _even_in_kernel(x_ref, g_ref, w_ref, cos_ref, sin_ref, qs_ref, ks_ref,
                    qa_ref, qin_ref, ka_ref, kout_ref, va_ref, ga_ref,
                    qn_ref, q4_ref, q16_ref, kn_ref, k4_ref, k16_ref, vn_ref, v4_ref, v16_ref,
                    slab_ref):
    tm = x_ref.shape[0]
    x = x_ref[...]
    ms = jnp.mean(x * x, axis=-1, keepdims=True)
    hn = (x * lax.rsqrt(ms + EPS) * g_ref[...]).astype(BF16)

    def proj(c):
        return jnp.dot(hn, w_ref[:, c * B_W:(c + 1) * B_W], preferred_element_type=F32)

    cos = cos_ref[...]
    sin = sin_ref[...]
    lane = lax.broadcasted_iota(jnp.int32, (tm, LANES), 1)
    half = RET_DK // 2
    first_half = (lane % RET_DK) < half

    def rot(t):
        partner = jnp.where(first_half, pltpu.roll(t, LANES - half, 1), pltpu.roll(t, half, 1))
        return t * cos + partner * sin

    groups = [slice(s * LANES, (s + 1) * LANES) for s in range(B_W // LANES)]
    r = proj(0)
    for sl in groups:
        q = rot(r[:, sl])
        qa_ref[:, sl] = q.astype(BF16)
        qin_ref[:, sl] = (q * qs_ref[:, sl]).astype(BF16)
    r = proj(1)
    for sl in groups:
        k = rot(r[:, sl]) * (RET_DK ** -0.5)
        ka_ref[:, sl] = k.astype(BF16)
        kout_ref[:, sl] = (k * ks_ref[:, sl]).astype(BF16)
    va_ref[:, 0:B_W] = proj(2).astype(BF16)
    va_ref[:, B_W:2 * B_W] = proj(3).astype(BF16)
    ga_ref[:, 0:B_W] = proj(4)
    ga_ref[:, B_W:2 * B_W] = proj(5)
    for c, nat_ref, r4_ref, r16_ref, scale in ((6, qn_ref, q4_ref, q16_ref, DIL_DH ** -0.5),
                                               (7, kn_ref, k4_ref, k16_ref, None),
                                               (8, vn_ref, v4_ref, v16_ref, None)):
        r = proj(c)
        if scale is not None:
            r = r * scale
        nat_ref[...] = r.astype(BF16)
        for s, sl in enumerate(groups):
            slab_ref[s] = r[:, sl]
        for s, sl in enumerate(groups):
            for rho in range(4):
                r4_ref[rho, :, sl] = slab_ref[s, pl.ds(rho, tm // 4, stride=4), :].astype(BF16)
            for rho in range(16):
                r16_ref[rho, :, sl] = slab_ref[s, pl.ds(rho, tm // 16, stride=16), :].astype(BF16)


def _even_in(x, g, w, seq):
    tokens, d = x.shape
    bn = tokens // seq
    tm = EVEN_TM
    tps = seq // tm
    _, q_scale, k_scale, _, cos_t, sin_t = _retention_tables(seq)
    q_scale = jnp.tile(q_scale, (tm // RET_CHUNK, 1))
    k_scale = jnp.tile(k_scale, (tm // RET_CHUNK, 1))
    const = lambda shape: pl.BlockSpec(shape, lambda i: (0,) * len(shape))
    flat = lambda w_: pl.BlockSpec((tm, w_), lambda i: (i, 0))
    nat = pl.BlockSpec((None, tm, B_W), lambda i: (i // tps, i % tps, 0))
    by = lambda r: pl.BlockSpec((None, r, tm // r, B_W), lambda i: (i // tps, 0, i % tps, 0))
    sds = jax.ShapeDtypeStruct
    lay = [sds((bn, seq, B_W), BF16), sds((bn, 4, seq // 4, B_W), BF16),
           sds((bn, 16, seq // 16, B_W), BF16)]
    return pl.pallas_call(
        _even_in_kernel,
        grid=(tokens // tm,),
        in_specs=[
            flat(d), const((1, d)),
            pl.BlockSpec(w.shape, lambda i: (0, 0), pipeline_mode=pl.Buffered(1)),
            pl.BlockSpec((tm, LANES), lambda i: (i % tps, 0)),
            pl.BlockSpec((tm, LANES), lambda i: (i % tps, 0)),
            const((tm, A_QK)), const((tm, A_QK)),
        ],
        out_specs=[flat(A_QK)] * 4 + [flat(A_V), flat(A_V)] + [nat, by(4), by(16)] * 3,
        out_shape=[sds((tokens, A_QK), BF16)] * 4 + [sds((tokens, A_V), BF16), sds((tokens, A_V), F32)]
        + lay * 3,
        scratch_shapes=[pltpu.VMEM((B_W // LANES, tm, LANES), F32)],
        compiler_params=_params("arbitrary"),
        name="even_in",
    )(x, g, w, cos_t, sin_t, q_scale, k_scale)


def _retention_kernel(q_ref, qin_ref, k_ref, kout_ref, v_ref, g_ref, dec_ref, cd_ref, rn_ref, o_ref,
                      st_ref):
    @pl.when(pl.program_id(1) == 0)
    def _():
        st_ref[...] = jnp.zeros_like(st_ref)

    lane = lax.broadcasted_iota(jnp.int32, (RET_CHUNK, LANES), 1)
    head0 = lane < RET_DK
    for p in range(RET_HEADS // 2):
        sl = slice(p * LANES, (p + 1) * LANES)
        q, q_in, k, k_out = q_ref[:, sl], qin_ref[:, sl], k_ref[:, sl], kout_ref[:, sl]
        for a in range(2):
            h = 2 * p + a
            hs = slice(h * RET_DV, (h + 1) * RET_DV)
            mask = head0 if a == 0 else jnp.logical_not(head0)
            qm = jnp.where(mask, q, 0)
            qim = jnp.where(mask, q_in, 0)
            v = v_ref[:, hs]
            s = lax.dot_general(qm, k, NT_DIMS, preferred_element_type=F32) * dec_ref[h]
            intra = jnp.dot(s.astype(BF16), v, preferred_element_type=F32)
            st = st_ref[h]
            inter = jnp.dot(qim, st.astype(BF16), preferred_element_type=F32)
            st_ref[h] = st * cd_ref[h] + lax.dot_general(k_out, v, TN_DIMS,
                                                         preferred_element_type=F32)
            y = intra + inter
            yc = y - jnp.mean(y, axis=-1, keepdims=True)
            yn = yc * lax.rsqrt(jnp.mean(yc * yc, axis=-1, keepdims=True) + EPS)
            o_ref[:, hs] = (yn * rn_ref[:, hs] * _silu(g_ref[:, hs])).astype(o_ref.dtype)


def _retention(qa, qin, ka, kout, va, ga, ret_norm, bn, seq):
    c = RET_CHUNK
    decay, _, _, chunk_decay, _, _ = _retention_tables(seq)
    nchunk = seq // c
    blk = lambda w: pl.BlockSpec((c, w), lambda b, n: (b * nchunk + n, 0))
    return pl.pallas_call(
        _retention_kernel,
        grid=(bn, nchunk),
        in_specs=[
            blk(A_QK), blk(A_QK), blk(A_QK), blk(A_QK), blk(A_V), blk(A_V),
            pl.BlockSpec((RET_HEADS, c, c), lambda b, n: (0, 0, 0)),
            pl.BlockSpec((RET_HEADS, 1, RET_DV), lambda b, n: (0, 0, 0)),
            pl.BlockSpec((1, A_V), lambda b, n: (0, 0)),
        ],
        out_specs=blk(A_V),
        out_shape=jax.ShapeDtypeStruct((bn * seq, A_V), BF16),
        scratch_shapes=[pltpu.VMEM((RET_HEADS, 2 * RET_DK, RET_DV), F32)],
        compiler_params=_params("arbitrary", "arbitrary"),
        name="retention",
    )(qa, qin, ka, kout, va, ga, decay, chunk_decay, ret_norm)


def _rel_bucket(dist):
    max_exact = REL_BUCKETS // 2
    d = dist.astype(F32)
    large = max_exact + (jnp.log(jnp.maximum(d, 1.0) / max_exact) / math.log(REL_MAX_DIST / max_exact)
                         * (REL_BUCKETS - max_exact)).astype(jnp.int32)
    large = jnp.minimum(large, REL_BUCKETS - 1)
    return jnp.where(dist < max_exact, dist, large)


DIL_UNROLL = 8


def _dilated_buckets():
    lb = DIL_BLOCK
    a_idx = jnp.arange(lb)[:, None]
    c_idx = jnp.arange(2 * lb)[None, :]
    dist = lb + a_idx - c_idx
    tables = []
    for window, dilation in DIL_BRANCHES:
        valid = (dist >= 0) & (dist <= window // dilation)
        tables.append(jnp.where(valid, _rel_bucket(jnp.maximum(dist, 0) * dilation), -1))
    return jnp.stack(tables).astype(jnp.int32)


def _dilated_kernel(rb_ref, bucket_ref, qn_ref, kn_ref, vn_ref, q4_ref, k4_ref, v4_ref, q16_ref,
                    k16_ref, v16_ref, y_ref, bias_ref, acc_ref, m_ref, l_ref):
    lb = DIL_BLOCK
    seq = qn_ref.shape[0]
    pair = pl.program_id(1)

    @pl.when(pl.program_id(0) == 0)
    def _():
        for br in range(len(DIL_BRANCHES)):
            bucket = bucket_ref[br]
            for a in range(2):
                h = 2 * pair + a
                t = jnp.full((lb, 2 * lb), MASK_VALUE, F32)
                for j in range(REL_BUCKETS):
                    t = jnp.where(bucket == j, rb_ref[j, h], t)
                bias_ref[br * DIL_HEADS + h] = t

    lane = lax.broadcasted_iota(jnp.int32, (lb, LANES), 1)
    head0 = lane < DIL_DH
    prev_cols = lax.broadcasted_iota(jnp.int32, (lb, 2 * lb), 1) < lb

    def attend(br, first, q, kp, kc, vp, vc):
        k = jnp.concatenate([kp, kc], axis=0)
        v = jnp.concatenate([vp, vc], axis=0)
        no_prev = jnp.where(jnp.logical_and(prev_cols, first), MASK_VALUE, 0.0)
        parts = []
        for a in range(2):
            mask = head0 if a == 0 else jnp.logical_not(head0)
            s = lax.dot_general(jnp.where(mask, q, 0), k, NT_DIMS, preferred_element_type=F32)
            s = s + bias_ref[br * DIL_HEADS + 2 * pair + a] + no_prev
            m = jnp.max(s, axis=-1, keepdims=True)
            e = jnp.exp(s - m)
            parts.append((jnp.dot(e.astype(BF16), v, preferred_element_type=F32), m,
                          jnp.sum(e, axis=-1, keepdims=True)))
        return tuple(jnp.where(head0, x0, x1) for x0, x1 in zip(*parts))

    def natural_block(n, carry):
        r0 = pl.multiple_of(n * lb, lb)
        rp = pl.multiple_of(jnp.maximum(n - 1, 0) * lb, lb)
        cur, prev = pl.ds(r0, lb), pl.ds(rp, lb)
        acc, m, l = attend(0, n == 0, qn_ref[cur, :], kn_ref[prev, :], kn_ref[cur, :],
                           vn_ref[prev, :], vn_ref[cur, :])
        acc_ref[cur, :] = acc
        m_ref[cur, :] = m
        l_ref[cur, :] = l
        return carry

    lax.fori_loop(0, seq // lb, natural_block, 0, unroll=DIL_UNROLL)

    def dilated_blocks(br, r, q_ref, k_ref, v_ref):
        nb = seq // (r * lb)

        def body(idx, carry):
            rho, n = idx // nb, idx % nb
            r0 = pl.multiple_of(n * lb, lb)
            rp = pl.multiple_of(jnp.maximum(n - 1, 0) * lb, lb)
            cur, prev = pl.ds(r0, lb), pl.ds(rp, lb)
            acc, m, l = attend(br, n == 0, q_ref[rho, cur, :], k_ref[rho, prev, :], k_ref[rho, cur, :],
                               v_ref[rho, prev, :], v_ref[rho, cur, :])
            rows = pl.ds(n * (lb * r) + rho, lb, stride=r)
            m_old = m_ref[rows, :]
            m_new = jnp.maximum(m_old, m)
            c_old = jnp.exp(m_old - m_new)
            c_cur = jnp.exp(m - m_new)
            acc_ref[rows, :] = acc_ref[rows, :] * c_old + acc * c_cur
            l_ref[rows, :] = l_ref[rows, :] * c_old + l * c_cur
            m_ref[rows, :] = m_new
            return carry

        lax.fori_loop(0, r * nb, body, 0, unroll=DIL_UNROLL)

    dilated_blocks(1, DIL_BRANCHES[1][1], q4_ref, k4_ref, v4_ref)
    dilated_blocks(2, DIL_BRANCHES[2][1], q16_ref, k16_ref, v16_ref)

    def finish(n, carry):
        rows = pl.ds(pl.multiple_of(n * lb, lb), lb)
        y_ref[rows, :] = (acc_ref[rows, :] / l_ref[rows, :]).astype(y_ref.dtype)
        return carry

    lax.fori_loop(0, seq // lb, finish, 0)


def _dilated(layouts, rel_bias, bn, seq):
    lb = DIL_BLOCK
    assert all(w // r <= lb and seq % (r * lb) == 0 for w, r in DIL_BRANCHES)
    assert tuple(r for _, r in DIL_BRANCHES) == (1, 4, 16)
    nat = pl.BlockSpec((None, seq, LANES), lambda b, p: (b, 0, p))
    by = lambda r: pl.BlockSpec((None, r, seq // r, LANES), lambda b, p: (b, 0, 0, p))
    nbr = len(DIL_BRANCHES)
    return pl.pallas_call(
        _dilated_kernel,
        grid=(bn, DIL_HEADS // 2),
        in_specs=[pl.BlockSpec(memory_space=pltpu.SMEM),
                  pl.BlockSpec((nbr, lb, 2 * lb), lambda b, p: (0, 0, 0))]
        + [nat] * 3 + [by(4)] * 3 + [by(16)] * 3,
        out_specs=nat,
        out_shape=jax.ShapeDtypeStruct((bn, seq, B_W), BF16),
        scratch_shapes=[pltpu.VMEM((nbr * DIL_HEADS, lb, 2 * lb), F32),
                        pltpu.VMEM((seq, LANES), F32), pltpu.VMEM((seq, LANES), F32),
                        pltpu.VMEM((seq, LANES), F32)],
        compiler_params=_params("arbitrary", "arbitrary"),
        name="dilated",
    )(rel_bias.astype(F32), _dilated_buckets(), *layouts)


def _even_out_kernel(x_ref, ya_ref, yb_ref, wa_ref, wb_ref, h_ref):
    acc = jnp.dot(ya_ref[...], wa_ref[...], preferred_element_type=F32)
    acc = acc + jnp.dot(yb_ref[...], wb_ref[...], preferred_element_type=F32)
    h_ref[...] = x_ref[...] + acc


def _even_out(x, ya, yb, w_out, tm):
    m = x.shape[0]
    row = lambda w: pl.BlockSpec((tm, w), lambda i: (i, 0))
    return pl.pallas_call(
        _even_out_kernel,
        grid=(m // tm,),
        in_specs=[row(D_MODEL), row(A_V), row(B_W),
                  pl.BlockSpec((A_V, D_MODEL), lambda i: (0, 0)),
                  pl.BlockSpec((B_W, D_MODEL), lambda i: (A_V // B_W, 0))],
        out_specs=row(D_MODEL),
        out_shape=jax.ShapeDtypeStruct((m, D_MODEL), F32),
        compiler_params=_params("arbitrary"),
        name="even_out",
    )(x, ya, yb, w_out, w_out)


def _ffn_kernel(tiles_per_seq, final, h_ref, g_ref, wa_ref, wb_ref, cwa_ref, cwb_ref, cba_ref,
                cbb_ref, wd_ref, fg_ref, o_ref, hn_ref, acc_ref, ua_ref, ub_ref, halo_a_ref,
                halo_b_ref):
    i = pl.program_id(0)
    j = pl.program_id(1)
    tm = h_ref.shape[0]
    pad = SUBLANES

    @pl.when(j == 0)
    def _():
        x = h_ref[...]
        ms = jnp.mean(x * x, axis=-1, keepdims=True)
        hn_ref[...] = (x * lax.rsqrt(ms + EPS) * g_ref[...]).astype(BF16)
        acc_ref[...] = jnp.zeros_like(acc_ref)

    @pl.when((i % tiles_per_seq) == 0)
    def _():
        halo_a_ref[j] = jnp.zeros(halo_a_ref.shape[1:], F32)
        halo_b_ref[j] = jnp.zeros(halo_b_ref.shape[1:], F32)

    def conv_half(w_ref, cw_ref, cb_ref, u_ref, halo_ref):
        u = jnp.dot(hn_ref[...], w_ref[...], preferred_element_type=F32)
        u_ref[0:pad, :] = halo_ref[j]
        u_ref[pad:, :] = u
        halo_ref[j] = u[tm - pad:, :]
        cw = cw_ref[...]
        return (cw[0:1, :] * u_ref[pad - 2:pad - 2 + tm, :] + cw[1:2, :] * u_ref[pad - 1:pad - 1 + tm, :]
                + cw[2:3, :] * u + cb_ref[...])

    ca = conv_half(wa_ref, cwa_ref, cba_ref, ua_ref, halo_a_ref)
    cb = conv_half(wb_ref, cwb_ref, cbb_ref, ub_ref, halo_b_ref)
    act = (_silu(ca) * cb).astype(BF16)
    acc_ref[...] += jnp.dot(act, wd_ref[...], preferred_element_type=F32)

    @pl.when(j == pl.num_programs(1) - 1)
    def _():
        y = h_ref[...] + acc_ref[...]
        if final:
            ms = jnp.mean(y * y, axis=-1, keepdims=True)
            y = y * lax.rsqrt(ms + EPS) * fg_ref[...]
        o_ref[...] = y


def _ffn(h, g, w_up, conv_w, conv_b, w_down, final_g, seq, tm, tn, final):
    m = h.shape[0]
    nj = D_FF // tn
    col = lambda rows, off: pl.BlockSpec((rows, tn), lambda i, j: (0, j + off))
    return pl.pallas_call(
        functools.partial(_ffn_kernel, seq // tm, final),
        grid=(m // tm, nj),
        in_specs=[
            pl.BlockSpec((tm, D_MODEL), lambda i, j: (i, 0)),
            pl.BlockSpec((1, D_MODEL), lambda i, j: (0, 0)),
            col(D_MODEL, 0), col(D_MODEL, nj),
            col(CONV_WIDTH, 0), col(CONV_WIDTH, nj),
            col(1, 0), col(1, nj),
            pl.BlockSpec((tn, D_MODEL), lambda i, j: (j, 0)),
            pl.BlockSpec((1, D_MODEL), lambda i, j: (0, 0)),
        ],
        out_specs=pl.BlockSpec((tm, D_MODEL), lambda i, j: (i, 0)),
        out_shape=jax.ShapeDtypeStruct((m, D_MODEL), F32),
        scratch_shapes=[
            pltpu.VMEM((tm, D_MODEL), BF16),
            pltpu.VMEM((tm, D_MODEL), F32),
            pltpu.VMEM((tm + SUBLANES, tn), F32),
            pltpu.VMEM((tm + SUBLANES, tn), F32),
            pltpu.VMEM((nj, SUBLANES, tn), F32),
            pltpu.VMEM((nj, SUBLANES, tn), F32),
        ],
        compiler_params=_params("arbitrary", "arbitrary"),
        name="conv_ffn",
    )(h, g, w_up, w_up, conv_w, conv_w, conv_b, conv_b, w_down, final_g)


HGRN_ROWS = 256
HGRN_GROUP = 128


def _hgrn_kernel(layer, q_ref, f_ref, i_ref, g_ref, lbp_ref, on_ref, o_ref, st_ref):
    @pl.when(pl.program_id(2) == 0)
    def _():
        st_ref[...] = jnp.zeros_like(st_ref)

    c = HGRN_CHUNK
    grp = HGRN_GROUP
    lbp = lbp_ref[...]
    ex = jnp.exp(lbp - jnp.max(lbp, axis=0, keepdims=True))
    sm = ex / jnp.sum(ex, axis=0, keepdims=True)
    lower = jnp.sum(sm[0:layer + 1, :], axis=0, keepdims=True) - sm[0:1, :]

    row = lax.broadcasted_iota(jnp.int32, (grp, grp), 0)
    colm = lax.broadcasted_iota(jnp.int32, (grp, grp), 1)
    causal = jnp.logical_and(row // c == colm // c, row >= colm)
    tri = jnp.where(causal, 1.0, 0.0).astype(BF16)

    st = st_ref[...]
    for r0 in range(0, q_ref.shape[0], grp):
        rows = slice(r0, r0 + grp)
        q = _silu(q_ref[rows, :])
        fg = lower + (1.0 - lower) * jax.nn.sigmoid(f_ref[rows, :])
        k = 1.0 - fg
        log_f = jnp.log(fg)
        hi = log_f.astype(BF16)
        rem = log_f - hi.astype(F32)
        mid = rem.astype(BF16)
        lo = (rem - mid.astype(F32)).astype(BF16)
        b = (jnp.dot(tri, hi, preferred_element_type=F32) + jnp.dot(tri, mid, preferred_element_type=F32)
             + jnp.dot(tri, lo, preferred_element_type=F32))
        v = i_ref[rows, :].astype(BF16)
        q_t = (q * jnp.exp(b)).astype(BF16)
        k_t = (k * jnp.exp(-b)).astype(BF16)
        s = lax.dot_general(q_t, k_t, NT_DIMS, preferred_element_type=F32)
        s = jnp.where(causal, s, 0.0).astype(BF16)
        intra = jnp.dot(s, v, preferred_element_type=F32)
        inters = []
        for n in range(grp // c):
            cs = slice(n * c, (n + 1) * c)
            b_last = b[(n + 1) * c - 1:(n + 1) * c, :]
            k_end = (k[cs, :] * jnp.exp(b_last - b[cs, :])).astype(BF16)
            inters.append(lax.dot_general(q_t[cs, :], st.astype(BF16), NT_DIMS,
                                          preferred_element_type=F32))
            st = st * jnp.exp(b_last) + lax.dot_general(v[cs, :], k_end, TN_DIMS,
                                                        preferred_element_type=F32)
        y = intra + jnp.concatenate(inters, axis=0)
        yn = y * lax.rsqrt(jnp.mean(y * y, axis=-1, keepdims=True) + EPS)
        o_ref[rows, :] = (yn * on_ref[...] * _silu(g_ref[rows, :])).astype(o_ref.dtype)
    st_ref[...] = st


def _hgrn(proj, hgrn_lb, out_norm, layer, seq):
    bn = proj.shape[0]
    rows = HGRN_ROWS
    nh = HGRN_HEADS
    blk = lambda off: pl.BlockSpec((None, rows, HGRN_DK), lambda b, h, n: (b, n, h + off))
    return pl.pallas_call(
        functools.partial(_hgrn_kernel, layer),
        grid=(bn, nh, seq // rows),
        in_specs=[
            blk(0), blk(nh), blk(2 * nh), blk(3 * nh),
            pl.BlockSpec((hgrn_lb.shape[0], HGRN_DK), lambda b, h, n: (0, h)),
            pl.BlockSpec((1, HGRN_DV), lambda b, h, n: (0, h)),
        ],
        out_specs=pl.BlockSpec((None, rows, HGRN_DV), lambda b, h, n: (b, n, h)),
        out_shape=jax.ShapeDtypeStruct((bn, seq, C_V), F32),
        scratch_shapes=[pltpu.VMEM((HGRN_DV, HGRN_DK), F32)],
        compiler_params=_params("arbitrary", "arbitrary", "arbitrary"),
        name="hgrn2",
    )(proj, proj, proj, proj, hgrn_lb, out_norm)


def _matmul_residual_kernel(h_ref, y_ref, w_ref, o_ref):
    o_ref[...] = h_ref[...] + jnp.dot(y_ref[...].astype(BF16), w_ref[...],
                                      preferred_element_type=F32)


def _matmul_residual(h, y, w, tm):
    m, n = h.shape
    k = y.shape[1]
    return pl.pallas_call(
        _matmul_residual_kernel,
        grid=(m // tm,),
        in_specs=[
            pl.BlockSpec((tm, n), lambda i: (i, 0)),
            pl.BlockSpec((tm, k), lambda i: (i, 0)),
            pl.BlockSpec((k, n), lambda i: (0, 0)),
        ],
        out_specs=pl.BlockSpec((tm, n), lambda i: (i, 0)),
        out_shape=jax.ShapeDtypeStruct((m, n), F32),
        compiler_params=_params("arbitrary"),
        name="matmul_residual",
    )(h, y, w)


def kernel(x, even_w_in, even_w_out, ret_norm, rel_bias, odd_w_in, odd_w_out, hgrn_lb, hgrn_norm,
           mix_norm, ffn_norm, ffn_w_up, ffn_conv_w, ffn_conv_b, ffn_w_down, final_norm):
    bn, seq, d = x.shape
    tokens = bn * seq
    row = lambda a: a.reshape(1, -1).astype(F32)
    h = x.reshape(tokens, d)
    final_g = row(final_norm)

    outs = _even_in(h, row(mix_norm[0]), even_w_in[0].astype(BF16), seq)
    qa, qin, ka, kout, va, ga = outs[:6]
    qn, q4, q16, kn, k4, k16, vn, v4, v16 = outs[6:]
    ya = _retention(qa, qin, ka, kout, va, ga, row(ret_norm[0]), bn, seq)
    yb = _dilated((qn, kn, vn, q4, k4, v4, q16, k16, v16), rel_bias, bn, seq)
    h = _even_out(h, ya, yb.reshape(tokens, B_W), even_w_out[0].astype(BF16), tm=1024)
    h = _ffn(h, row(ffn_norm[0]), ffn_w_up[0].astype(BF16), ffn_conv_w[0].astype(F32),
             row(ffn_conv_b[0]), ffn_w_down[0].astype(BF16), final_g, seq, tm=1024, tn=256,
             final=False)

    proj = _norm_matmul(h, row(mix_norm[1]), odd_w_in[0].astype(BF16), tm=1024, tn=1024)
    y = _hgrn(proj.reshape(bn, seq, ODD_IN), hgrn_lb.astype(F32), row(hgrn_norm[0]), 1, seq)
    h = _matmul_residual(h, y.reshape(tokens, C_V), odd_w_out[0].astype(BF16), tm=1024)
    h = _ffn(h, row(ffn_norm[1]), ffn_w_up[1].astype(BF16), ffn_conv_w[1].astype(F32),
             row(ffn_conv_b[1]), ffn_w_down[1].astype(BF16), final_g, seq, tm=1024, tn=256,
             final=True)
    return h.reshape(bn, seq, d)
```

```python
import functools
import math

import numpy as np
import jax
import jax.numpy as jnp
from jax import lax
from jax.experimental import pallas as pl
from jax.experimental.pallas import tpu as pltpu

F32 = jnp.float32
BF16 = jnp.bfloat16

D_MODEL = 1024
RET_HEADS = 8
RET_DK = 64
RET_DV = 128
RET_CHUNK = 128
ROPE_BASE = 10000.0
DIL_HEADS = 8
DIL_DH = 64
DIL_BRANCHES = ((128, 1), (512, 4), (2048, 16))
DIL_BLOCK = 128
HGRN_HEADS = 8
HGRN_DK = 128
HGRN_DV = 128
HGRN_CHUNK = 32
REL_BUCKETS = 32
REL_MAX_DIST = 2048
D_FF = 2816
CONV_WIDTH = 3
EPS = 1e-6

A_QK = RET_HEADS * RET_DK
A_V = RET_HEADS * RET_DV
B_W = DIL_HEADS * DIL_DH
EVEN_IN = 2 * A_QK + 2 * A_V + 3 * B_W
C_K = HGRN_HEADS * HGRN_DK
C_V = HGRN_HEADS * HGRN_DV
ODD_IN = 2 * C_K + 2 * C_V

LANES = 128
SUBLANES = 8
VMEM_LIMIT = 48 * 1024 * 1024
MASK_VALUE = -1e30

NT_DIMS = (((1,), (1,)), ((), ()))
TN_DIMS = (((0,), (0,)), ((), ()))


def _params(*semantics):
    return pltpu.CompilerParams(dimension_semantics=semantics, vmem_limit_bytes=VMEM_LIMIT)


def _silu(x):
    return x * jax.nn.sigmoid(x)


def _norm_matmul_kernel(x_ref, g_ref, w_ref, o_ref, hn_ref):
    @pl.when(pl.program_id(1) == 0)
    def _():
        x = x_ref[...]
        ms = jnp.mean(x * x, axis=-1, keepdims=True)
        hn_ref[...] = (x * lax.rsqrt(ms + EPS) * g_ref[...]).astype(BF16)

    o_ref[...] = jnp.dot(hn_ref[...], w_ref[...], preferred_element_type=F32).astype(o_ref.dtype)


def _norm_matmul(x, g, w, tm, tn):
    m, k = x.shape
    n = w.shape[1]
    return pl.pallas_call(
        _norm_matmul_kernel,
        grid=(m // tm, n // tn),
        in_specs=[
            pl.BlockSpec((tm, k), lambda i, j: (i, 0)),
            pl.BlockSpec((1, k), lambda i, j: (0, 0)),
            pl.BlockSpec((k, tn), lambda i, j: (0, j)),
        ],
        out_specs=pl.BlockSpec((tm, tn), lambda i, j: (i, j)),
        out_shape=jax.ShapeDtypeStruct((m, n), F32),
        scratch_shapes=[pltpu.VMEM((tm, k), BF16)],
        compiler_params=_params("arbitrary", "arbitrary"),
        name="norm_matmul",
    )(x, g, w)


def _retention_tables(seq):
    h = jnp.arange(RET_HEADS, dtype=F32)
    log_g = jnp.log1p(-jnp.exp2(-5.0 - h))
    pos = jnp.arange(RET_CHUNK, dtype=F32)
    diff = pos[:, None] - pos[None, :]
    decay = jnp.where(diff >= 0, jnp.exp(jnp.maximum(diff, 0.0)[None] * log_g[:, None, None]), 0.0)
    q_scale = jnp.repeat(jnp.exp((pos + 1.0)[:, None] * log_g[None, :]), RET_DK, axis=1)
    k_scale = jnp.repeat(jnp.exp((RET_CHUNK - 1.0 - pos)[:, None] * log_g[None, :]), RET_DK, axis=1)
    chunk_decay = jnp.broadcast_to(jnp.exp(RET_CHUNK * log_g)[:, None, None], (RET_HEADS, 1, RET_DV))
    inv = ROPE_BASE ** (-jnp.arange(0, RET_DK, 2, dtype=F32) / RET_DK)
    ang = jnp.arange(seq, dtype=F32)[:, None] * inv[None, :]
    cos, sin = jnp.cos(ang), jnp.sin(ang)
    reps = LANES // RET_DK
    cos_t = jnp.tile(jnp.concatenate([cos, cos], axis=1), (1, reps))
    sin_t = jnp.tile(jnp.concatenate([-sin, sin], axis=1), (1, reps))
    return decay, q_scale, k_scale, chunk_decay, cos_t, sin_t


EVEN_TM = 512


def _even_in_kernel(x_ref, g_ref, w_ref, cos_ref, sin_ref, qs_ref, ks_ref,
                    qa_ref, qin_ref, ka_ref, kout_ref, va_ref, ga_ref,
                    qn_ref, q4_ref, q16_ref, kn_ref, k4_ref, k16_ref, vn_ref, v4_ref, v16_ref,
                    slab_ref):
    tm = x_ref.shape[0]
    x = x_ref[...]
    ms = jnp.mean(x * x, axis=-1, keepdims=True)
    hn = (x * lax.rsqrt(ms + EPS) * g_ref[...]).astype(BF16)

    def proj(c):
        return jnp.dot(hn, w_ref[:, c * B_W:(c + 1) * B_W], preferred_element_type=F32)

    cos = cos_ref[...]
    sin = sin_ref[...]
    lane = lax.broadcasted_iota(jnp.int32, (tm, LANES), 1)
    half = RET_DK // 2
    first_half = (lane % RET_DK) < half

    def rot(t):
        partner = jnp.where(first_half, pltpu.roll(t, LANES - half, 1), pltpu.roll(t, half, 1))
        return t * cos + partner * sin

    groups = [slice(s * LANES, (s + 1) * LANES) for s in range(B_W // LANES)]
    r = proj(0)
    for sl in groups:
        q = rot(r[:, sl])
        qa_ref[:, sl] = q.astype(BF16)
        qin_ref[:, sl] = (q * qs_ref[:, sl]).astype(BF16)
    r = proj(1)
    for sl in groups:
        k = rot(r[:, sl]) * (RET_DK ** -0.5)
        ka_ref[:, sl] = k.astype(BF16)
        kout_ref[:, sl] = (k * ks_ref[:, sl]).astype(BF16)
    va_ref[:, 0:B_W] = proj(2).astype(BF16)
    va_ref[:, B_W:2 * B_W] = proj(3).astype(BF16)
    ga_ref[:, 0:B_W] = proj(4)
    ga_ref[:, B_W:2 * B_W] = proj(5)
    for c, nat_ref, r4_ref, r16_ref, scale in ((6, qn_ref, q4_ref, q16_ref, DIL_DH ** -0.5),
                                               (7, kn_ref, k4_ref, k16_ref, None),
                                               (8, vn_ref, v4_ref, v16_ref, None)):
        r = proj(c)
        if scale is not None:
            r = r * scale
        nat_ref[...] = r.astype(BF16)
        for s, sl in enumerate(groups):
            slab_ref[s] = r[:, sl]
        for s, sl in enumerate(groups):
            for rho in range(4):
                r4_ref[rho, :, sl] = slab_ref[s, pl.ds(rho, tm // 4, stride=4), :].astype(BF16)
            for rho in range(16):
                r16_ref[rho, :, sl] = slab_ref[s, pl.ds(rho, tm // 16, stride=16), :].astype(BF16)


def _even_in(x, g, w, seq):
    tokens, d = x.shape
    bn = tokens // seq
    tm = EVEN_TM
    tps = seq // tm
    _, q_scale, k_scale, _, cos_t, sin_t = _retention_tables(seq)
    q_scale = jnp.tile(q_scale, (tm // RET_CHUNK, 1))
    k_scale = jnp.tile(k_scale, (tm // RET_CHUNK, 1))
    const = lambda shape: pl.BlockSpec(shape, lambda i: (0,) * len(shape))
    flat = lambda w_: pl.BlockSpec((tm, w_), lambda i: (i, 0))
    nat = pl.BlockSpec((None, tm, B_W), lambda i: (i // tps, i % tps, 0))
    by = lambda r: pl.BlockSpec((None, r, tm // r, B_W), lambda i: (i // tps, 0, i % tps, 0))
    sds = jax.ShapeDtypeStruct
    lay = [sds((bn, seq, B_W), BF16), sds((bn, 4, seq // 4, B_W), BF16),
           sds((bn, 16, seq // 16, B_W), BF16)]
    return pl.pallas_call(
        _even_in_kernel,
        grid=(tokens // tm,),
        in_specs=[
            flat(d), const((1, d)),
            pl.BlockSpec(w.shape, lambda i: (0, 0), pipeline_mode=pl.Buffered(1)),
            pl.BlockSpec((tm, LANES), lambda i: (i % tps, 0)),
            pl.BlockSpec((tm, LANES), lambda i: (i % tps, 0)),
            const((tm, A_QK)), const((tm, A_QK)),
        ],
        out_specs=[flat(A_QK)] * 4 + [flat(A_V), flat(A_V)] + [nat, by(4), by(16)] * 3,
        out_shape=[sds((tokens, A_QK), BF16)] * 4 + [sds((tokens, A_V), BF16), sds((tokens, A_V), F32)]
        + lay * 3,
        scratch_shapes=[pltpu.VMEM((B_W // LANES, tm, LANES), F32)],
        compiler_params=_params("arbitrary"),
        name="even_in",
    )(x, g, w, cos_t, sin_t, q_scale, k_scale)


def _retention_kernel(q_ref, qin_ref, k_ref, kout_ref, v_ref, g_ref, dec_ref, cd_ref, rn_ref, o_ref,
                      st_ref):
    @pl.when(pl.program_id(1) == 0)
    def _():
        st_ref[...] = jnp.zeros_like(st_ref)

    lane = lax.broadcasted_iota(jnp.int32, (RET_CHUNK, LANES), 1)
    head0 = lane < RET_DK
    for p in range(RET_HEADS // 2):
        sl = slice(p * LANES, (p + 1) * LANES)
        q, q_in, k, k_out = q_ref[:, sl], qin_ref[:, sl], k_ref[:, sl], kout_ref[:, sl]
        for a in range(2):
            h = 2 * p + a
            hs = slice(h * RET_DV, (h + 1) * RET_DV)
            mask = head0 if a == 0 else jnp.logical_not(head0)
            qm = jnp.where(mask, q, 0)
            qim = jnp.where(mask, q_in, 0)
            v = v_ref[:, hs]
            s = lax.dot_general(qm, k, NT_DIMS, preferred_element_type=F32) * dec_ref[h]
            intra = jnp.dot(s.astype(BF16), v, preferred_element_type=F32)
            st = st_ref[h]
            inter = jnp.dot(qim, st.astype(BF16), preferred_element_type=F32)
            st_ref[h] = st * cd_ref[h] + lax.dot_general(k_out, v, TN_DIMS,
                                                         preferred_element_type=F32)
            y = intra + inter
            yc = y - jnp.mean(y, axis=-1, keepdims=True)
            yn = yc * lax.rsqrt(jnp.mean(yc * yc, axis=-1, keepdims=True) + EPS)
            o_ref[:, hs] = (yn * rn_ref[:, hs] * _silu(g_ref[:, hs])).astype(o_ref.dtype)


def _retention(qa, qin, ka, kout, va, ga, ret_norm, bn, seq):
    c = RET_CHUNK
    decay, _, _, chunk_decay, _, _ = _retention_tables(seq)
    nchunk = seq // c
    blk = lambda w: pl.BlockSpec((c, w), lambda b, n: (b * nchunk + n, 0))
    return pl.pallas_call(
        _retention_kernel,
        grid=(bn, nchunk),
        in_specs=[
            blk(A_QK), blk(A_QK), blk(A_QK), blk(A_QK), blk(A_V), blk(A_V),
            pl.BlockSpec((RET_HEADS, c, c), lambda b, n: (0, 0, 0)),
            pl.BlockSpec((RET_HEADS, 1, RET_DV), lambda b, n: (0, 0, 0)),
            pl.BlockSpec((1, A_V), lambda b, n: (0, 0)),
        ],
        out_specs=blk(A_V),
        out_shape=jax.ShapeDtypeStruct((bn * seq, A_V), BF16),
        scratch_shapes=[pltpu.VMEM((RET_HEADS, 2 * RET_DK, RET_DV), F32)],
        compiler_params=_params("arbitrary", "arbitrary"),
        name="retention",
    )(qa, qin, ka, kout, va, ga, decay, chunk_decay, ret_norm)


def _rel_bucket(dist):
    max_exact = REL_BUCKETS // 2
    d = dist.astype(F32)
    large = max_exact + (jnp.log(jnp.maximum(d, 1.0) / max_exact) / math.log(REL_MAX_DIST / max_exact)
                         * (REL_BUCKETS - max_exact)).astype(jnp.int32)
    large = jnp.minimum(large, REL_BUCKETS - 1)
    return jnp.where(dist < max_exact, dist, large)


DIL_UNROLL = 8


def _dilated_buckets():
    lb = DIL_BLOCK
    a_idx = jnp.arange(lb)[:, None]
    c_idx = jnp.arange(2 * lb)[None, :]
    dist = lb + a_idx - c_idx
    tables = []
    for window, dilation in DIL_BRANCHES:
        valid = (dist >= 0) & (dist <= window // dilation)
        tables.append(jnp.where(valid, _rel_bucket(jnp.maximum(dist, 0) * dilation), -1))
    return jnp.stack(tables).astype(jnp.int32)


def _dilated_kernel(rb_ref, bucket_ref, qn_ref, kn_ref, vn_ref, q4_ref, k4_ref, v4_ref, q16_ref,
                    k16_ref, v16_ref, y_ref, bias_ref, acc_ref, m_ref, l_ref):
    lb = DIL_BLOCK
    seq = qn_ref.shape[0]
    pair = pl.program_id(1)

    @pl.when(pl.program_id(0) == 0)
    def _():
        for br in range(len(DIL_BRANCHES)):
            bucket = bucket_ref[br]
            for a in range(2):
                h = 2 * pair + a
                t = jnp.full((lb, 2 * lb), MASK_VALUE, F32)
                for j in range(REL_BUCKETS):
                    t = jnp.where(bucket == j, rb_ref[j, h], t)
                bias_ref[br * DIL_HEADS + h] = t

    lane = lax.broadcasted_iota(jnp.int32, (lb, LANES), 1)
    head0 = lane < DIL_DH
    prev_cols = lax.broadcasted_iota(jnp.int32, (lb, 2 * lb), 1) < lb

    def attend(br, first, q, kp, kc, vp, vc):
        k = jnp.concatenate([kp, kc], axis=0)
        v = jnp.concatenate([vp, vc], axis=0)
        no_prev = jnp.where(jnp.logical_and(prev_cols, first), MASK_VALUE, 0.0)
        parts = []
        for a in range(2):
            mask = head0 if a == 0 else jnp.logical_not(head0)
            s = lax.dot_general(jnp.where(mask, q, 0), k, NT_DIMS, preferred_element_type=F32)
            s = s + bias_ref[br * DIL_HEADS + 2 * pair + a] + no_prev
            m = jnp.max(s, axis=-1, keepdims=True)
            e = jnp.exp(s - m)
            parts.append((jnp.dot(e.astype(BF16), v, preferred_element_type=F32), m,
                          jnp.sum(e, axis=-1, keepdims=True)))
        return tuple(jnp.where(head0, x0, x1) for x0, x1 in zip(*parts))

    def natural_block(n, carry):
        r0 = pl.multiple_of(n * lb, lb)
        rp = pl.multiple_of(jnp.maximum(n - 1, 0) * lb, lb)
        cur, prev = pl.ds(r0, lb), pl.ds(rp, lb)
        acc, m, l = attend(0, n == 0, qn_ref[cur, :], kn_ref[prev, :], kn_ref[cur, :],
                           vn_ref[prev, :], vn_ref[cur, :])
        acc_ref[cur, :] = acc
        m_ref[cur, :] = m
        l_ref[cur, :] = l
        return carry

    lax.fori_loop(0, seq // lb, natural_block, 0, unroll=DIL_UNROLL)

    def dilated_blocks(br, r, q_ref, k_ref, v_ref):
        nb = seq // (r * lb)

        def body(idx, carry):
            rho, n = idx // nb, idx % nb
            r0 = pl.multiple_of(n * lb, lb)
            rp = pl.multiple_of(jnp.maximum(n - 1, 0) * lb, lb)
            cur, prev = pl.ds(r0, lb), pl.ds(rp, lb)
            acc, m, l = attend(br, n == 0, q_ref[rho, cur, :], k_ref[rho, prev, :], k_ref[rho, cur, :],
                               v_ref[rho, prev, :], v_ref[rho, cur, :])
            rows = pl.ds(n * (lb * r) + rho, lb, stride=r)
            m_old = m_ref[rows, :]
            m_new = jnp.maximum(m_old, m)
            c_old = jnp.exp(m_old - m_new)
            c_cur = jnp.exp(m - m_new)
            acc_ref[rows, :] = acc_ref[rows, :] * c_old + acc * c_cur
            l_ref[rows, :] = l_ref[rows, :] * c_old + l * c_cur
            m_ref[rows, :] = m_new
            return carry

        lax.fori_loop(0, r * nb, body, 0, unroll=DIL_UNROLL)

    dilated_blocks(1, DIL_BRANCHES[1][1], q4_ref, k4_ref, v4_ref)
    dilated_blocks(2, DIL_BRANCHES[2][1], q16_ref, k16_ref, v16_ref)

    def finish(n, carry):
        rows = pl.ds(pl.multiple_of(n * lb, lb), lb)
        y_ref[rows, :] = (acc_ref[rows, :] / l_ref[rows, :]).astype(y_ref.dtype)
        return carry

    lax.fori_loop(0, seq // lb, finish, 0)


def _dilated(layouts, rel_bias, bn, seq):
    lb = DIL_BLOCK
    assert all(w // r <= lb and seq % (r * lb) == 0 for w, r in DIL_BRANCHES)
    assert tuple(r for _, r in DIL_BRANCHES) == (1, 4, 16)
    nat = pl.BlockSpec((None, seq, LANES), lambda b, p: (b, 0, p))
    by = lambda r: pl.BlockSpec((None, r, seq // r, LANES), lambda b, p: (b, 0, 0, p))
    nbr = len(DIL_BRANCHES)
    return pl.pallas_call(
        _dilated_kernel,
        grid=(bn, DIL_HEADS // 2),
        in_specs=[pl.BlockSpec(memory_space=pltpu.SMEM),
                  pl.BlockSpec((nbr, lb, 2 * lb), lambda b, p: (0, 0, 0))]
        + [nat] * 3 + [by(4)] * 3 + [by(16)] * 3,
        out_specs=nat,
        out_shape=jax.ShapeDtypeStruct((bn, seq, B_W), BF16),
        scratch_shapes=[pltpu.VMEM((nbr * DIL_HEADS, lb, 2 * lb), F32),
                        pltpu.VMEM((seq, LANES), F32), pltpu.VMEM((seq, LANES), F32),
                        pltpu.VMEM((seq, LANES), F32)],
        compiler_params=_params("arbitrary", "arbitrary"),
        name="dilated",
    )(rel_bias.astype(F32), _dilated_buckets(), *layouts)


def _even_out_kernel(x_ref, ya_ref, yb_ref, wa_ref, wb_ref, h_ref):
    acc = jnp.dot(ya_ref[...], wa_ref[...], preferred_element_type=F32)
    acc = acc + jnp.dot(yb_ref[...], wb_ref[...], preferred_element_type=F32)
    h_ref[...] = x_ref[...] + acc


def _even_out(x, ya, yb, w_out, tm):
    m = x.shape[0]
    row = lambda w: pl.BlockSpec((tm, w), lambda i: (i, 0))
    return pl.pallas_call(
        _even_out_kernel,
        grid=(m // tm,),
        in_specs=[row(D_MODEL), row(A_V), row(B_W),
                  pl.BlockSpec((A_V, D_MODEL), lambda i: (0, 0)),
                  pl.BlockSpec((B_W, D_MODEL), lambda i: (A_V // B_W, 0))],
        out_specs=row(D_MODEL),
        out_shape=jax.ShapeDtypeStruct((m, D_MODEL), F32),
        compiler_params=_params("arbitrary"),
        name="even_out",
    )(x, ya, yb, w_out, w_out)


FFN_TM = 512
FFN_R = 8
FFN_SPLIT = 2


def _ffn_kernel(tiles_per_seq, final, h_ref, g_ref, wup_ref, cw_ref, cb_ref, wd_ref, fg_ref, o_ref,
                slab_ref, hn_ref, act_ref, acc_ref, halo_ref):
    i = pl.program_id(0)
    j = pl.program_id(1)
    tm = h_ref.shape[0]
    rows = tm // FFN_R
    nslab = D_MODEL // LANES
    nchunk = act_ref.shape[1] // LANES
    group = lambda g: slice(g * rows, (g + 1) * rows)

    @pl.when(j == 0)
    def _():
        for s in range(nslab):
            slab_ref[s] = h_ref[:, s * LANES:(s + 1) * LANES]
        for g in range(FFN_R):
            x = jnp.concatenate([slab_ref[s, pl.ds(g, rows, stride=FFN_R), :] for s in range(nslab)],
                                axis=1)
            ms = jnp.mean(x * x, axis=-1, keepdims=True)
            hn_ref[group(g), :] = (x * lax.rsqrt(ms + EPS) * g_ref[...]).astype(BF16)

    @pl.when((i % tiles_per_seq) == 0)
    def _():
        for c in range(nchunk):
            halo_ref[j * nchunk + c] = jnp.zeros(halo_ref.shape[1:], F32)

    first_row = lax.broadcasted_iota(jnp.int32, (rows, 2 * LANES), 0) == 0

    def shift_down(block, fill):
        return jnp.where(first_row, fill, pltpu.roll(block, 1, 0))

    for c in range(nchunk):
        cols = slice(c * 2 * LANES, (c + 1) * 2 * LANES)
        u = jnp.dot(hn_ref[...], wup_ref[:, cols], preferred_element_type=F32)
        ug = [u[group(g), :] for g in range(FFN_R)]
        tails = halo_ref[j * nchunk + c]
        prev2 = shift_down(ug[FFN_R - 2], tails[SUBLANES - 1:SUBLANES, :])
        prev1 = shift_down(ug[FFN_R - 1], tails[2 * SUBLANES - 1:2 * SUBLANES, :])
        halo_ref[j * nchunk + c] = jnp.concatenate(
            [ug[FFN_R - 2][rows - SUBLANES:, :], ug[FFN_R - 1][rows - SUBLANES:, :]], axis=0)
        back1 = [prev1] + ug[:FFN_R - 1]
        back2 = [prev2, prev1] + ug[:FFN_R - 2]
        cw = cw_ref[:, cols]
        cb = cb_ref[:, cols]
        for g in range(FFN_R):
            conv = cw[0:1, :] * back2[g] + cw[1:2, :] * back1[g] + cw[2:3, :] * ug[g] + cb
            act = _silu(conv[:, 0:LANES]) * conv[:, LANES:2 * LANES]
            act_ref[group(g), c * LANES:(c + 1) * LANES] = act.astype(BF16)

    part = jnp.dot(act_ref[...], wd_ref[...], preferred_element_type=F32)

    @pl.when(j == 0)
    def _():
        acc_ref[...] = part

    @pl.when(j > 0)
    def _():
        acc_ref[...] += part

    @pl.when(j == pl.num_programs(1) - 1)
    def _():
        for g in range(FFN_R):
            for s in range(nslab):
                slab_ref[s, pl.ds(g, rows, stride=FFN_R), :] = acc_ref[group(g), s * LANES:(s + 1) * LANES]
        y = h_ref[...] + jnp.concatenate([slab_ref[s] for s in range(nslab)], axis=1)
        if final:
            ms = jnp.mean(y * y, axis=-1, keepdims=True)
            y = y * lax.rsqrt(ms + EPS) * fg_ref[...]
        o_ref[...] = y


def _pair_gate_value(a):
    lead = a.shape[:-1]
    a = a.reshape(*lead, 2, D_FF // LANES, LANES)
    return jnp.swapaxes(a, -3, -2).reshape(*lead, 2 * D_FF)


def _ffn(h, g, w_up, conv_w, conv_b, w_down, final_g, seq, final):
    m = h.shape[0]
    tm = FFN_TM
    nj = FFN_SPLIT
    tn = D_FF // nj
    w_up = _pair_gate_value(w_up).astype(BF16)
    conv_w = _pair_gate_value(conv_w.astype(F32))
    conv_b = _pair_gate_value(conv_b.astype(F32).reshape(1, -1))
    col = lambda r: pl.BlockSpec((r, 2 * tn), lambda i, j: (0, j))
    return pl.pallas_call(
        functools.partial(_ffn_kernel, seq // tm, final),
        grid=(m // tm, nj),
        in_specs=[
            pl.BlockSpec((tm, D_MODEL), lambda i, j: (i, 0)),
            pl.BlockSpec((1, D_MODEL), lambda i, j: (0, 0)),
            col(D_MODEL), col(CONV_WIDTH), col(1),
            pl.BlockSpec((tn, D_MODEL), lambda i, j: (j, 0)),
            pl.BlockSpec((1, D_MODEL), lambda i, j: (0, 0)),
        ],
        out_specs=pl.BlockSpec((tm, D_MODEL), lambda i, j: (i, 0)),
        out_shape=jax.ShapeDtypeStruct((m, D_MODEL), F32),
        scratch_shapes=[
            pltpu.VMEM((D_MODEL // LANES, tm, LANES), F32),
            pltpu.VMEM((tm, D_MODEL), BF16),
            pltpu.VMEM((tm, tn), BF16),
            pltpu.VMEM((tm, D_MODEL), F32),
            pltpu.VMEM((D_FF // LANES, 2 * SUBLANES, 2 * LANES), F32),
        ],
        compiler_params=_params("arbitrary", "arbitrary"),
        name="conv_ffn",
    )(h, g, w_up, conv_w, conv_b, w_down.astype(BF16), final_g)


HGRN_GROUP = 128


def _hgrn_kernel(layer, q_ref, f_ref, i_ref, g_ref, lbp_ref, on_ref, o_ref, st_ref):
    @pl.when(pl.program_id(1) == 0)
    def _():
        st_ref[...] = jnp.zeros_like(st_ref)

    c = HGRN_CHUNK
    grp = HGRN_GROUP
    lbp = lbp_ref[...]
    ex = jnp.exp(lbp - jnp.max(lbp, axis=0, keepdims=True))
    sm = ex / jnp.sum(ex, axis=0, keepdims=True)
    lower_all = jnp.sum(sm[0:layer + 1, :], axis=0, keepdims=True) - sm[0:1, :]

    row = lax.broadcasted_iota(jnp.int32, (grp, grp), 0)
    colm = lax.broadcasted_iota(jnp.int32, (grp, grp), 1)
    causal = jnp.logical_and(row // c == colm // c, row >= colm)
    tri = jnp.where(causal, 1.0, 0.0).astype(BF16)

    heads = range(HGRN_HEADS)
    hk = [slice(h * HGRN_DK, (h + 1) * HGRN_DK) for h in heads]
    hv = [slice(h * HGRN_DV, (h + 1) * HGRN_DV) for h in heads]
    chunks = [slice(n * c, (n + 1) * c) for n in range(grp // c)]

    q = _silu(q_ref[...])
    fg = lower_all + (1.0 - lower_all) * jax.nn.sigmoid(f_ref[...])
    k = 1.0 - fg
    log_f = jnp.log(fg)
    hi = log_f.astype(BF16)
    rem = log_f - hi.astype(F32)
    mid = rem.astype(BF16)
    lo = (rem - mid.astype(F32)).astype(BF16)
    parts = jnp.dot(tri, jnp.concatenate([hi, mid, lo], axis=1), preferred_element_type=F32)
    b = parts[:, 0:C_K] + parts[:, C_K:2 * C_K] + parts[:, 2 * C_K:3 * C_K]
    v = i_ref[...].astype(BF16)
    q_t = (q * jnp.exp(b)).astype(BF16)
    k_t = (k * jnp.exp(-b)).astype(BF16)
    scores = [lax.dot_general(q_t[:, hk[h]], k_t[:, hk[h]], NT_DIMS, preferred_element_type=F32)
              for h in heads]
    scores = [jnp.where(causal, s, 0.0).astype(BF16) for s in scores]
    intra = [jnp.dot(scores[h], v[:, hv[h]], preferred_element_type=F32) for h in heads]
    b_last = [b[cs.stop - 1:cs.stop, :] for cs in chunks]
    k_end = [(k[cs, :] * jnp.exp(bl - b[cs, :])).astype(BF16) for cs, bl in zip(chunks, b_last)]
    decay = [jnp.exp(bl) for bl in b_last]
    st = [st_ref[h] for h in heads]
    inter = [[] for _ in heads]
    for n, cs in enumerate(chunks):
        for h in heads:
            inter[h].append(lax.dot_general(q_t[cs, hk[h]], st[h].astype(BF16), NT_DIMS,
                                            preferred_element_type=F32))
            st[h] = st[h] * decay[n][:, hk[h]] + lax.dot_general(
                v[cs, hv[h]], k_end[n][:, hk[h]], TN_DIMS, preferred_element_type=F32)
    gate = on_ref[...] * _silu(g_ref[...])
    for h in heads:
        st_ref[h] = st[h]
        y = intra[h] + jnp.concatenate(inter[h], axis=0)
        yn = y * lax.rsqrt(jnp.mean(y * y, axis=-1, keepdims=True) + EPS)
        o_ref[:, hv[h]] = (yn * gate[:, hv[h]]).astype(o_ref.dtype)


def _hgrn(proj, hgrn_lb, out_norm, layer, seq):
    bn = proj.shape[0]
    rows = HGRN_GROUP
    blk = lambda col: pl.BlockSpec((None, rows, C_K), lambda b, n: (b, n, col))
    return pl.pallas_call(
        functools.partial(_hgrn_kernel, layer),
        grid=(bn, seq // rows),
        in_specs=[
            blk(0), blk(1), blk(2), blk(3),
            pl.BlockSpec((hgrn_lb.shape[0], C_K), lambda b, n: (0, 0)),
            pl.BlockSpec((1, C_V), lambda b, n: (0, 0)),
        ],
        out_specs=pl.BlockSpec((None, rows, C_V), lambda b, n: (b, n, 0)),
        out_shape=jax.ShapeDtypeStruct((bn, seq, C_V), BF16),
        scratch_shapes=[pltpu.VMEM((HGRN_HEADS, HGRN_DV, HGRN_DK), F32)],
        compiler_params=_params("arbitrary", "arbitrary"),
        name="hgrn2",
    )(proj, proj, proj, proj, hgrn_lb, out_norm)


def _matmul_residual_kernel(h_ref, y_ref, w_ref, o_ref):
    o_ref[...] = h_ref[...] + jnp.dot(y_ref[...], w_ref[...], preferred_element_type=F32)


def _matmul_residual(h, y, w, tm):
    m, n = h.shape
    k = y.shape[1]
    return pl.pallas_call(
        _matmul_residual_kernel,
        grid=(m // tm,),
        in_specs=[
            pl.BlockSpec((tm, n), lambda i: (i, 0)),
            pl.BlockSpec((tm, k), lambda i: (i, 0)),
            pl.BlockSpec((k, n), lambda i: (0, 0)),
        ],
        out_specs=pl.BlockSpec((tm, n), lambda i: (i, 0)),
        out_shape=jax.ShapeDtypeStruct((m, n), F32),
        compiler_params=_params("arbitrary"),
        name="matmul_residual",
    )(h, y, w)


def kernel(x, even_w_in, even_w_out, ret_norm, rel_bias, odd_w_in, odd_w_out, hgrn_lb, hgrn_norm,
           mix_norm, ffn_norm, ffn_w_up, ffn_conv_w, ffn_conv_b, ffn_w_down, final_norm):
    bn, seq, d = x.shape
    tokens = bn * seq
    row = lambda a: a.reshape(1, -1).astype(F32)
    h = x.reshape(tokens, d)
    final_g = row(final_norm)

    outs = _even_in(h, row(mix_norm[0]), even_w_in[0].astype(BF16), seq)
    qa, qin, ka, kout, va, ga = outs[:6]
    qn, q4, q16, kn, k4, k16, vn, v4, v16 = outs[6:]
    ya = _retention(qa, qin, ka, kout, va, ga, row(ret_norm[0]), bn, seq)
    yb = _dilated((qn, kn, vn, q4, k4, v4, q16, k16, v16), rel_bias, bn, seq)
    h = _even_out(h, ya, yb.reshape(tokens, B_W), even_w_out[0].astype(BF16), tm=1024)
    h = _ffn(h, row(ffn_norm[0]), ffn_w_up[0], ffn_conv_w[0], ffn_conv_b[0], ffn_w_down[0], final_g,
             seq, final=False)

    proj = _norm_matmul(h, row(mix_norm[1]), odd_w_in[0].astype(BF16), tm=1024, tn=1024)
    y = _hgrn(proj.reshape(bn, seq, ODD_IN), hgrn_lb.astype(F32), row(hgrn_norm[0]), 1, seq)
    h = _matmul_residual(h, y.reshape(tokens, C_V), odd_w_out[0].astype(BF16), tm=1024)
    h = _ffn(h, row(ffn_norm[1]), ffn_w_up[1], ffn_conv_w[1], ffn_conv_b[1], ffn_w_down[1], final_g,
             seq, final=True)
    return h.reshape(bn, seq, d)
```

```python
import functools
import math

import numpy as np
import jax
import jax.numpy as jnp
from jax import lax
from jax.experimental import pallas as pl
from jax.experimental.pallas import tpu as pltpu

F32 = jnp.float32
BF16 = jnp.bfloat16

D_MODEL = 1024
RET_HEADS = 8
RET_DK = 64
RET_DV = 128
RET_CHUNK = 128
ROPE_BASE = 10000.0
DIL_HEADS = 8
DIL_DH = 64
DIL_BRANCHES = ((128, 1), (512, 4), (2048, 16))
DIL_BLOCK = 128
HGRN_HEADS = 8
HGRN_DK = 128
HGRN_DV = 128
HGRN_CHUNK = 32
REL_BUCKETS = 32
REL_MAX_DIST = 2048
D_FF = 2816
CONV_WIDTH = 3
EPS = 1e-6

A_QK = RET_HEADS * RET_DK
A_V = RET_HEADS * RET_DV
B_W = DIL_HEADS * DIL_DH
EVEN_IN = 2 * A_QK + 2 * A_V + 3 * B_W
C_K = HGRN_HEADS * HGRN_DK
C_V = HGRN_HEADS * HGRN_DV
ODD_IN = 2 * C_K + 2 * C_V

LANES = 128
SUBLANES = 8
VMEM_LIMIT = 48 * 1024 * 1024
MASK_VALUE = -1e30

NT_DIMS = (((1,), (1,)), ((), ()))
TN_DIMS = (((0,), (0,)), ((), ()))


def _params(*semantics):
    return pltpu.CompilerParams(dimension_semantics=semantics, vmem_limit_bytes=VMEM_LIMIT)


def _silu(x):
    return x * jax.nn.sigmoid(x)


def _norm_matmul_kernel(x_ref, g_ref, w_ref, o_ref, hn_ref):
    @pl.when(pl.program_id(1) == 0)
    def _():
        x = x_ref[...]
        ms = jnp.mean(x * x, axis=-1, keepdims=True)
        hn_ref[...] = (x * lax.rsqrt(ms + EPS) * g_ref[...]).astype(BF16)

    o_ref[...] = jnp.dot(hn_ref[...], w_ref[...], preferred_element_type=F32).astype(o_ref.dtype)


def _norm_matmul(x, g, w, tm, tn):
    m, k = x.shape
    n = w.shape[1]
    return pl.pallas_call(
        _norm_matmul_kernel,
        grid=(m // tm, n // tn),
        in_specs=[
            pl.BlockSpec((tm, k), lambda i, j: (i, 0)),
            pl.BlockSpec((1, k), lambda i, j: (0, 0)),
            pl.BlockSpec((k, tn), lambda i, j: (0, j)),
        ],
        out_specs=pl.BlockSpec((tm, tn), lambda i, j: (i, j)),
        out_shape=jax.ShapeDtypeStruct((m, n), F32),
        scratch_shapes=[pltpu.VMEM((tm, k), BF16)],
        compiler_params=_params("arbitrary", "arbitrary"),
        name="norm_matmul",
    )(x, g, w)


def _retention_tables(seq):
    h = jnp.arange(RET_HEADS, dtype=F32)
    log_g = jnp.log1p(-jnp.exp2(-5.0 - h))
    pos = jnp.arange(RET_CHUNK, dtype=F32)
    diff = pos[:, None] - pos[None, :]
    decay = jnp.where(diff >= 0, jnp.exp(jnp.maximum(diff, 0.0)[None] * log_g[:, None, None]), 0.0)
    q_scale = jnp.repeat(jnp.exp((pos + 1.0)[:, None] * log_g[None, :]), RET_DK, axis=1)
    k_scale = jnp.repeat(jnp.exp((RET_CHUNK - 1.0 - pos)[:, None] * log_g[None, :]), RET_DK, axis=1)
    chunk_decay = jnp.broadcast_to(jnp.exp(RET_CHUNK * log_g)[:, None, None], (RET_HEADS, 1, RET_DV))
    inv = ROPE_BASE ** (-jnp.arange(0, RET_DK, 2, dtype=F32) / RET_DK)
    ang = jnp.arange(seq, dtype=F32)[:, None] * inv[None, :]
    cos, sin = jnp.cos(ang), jnp.sin(ang)
    reps = LANES // RET_DK
    cos_t = jnp.tile(jnp.concatenate([cos, cos], axis=1), (1, reps))
    sin_t = jnp.tile(jnp.concatenate([-sin, sin], axis=1), (1, reps))
    return decay, q_scale, k_scale, chunk_decay, cos_t, sin_t


EVEN_TM = 512


def _even_in_kernel(x_ref, g_ref, w_ref, cos_ref, sin_ref, qs_ref, ks_ref,
                    qa_ref, qin_ref, ka_ref, kout_ref, va_ref, ga_ref,
                    qn_ref, q4_ref, q16_ref, kn_ref, k4_ref, k16_ref, vn_ref, v4_ref, v16_ref,
                    slab_ref):
    tm = x_ref.shape[0]
    x = x_ref[...]
    ms = jnp.mean(x * x, axis=-1, keepdims=True)
    hn = (x * lax.rsqrt(ms + EPS) * g_ref[...]).astype(BF16)

    def proj(c):
        return jnp.dot(hn, w_ref[:, c * B_W:(c + 1) * B_W], preferred_element_type=F32)

    cos = cos_ref[...]
    sin = sin_ref[...]
    lane = lax.broadcasted_iota(jnp.int32, (tm, LANES), 1)
    half = RET_DK // 2
    first_half = (lane % RET_DK) < half

    def rot(t):
        partner = jnp.where(first_half, pltpu.roll(t, LANES - half, 1), pltpu.roll(t, half, 1))
        return t * cos + partner * sin

    groups = [slice(s * LANES, (s + 1) * LANES) for s in range(B_W // LANES)]
    r = proj(0)
    for sl in groups:
        q = rot(r[:, sl])
        qa_ref[:, sl] = q.astype(BF16)
        qin_ref[:, sl] = (q * qs_ref[:, sl]).astype(BF16)
    r = proj(1)
    for sl in groups:
        k = rot(r[:, sl]) * (RET_DK ** -0.5)
        ka_ref[:, sl] = k.astype(BF16)
        kout_ref[:, sl] = (k * ks_ref[:, sl]).astype(BF16)
    va_ref[:, 0:B_W] = proj(2).astype(BF16)
    va_ref[:, B_W:2 * B_W] = proj(3).astype(BF16)
    ga_ref[:, 0:B_W] = proj(4)
    ga_ref[:, B_W:2 * B_W] = proj(5)
    for c, nat_ref, r4_ref, r16_ref, scale in ((6, qn_ref, q4_ref, q16_ref, DIL_DH ** -0.5),
                                               (7, kn_ref, k4_ref, k16_ref, None),
                                               (8, vn_ref, v4_ref, v16_ref, None)):
        r = proj(c)
        if scale is not None:
            r = r * scale
        nat_ref[...] = r.astype(BF16)
        for s, sl in enumerate(groups):
            slab_ref[s] = r[:, sl]
        for s, sl in enumerate(groups):
            for rho in range(4):
                r4_ref[rho, :, sl] = slab_ref[s, pl.ds(rho, tm // 4, stride=4), :].astype(BF16)
            for rho in range(16):
                r16_ref[rho, :, sl] = slab_ref[s, pl.ds(rho, tm // 16, stride=16), :].astype(BF16)


def _even_in(x, g, w, seq):
    tokens, d = x.shape
    bn = tokens // seq
    tm = EVEN_TM
    tps = seq // tm
    _, q_scale, k_scale, _, cos_t, sin_t = _retention_tables(seq)
    q_scale = jnp.tile(q_scale, (tm // RET_CHUNK, 1))
    k_scale = jnp.tile(k_scale, (tm // RET_CHUNK, 1))
    const = lambda shape: pl.BlockSpec(shape, lambda i: (0,) * len(shape))
    flat = lambda w_: pl.BlockSpec((tm, w_), lambda i: (i, 0))
    nat = pl.BlockSpec((None, tm, B_W), lambda i: (i // tps, i % tps, 0))
    by = lambda r: pl.BlockSpec((None, r, tm // r, B_W), lambda i: (i // tps, 0, i % tps, 0))
    sds = jax.ShapeDtypeStruct
    lay = [sds((bn, seq, B_W), BF16), sds((bn, 4, seq // 4, B_W), BF16),
           sds((bn, 16, seq // 16, B_W), BF16)]
    return pl.pallas_call(
        _even_in_kernel,
        grid=(tokens // tm,),
        in_specs=[
            flat(d), const((1, d)),
            pl.BlockSpec(w.shape, lambda i: (0, 0), pipeline_mode=pl.Buffered(1)),
            pl.BlockSpec((tm, LANES), lambda i: (i % tps, 0)),
            pl.BlockSpec((tm, LANES), lambda i: (i % tps, 0)),
            const((tm, A_QK)), const((tm, A_QK)),
        ],
        out_specs=[flat(A_QK)] * 4 + [flat(A_V), flat(A_V)] + [nat, by(4), by(16)] * 3,
        out_shape=[sds((tokens, A_QK), BF16)] * 4 + [sds((tokens, A_V), BF16), sds((tokens, A_V), F32)]
        + lay * 3,
        scratch_shapes=[pltpu.VMEM((B_W // LANES, tm, LANES), F32)],
        compiler_params=_params("arbitrary"),
        name="even_in",
    )(x, g, w, cos_t, sin_t, q_scale, k_scale)


def _retention_kernel(q_ref, qin_ref, k_ref, kout_ref, v_ref, g_ref, dec_ref, cd_ref, rn_ref, o_ref,
                      st_ref):
    @pl.when(pl.program_id(1) == 0)
    def _():
        st_ref[...] = jnp.zeros_like(st_ref)

    lane = lax.broadcasted_iota(jnp.int32, (RET_CHUNK, LANES), 1)
    head0 = lane < RET_DK
    for p in range(RET_HEADS // 2):
        sl = slice(p * LANES, (p + 1) * LANES)
        q, q_in, k, k_out = q_ref[:, sl], qin_ref[:, sl], k_ref[:, sl], kout_ref[:, sl]
        for a in range(2):
            h = 2 * p + a
            hs = slice(h * RET_DV, (h + 1) * RET_DV)
            mask = head0 if a == 0 else jnp.logical_not(head0)
            qm = jnp.where(mask, q, 0)
            qim = jnp.where(mask, q_in, 0)
            v = v_ref[:, hs]
            s = lax.dot_general(qm, k, NT_DIMS, preferred_element_type=F32) * dec_ref[h]
            intra = jnp.dot(s.astype(BF16), v, preferred_element_type=F32)
            st = st_ref[h]
            inter = jnp.dot(qim, st.astype(BF16), preferred_element_type=F32)
            st_ref[h] = st * cd_ref[h] + lax.dot_general(k_out, v, TN_DIMS,
                                                         preferred_element_type=F32)
            y = intra + inter
            yc = y - jnp.mean(y, axis=-1, keepdims=True)
            yn = yc * lax.rsqrt(jnp.mean(yc * yc, axis=-1, keepdims=True) + EPS)
            o_ref[:, hs] = (yn * rn_ref[:, hs] * _silu(g_ref[:, hs])).astype(o_ref.dtype)


def _retention(qa, qin, ka, kout, va, ga, ret_norm, bn, seq):
    c = RET_CHUNK
    decay, _, _, chunk_decay, _, _ = _retention_tables(seq)
    nchunk = seq // c
    blk = lambda w: pl.BlockSpec((c, w), lambda b, n: (b * nchunk + n, 0))
    return pl.pallas_call(
        _retention_kernel,
        grid=(bn, nchunk),
        in_specs=[
            blk(A_QK), blk(A_QK), blk(A_QK), blk(A_QK), blk(A_V), blk(A_V),
            pl.BlockSpec((RET_HEADS, c, c), lambda b, n: (0, 0, 0)),
            pl.BlockSpec((RET_HEADS, 1, RET_DV), lambda b, n: (0, 0, 0)),
            pl.BlockSpec((1, A_V), lambda b, n: (0, 0)),
        ],
        out_specs=blk(A_V),
        out_shape=jax.ShapeDtypeStruct((bn * seq, A_V), BF16),
        scratch_shapes=[pltpu.VMEM((RET_HEADS, 2 * RET_DK, RET_DV), F32)],
        compiler_params=_params("arbitrary", "arbitrary"),
        name="retention",
    )(qa, qin, ka, kout, va, ga, decay, chunk_decay, ret_norm)


def _rel_bucket(dist):
    max_exact = REL_BUCKETS // 2
    d = dist.astype(F32)
    large = max_exact + (jnp.log(jnp.maximum(d, 1.0) / max_exact) / math.log(REL_MAX_DIST / max_exact)
                         * (REL_BUCKETS - max_exact)).astype(jnp.int32)
    large = jnp.minimum(large, REL_BUCKETS - 1)
    return jnp.where(dist < max_exact, dist, large)


DIL_UNROLL = 8


def _dilated_buckets():
    lb = DIL_BLOCK
    a_idx = jnp.arange(lb)[:, None]
    c_idx = jnp.arange(2 * lb)[None, :]
    dist = lb + a_idx - c_idx
    tables = []
    for window, dilation in DIL_BRANCHES:
        valid = (dist >= 0) & (dist <= window // dilation)
        tables.append(jnp.where(valid, _rel_bucket(jnp.maximum(dist, 0) * dilation), -1))
    return jnp.stack(tables).astype(jnp.int32)


def _dilated_kernel(rb_ref, bucket_ref, qn_ref, kn_ref, vn_ref, q4_ref, k4_ref, v4_ref, q16_ref,
                    k16_ref, v16_ref, y_ref, bias_ref, acc_ref, m_ref, l_ref):
    lb = DIL_BLOCK
    seq = qn_ref.shape[0]
    pair = pl.program_id(1)

    @pl.when(pl.program_id(0) == 0)
    def _():
        for br in range(len(DIL_BRANCHES)):
            bucket = bucket_ref[br]
            for a in range(2):
                h = 2 * pair + a
                t = jnp.full((lb, 2 * lb), MASK_VALUE, F32)
                for j in range(REL_BUCKETS):
                    t = jnp.where(bucket == j, rb_ref[j, h], t)
                bias_ref[br * DIL_HEADS + h] = t

    lane = lax.broadcasted_iota(jnp.int32, (lb, LANES), 1)
    head0 = lane < DIL_DH
    prev_cols = lax.broadcasted_iota(jnp.int32, (lb, 2 * lb), 1) < lb

    def attend(br, first, q, kp, kc, vp, vc):
        k = jnp.concatenate([kp, kc], axis=0)
        v = jnp.concatenate([vp, vc], axis=0)
        no_prev = jnp.where(jnp.logical_and(prev_cols, first), MASK_VALUE, 0.0)
        parts = []
        for a in range(2):
            mask = head0 if a == 0 else jnp.logical_not(head0)
            s = lax.dot_general(jnp.where(mask, q, 0), k, NT_DIMS, preferred_element_type=F32)
            s = s + bias_ref[br * DIL_HEADS + 2 * pair + a] + no_prev
            m = jnp.max(s, axis=-1, keepdims=True)
            e = jnp.exp(s - m)
            parts.append((jnp.dot(e.astype(BF16), v, preferred_element_type=F32), m,
                          jnp.sum(e, axis=-1, keepdims=True)))
        return tuple(jnp.where(head0, x0, x1) for x0, x1 in zip(*parts))

    def natural_block(n, carry):
        r0 = pl.multiple_of(n * lb, lb)
        rp = pl.multiple_of(jnp.maximum(n - 1, 0) * lb, lb)
        cur, prev = pl.ds(r0, lb), pl.ds(rp, lb)
        acc, m, l = attend(0, n == 0, qn_ref[cur, :], kn_ref[prev, :], kn_ref[cur, :],
                           vn_ref[prev, :], vn_ref[cur, :])
        acc_ref[cur, :] = acc
        m_ref[cur, :] = m
        l_ref[cur, :] = l
        return carry

    lax.fori_loop(0, seq // lb, natural_block, 0, unroll=DIL_UNROLL)

    def dilated_blocks(br, r, q_ref, k_ref, v_ref):
        nb = seq // (r * lb)

        def body(idx, carry):
            rho, n = idx // nb, idx % nb
            r0 = pl.multiple_of(n * lb, lb)
            rp = pl.multiple_of(jnp.maximum(n - 1, 0) * lb, lb)
            cur, prev = pl.ds(r0, lb), pl.ds(rp, lb)
            acc, m, l = attend(br, n == 0, q_ref[rho, cur, :], k_ref[rho, prev, :], k_ref[rho, cur, :],
                               v_ref[rho, prev, :], v_ref[rho, cur, :])
            rows = pl.ds(n * (lb * r) + rho, lb, stride=r)
            m_old = m_ref[rows, :]
            m_new = jnp.maximum(m_old, m)
            c_old = jnp.exp(m_old - m_new)
            c_cur = jnp.exp(m - m_new)
            acc_ref[rows, :] = acc_ref[rows, :] * c_old + acc * c_cur
            l_ref[rows, :] = l_ref[rows, :] * c_old + l * c_cur
            m_ref[rows, :] = m_new
            return carry

        lax.fori_loop(0, r * nb, body, 0, unroll=DIL_UNROLL)

    dilated_blocks(1, DIL_BRANCHES[1][1], q4_ref, k4_ref, v4_ref)
    dilated_blocks(2, DIL_BRANCHES[2][1], q16_ref, k16_ref, v16_ref)

    def finish(n, carry):
        rows = pl.ds(pl.multiple_of(n * lb, lb), lb)
        y_ref[rows, :] = (acc_ref[rows, :] / l_ref[rows, :]).astype(y_ref.dtype)
        return carry

    lax.fori_loop(0, seq // lb, finish, 0)


def _dilated(layouts, rel_bias, bn, seq):
    lb = DIL_BLOCK
    assert all(w // r <= lb and seq % (r * lb) == 0 for w, r in DIL_BRANCHES)
    assert tuple(r for _, r in DIL_BRANCHES) == (1, 4, 16)
    nat = pl.BlockSpec((None, seq, LANES), lambda b, p: (b, 0, p))
    by = lambda r: pl.BlockSpec((None, r, seq // r, LANES), lambda b, p: (b, 0, 0, p))
    nbr = len(DIL_BRANCHES)
    return pl.pallas_call(
        _dilated_kernel,
        grid=(bn, DIL_HEADS // 2),
        in_specs=[pl.BlockSpec(memory_space=pltpu.SMEM),
                  pl.BlockSpec((nbr, lb, 2 * lb), lambda b, p: (0, 0, 0))]
        + [nat] * 3 + [by(4)] * 3 + [by(16)] * 3,
        out_specs=nat,
        out_shape=jax.ShapeDtypeStruct((bn, seq, B_W), BF16),
        scratch_shapes=[pltpu.VMEM((nbr * DIL_HEADS, lb, 2 * lb), F32),
                        pltpu.VMEM((seq, LANES), F32), pltpu.VMEM((seq, LANES), F32),
                        pltpu.VMEM((seq, LANES), F32)],
        compiler_params=_params("arbitrary", "arbitrary"),
        name="dilated",
    )(rel_bias.astype(F32), _dilated_buckets(), *layouts)


def _even_out_kernel(x_ref, ya_ref, yb_ref, wa_ref, wb_ref, h_ref):
    acc = jnp.dot(ya_ref[...], wa_ref[...], preferred_element_type=F32)
    acc = acc + jnp.dot(yb_ref[...], wb_ref[...], preferred_element_type=F32)
    h_ref[...] = x_ref[...] + acc


def _even_out(x, ya, yb, w_out, tm):
    m = x.shape[0]
    row = lambda w: pl.BlockSpec((tm, w), lambda i: (i, 0))
    return pl.pallas_call(
        _even_out_kernel,
        grid=(m // tm,),
        in_specs=[row(D_MODEL), row(A_V), row(B_W),
                  pl.BlockSpec((A_V, D_MODEL), lambda i: (0, 0)),
                  pl.BlockSpec((B_W, D_MODEL), lambda i: (A_V // B_W, 0))],
        out_specs=row(D_MODEL),
        out_shape=jax.ShapeDtypeStruct((m, D_MODEL), F32),
        compiler_params=_params("arbitrary"),
        name="even_out",
    )(x, ya, yb, w_out, w_out)


FFN_TM = 512
FFN_R = 8
FFN_CHUNK = 256


def _ffn_kernel(tiles_per_seq, final, h_ref, g_ref, wup_ref, cw_ref, cb_ref, wd_ref, fg_ref, o_ref,
                slab_ref, hn_ref, act_ref, halo_ref):
    tm = h_ref.shape[0]
    rows = tm // FFN_R
    nslab = D_MODEL // LANES
    ck = FFN_CHUNK
    group = lambda g: slice(g * rows, (g + 1) * rows)

    for s in range(nslab):
        slab_ref[s] = h_ref[:, s * LANES:(s + 1) * LANES]
    for g in range(FFN_R):
        x = jnp.concatenate([slab_ref[s, pl.ds(g, rows, stride=FFN_R), :] for s in range(nslab)],
                            axis=1)
        ms = jnp.mean(x * x, axis=-1, keepdims=True)
        hn_ref[group(g), :] = (x * lax.rsqrt(ms + EPS) * g_ref[...]).astype(BF16)

    @pl.when((pl.program_id(0) % tiles_per_seq) == 0)
    def _():
        halo_ref[...] = jnp.zeros_like(halo_ref)

    first_row = lax.broadcasted_iota(jnp.int32, (rows, ck), 0) == 0

    def shift_down(block, fill):
        return jnp.where(first_row, fill, pltpu.roll(block, 1, 0))

    def conv(cols, slot):
        u = jnp.dot(hn_ref[...], wup_ref[:, cols], preferred_element_type=F32)
        ug = [u[group(g), :] for g in range(FFN_R)]
        tails = halo_ref[slot]
        prev2 = shift_down(ug[FFN_R - 2], tails[SUBLANES - 1:SUBLANES, :])
        prev1 = shift_down(ug[FFN_R - 1], tails[2 * SUBLANES - 1:2 * SUBLANES, :])
        halo_ref[slot] = jnp.concatenate(
            [ug[FFN_R - 2][rows - SUBLANES:, :], ug[FFN_R - 1][rows - SUBLANES:, :]], axis=0)
        back1 = [prev1] + ug[:FFN_R - 1]
        back2 = [prev2, prev1] + ug[:FFN_R - 2]
        cw = cw_ref[:, cols]
        cb = cb_ref[:, cols]
        return [cw[0:1, :] * back2[g] + cw[1:2, :] * back1[g] + cw[2:3, :] * ug[g] + cb
                for g in range(FFN_R)]

    for c in range(D_FF // ck):
        gate = conv(slice(c * ck, (c + 1) * ck), 2 * c)
        value = conv(slice(D_FF + c * ck, D_FF + (c + 1) * ck), 2 * c + 1)
        for g in range(FFN_R):
            act_ref[group(g), c * ck:(c + 1) * ck] = (_silu(gate[g]) * value[g]).astype(BF16)

    down = jnp.dot(act_ref[...], wd_ref[...], preferred_element_type=F32)
    for g in range(FFN_R):
        for s in range(nslab):
            slab_ref[s, pl.ds(g, rows, stride=FFN_R), :] = down[group(g), s * LANES:(s + 1) * LANES]
    y = h_ref[...] + jnp.concatenate([slab_ref[s] for s in range(nslab)], axis=1)
    if final:
        ms = jnp.mean(y * y, axis=-1, keepdims=True)
        y = y * lax.rsqrt(ms + EPS) * fg_ref[...]
    o_ref[...] = y


def _ffn(h, g, w_up, conv_w, conv_b, w_down, final_g, seq, final):
    m = h.shape[0]
    tm = FFN_TM
    assert D_FF % FFN_CHUNK == 0 and tm % (FFN_R * 2 * SUBLANES) == 0 and seq % tm == 0
    const = lambda shape: pl.BlockSpec(shape, lambda i: (0, 0), pipeline_mode=pl.Buffered(1))
    return pl.pallas_call(
        functools.partial(_ffn_kernel, seq // tm, final),
        grid=(m // tm,),
        in_specs=[
            pl.BlockSpec((tm, D_MODEL), lambda i: (i, 0)),
            const((1, D_MODEL)),
            const((D_MODEL, 2 * D_FF)), const((CONV_WIDTH, 2 * D_FF)), const((1, 2 * D_FF)),
            const((D_FF, D_MODEL)),
            const((1, D_MODEL)),
        ],
        out_specs=pl.BlockSpec((tm, D_MODEL), lambda i: (i, 0)),
        out_shape=jax.ShapeDtypeStruct((m, D_MODEL), F32),
        scratch_shapes=[
            pltpu.VMEM((D_MODEL // LANES, tm, LANES), F32),
            pltpu.VMEM((tm, D_MODEL), BF16),
            pltpu.VMEM((tm, D_FF), BF16),
            pltpu.VMEM((2 * D_FF // FFN_CHUNK, 2 * SUBLANES, FFN_CHUNK), F32),
        ],
        compiler_params=_params("arbitrary"),
        name="conv_ffn",
    )(h, g, w_up.astype(BF16), conv_w.astype(F32), conv_b.astype(F32).reshape(1, -1),
      w_down.astype(BF16), final_g)


HGRN_GROUP = 128


def _hgrn_kernel(layer, q_ref, f_ref, i_ref, g_ref, lbp_ref, on_ref, o_ref, st_ref):
    @pl.when(pl.program_id(1) == 0)
    def _():
        st_ref[...] = jnp.zeros_like(st_ref)

    c = HGRN_CHUNK
    grp = HGRN_GROUP
    lbp = lbp_ref[...]
    ex = jnp.exp(lbp - jnp.max(lbp, axis=0, keepdims=True))
    sm = ex / jnp.sum(ex, axis=0, keepdims=True)
    lower_all = jnp.sum(sm[0:layer + 1, :], axis=0, keepdims=True) - sm[0:1, :]

    row = lax.broadcasted_iota(jnp.int32, (grp, grp), 0)
    colm = lax.broadcasted_iota(jnp.int32, (grp, grp), 1)
    causal = jnp.logical_and(row // c == colm // c, row >= colm)
    tri = jnp.where(causal, 1.0, 0.0).astype(BF16)

    heads = range(HGRN_HEADS)
    hk = [slice(h * HGRN_DK, (h + 1) * HGRN_DK) for h in heads]
    hv = [slice(h * HGRN_DV, (h + 1) * HGRN_DV) for h in heads]
    chunks = [slice(n * c, (n + 1) * c) for n in range(grp // c)]

    q = _silu(q_ref[...])
    fg = lower_all + (1.0 - lower_all) * jax.nn.sigmoid(f_ref[...])
    k = 1.0 - fg
    log_f = jnp.log(fg)
    hi = log_f.astype(BF16)
    rem = log_f - hi.astype(F32)
    mid = rem.astype(BF16)
    lo = (rem - mid.astype(F32)).astype(BF16)
    parts = jnp.dot(tri, jnp.concatenate([hi, mid, lo], axis=1), preferred_element_type=F32)
    b = parts[:, 0:C_K] + parts[:, C_K:2 * C_K] + parts[:, 2 * C_K:3 * C_K]
    v = i_ref[...].astype(BF16)
    q_t = (q * jnp.exp(b)).astype(BF16)
    k_t = (k * jnp.exp(-b)).astype(BF16)
    scores = [lax.dot_general(q_t[:, hk[h]], k_t[:, hk[h]], NT_DIMS, preferred_element_type=F32)
              for h in heads]
    scores = [jnp.where(causal, s, 0.0).astype(BF16) for s in scores]
    intra = [jnp.dot(scores[h], v[:, hv[h]], preferred_element_type=F32) for h in heads]
    b_last = [b[cs.stop - 1:cs.stop, :] for cs in chunks]
    k_end = [(k[cs, :] * jnp.exp(bl - b[cs, :])).astype(BF16) for cs, bl in zip(chunks, b_last)]
    decay = [jnp.exp(bl) for bl in b_last]
    st = [st_ref[h] for h in heads]
    inter = [[] for _ in heads]
    for n, cs in enumerate(chunks):
        for h in heads:
            inter[h].append(lax.dot_general(q_t[cs, hk[h]], st[h].astype(BF16), NT_DIMS,
                                            preferred_element_type=F32))
            st[h] = st[h] * decay[n][:, hk[h]] + lax.dot_general(
                v[cs, hv[h]], k_end[n][:, hk[h]], TN_DIMS, preferred_element_type=F32)
    gate = on_ref[...] * _silu(g_ref[...])
    for h in heads:
        st_ref[h] = st[h]
        y = intra[h] + jnp.concatenate(inter[h], axis=0)
        yn = y * lax.rsqrt(jnp.mean(y * y, axis=-1, keepdims=True) + EPS)
        o_ref[:, hv[h]] = (yn * gate[:, hv[h]]).astype(o_ref.dtype)


def _hgrn(proj, hgrn_lb, out_norm, layer, seq):
    bn = proj.shape[0]
    rows = HGRN_GROUP
    blk = lambda col: pl.BlockSpec((None, rows, C_K), lambda b, n: (b, n, col))
    return pl.pallas_call(
        functools.partial(_hgrn_kernel, layer),
        grid=(bn, seq // rows),
        in_specs=[
            blk(0), blk(1), blk(2), blk(3),
            pl.BlockSpec((hgrn_lb.shape[0], C_K), lambda b, n: (0, 0)),
            pl.BlockSpec((1, C_V), lambda b, n: (0, 0)),
        ],
        out_specs=pl.BlockSpec((None, rows, C_V), lambda b, n: (b, n, 0)),
        out_shape=jax.ShapeDtypeStruct((bn, seq, C_V), BF16),
        scratch_shapes=[pltpu.VMEM((HGRN_HEADS, HGRN_DV, HGRN_DK), F32)],
        compiler_params=_params("arbitrary", "arbitrary"),
        name="hgrn2",
    )(proj, proj, proj, proj, hgrn_lb, out_norm)


def _matmul_residual_kernel(h_ref, y_ref, w_ref, o_ref):
    o_ref[...] = h_ref[...] + jnp.dot(y_ref[...], w_ref[...], preferred_element_type=F32)


def _matmul_residual(h, y, w, tm):
    m, n = h.shape
    k = y.shape[1]
    return pl.pallas_call(
        _matmul_residual_kernel,
        grid=(m // tm,),
        in_specs=[
            pl.BlockSpec((tm, n), lambda i: (i, 0)),
            pl.BlockSpec((tm, k), lambda i: (i, 0)),
            pl.BlockSpec((k, n), lambda i: (0, 0)),
        ],
        out_specs=pl.BlockSpec((tm, n), lambda i: (i, 0)),
        out_shape=jax.ShapeDtypeStruct((m, n), F32),
        compiler_params=_params("arbitrary"),
        name="matmul_residual",
    )(h, y, w)


def kernel(x, even_w_in, even_w_out, ret_norm, rel_bias, odd_w_in, odd_w_out, hgrn_lb, hgrn_norm,
           mix_norm, ffn_norm, ffn_w_up, ffn_conv_w, ffn_conv_b, ffn_w_down, final_norm):
    bn, seq, d = x.shape
    tokens = bn * seq
    row = lambda a: a.reshape(1, -1).astype(F32)
    h = x.reshape(tokens, d)
    final_g = row(final_norm)

    outs = _even_in(h, row(mix_norm[0]), even_w_in[0].astype(BF16), seq)
    qa, qin, ka, kout, va, ga = outs[:6]
    qn, q4, q16, kn, k4, k16, vn, v4, v16 = outs[6:]
    ya = _retention(qa, qin, ka, kout, va, ga, row(ret_norm[0]), bn, seq)
    yb = _dilated((qn, kn, vn, q4, k4, v4, q16, k16, v16), rel_bias, bn, seq)
    h = _even_out(h, ya, yb.reshape(tokens, B_W), even_w_out[0].astype(BF16), tm=1024)
    h = _ffn(h, row(ffn_norm[0]), ffn_w_up[0], ffn_conv_w[0], ffn_conv_b[0], ffn_w_down[0], final_g,
             seq, final=False)

    proj = _norm_matmul(h, row(mix_norm[1]), odd_w_in[0].astype(BF16), tm=1024, tn=1024)
    y = _hgrn(proj.reshape(bn, seq, ODD_IN), hgrn_lb.astype(F32), row(hgrn_norm[0]), 1, seq)
    h = _matmul_residual(h, y.reshape(tokens, C_V), odd_w_out[0].astype(BF16), tm=1024)
    h = _ffn(h, row(ffn_norm[1]), ffn_w_up[1], ffn_conv_w[1], ffn_conv_b[1], ffn_w_down[1], final_g,
             seq, final=True)
    return h.reshape(bn, seq, d)
```

```python
import functools
import math

import numpy as np
import jax
import jax.numpy as jnp
from jax import lax
from jax.experimental import pallas as pl
from jax.experimental.pallas import tpu as pltpu

F32 = jnp.float32
BF16 = jnp.bfloat16

D_MODEL = 1024
RET_HEADS = 8
RET_DK = 64
RET_DV = 128
RET_CHUNK = 128
ROPE_BASE = 10000.0
DIL_HEADS = 8
DIL_DH = 64
DIL_BRANCHES = ((128, 1), (512, 4), (2048, 16))
DIL_BLOCK = 128
HGRN_HEADS = 8
HGRN_DK = 128
HGRN_DV = 128
HGRN_CHUNK = 32
REL_BUCKETS = 32
REL_MAX_DIST = 2048
D_FF = 2816
CONV_WIDTH = 3
EPS = 1e-6

A_QK = RET_HEADS * RET_DK
A_V = RET_HEADS * RET_DV
B_W = DIL_HEADS * DIL_DH
EVEN_IN = 2 * A_QK + 2 * A_V + 3 * B_W
C_K = HGRN_HEADS * HGRN_DK
C_V = HGRN_HEADS * HGRN_DV
ODD_IN = 2 * C_K + 2 * C_V

LANES = 128
SUBLANES = 8
VMEM_LIMIT = 48 * 1024 * 1024
MASK_VALUE = -1e30

NT_DIMS = (((1,), (1,)), ((), ()))
TN_DIMS = (((0,), (0,)), ((), ()))


def _params(*semantics):
    return pltpu.CompilerParams(dimension_semantics=semantics, vmem_limit_bytes=VMEM_LIMIT)


def _silu(x):
    return x * jax.nn.sigmoid(x)


def _retention_tables(seq):
    h = jnp.arange(RET_HEADS, dtype=F32)
    log_g = jnp.log1p(-jnp.exp2(-5.0 - h))
    pos = jnp.arange(RET_CHUNK, dtype=F32)
    diff = pos[:, None] - pos[None, :]
    decay = jnp.where(diff >= 0, jnp.exp(jnp.maximum(diff, 0.0)[None] * log_g[:, None, None]), 0.0)
    q_scale = jnp.repeat(jnp.exp((pos + 1.0)[:, None] * log_g[None, :]), RET_DK, axis=1)
    k_scale = jnp.repeat(jnp.exp((RET_CHUNK - 1.0 - pos)[:, None] * log_g[None, :]), RET_DK, axis=1)
    chunk_decay = jnp.broadcast_to(jnp.exp(RET_CHUNK * log_g)[:, None, None], (RET_HEADS, 1, RET_DV))
    inv = ROPE_BASE ** (-jnp.arange(0, RET_DK, 2, dtype=F32) / RET_DK)
    ang = jnp.arange(seq, dtype=F32)[:, None] * inv[None, :]
    cos, sin = jnp.cos(ang), jnp.sin(ang)
    reps = LANES // RET_DK
    cos_t = jnp.tile(jnp.concatenate([cos, cos], axis=1), (1, reps))
    sin_t = jnp.tile(jnp.concatenate([-sin, sin], axis=1), (1, reps))
    return decay, q_scale, k_scale, chunk_decay, cos_t, sin_t


EVEN_TM = 512


def _even_in_kernel(x_ref, g_ref, w_ref, cos_ref, sin_ref, qs_ref, ks_ref,
                    qa_ref, qin_ref, ka_ref, kout_ref, va_ref, ga_ref,
                    qn_ref, q4_ref, q16_ref, kn_ref, k4_ref, k16_ref, vn_ref, v4_ref, v16_ref,
                    slab_ref):
    tm = x_ref.shape[0]
    x = x_ref[...]
    ms = jnp.mean(x * x, axis=-1, keepdims=True)
    hn = (x * lax.rsqrt(ms + EPS) * g_ref[...]).astype(BF16)

    def proj(c):
        return jnp.dot(hn, w_ref[:, c * B_W:(c + 1) * B_W], preferred_element_type=F32)

    cos = cos_ref[...]
    sin = sin_ref[...]
    lane = lax.broadcasted_iota(jnp.int32, (tm, LANES), 1)
    half = RET_DK // 2
    first_half = (lane % RET_DK) < half

    def rot(t):
        partner = jnp.where(first_half, pltpu.roll(t, LANES - half, 1), pltpu.roll(t, half, 1))
        return t * cos + partner * sin

    groups = [slice(s * LANES, (s + 1) * LANES) for s in range(B_W // LANES)]
    r = proj(0)
    for sl in groups:
        q = rot(r[:, sl])
        qa_ref[:, sl] = q.astype(BF16)
        qin_ref[:, sl] = (q * qs_ref[:, sl]).astype(BF16)
    r = proj(1)
    for sl in groups:
        k = rot(r[:, sl]) * (RET_DK ** -0.5)
        ka_ref[:, sl] = k.astype(BF16)
        kout_ref[:, sl] = (k * ks_ref[:, sl]).astype(BF16)
    va_ref[:, 0:B_W] = proj(2).astype(BF16)
    va_ref[:, B_W:2 * B_W] = proj(3).astype(BF16)
    ga_ref[:, 0:B_W] = proj(4)
    ga_ref[:, B_W:2 * B_W] = proj(5)
    for c, nat_ref, r4_ref, r16_ref, scale in ((6, qn_ref, q4_ref, q16_ref, DIL_DH ** -0.5),
                                               (7, kn_ref, k4_ref, k16_ref, None),
                                               (8, vn_ref, v4_ref, v16_ref, None)):
        r = proj(c)
        if scale is not None:
            r = r * scale
        nat_ref[...] = r.astype(BF16)
        for s, sl in enumerate(groups):
            slab_ref[s] = r[:, sl]
        for s, sl in enumerate(groups):
            for rho in range(4):
                r4_ref[rho, :, sl] = slab_ref[s, pl.ds(rho, tm // 4, stride=4), :].astype(BF16)
            for rho in range(16):
                r16_ref[rho, :, sl] = slab_ref[s, pl.ds(rho, tm // 16, stride=16), :].astype(BF16)


def _even_in(x, g, w, seq):
    tokens, d = x.shape
    bn = tokens // seq
    tm = EVEN_TM
    tps = seq // tm
    _, q_scale, k_scale, _, cos_t, sin_t = _retention_tables(seq)
    q_scale = jnp.tile(q_scale, (tm // RET_CHUNK, 1))
    k_scale = jnp.tile(k_scale, (tm // RET_CHUNK, 1))
    const = lambda shape: pl.BlockSpec(shape, lambda i: (0,) * len(shape))
    flat = lambda w_: pl.BlockSpec((tm, w_), lambda i: (i, 0))
    nat = pl.BlockSpec((None, tm, B_W), lambda i: (i // tps, i % tps, 0))
    by = lambda r: pl.BlockSpec((None, r, tm // r, B_W), lambda i: (i // tps, 0, i % tps, 0))
    sds = jax.ShapeDtypeStruct
    lay = [sds((bn, seq, B_W), BF16), sds((bn, 4, seq // 4, B_W), BF16),
           sds((bn, 16, seq // 16, B_W), BF16)]
    return pl.pallas_call(
        _even_in_kernel,
        grid=(tokens // tm,),
        in_specs=[
            flat(d), const((1, d)),
            pl.BlockSpec(w.shape, lambda i: (0, 0), pipeline_mode=pl.Buffered(1)),
            pl.BlockSpec((tm, LANES), lambda i: (i % tps, 0)),
            pl.BlockSpec((tm, LANES), lambda i: (i % tps, 0)),
            const((tm, A_QK)), const((tm, A_QK)),
        ],
        out_specs=[flat(A_QK)] * 4 + [flat(A_V), flat(A_V)] + [nat, by(4), by(16)] * 3,
        out_shape=[sds((tokens, A_QK), BF16)] * 4 + [sds((tokens, A_V), BF16), sds((tokens, A_V), F32)]
        + lay * 3,
        scratch_shapes=[pltpu.VMEM((B_W // LANES, tm, LANES), F32)],
        compiler_params=_params("arbitrary"),
        name="even_in",
    )(x, g, w, cos_t, sin_t, q_scale, k_scale)


def _retention_kernel(q_ref, qin_ref, k_ref, kout_ref, v_ref, g_ref, dec_ref, cd_ref, rn_ref, o_ref,
                      st_ref):
    @pl.when(pl.program_id(1) == 0)
    def _():
        st_ref[...] = jnp.zeros_like(st_ref)

    lane = lax.broadcasted_iota(jnp.int32, (RET_CHUNK, LANES), 1)
    head0 = lane < RET_DK
    for p in range(RET_HEADS // 2):
        sl = slice(p * LANES, (p + 1) * LANES)
        q, q_in, k, k_out = q_ref[:, sl], qin_ref[:, sl], k_ref[:, sl], kout_ref[:, sl]
        for a in range(2):
            h = 2 * p + a
            hs = slice(h * RET_DV, (h + 1) * RET_DV)
            mask = head0 if a == 0 else jnp.logical_not(head0)
            qm = jnp.where(mask, q, 0)
            qim = jnp.where(mask, q_in, 0)
            v = v_ref[:, hs]
            s = lax.dot_general(qm, k, NT_DIMS, preferred_element_type=F32) * dec_ref[h]
            intra = jnp.dot(s.astype(BF16), v, preferred_element_type=F32)
            st = st_ref[h]
            inter = jnp.dot(qim, st.astype(BF16), preferred_element_type=F32)
            st_ref[h] = st * cd_ref[h] + lax.dot_general(k_out, v, TN_DIMS,
                                                         preferred_element_type=F32)
            y = intra + inter
            yc = y - jnp.mean(y, axis=-1, keepdims=True)
            yn = yc * lax.rsqrt(jnp.mean(yc * yc, axis=-1, keepdims=True) + EPS)
            o_ref[:, hs] = (yn * rn_ref[:, hs] * _silu(g_ref[:, hs])).astype(o_ref.dtype)


def _retention(qa, qin, ka, kout, va, ga, ret_norm, bn, seq):
    c = RET_CHUNK
    decay, _, _, chunk_decay, _, _ = _retention_tables(seq)
    nchunk = seq // c
    blk = lambda w: pl.BlockSpec((c, w), lambda b, n: (b * nchunk + n, 0))
    return pl.pallas_call(
        _retention_kernel,
        grid=(bn, nchunk),
        in_specs=[
            blk(A_QK), blk(A_QK), blk(A_QK), blk(A_QK), blk(A_V), blk(A_V),
            pl.BlockSpec((RET_HEADS, c, c), lambda b, n: (0, 0, 0)),
            pl.BlockSpec((RET_HEADS, 1, RET_DV), lambda b, n: (0, 0, 0)),
            pl.BlockSpec((1, A_V), lambda b, n: (0, 0)),
        ],
        out_specs=blk(A_V),
        out_shape=jax.ShapeDtypeStruct((bn * seq, A_V), BF16),
        scratch_shapes=[pltpu.VMEM((RET_HEADS, 2 * RET_DK, RET_DV), F32)],
        compiler_params=_params("arbitrary", "arbitrary"),
        name="retention",
    )(qa, qin, ka, kout, va, ga, decay, chunk_decay, ret_norm)


def _rel_bucket(dist):
    max_exact = REL_BUCKETS // 2
    d = dist.astype(F32)
    large = max_exact + (jnp.log(jnp.maximum(d, 1.0) / max_exact) / math.log(REL_MAX_DIST / max_exact)
                         * (REL_BUCKETS - max_exact)).astype(jnp.int32)
    large = jnp.minimum(large, REL_BUCKETS - 1)
    return jnp.where(dist < max_exact, dist, large)


DIL_UNROLL = 8


def _dilated_buckets():
    lb = DIL_BLOCK
    a_idx = jnp.arange(lb)[:, None]
    c_idx = jnp.arange(2 * lb)[None, :]
    dist = lb + a_idx - c_idx
    tables = []
    for window, dilation in DIL_BRANCHES:
        valid = (dist >= 0) & (dist <= window // dilation)
        tables.append(jnp.where(valid, _rel_bucket(jnp.maximum(dist, 0) * dilation), -1))
    return jnp.stack(tables).astype(jnp.int32)


def _dilated_kernel(rb_ref, bucket_ref, qn_ref, kn_ref, vn_ref, q4_ref, k4_ref, v4_ref, q16_ref,
                    k16_ref, v16_ref, y_ref, bias_ref, acc_ref, m_ref, l_ref):
    lb = DIL_BLOCK
    seq = qn_ref.shape[0]
    pair = pl.program_id(1)

    @pl.when(pl.program_id(0) == 0)
    def _():
        for br in range(len(DIL_BRANCHES)):
            bucket = bucket_ref[br]
            for a in range(2):
                h = 2 * pair + a
                t = jnp.full((lb, 2 * lb), MASK_VALUE, F32)
                for j in range(REL_BUCKETS):
                    t = jnp.where(bucket == j, rb_ref[j, h], t)
                bias_ref[br * DIL_HEADS + h] = t

    lane = lax.broadcasted_iota(jnp.int32, (lb, LANES), 1)
    head0 = lane < DIL_DH
    prev_cols = lax.broadcasted_iota(jnp.int32, (lb, 2 * lb), 1) < lb

    def attend(br, first, q, kp, kc, vp, vc):
        k = jnp.concatenate([kp, kc], axis=0)
        v = jnp.concatenate([vp, vc], axis=0)
        no_prev = jnp.where(jnp.logical_and(prev_cols, first), MASK_VALUE, 0.0)
        parts = []
        for a in range(2):
            mask = head0 if a == 0 else jnp.logical_not(head0)
            s = lax.dot_general(jnp.where(mask, q, 0), k, NT_DIMS, preferred_element_type=F32)
            s = s + bias_ref[br * DIL_HEADS + 2 * pair + a] + no_prev
            m = jnp.max(s, axis=-1, keepdims=True)
            e = jnp.exp(s - m)
            parts.append((jnp.dot(e.astype(BF16), v, preferred_element_type=F32), m,
                          jnp.sum(e, axis=-1, keepdims=True)))
        return tuple(jnp.where(head0, x0, x1) for x0, x1 in zip(*parts))

    def natural_block(n, carry):
        r0 = pl.multiple_of(n * lb, lb)
        rp = pl.multiple_of(jnp.maximum(n - 1, 0) * lb, lb)
        cur, prev = pl.ds(r0, lb), pl.ds(rp, lb)
        acc, m, l = attend(0, n == 0, qn_ref[cur, :], kn_ref[prev, :], kn_ref[cur, :],
                           vn_ref[prev, :], vn_ref[cur, :])
        acc_ref[cur, :] = acc
        m_ref[cur, :] = m
        l_ref[cur, :] = l
        return carry

    lax.fori_loop(0, seq // lb, natural_block, 0, unroll=DIL_UNROLL)

    def dilated_blocks(br, r, q_ref, k_ref, v_ref):
        nb = seq // (r * lb)

        def body(idx, carry):
            rho, n = idx // nb, idx % nb
            r0 = pl.multiple_of(n * lb, lb)
            rp = pl.multiple_of(jnp.maximum(n - 1, 0) * lb, lb)
            cur, prev = pl.ds(r0, lb), pl.ds(rp, lb)
            acc, m, l = attend(br, n == 0, q_ref[rho, cur, :], k_ref[rho, prev, :], k_ref[rho, cur, :],
                               v_ref[rho, prev, :], v_ref[rho, cur, :])
            rows = pl.ds(n * (lb * r) + rho, lb, stride=r)
            m_old = m_ref[rows, :]
            m_new = jnp.maximum(m_old, m)
            c_old = jnp.exp(m_old - m_new)
            c_cur = jnp.exp(m - m_new)
            acc_ref[rows, :] = acc_ref[rows, :] * c_old + acc * c_cur
            l_ref[rows, :] = l_ref[rows, :] * c_old + l * c_cur
            m_ref[rows, :] = m_new
            return carry

        lax.fori_loop(0, r * nb, body, 0, unroll=DIL_UNROLL)

    dilated_blocks(1, DIL_BRANCHES[1][1], q4_ref, k4_ref, v4_ref)
    dilated_blocks(2, DIL_BRANCHES[2][1], q16_ref, k16_ref, v16_ref)

    def finish(n, carry):
        rows = pl.ds(pl.multiple_of(n * lb, lb), lb)
        y_ref[rows, :] = (acc_ref[rows, :] / l_ref[rows, :]).astype(y_ref.dtype)
        return carry

    lax.fori_loop(0, seq // lb, finish, 0)


def _dilated(layouts, rel_bias, bn, seq):
    lb = DIL_BLOCK
    assert all(w // r <= lb and seq % (r * lb) == 0 for w, r in DIL_BRANCHES)
    assert tuple(r for _, r in DIL_BRANCHES) == (1, 4, 16)
    nat = pl.BlockSpec((None, seq, LANES), lambda b, p: (b, 0, p))
    by = lambda r: pl.BlockSpec((None, r, seq // r, LANES), lambda b, p: (b, 0, 0, p))
    nbr = len(DIL_BRANCHES)
    return pl.pallas_call(
        _dilated_kernel,
        grid=(bn, DIL_HEADS // 2),
        in_specs=[pl.BlockSpec(memory_space=pltpu.SMEM),
                  pl.BlockSpec((nbr, lb, 2 * lb), lambda b, p: (0, 0, 0))]
        + [nat] * 3 + [by(4)] * 3 + [by(16)] * 3,
        out_specs=nat,
        out_shape=jax.ShapeDtypeStruct((bn, seq, B_W), BF16),
        scratch_shapes=[pltpu.VMEM((nbr * DIL_HEADS, lb, 2 * lb), F32),
                        pltpu.VMEM((seq, LANES), F32), pltpu.VMEM((seq, LANES), F32),
                        pltpu.VMEM((seq, LANES), F32)],
        compiler_params=_params("arbitrary", "arbitrary"),
        name="dilated",
    )(rel_bias.astype(F32), _dilated_buckets(), *layouts)


def _even_out_kernel(x_ref, ya_ref, yb_ref, wa_ref, wb_ref, h_ref):
    acc = jnp.dot(ya_ref[...], wa_ref[...], preferred_element_type=F32)
    acc = acc + jnp.dot(yb_ref[...], wb_ref[...], preferred_element_type=F32)
    h_ref[...] = x_ref[...] + acc


def _even_out(x, ya, yb, w_out, tm):
    m = x.shape[0]
    row = lambda w: pl.BlockSpec((tm, w), lambda i: (i, 0))
    return pl.pallas_call(
        _even_out_kernel,
        grid=(m // tm,),
        in_specs=[row(D_MODEL), row(A_V), row(B_W),
                  pl.BlockSpec((A_V, D_MODEL), lambda i: (0, 0)),
                  pl.BlockSpec((B_W, D_MODEL), lambda i: (A_V // B_W, 0))],
        out_specs=row(D_MODEL),
        out_shape=jax.ShapeDtypeStruct((m, D_MODEL), F32),
        compiler_params=_params("arbitrary"),
        name="even_out",
    )(x, ya, yb, w_out, w_out)


FFN_TM = 512
FFN_R = 8
FFN_CHUNK = 256


def _ffn_kernel(tiles_per_seq, final, h_ref, g_ref, wup_ref, cw_ref, cb_ref, wd_ref, fg_ref, o_ref,
                slab_ref, hn_ref, act_ref, halo_ref):
    tm = h_ref.shape[0]
    rows = tm // FFN_R
    nslab = D_MODEL // LANES
    ck = FFN_CHUNK
    group = lambda g: slice(g * rows, (g + 1) * rows)

    for s in range(nslab):
        slab_ref[s] = h_ref[:, s * LANES:(s + 1) * LANES]
    for g in range(FFN_R):
        x = jnp.concatenate([slab_ref[s, pl.ds(g, rows, stride=FFN_R), :] for s in range(nslab)],
                            axis=1)
        ms = jnp.mean(x * x, axis=-1, keepdims=True)
        hn_ref[group(g), :] = (x * lax.rsqrt(ms + EPS) * g_ref[...]).astype(BF16)

    @pl.when((pl.program_id(0) % tiles_per_seq) == 0)
    def _():
        halo_ref[...] = jnp.zeros_like(halo_ref)

    first_row = lax.broadcasted_iota(jnp.int32, (rows, ck), 0) == 0

    def shift_down(block, fill):
        return jnp.where(first_row, fill, pltpu.roll(block, 1, 0))

    def conv(cols, slot):
        u = jnp.dot(hn_ref[...], wup_ref[:, cols], preferred_element_type=F32)
        ug = [u[group(g), :] for g in range(FFN_R)]
        tails = halo_ref[slot]
        prev2 = shift_down(ug[FFN_R - 2], tails[SUBLANES - 1:SUBLANES, :])
        prev1 = shift_down(ug[FFN_R - 1], tails[2 * SUBLANES - 1:2 * SUBLANES, :])
        halo_ref[slot] = jnp.concatenate(
            [ug[FFN_R - 2][rows - SUBLANES:, :], ug[FFN_R - 1][rows - SUBLANES:, :]], axis=0)
        back1 = [prev1] + ug[:FFN_R - 1]
        back2 = [prev2, prev1] + ug[:FFN_R - 2]
        cw = cw_ref[:, cols]
        cb = cb_ref[:, cols]
        return [cw[0:1, :] * back2[g] + cw[1:2, :] * back1[g] + cw[2:3, :] * ug[g] + cb
                for g in range(FFN_R)]

    for c in range(D_FF // ck):
        gate = conv(slice(c * ck, (c + 1) * ck), 2 * c)
        value = conv(slice(D_FF + c * ck, D_FF + (c + 1) * ck), 2 * c + 1)
        for g in range(FFN_R):
            act_ref[group(g), c * ck:(c + 1) * ck] = (_silu(gate[g]) * value[g]).astype(BF16)

    down = jnp.dot(act_ref[...], wd_ref[...], preferred_element_type=F32)
    for g in range(FFN_R):
        for s in range(nslab):
            slab_ref[s, pl.ds(g, rows, stride=FFN_R), :] = down[group(g), s * LANES:(s + 1) * LANES]
    y = h_ref[...] + jnp.concatenate([slab_ref[s] for s in range(nslab)], axis=1)
    if final:
        ms = jnp.mean(y * y, axis=-1, keepdims=True)
        y = y * lax.rsqrt(ms + EPS) * fg_ref[...]
    o_ref[...] = y


def _ffn(h, g, w_up, conv_w, conv_b, w_down, final_g, seq, final):
    m = h.shape[0]
    tm = FFN_TM
    assert D_FF % FFN_CHUNK == 0 and tm % (FFN_R * 2 * SUBLANES) == 0 and seq % tm == 0
    const = lambda shape: pl.BlockSpec(shape, lambda i: (0, 0), pipeline_mode=pl.Buffered(1))
    return pl.pallas_call(
        functools.partial(_ffn_kernel, seq // tm, final),
        grid=(m // tm,),
        in_specs=[
            pl.BlockSpec((tm, D_MODEL), lambda i: (i, 0)),
            const((1, D_MODEL)),
            const((D_MODEL, 2 * D_FF)), const((CONV_WIDTH, 2 * D_FF)), const((1, 2 * D_FF)),
            const((D_FF, D_MODEL)),
            const((1, D_MODEL)),
        ],
        out_specs=pl.BlockSpec((tm, D_MODEL), lambda i: (i, 0)),
        out_shape=jax.ShapeDtypeStruct((m, D_MODEL), F32),
        scratch_shapes=[
            pltpu.VMEM((D_MODEL // LANES, tm, LANES), F32),
            pltpu.VMEM((tm, D_MODEL), BF16),
            pltpu.VMEM((tm, D_FF), BF16),
            pltpu.VMEM((2 * D_FF // FFN_CHUNK, 2 * SUBLANES, FFN_CHUNK), F32),
        ],
        compiler_params=_params("arbitrary"),
        name="conv_ffn",
    )(h, g, w_up.astype(BF16), conv_w.astype(F32), conv_b.astype(F32).reshape(1, -1),
      w_down.astype(BF16), final_g)


HGRN_GROUP = 128
HGRN_TM = 256
HGRN_PROJ_COLS = 256


def _emit(fill, count):
    for _ in range(count):
        thunk = next(fill, None)
        if thunk is not None:
            thunk()


def _hgrn_group(proj_ref, rows, lower_all, gain, causal, tri, st, fill):
    c = HGRN_CHUNK
    heads = range(HGRN_HEADS)
    hk = [slice(h * HGRN_DK, (h + 1) * HGRN_DK) for h in heads]
    hv = [slice(h * HGRN_DV, (h + 1) * HGRN_DV) for h in heads]
    chunks = [slice(n * c, (n + 1) * c) for n in range(HGRN_GROUP // c)]

    q = _silu(proj_ref[rows, 0:C_K])
    fg = lower_all + (1.0 - lower_all) * jax.nn.sigmoid(proj_ref[rows, C_K:2 * C_K])
    k = 1.0 - fg
    log_f = jnp.log(fg)
    hi = log_f.astype(BF16)
    rem = log_f - hi.astype(F32)
    mid = rem.astype(BF16)
    lo = (rem - mid.astype(F32)).astype(BF16)
    _emit(fill, 2)
    parts = jnp.dot(tri, jnp.concatenate([hi, mid, lo], axis=1), preferred_element_type=F32)
    b = parts[:, 0:C_K] + parts[:, C_K:2 * C_K] + parts[:, 2 * C_K:3 * C_K]
    v = proj_ref[rows, 2 * C_K:2 * C_K + C_V].astype(BF16)
    q_t = (q * jnp.exp(b)).astype(BF16)
    k_t = (k * jnp.exp(-b)).astype(BF16)
    _emit(fill, 2)
    scores = [lax.dot_general(q_t[:, hk[h]], k_t[:, hk[h]], NT_DIMS, preferred_element_type=F32)
              for h in heads]
    scores = [jnp.where(causal, s, 0.0).astype(BF16) for s in scores]
    intra = [jnp.dot(scores[h], v[:, hv[h]], preferred_element_type=F32) for h in heads]
    b_last = [b[cs.stop - 1:cs.stop, :] for cs in chunks]
    k_end = [(k[cs, :] * jnp.exp(bl - b[cs, :])).astype(BF16) for cs, bl in zip(chunks, b_last)]
    decay = [jnp.exp(bl) for bl in b_last]
    _emit(fill, 2)
    st = list(st)
    inter = [[] for _ in heads]
    for n, cs in enumerate(chunks):
        for h in heads:
            inter[h].append(lax.dot_general(q_t[cs, hk[h]], st[h].astype(BF16), NT_DIMS,
                                            preferred_element_type=F32))
            st[h] = st[h] * decay[n][:, hk[h]] + lax.dot_general(
                v[cs, hv[h]], k_end[n][:, hk[h]], TN_DIMS, preferred_element_type=F32)
    gate = gain * _silu(proj_ref[rows, 2 * C_K + C_V:2 * C_K + 2 * C_V])
    _emit(fill, 2)
    outs = []
    for h in heads:
        y = intra[h] + jnp.concatenate(inter[h], axis=0)
        yn = y * lax.rsqrt(jnp.mean(y * y, axis=-1, keepdims=True) + EPS)
        outs.append((yn * gate[:, hv[h]]).astype(BF16))
    return outs, st


def _hgrn_layer_kernel(layer, tiles_per_seq, h0_ref, hnext_ref, g_ref, w_ref, lbp_ref, on_ref, o_ref,
                       proj_a, proj_b, st_ref):
    t = pl.program_id(0)
    tm = o_ref.shape[0]
    grp = HGRN_GROUP

    def projection(h_ref, dst):
        x = h_ref[...]
        ms = jnp.mean(x * x, axis=-1, keepdims=True)
        hn = (x * lax.rsqrt(ms + EPS) * g_ref[...]).astype(BF16)

        def piece(c):
            cols = slice(c * HGRN_PROJ_COLS, (c + 1) * HGRN_PROJ_COLS)

            def run():
                dst[:, cols] = jnp.dot(hn, w_ref[:, cols], preferred_element_type=F32)
            return run

        return [piece(c) for c in range(ODD_IN // HGRN_PROJ_COLS)]

    @pl.when(t == 0)
    def _():
        for run in projection(h0_ref, proj_a):
            run()

    @pl.when(t % tiles_per_seq == 0)
    def _():
        st_ref[...] = jnp.zeros_like(st_ref)

    def step(src, dst):
        fill = iter(projection(hnext_ref, dst))
        lbp = lbp_ref[...]
        ex = jnp.exp(lbp - jnp.max(lbp, axis=0, keepdims=True))
        sm = ex / jnp.sum(ex, axis=0, keepdims=True)
        lower_all = jnp.sum(sm[0:layer + 1, :], axis=0, keepdims=True) - sm[0:1, :]
        row = lax.broadcasted_iota(jnp.int32, (grp, grp), 0)
        colm = lax.broadcasted_iota(jnp.int32, (grp, grp), 1)
        causal = jnp.logical_and(row // HGRN_CHUNK == colm // HGRN_CHUNK, row >= colm)
        tri = jnp.where(causal, 1.0, 0.0).astype(BF16)
        gain = on_ref[...]
        st = [st_ref[h] for h in range(HGRN_HEADS)]
        for r0 in range(0, tm, grp):
            outs, st = _hgrn_group(src, slice(r0, r0 + grp), lower_all, gain, causal, tri, st, fill)
            for h, y in enumerate(outs):
                o_ref[r0:r0 + grp, h * HGRN_DV:(h + 1) * HGRN_DV] = y
        for run in fill:
            run()
        for h in range(HGRN_HEADS):
            st_ref[h] = st[h]

    @pl.when(t % 2 == 0)
    def _():
        step(proj_a, proj_b)

    @pl.when(t % 2 == 1)
    def _():
        step(proj_b, proj_a)


def _hgrn_layer(h, g, w, hgrn_lb, out_norm, layer, seq):
    tokens, d = h.shape
    tm = HGRN_TM
    ntiles = tokens // tm
    const = lambda shape: pl.BlockSpec(shape, lambda t: (0, 0), pipeline_mode=pl.Buffered(1))
    return pl.pallas_call(
        functools.partial(_hgrn_layer_kernel, layer, seq // tm),
        grid=(ntiles,),
        in_specs=[
            const((tm, d)),
            pl.BlockSpec((tm, d), lambda t: (jnp.minimum(t + 1, ntiles - 1), 0)),
            const((1, d)), const(w.shape), const(hgrn_lb.shape), const((1, C_V)),
        ],
        out_specs=pl.BlockSpec((tm, C_V), lambda t: (t, 0)),
        out_shape=jax.ShapeDtypeStruct((tokens, C_V), BF16),
        scratch_shapes=[pltpu.VMEM((tm, ODD_IN), F32), pltpu.VMEM((tm, ODD_IN), F32),
                        pltpu.VMEM((HGRN_HEADS, HGRN_DV, HGRN_DK), F32)],
        compiler_params=_params("arbitrary"),
        name="hgrn_layer",
    )(h, h, g, w, hgrn_lb, out_norm)


def _matmul_residual_kernel(h_ref, y_ref, w_ref, o_ref):
    o_ref[...] = h_ref[...] + jnp.dot(y_ref[...], w_ref[...], preferred_element_type=F32)


def _matmul_residual(h, y, w, tm):
    m, n = h.shape
    k = y.shape[1]
    return pl.pallas_call(
        _matmul_residual_kernel,
        grid=(m // tm,),
        in_specs=[
            pl.BlockSpec((tm, n), lambda i: (i, 0)),
            pl.BlockSpec((tm, k), lambda i: (i, 0)),
            pl.BlockSpec((k, n), lambda i: (0, 0)),
        ],
        out_specs=pl.BlockSpec((tm, n), lambda i: (i, 0)),
        out_shape=jax.ShapeDtypeStruct((m, n), F32),
        compiler_params=_params("arbitrary"),
        name="matmul_residual",
    )(h, y, w)


def kernel(x, even_w_in, even_w_out, ret_norm, rel_bias, odd_w_in, odd_w_out, hgrn_lb, hgrn_norm,
           mix_norm, ffn_norm, ffn_w_up, ffn_conv_w, ffn_conv_b, ffn_w_down, final_norm):
    bn, seq, d = x.shape
    tokens = bn * seq
    row = lambda a: a.reshape(1, -1).astype(F32)
    h = x.reshape(tokens, d)
    final_g = row(final_norm)

    outs = _even_in(h, row(mix_norm[0]), even_w_in[0].astype(BF16), seq)
    qa, qin, ka, kout, va, ga = outs[:6]
    qn, q4, q16, kn, k4, k16, vn, v4, v16 = outs[6:]
    ya = _retention(qa, qin, ka, kout, va, ga, row(ret_norm[0]), bn, seq)
    yb = _dilated((qn, kn, vn, q4, k4, v4, q16, k16, v16), rel_bias, bn, seq)
    h = _even_out(h, ya, yb.reshape(tokens, B_W), even_w_out[0].astype(BF16), tm=1024)
    h = _ffn(h, row(ffn_norm[0]), ffn_w_up[0], ffn_conv_w[0], ffn_conv_b[0], ffn_w_down[0], final_g,
             seq, final=False)

    y = _hgrn_layer(h, row(mix_norm[1]), odd_w_in[0].astype(BF16), hgrn_lb.astype(F32),
                    row(hgrn_norm[0]), 1, seq)
    h = _matmul_residual(h, y, odd_w_out[0].astype(BF16), tm=1024)
    h = _ffn(h, row(ffn_norm[1]), ffn_w_up[1], ffn_conv_w[1], ffn_conv_b[1], ffn_w_down[1], final_g,
             seq, final=True)
    return h.reshape(bn, seq, d)
```

```python
import functools
import math

import numpy as np
import jax
import jax.numpy as jnp
from jax import lax
from jax.experimental import pallas as pl
from jax.experimental.pallas import tpu as pltpu

F32 = jnp.float32
BF16 = jnp.bfloat16

D_MODEL = 1024
RET_HEADS = 8
RET_DK = 64
RET_DV = 128
RET_CHUNK = 128
ROPE_BASE = 10000.0
DIL_HEADS = 8
DIL_DH = 64
DIL_BRANCHES = ((128, 1), (512, 4), (2048, 16))
DIL_BLOCK = 128
HGRN_HEADS = 8
HGRN_DK = 128
HGRN_DV = 128
HGRN_CHUNK = 32
REL_BUCKETS = 32
REL_MAX_DIST = 2048
D_FF = 2816
CONV_WIDTH = 3
EPS = 1e-6

A_QK = RET_HEADS * RET_DK
A_V = RET_HEADS * RET_DV
B_W = DIL_HEADS * DIL_DH
EVEN_IN = 2 * A_QK + 2 * A_V + 3 * B_W
C_K = HGRN_HEADS * HGRN_DK
C_V = HGRN_HEADS * HGRN_DV
ODD_IN = 2 * C_K + 2 * C_V

LANES = 128
SUBLANES = 8
VMEM_LIMIT = 48 * 1024 * 1024
MASK_VALUE = -1e30

NT_DIMS = (((1,), (1,)), ((), ()))
TN_DIMS = (((0,), (0,)), ((), ()))


def _params(*semantics):
    return pltpu.CompilerParams(dimension_semantics=semantics, vmem_limit_bytes=VMEM_LIMIT)


def _silu(x):
    return x * jax.nn.sigmoid(x)


def _emit(fill, count):
    for _ in range(count):
        thunk = next(fill, None)
        if thunk is not None:
            thunk()


def _retention_tables(seq):
    h = jnp.arange(RET_HEADS, dtype=F32)
    log_g = jnp.log1p(-jnp.exp2(-5.0 - h))
    pos = jnp.arange(RET_CHUNK, dtype=F32)
    diff = pos[:, None] - pos[None, :]
    decay = jnp.where(diff >= 0, jnp.exp(jnp.maximum(diff, 0.0)[None] * log_g[:, None, None]), 0.0)
    q_scale = jnp.repeat(jnp.exp((pos + 1.0)[:, None] * log_g[None, :]), RET_DK, axis=1)
    k_scale = jnp.repeat(jnp.exp((RET_CHUNK - 1.0 - pos)[:, None] * log_g[None, :]), RET_DK, axis=1)
    chunk_decay = jnp.broadcast_to(jnp.exp(RET_CHUNK * log_g)[:, None, None], (RET_HEADS, 1, RET_DV))
    inv = ROPE_BASE ** (-jnp.arange(0, RET_DK, 2, dtype=F32) / RET_DK)
    ang = jnp.arange(seq, dtype=F32)[:, None] * inv[None, :]
    cos, sin = jnp.cos(ang), jnp.sin(ang)
    reps = LANES // RET_DK
    cos_t = jnp.tile(jnp.concatenate([cos, cos], axis=1), (1, reps))
    sin_t = jnp.tile(jnp.concatenate([-sin, sin], axis=1), (1, reps))
    return decay, q_scale, k_scale, chunk_decay, cos_t, sin_t


EVEN_TM = 512


def _even_in_kernel(tiles_per_seq, x_ref, g_ref, w_ref, cos_ref, sin_ref, qs_ref, ks_ref, dec_ref,
                    cd_ref, rn_ref, ya_ref, qn_ref, q4_ref, q16_ref, kn_ref, k4_ref, k16_ref,
                    vn_ref, v4_ref, v16_ref, slab_ref, st_ref):
    tm = x_ref.shape[0]
    chunk = RET_CHUNK

    @pl.when(pl.program_id(0) % tiles_per_seq == 0)
    def _():
        st_ref[...] = jnp.zeros_like(st_ref)

    x = x_ref[...]
    ms = jnp.mean(x * x, axis=-1, keepdims=True)
    hn = (x * lax.rsqrt(ms + EPS) * g_ref[...]).astype(BF16)

    def proj(c):
        return jnp.dot(hn, w_ref[:, c * B_W:(c + 1) * B_W], preferred_element_type=F32)

    cos = cos_ref[...]
    sin = sin_ref[...]
    lane = lax.broadcasted_iota(jnp.int32, (tm, LANES), 1)
    half = RET_DK // 2
    first_half = (lane % RET_DK) < half

    def rot(t):
        partner = jnp.where(first_half, pltpu.roll(t, LANES - half, 1), pltpu.roll(t, half, 1))
        return t * cos + partner * sin

    groups = [slice(s * LANES, (s + 1) * LANES) for s in range(B_W // LANES)]

    r = proj(0)
    q_rot = [rot(r[:, sl]) for sl in groups]
    q = [t.astype(BF16) for t in q_rot]
    q_in = [(t * qs_ref[:, sl]).astype(BF16) for t, sl in zip(q_rot, groups)]
    r = proj(1)
    k_rot = [rot(r[:, sl]) * (RET_DK ** -0.5) for sl in groups]
    k = [t.astype(BF16) for t in k_rot]
    k_out = [(t * ks_ref[:, sl]).astype(BF16) for t, sl in zip(k_rot, groups)]
    v = jnp.concatenate([proj(2), proj(3)], axis=1).astype(BF16)
    gate = rn_ref[...] * _silu(jnp.concatenate([proj(4), proj(5)], axis=1))

    def dilated_pieces(c, slot, nat_ref, r4_ref, r16_ref, scale):
        def project():
            r = proj(c)
            if scale is not None:
                r = r * scale
            nat_ref[...] = r.astype(BF16)
            for s, sl in enumerate(groups):
                slab_ref[slot, s] = r[:, sl]

        def by4():
            for s, sl in enumerate(groups):
                for rho in range(4):
                    r4_ref[rho, :, sl] = slab_ref[slot, s, pl.ds(rho, tm // 4, stride=4), :].astype(BF16)

        def by16():
            for s, sl in enumerate(groups):
                for rho in range(16):
                    r16_ref[rho, :, sl] = slab_ref[slot, s, pl.ds(rho, tm // 16, stride=16), :].astype(BF16)

        return [project, by4, by16]

    fill = iter(dilated_pieces(6, 0, qn_ref, q4_ref, q16_ref, DIL_DH ** -0.5)
                + dilated_pieces(7, 1, kn_ref, k4_ref, k16_ref, None)
                + dilated_pieces(8, 2, vn_ref, v4_ref, v16_ref, None))

    head0 = lax.broadcasted_iota(jnp.int32, (chunk, LANES), 1) < RET_DK
    heads = range(RET_HEADS)
    hs = [slice(h * RET_DV, (h + 1) * RET_DV) for h in heads]
    st = [st_ref[h] for h in heads]
    for ci in range(tm // chunk):
        rows = slice(ci * chunk, (ci + 1) * chunk)
        mask = [head0 if h % 2 == 0 else jnp.logical_not(head0) for h in heads]
        scores = [lax.dot_general(jnp.where(mask[h], q[h // 2][rows], 0), k[h // 2][rows], NT_DIMS,
                                  preferred_element_type=F32) * dec_ref[h] for h in heads]
        _emit(fill, 1)
        intra = [jnp.dot(scores[h].astype(BF16), v[rows, hs[h]], preferred_element_type=F32)
                 for h in heads]
        inter = [jnp.dot(jnp.where(mask[h], q_in[h // 2][rows], 0), st[h].astype(BF16),
                         preferred_element_type=F32) for h in heads]
        st = [st[h] * cd_ref[h] + lax.dot_general(k_out[h // 2][rows], v[rows, hs[h]], TN_DIMS,
                                                  preferred_element_type=F32) for h in heads]
        _emit(fill, 1)
        for h in heads:
            y = intra[h] + inter[h]
            yc = y - jnp.mean(y, axis=-1, keepdims=True)
            yn = yc * lax.rsqrt(jnp.mean(yc * yc, axis=-1, keepdims=True) + EPS)
            ya_ref[rows, hs[h]] = (yn * gate[rows, hs[h]]).astype(BF16)
        _emit(fill, 1)
    for run in fill:
        run()
    for h in heads:
        st_ref[h] = st[h]


def _even_in(x, g, w, ret_norm, seq):
    tokens, d = x.shape
    bn = tokens // seq
    tm = EVEN_TM
    tps = seq // tm
    decay, q_scale, k_scale, chunk_decay, cos_t, sin_t = _retention_tables(seq)
    q_scale = jnp.tile(q_scale, (tm // RET_CHUNK, 1))
    k_scale = jnp.tile(k_scale, (tm // RET_CHUNK, 1))
    const = lambda shape: pl.BlockSpec(shape, lambda i: (0,) * len(shape), pipeline_mode=pl.Buffered(1))
    flat = lambda w_: pl.BlockSpec((tm, w_), lambda i: (i, 0))
    nat = pl.BlockSpec((None, tm, B_W), lambda i: (i // tps, i % tps, 0))
    by = lambda r: pl.BlockSpec((None, r, tm // r, B_W), lambda i: (i // tps, 0, i % tps, 0))
    sds = jax.ShapeDtypeStruct
    lay = [sds((bn, seq, B_W), BF16), sds((bn, 4, seq // 4, B_W), BF16),
           sds((bn, 16, seq // 16, B_W), BF16)]
    return pl.pallas_call(
        functools.partial(_even_in_kernel, tps),
        grid=(tokens // tm,),
        in_specs=[
            flat(d), const((1, d)), const(w.shape),
            pl.BlockSpec((tm, LANES), lambda i: (i % tps, 0)),
            pl.BlockSpec((tm, LANES), lambda i: (i % tps, 0)),
            const((tm, A_QK)), const((tm, A_QK)),
            const(decay.shape), const(chunk_decay.shape), const((1, A_V)),
        ],
        out_specs=[flat(A_V)] + [nat, by(4), by(16)] * 3,
        out_shape=[sds((tokens, A_V), BF16)] + lay * 3,
        scratch_shapes=[pltpu.VMEM((3, B_W // LANES, tm, LANES), F32),
                        pltpu.VMEM((RET_HEADS, 2 * RET_DK, RET_DV), F32)],
        compiler_params=_params("arbitrary"),
        name="even_in",
    )(x, g, w, cos_t, sin_t, q_scale, k_scale, decay, chunk_decay, ret_norm)


def _rel_bucket(dist):
    max_exact = REL_BUCKETS // 2
    d = dist.astype(F32)
    large = max_exact + (jnp.log(jnp.maximum(d, 1.0) / max_exact) / math.log(REL_MAX_DIST / max_exact)
                         * (REL_BUCKETS - max_exact)).astype(jnp.int32)
    large = jnp.minimum(large, REL_BUCKETS - 1)
    return jnp.where(dist < max_exact, dist, large)


DIL_UNROLL = 8


def _dilated_buckets():
    lb = DIL_BLOCK
    a_idx = jnp.arange(lb)[:, None]
    c_idx = jnp.arange(2 * lb)[None, :]
    dist = lb + a_idx - c_idx
    tables = []
    for window, dilation in DIL_BRANCHES:
        valid = (dist >= 0) & (dist <= window // dilation)
        tables.append(jnp.where(valid, _rel_bucket(jnp.maximum(dist, 0) * dilation), -1))
    return jnp.stack(tables).astype(jnp.int32)


def _dilated_kernel(rb_ref, bucket_ref, qn_ref, kn_ref, vn_ref, q4_ref, k4_ref, v4_ref, q16_ref,
                    k16_ref, v16_ref, y_ref, bias_ref, acc_ref, m_ref, l_ref):
    lb = DIL_BLOCK
    seq = qn_ref.shape[0]
    pair = pl.program_id(1)

    @pl.when(pl.program_id(0) == 0)
    def _():
        for br in range(len(DIL_BRANCHES)):
            bucket = bucket_ref[br]
            for a in range(2):
                h = 2 * pair + a
                t = jnp.full((lb, 2 * lb), MASK_VALUE, F32)
                for j in range(REL_BUCKETS):
                    t = jnp.where(bucket == j, rb_ref[j, h], t)
                bias_ref[br * DIL_HEADS + h] = t

    lane = lax.broadcasted_iota(jnp.int32, (lb, LANES), 1)
    head0 = lane < DIL_DH
    prev_cols = lax.broadcasted_iota(jnp.int32, (lb, 2 * lb), 1) < lb

    def attend(br, first, q, kp, kc, vp, vc):
        k = jnp.concatenate([kp, kc], axis=0)
        v = jnp.concatenate([vp, vc], axis=0)
        no_prev = jnp.where(jnp.logical_and(prev_cols, first), MASK_VALUE, 0.0)
        parts = []
        for a in range(2):
            mask = head0 if a == 0 else jnp.logical_not(head0)
            s = lax.dot_general(jnp.where(mask, q, 0), k, NT_DIMS, preferred_element_type=F32)
            s = s + bias_ref[br * DIL_HEADS + 2 * pair + a] + no_prev
            m = jnp.max(s, axis=-1, keepdims=True)
            e = jnp.exp(s - m)
            parts.append((jnp.dot(e.astype(BF16), v, preferred_element_type=F32), m,
                          jnp.sum(e, axis=-1, keepdims=True)))
        return tuple(jnp.where(head0, x0, x1) for x0, x1 in zip(*parts))

    def natural_block(n, carry):
        r0 = pl.multiple_of(n * lb, lb)
        rp = pl.multiple_of(jnp.maximum(n - 1, 0) * lb, lb)
        cur, prev = pl.ds(r0, lb), pl.ds(rp, lb)
        acc, m, l = attend(0, n == 0, qn_ref[cur, :], kn_ref[prev, :], kn_ref[cur, :],
                           vn_ref[prev, :], vn_ref[cur, :])
        acc_ref[cur, :] = acc
        m_ref[cur, :] = m
        l_ref[cur, :] = l
        return carry

    lax.fori_loop(0, seq // lb, natural_block, 0, unroll=DIL_UNROLL)

    def dilated_blocks(br, r, q_ref, k_ref, v_ref):
        nb = seq // (r * lb)

        def body(idx, carry):
            rho, n = idx // nb, idx % nb
            r0 = pl.multiple_of(n * lb, lb)
            rp = pl.multiple_of(jnp.maximum(n - 1, 0) * lb, lb)
            cur, prev = pl.ds(r0, lb), pl.ds(rp, lb)
            acc, m, l = attend(br, n == 0, q_ref[rho, cur, :], k_ref[rho, prev, :], k_ref[rho, cur, :],
                               v_ref[rho, prev, :], v_ref[rho, cur, :])
            rows = pl.ds(n * (lb * r) + rho, lb, stride=r)
            m_old = m_ref[rows, :]
            m_new = jnp.maximum(m_old, m)
            c_old = jnp.exp(m_old - m_new)
            c_cur = jnp.exp(m - m_new)
            acc_ref[rows, :] = acc_ref[rows, :] * c_old + acc * c_cur
            l_ref[rows, :] = l_ref[rows, :] * c_old + l * c_cur
            m_ref[rows, :] = m_new
            return carry

        lax.fori_loop(0, r * nb, body, 0, unroll=DIL_UNROLL)

    dilated_blocks(1, DIL_BRANCHES[1][1], q4_ref, k4_ref, v4_ref)
    dilated_blocks(2, DIL_BRANCHES[2][1], q16_ref, k16_ref, v16_ref)

    def finish(n, carry):
        rows = pl.ds(pl.multiple_of(n * lb, lb), lb)
        y_ref[rows, :] = (acc_ref[rows, :] / l_ref[rows, :]).astype(y_ref.dtype)
        return carry

    lax.fori_loop(0, seq // lb, finish, 0)


def _dilated(layouts, rel_bias, bn, seq):
    lb = DIL_BLOCK
    assert all(w // r <= lb and seq % (r * lb) == 0 for w, r in DIL_BRANCHES)
    assert tuple(r for _, r in DIL_BRANCHES) == (1, 4, 16)
    nat = pl.BlockSpec((None, seq, LANES), lambda b, p: (b, 0, p))
    by = lambda r: pl.BlockSpec((None, r, seq // r, LANES), lambda b, p: (b, 0, 0, p))
    nbr = len(DIL_BRANCHES)
    return pl.pallas_call(
        _dilated_kernel,
        grid=(bn, DIL_HEADS // 2),
        in_specs=[pl.BlockSpec(memory_space=pltpu.SMEM),
                  pl.BlockSpec((nbr, lb, 2 * lb), lambda b, p: (0, 0, 0))]
        + [nat] * 3 + [by(4)] * 3 + [by(16)] * 3,
        out_specs=nat,
        out_shape=jax.ShapeDtypeStruct((bn, seq, B_W), BF16),
        scratch_shapes=[pltpu.VMEM((nbr * DIL_HEADS, lb, 2 * lb), F32),
                        pltpu.VMEM((seq, LANES), F32), pltpu.VMEM((seq, LANES), F32),
                        pltpu.VMEM((seq, LANES), F32)],
        compiler_params=_params("arbitrary", "arbitrary"),
        name="dilated",
    )(rel_bias.astype(F32), _dilated_buckets(), *layouts)


def _even_out_kernel(x_ref, ya_ref, yb_ref, wa_ref, wb_ref, h_ref):
    acc = jnp.dot(ya_ref[...], wa_ref[...], preferred_element_type=F32)
    acc = acc + jnp.dot(yb_ref[...], wb_ref[...], preferred_element_type=F32)
    h_ref[...] = x_ref[...] + acc


def _even_out(x, ya, yb, w_out, tm):
    m = x.shape[0]
    row = lambda w: pl.BlockSpec((tm, w), lambda i: (i, 0))
    return pl.pallas_call(
        _even_out_kernel,
        grid=(m // tm,),
        in_specs=[row(D_MODEL), row(A_V), row(B_W),
                  pl.BlockSpec((A_V, D_MODEL), lambda i: (0, 0)),
                  pl.BlockSpec((B_W, D_MODEL), lambda i: (A_V // B_W, 0))],
        out_specs=row(D_MODEL),
        out_shape=jax.ShapeDtypeStruct((m, D_MODEL), F32),
        compiler_params=_params("arbitrary"),
        name="even_out",
    )(x, ya, yb, w_out, w_out)


FFN_TM = 512
FFN_R = 8
FFN_CHUNK = 256


def _ffn_kernel(tiles_per_seq, final, h_ref, g_ref, wup_ref, cw_ref, cb_ref, wd_ref, fg_ref, o_ref,
                slab_ref, hn_ref, act_ref, halo_ref):
    tm = h_ref.shape[0]
    rows = tm // FFN_R
    nslab = D_MODEL // LANES
    ck = FFN_CHUNK
    group = lambda g: slice(g * rows, (g + 1) * rows)

    for s in range(nslab):
        slab_ref[s] = h_ref[:, s * LANES:(s + 1) * LANES]
    for g in range(FFN_R):
        x = jnp.concatenate([slab_ref[s, pl.ds(g, rows, stride=FFN_R), :] for s in range(nslab)],
                            axis=1)
        ms = jnp.mean(x * x, axis=-1, keepdims=True)
        hn_ref[group(g), :] = (x * lax.rsqrt(ms + EPS) * g_ref[...]).astype(BF16)

    @pl.when((pl.program_id(0) % tiles_per_seq) == 0)
    def _():
        halo_ref[...] = jnp.zeros_like(halo_ref)

    first_row = lax.broadcasted_iota(jnp.int32, (rows, ck), 0) == 0

    def shift_down(block, fill):
        return jnp.where(first_row, fill, pltpu.roll(block, 1, 0))

    def conv(cols, slot):
        u = jnp.dot(hn_ref[...], wup_ref[:, cols], preferred_element_type=F32)
        ug = [u[group(g), :] for g in range(FFN_R)]
        tails = halo_ref[slot]
        prev2 = shift_down(ug[FFN_R - 2], tails[SUBLANES - 1:SUBLANES, :])
        prev1 = shift_down(ug[FFN_R - 1], tails[2 * SUBLANES - 1:2 * SUBLANES, :])
        halo_ref[slot] = jnp.concatenate(
            [ug[FFN_R - 2][rows - SUBLANES:, :], ug[FFN_R - 1][rows - SUBLANES:, :]], axis=0)
        back1 = [prev1] + ug[:FFN_R - 1]
        back2 = [prev2, prev1] + ug[:FFN_R - 2]
        cw = cw_ref[:, cols]
        cb = cb_ref[:, cols]
        return [cw[0:1, :] * back2[g] + cw[1:2, :] * back1[g] + cw[2:3, :] * ug[g] + cb
                for g in range(FFN_R)]

    for c in range(D_FF // ck):
        gate = conv(slice(c * ck, (c + 1) * ck), 2 * c)
        value = conv(slice(D_FF + c * ck, D_FF + (c + 1) * ck), 2 * c + 1)
        for g in range(FFN_R):
            act_ref[group(g), c * ck:(c + 1) * ck] = (_silu(gate[g]) * value[g]).astype(BF16)

    down = jnp.dot(act_ref[...], wd_ref[...], preferred_element_type=F32)
    for g in range(FFN_R):
        for s in range(nslab):
            slab_ref[s, pl.ds(g, rows, stride=FFN_R), :] = down[group(g), s * LANES:(s + 1) * LANES]
    y = h_ref[...] + jnp.concatenate([slab_ref[s] for s in range(nslab)], axis=1)
    if final:
        ms = jnp.mean(y * y, axis=-1, keepdims=True)
        y = y * lax.rsqrt(ms + EPS) * fg_ref[...]
    o_ref[...] = y


def _ffn(h, g, w_up, conv_w, conv_b, w_down, final_g, seq, final):
    m = h.shape[0]
    tm = FFN_TM
    assert D_FF % FFN_CHUNK == 0 and tm % (FFN_R * 2 * SUBLANES) == 0 and seq % tm == 0
    const = lambda shape: pl.BlockSpec(shape, lambda i: (0, 0), pipeline_mode=pl.Buffered(1))
    return pl.pallas_call(
        functools.partial(_ffn_kernel, seq // tm, final),
        grid=(m // tm,),
        in_specs=[
            pl.BlockSpec((tm, D_MODEL), lambda i: (i, 0)),
            const((1, D_MODEL)),
            const((D_MODEL, 2 * D_FF)), const((CONV_WIDTH, 2 * D_FF)), const((1, 2 * D_FF)),
            const((D_FF, D_MODEL)),
            const((1, D_MODEL)),
        ],
        out_specs=pl.BlockSpec((tm, D_MODEL), lambda i: (i, 0)),
        out_shape=jax.ShapeDtypeStruct((m, D_MODEL), F32),
        scratch_shapes=[
            pltpu.VMEM((D_MODEL // LANES, tm, LANES), F32),
            pltpu.VMEM((tm, D_MODEL), BF16),
            pltpu.VMEM((tm, D_FF), BF16),
            pltpu.VMEM((2 * D_FF // FFN_CHUNK, 2 * SUBLANES, FFN_CHUNK), F32),
        ],
        compiler_params=_params("arbitrary"),
        name="conv_ffn",
    )(h, g, w_up.astype(BF16), conv_w.astype(F32), conv_b.astype(F32).reshape(1, -1),
      w_down.astype(BF16), final_g)


HGRN_GROUP = 128
HGRN_TM = 256
HGRN_PROJ_COLS = 256


def _hgrn_group(proj_ref, rows, lower_all, gain, causal, tri, st, fill):
    c = HGRN_CHUNK
    heads = range(HGRN_HEADS)
    hk = [slice(h * HGRN_DK, (h + 1) * HGRN_DK) for h in heads]
    hv = [slice(h * HGRN_DV, (h + 1) * HGRN_DV) for h in heads]
    chunks = [slice(n * c, (n + 1) * c) for n in range(HGRN_GROUP // c)]

    q = _silu(proj_ref[rows, 0:C_K])
    fg = lower_all + (1.0 - lower_all) * jax.nn.sigmoid(proj_ref[rows, C_K:2 * C_K])
    k = 1.0 - fg
    log_f = jnp.log(fg)
    hi = log_f.astype(BF16)
    rem = log_f - hi.astype(F32)
    mid = rem.astype(BF16)
    lo = (rem - mid.astype(F32)).astype(BF16)
    _emit(fill, 2)
    parts = jnp.dot(tri, jnp.concatenate([hi, mid, lo], axis=1), preferred_element_type=F32)
    b = parts[:, 0:C_K] + parts[:, C_K:2 * C_K] + parts[:, 2 * C_K:3 * C_K]
    v = proj_ref[rows, 2 * C_K:2 * C_K + C_V].astype(BF16)
    q_t = (q * jnp.exp(b)).astype(BF16)
    k_t = (k * jnp.exp(-b)).astype(BF16)
    _emit(fill, 2)
    scores = [lax.dot_general(q_t[:, hk[h]], k_t[:, hk[h]], NT_DIMS, preferred_element_type=F32)
              for h in heads]
    scores = [jnp.where(causal, s, 0.0).astype(BF16) for s in scores]
    intra = [jnp.dot(scores[h], v[:, hv[h]], preferred_element_type=F32) for h in heads]
    b_last = [b[cs.stop - 1:cs.stop, :] for cs in chunks]
    k_end = [(k[cs, :] * jnp.exp(bl - b[cs, :])).astype(BF16) for cs, bl in zip(chunks, b_last)]
    decay = [jnp.exp(bl) for bl in b_last]
    _emit(fill, 2)
    st = list(st)
    inter = [[] for _ in heads]
    for n, cs in enumerate(chunks):
        for h in heads:
            inter[h].append(lax.dot_general(q_t[cs, hk[h]], st[h].astype(BF16), NT_DIMS,
                                            preferred_element_type=F32))
            st[h] = st[h] * decay[n][:, hk[h]] + lax.dot_general(
                v[cs, hv[h]], k_end[n][:, hk[h]], TN_DIMS, preferred_element_type=F32)
    gate = gain * _silu(proj_ref[rows, 2 * C_K + C_V:2 * C_K + 2 * C_V])
    _emit(fill, 2)
    outs = []
    for h in heads:
        y = intra[h] + jnp.concatenate(inter[h], axis=0)
        yn = y * lax.rsqrt(jnp.mean(y * y, axis=-1, keepdims=True) + EPS)
        outs.append((yn * gate[:, hv[h]]).astype(BF16))
    return outs, st


def _hgrn_layer_kernel(layer, tiles_per_seq, h0_ref, hnext_ref, g_ref, w_ref, lbp_ref, on_ref, o_ref,
                       proj_a, proj_b, st_ref):
    t = pl.program_id(0)
    tm = o_ref.shape[0]
    grp = HGRN_GROUP

    def projection(h_ref, dst):
        x = h_ref[...]
        ms = jnp.mean(x * x, axis=-1, keepdims=True)
        hn = (x * lax.rsqrt(ms + EPS) * g_ref[...]).astype(BF16)

        def piece(c):
            cols = slice(c * HGRN_PROJ_COLS, (c + 1) * HGRN_PROJ_COLS)

            def run():
                dst[:, cols] = jnp.dot(hn, w_ref[:, cols], preferred_element_type=F32)
            return run

        return [piece(c) for c in range(ODD_IN // HGRN_PROJ_COLS)]

    @pl.when(t == 0)
    def _():
        for run in projection(h0_ref, proj_a):
            run()

    @pl.when(t % tiles_per_seq == 0)
    def _():
        st_ref[...] = jnp.zeros_like(st_ref)

    def step(src, dst):
        fill = iter(projection(hnext_ref, dst))
        lbp = lbp_ref[...]
        ex = jnp.exp(lbp - jnp.max(lbp, axis=0, keepdims=True))
        sm = ex / jnp.sum(ex, axis=0, keepdims=True)
        lower_all = jnp.sum(sm[0:layer + 1, :], axis=0, keepdims=True) - sm[0:1, :]
        row = lax.broadcasted_iota(jnp.int32, (grp, grp), 0)
        colm = lax.broadcasted_iota(jnp.int32, (grp, grp), 1)
        causal = jnp.logical_and(row // HGRN_CHUNK == colm // HGRN_CHUNK, row >= colm)
        tri = jnp.where(causal, 1.0, 0.0).astype(BF16)
        gain = on_ref[...]
        st = [st_ref[h] for h in range(HGRN_HEADS)]
        for r0 in range(0, tm, grp):
            outs, st = _hgrn_group(src, slice(r0, r0 + grp), lower_all, gain, causal, tri, st, fill)
            for h, y in enumerate(outs):
                o_ref[r0:r0 + grp, h * HGRN_DV:(h + 1) * HGRN_DV] = y
        for run in fill:
            run()
        for h in range(HGRN_HEADS):
            st_ref[h] = st[h]

    @pl.when(t % 2 == 0)
    def _():
        step(proj_a, proj_b)

    @pl.when(t % 2 == 1)
    def _():
        step(proj_b, proj_a)


def _hgrn_layer(h, g, w, hgrn_lb, out_norm, layer, seq):
    tokens, d = h.shape
    tm = HGRN_TM
    ntiles = tokens // tm
    const = lambda shape: pl.BlockSpec(shape, lambda t: (0, 0), pipeline_mode=pl.Buffered(1))
    return pl.pallas_call(
        functools.partial(_hgrn_layer_kernel, layer, seq // tm),
        grid=(ntiles,),
        in_specs=[
            const((tm, d)),
            pl.BlockSpec((tm, d), lambda t: (jnp.minimum(t + 1, ntiles - 1), 0)),
            const((1, d)), const(w.shape), const(hgrn_lb.shape), const((1, C_V)),
        ],
        out_specs=pl.BlockSpec((tm, C_V), lambda t: (t, 0)),
        out_shape=jax.ShapeDtypeStruct((tokens, C_V), BF16),
        scratch_shapes=[pltpu.VMEM((tm, ODD_IN), F32), pltpu.VMEM((tm, ODD_IN), F32),
                        pltpu.VMEM((HGRN_HEADS, HGRN_DV, HGRN_DK), F32)],
        compiler_params=_params("arbitrary"),
        name="hgrn_layer",
    )(h, h, g, w, hgrn_lb, out_norm)


def _matmul_residual_kernel(h_ref, y_ref, w_ref, o_ref):
    o_ref[...] = h_ref[...] + jnp.dot(y_ref[...], w_ref[...], preferred_element_type=F32)


def _matmul_residual(h, y, w, tm):
    m, n = h.shape
    k = y.shape[1]
    return pl.pallas_call(
        _matmul_residual_kernel,
        grid=(m // tm,),
        in_specs=[
            pl.BlockSpec((tm, n), lambda i: (i, 0)),
            pl.BlockSpec((tm, k), lambda i: (i, 0)),
            pl.BlockSpec((k, n), lambda i: (0, 0)),
        ],
        out_specs=pl.BlockSpec((tm, n), lambda i: (i, 0)),
        out_shape=jax.ShapeDtypeStruct((m, n), F32),
        compiler_params=_params("arbitrary"),
        name="matmul_residual",
    )(h, y, w)


def kernel(x, even_w_in, even_w_out, ret_norm, rel_bias, odd_w_in, odd_w_out, hgrn_lb, hgrn_norm,
           mix_norm, ffn_norm, ffn_w_up, ffn_conv_w, ffn_conv_b, ffn_w_down, final_norm):
    bn, seq, d = x.shape
    tokens = bn * seq
    row = lambda a: a.reshape(1, -1).astype(F32)
    h = x.reshape(tokens, d)
    final_g = row(final_norm)

    ya, qn, q4, q16, kn, k4, k16, vn, v4, v16 = _even_in(h, row(mix_norm[0]), even_w_in[0].astype(BF16),
                                                         row(ret_norm[0]), seq)
    yb = _dilated((qn, kn, vn, q4, k4, v4, q16, k16, v16), rel_bias, bn, seq)
    h = _even_out(h, ya, yb.reshape(tokens, B_W), even_w_out[0].astype(BF16), tm=1024)
    h = _ffn(h, row(ffn_norm[0]), ffn_w_up[0], ffn_conv_w[0], ffn_conv_b[0], ffn_w_down[0], final_g,
             seq, final=False)

    y = _hgrn_layer(h, row(mix_norm[1]), odd_w_in[0].astype(BF16), hgrn_lb.astype(F32),
                    row(hgrn_norm[0]), 1, seq)
    h = _matmul_residual(h, y, odd_w_out[0].astype(BF16), tm=1024)
    h = _ffn(h, row(ffn_norm[1]), ffn_w_up[1], ffn_conv_w[1], ffn_conv_b[1], ffn_w_down[1], final_g,
             seq, final=True)
    return h.reshape(bn, seq, d)
```

```python
import functools
import math

import numpy as np
import jax
import jax.numpy as jnp
from jax import lax
from jax.experimental import pallas as pl
from jax.experimental.pallas import tpu as pltpu

F32 = jnp.float32
BF16 = jnp.bfloat16

D_MODEL = 1024
RET_HEADS = 8
RET_DK = 64
RET_DV = 128
RET_CHUNK = 128
ROPE_BASE = 10000.0
DIL_HEADS = 8
DIL_DH = 64
DIL_BRANCHES = ((128, 1), (512, 4), (2048, 16))
DIL_BLOCK = 128
HGRN_HEADS = 8
HGRN_DK = 128
HGRN_DV = 128
HGRN_CHUNK = 32
REL_BUCKETS = 32
REL_MAX_DIST = 2048
D_FF = 2816
CONV_WIDTH = 3
EPS = 1e-6

A_QK = RET_HEADS * RET_DK
A_V = RET_HEADS * RET_DV
B_W = DIL_HEADS * DIL_DH
EVEN_IN = 2 * A_QK + 2 * A_V + 3 * B_W
C_K = HGRN_HEADS * HGRN_DK
C_V = HGRN_HEADS * HGRN_DV
ODD_IN = 2 * C_K + 2 * C_V

LANES = 128
SUBLANES = 8
VMEM_LIMIT = 48 * 1024 * 1024
MASK_VALUE = -1e30

NT_DIMS = (((1,), (1,)), ((), ()))
TN_DIMS = (((0,), (0,)), ((), ()))


def _params(*semantics):
    return pltpu.CompilerParams(dimension_semantics=semantics, vmem_limit_bytes=VMEM_LIMIT)


def _silu(x):
    return x * jax.nn.sigmoid(x)


def _emit(fill, count):
    for _ in range(count):
        thunk = next(fill, None)
        if thunk is not None:
            thunk()


BF16_SUBLANES = 2 * SUBLANES


def _rider_specs(weights, steps):
    in_specs, out_specs, out_shapes = [], [], []
    for arr, layer in weights:
        _, rows, cols = arr.shape
        rep = 1
        while (rows * rep) % (steps * BF16_SUBLANES):
            rep *= 2
        slab = rows * rep // steps
        in_specs.append(pl.BlockSpec((None, slab, cols), lambda i, layer=layer, rep=rep: (layer, i // rep, 0)))
        out_specs.append(pl.BlockSpec((slab, cols), lambda i, rep=rep: (i // rep, 0)))
        out_shapes.append(jax.ShapeDtypeStruct((rows, cols), BF16))
    return in_specs, out_specs, out_shapes


def _with_riders(body, n_in, n_out, n_riders):
    def kernel_fn(*refs):
        ins, refs = refs[:n_in], refs[n_in:]
        rider_in, refs = refs[:n_riders], refs[n_riders:]
        outs, refs = refs[:n_out], refs[n_out:]
        rider_out, scratch = refs[:n_riders], refs[n_riders:]
        for src, dst in zip(rider_in, rider_out):
            dst[...] = src[...].astype(dst.dtype)
        body(*ins, *outs, *scratch)
    return kernel_fn


def _retention_tables(seq):
    h = jnp.arange(RET_HEADS, dtype=F32)
    log_g = jnp.log1p(-jnp.exp2(-5.0 - h))
    pos = jnp.arange(RET_CHUNK, dtype=F32)
    diff = pos[:, None] - pos[None, :]
    decay = jnp.where(diff >= 0, jnp.exp(jnp.maximum(diff, 0.0)[None] * log_g[:, None, None]), 0.0)
    q_scale = jnp.repeat(jnp.exp((pos + 1.0)[:, None] * log_g[None, :]), RET_DK, axis=1)
    k_scale = jnp.repeat(jnp.exp((RET_CHUNK - 1.0 - pos)[:, None] * log_g[None, :]), RET_DK, axis=1)
    chunk_decay = jnp.broadcast_to(jnp.exp(RET_CHUNK * log_g)[:, None, None], (RET_HEADS, 1, RET_DV))
    inv = ROPE_BASE ** (-jnp.arange(0, RET_DK, 2, dtype=F32) / RET_DK)
    ang = jnp.arange(seq, dtype=F32)[:, None] * inv[None, :]
    cos, sin = jnp.cos(ang), jnp.sin(ang)
    reps = LANES // RET_DK
    cos_t = jnp.tile(jnp.concatenate([cos, cos], axis=1), (1, reps))
    sin_t = jnp.tile(jnp.concatenate([-sin, sin], axis=1), (1, reps))
    return decay, q_scale, k_scale, chunk_decay, cos_t, sin_t


EVEN_TM = 512


def _even_in_kernel(tiles_per_seq, x_ref, g_ref, w_ref, cos_ref, sin_ref, qs_ref, ks_ref, dec_ref,
                    cd_ref, rn_ref, ya_ref, qn_ref, q4_ref, q16_ref, kn_ref, k4_ref, k16_ref,
                    vn_ref, v4_ref, v16_ref, slab_ref, st_ref):
    tm = x_ref.shape[0]
    chunk = RET_CHUNK

    @pl.when(pl.program_id(0) % tiles_per_seq == 0)
    def _():
        st_ref[...] = jnp.zeros_like(st_ref)

    x = x_ref[...]
    ms = jnp.mean(x * x, axis=-1, keepdims=True)
    hn = (x * lax.rsqrt(ms + EPS) * g_ref[...]).astype(BF16)

    def proj(c):
        return jnp.dot(hn, w_ref[:, c * B_W:(c + 1) * B_W], preferred_element_type=F32)

    cos = cos_ref[...]
    sin = sin_ref[...]
    lane = lax.broadcasted_iota(jnp.int32, (tm, LANES), 1)
    half = RET_DK // 2
    first_half = (lane % RET_DK) < half

    def rot(t):
        partner = jnp.where(first_half, pltpu.roll(t, LANES - half, 1), pltpu.roll(t, half, 1))
        return t * cos + partner * sin

    groups = [slice(s * LANES, (s + 1) * LANES) for s in range(B_W // LANES)]

    r = proj(0)
    q_rot = [rot(r[:, sl]) for sl in groups]
    q = [t.astype(BF16) for t in q_rot]
    q_in = [(t * qs_ref[:, sl]).astype(BF16) for t, sl in zip(q_rot, groups)]
    r = proj(1)
    k_rot = [rot(r[:, sl]) * (RET_DK ** -0.5) for sl in groups]
    k = [t.astype(BF16) for t in k_rot]
    k_out = [(t * ks_ref[:, sl]).astype(BF16) for t, sl in zip(k_rot, groups)]
    v = jnp.concatenate([proj(2), proj(3)], axis=1).astype(BF16)
    gate = rn_ref[...] * _silu(jnp.concatenate([proj(4), proj(5)], axis=1))

    def dilated_pieces(c, slot, nat_ref, r4_ref, r16_ref, scale):
        def project():
            r = proj(c)
            if scale is not None:
                r = r * scale
            nat_ref[...] = r.astype(BF16)
            for s, sl in enumerate(groups):
                slab_ref[slot, s] = r[:, sl]

        def by4():
            for s, sl in enumerate(groups):
                for rho in range(4):
                    r4_ref[rho, :, sl] = slab_ref[slot, s, pl.ds(rho, tm // 4, stride=4), :].astype(BF16)

        def by16():
            for s, sl in enumerate(groups):
                for rho in range(16):
                    r16_ref[rho, :, sl] = slab_ref[slot, s, pl.ds(rho, tm // 16, stride=16), :].astype(BF16)

        return [project, by4, by16]

    fill = iter(dilated_pieces(6, 0, qn_ref, q4_ref, q16_ref, DIL_DH ** -0.5)
                + dilated_pieces(7, 1, kn_ref, k4_ref, k16_ref, None)
                + dilated_pieces(8, 2, vn_ref, v4_ref, v16_ref, None))

    head0 = lax.broadcasted_iota(jnp.int32, (chunk, LANES), 1) < RET_DK
    heads = range(RET_HEADS)
    hs = [slice(h * RET_DV, (h + 1) * RET_DV) for h in heads]
    st = [st_ref[h] for h in heads]
    for ci in range(tm // chunk):
        rows = slice(ci * chunk, (ci + 1) * chunk)
        mask = [head0 if h % 2 == 0 else jnp.logical_not(head0) for h in heads]
        scores = [lax.dot_general(jnp.where(mask[h], q[h // 2][rows], 0), k[h // 2][rows], NT_DIMS,
                                  preferred_element_type=F32) * dec_ref[h] for h in heads]
        _emit(fill, 1)
        intra = [jnp.dot(scores[h].astype(BF16), v[rows, hs[h]], preferred_element_type=F32)
                 for h in heads]
        inter = [jnp.dot(jnp.where(mask[h], q_in[h // 2][rows], 0), st[h].astype(BF16),
                         preferred_element_type=F32) for h in heads]
        st = [st[h] * cd_ref[h] + lax.dot_general(k_out[h // 2][rows], v[rows, hs[h]], TN_DIMS,
                                                  preferred_element_type=F32) for h in heads]
        _emit(fill, 1)
        for h in heads:
            y = intra[h] + inter[h]
            yc = y - jnp.mean(y, axis=-1, keepdims=True)
            yn = yc * lax.rsqrt(jnp.mean(yc * yc, axis=-1, keepdims=True) + EPS)
            ya_ref[rows, hs[h]] = (yn * gate[rows, hs[h]]).astype(BF16)
        _emit(fill, 1)
    for run in fill:
        run()
    for h in heads:
        st_ref[h] = st[h]


def _even_in(x, g, w, ret_norm, seq, riders):
    tokens, d = x.shape
    bn = tokens // seq
    tm = EVEN_TM
    tps = seq // tm
    decay, q_scale, k_scale, chunk_decay, cos_t, sin_t = _retention_tables(seq)
    q_scale = jnp.tile(q_scale, (tm // RET_CHUNK, 1))
    k_scale = jnp.tile(k_scale, (tm // RET_CHUNK, 1))
    const = lambda shape: pl.BlockSpec(shape, lambda i: (0,) * len(shape), pipeline_mode=pl.Buffered(1))
    flat = lambda w_: pl.BlockSpec((tm, w_), lambda i: (i, 0))
    nat = pl.BlockSpec((None, tm, B_W), lambda i: (i // tps, i % tps, 0))
    by = lambda r: pl.BlockSpec((None, r, tm // r, B_W), lambda i: (i // tps, 0, i % tps, 0))
    sds = jax.ShapeDtypeStruct
    lay = [sds((bn, seq, B_W), BF16), sds((bn, 4, seq // 4, B_W), BF16),
           sds((bn, 16, seq // 16, B_W), BF16)]
    steps = tokens // tm
    in_specs = [
        flat(d), const((1, d)), const(w.shape),
        pl.BlockSpec((tm, LANES), lambda i: (i % tps, 0)),
        pl.BlockSpec((tm, LANES), lambda i: (i % tps, 0)),
        const((tm, A_QK)), const((tm, A_QK)),
        const(decay.shape), const(chunk_decay.shape), const((1, A_V)),
    ]
    out_specs = [flat(A_V)] + [nat, by(4), by(16)] * 3
    rider_in, rider_out, rider_shapes = _rider_specs(riders, steps)
    return pl.pallas_call(
        _with_riders(functools.partial(_even_in_kernel, tps), len(in_specs), len(out_specs), len(riders)),
        grid=(steps,),
        in_specs=in_specs + rider_in,
        out_specs=out_specs + rider_out,
        out_shape=[sds((tokens, A_V), BF16)] + lay * 3 + rider_shapes,
        scratch_shapes=[pltpu.VMEM((3, B_W // LANES, tm, LANES), F32),
                        pltpu.VMEM((RET_HEADS, 2 * RET_DK, RET_DV), F32)],
        compiler_params=_params("arbitrary"),
        name="even_in",
    )(x, g, w, cos_t, sin_t, q_scale, k_scale, decay, chunk_decay, ret_norm, *[a for a, _ in riders])


def _rel_bucket(dist):
    max_exact = REL_BUCKETS // 2
    d = dist.astype(F32)
    large = max_exact + (jnp.log(jnp.maximum(d, 1.0) / max_exact) / math.log(REL_MAX_DIST / max_exact)
                         * (REL_BUCKETS - max_exact)).astype(jnp.int32)
    large = jnp.minimum(large, REL_BUCKETS - 1)
    return jnp.where(dist < max_exact, dist, large)


DIL_UNROLL = 8


def _dilated_buckets():
    lb = DIL_BLOCK
    a_idx = jnp.arange(lb)[:, None]
    c_idx = jnp.arange(2 * lb)[None, :]
    dist = lb + a_idx - c_idx
    tables = []
    for window, dilation in DIL_BRANCHES:
        valid = (dist >= 0) & (dist <= window // dilation)
        tables.append(jnp.where(valid, _rel_bucket(jnp.maximum(dist, 0) * dilation), -1))
    return jnp.stack(tables).astype(jnp.int32)


def _dilated_kernel(rb_ref, bucket_ref, qn_ref, kn_ref, vn_ref, q4_ref, k4_ref, v4_ref, q16_ref,
                    k16_ref, v16_ref, y_ref, bias_ref, acc_ref, m_ref, l_ref):
    lb = DIL_BLOCK
    seq = qn_ref.shape[0]
    pair = pl.program_id(1)

    @pl.when(pl.program_id(0) == 0)
    def _():
        for br in range(len(DIL_BRANCHES)):
            bucket = bucket_ref[br]
            for a in range(2):
                h = 2 * pair + a
                t = jnp.full((lb, 2 * lb), MASK_VALUE, F32)
                for j in range(REL_BUCKETS):
                    t = jnp.where(bucket == j, rb_ref[j, h], t)
                bias_ref[br * DIL_HEADS + h] = t

    lane = lax.broadcasted_iota(jnp.int32, (lb, LANES), 1)
    head0 = lane < DIL_DH
    prev_cols = lax.broadcasted_iota(jnp.int32, (lb, 2 * lb), 1) < lb

    def attend(br, first, q, kp, kc, vp, vc):
        k = jnp.concatenate([kp, kc], axis=0)
        v = jnp.concatenate([vp, vc], axis=0)
        no_prev = jnp.where(jnp.logical_and(prev_cols, first), MASK_VALUE, 0.0)
        parts = []
        for a in range(2):
            mask = head0 if a == 0 else jnp.logical_not(head0)
            s = lax.dot_general(jnp.where(mask, q, 0), k, NT_DIMS, preferred_element_type=F32)
            s = s + bias_ref[br * DIL_HEADS + 2 * pair + a] + no_prev
            m = jnp.max(s, axis=-1, keepdims=True)
            e = jnp.exp(s - m)
            parts.append((jnp.dot(e.astype(BF16), v, preferred_element_type=F32), m,
                          jnp.sum(e, axis=-1, keepdims=True)))
        return tuple(jnp.where(head0, x0, x1) for x0, x1 in zip(*parts))

    def natural_block(n, carry):
        r0 = pl.multiple_of(n * lb, lb)
        rp = pl.multiple_of(jnp.maximum(n - 1, 0) * lb, lb)
        cur, prev = pl.ds(r0, lb), pl.ds(rp, lb)
        acc, m, l = attend(0, n == 0, qn_ref[cur, :], kn_ref[prev, :], kn_ref[cur, :],
                           vn_ref[prev, :], vn_ref[cur, :])
        acc_ref[cur, :] = acc
        m_ref[cur, :] = m
        l_ref[cur, :] = l
        return carry

    lax.fori_loop(0, seq // lb, natural_block, 0, unroll=DIL_UNROLL)

    def dilated_blocks(br, r, q_ref, k_ref, v_ref):
        nb = seq // (r * lb)

        def body(idx, carry):
            rho, n = idx // nb, idx % nb
            r0 = pl.multiple_of(n * lb, lb)
            rp = pl.multiple_of(jnp.maximum(n - 1, 0) * lb, lb)
            cur, prev = pl.ds(r0, lb), pl.ds(rp, lb)
            acc, m, l = attend(br, n == 0, q_ref[rho, cur, :], k_ref[rho, prev, :], k_ref[rho, cur, :],
                               v_ref[rho, prev, :], v_ref[rho, cur, :])
            rows = pl.ds(n * (lb * r) + rho, lb, stride=r)
            m_old = m_ref[rows, :]
            m_new = jnp.maximum(m_old, m)
            c_old = jnp.exp(m_old - m_new)
            c_cur = jnp.exp(m - m_new)
            acc_ref[rows, :] = acc_ref[rows, :] * c_old + acc * c_cur
            l_ref[rows, :] = l_ref[rows, :] * c_old + l * c_cur
            m_ref[rows, :] = m_new
            return carry

        lax.fori_loop(0, r * nb, body, 0, unroll=DIL_UNROLL)

    dilated_blocks(1, DIL_BRANCHES[1][1], q4_ref, k4_ref, v4_ref)
    dilated_blocks(2, DIL_BRANCHES[2][1], q16_ref, k16_ref, v16_ref)

    def finish(n, carry):
        rows = pl.ds(pl.multiple_of(n * lb, lb), lb)
        y_ref[rows, :] = (acc_ref[rows, :] / l_ref[rows, :]).astype(y_ref.dtype)
        return carry

    lax.fori_loop(0, seq // lb, finish, 0)


def _dilated(layouts, rel_bias, bn, seq):
    lb = DIL_BLOCK
    assert all(w // r <= lb and seq % (r * lb) == 0 for w, r in DIL_BRANCHES)
    assert tuple(r for _, r in DIL_BRANCHES) == (1, 4, 16)
    nat = pl.BlockSpec((None, seq, LANES), lambda b, p: (b, 0, p))
    by = lambda r: pl.BlockSpec((None, r, seq // r, LANES), lambda b, p: (b, 0, 0, p))
    nbr = len(DIL_BRANCHES)
    return pl.pallas_call(
        _dilated_kernel,
        grid=(bn, DIL_HEADS // 2),
        in_specs=[pl.BlockSpec(memory_space=pltpu.SMEM),
                  pl.BlockSpec((nbr, lb, 2 * lb), lambda b, p: (0, 0, 0))]
        + [nat] * 3 + [by(4)] * 3 + [by(16)] * 3,
        out_specs=nat,
        out_shape=jax.ShapeDtypeStruct((bn, seq, B_W), BF16),
        scratch_shapes=[pltpu.VMEM((nbr * DIL_HEADS, lb, 2 * lb), F32),
                        pltpu.VMEM((seq, LANES), F32), pltpu.VMEM((seq, LANES), F32),
                        pltpu.VMEM((seq, LANES), F32)],
        compiler_params=_params("arbitrary", "arbitrary"),
        name="dilated",
    )(rel_bias.astype(F32), _dilated_buckets(), *layouts)


def _even_out_kernel(x_ref, ya_ref, yb_ref, wa_ref, wb_ref, h_ref):
    acc = jnp.dot(ya_ref[...], wa_ref[...], preferred_element_type=F32)
    acc = acc + jnp.dot(yb_ref[...], wb_ref[...], preferred_element_type=F32)
    h_ref[...] = x_ref[...] + acc


def _even_out(x, ya, yb, w_out, tm):
    m = x.shape[0]
    row = lambda w: pl.BlockSpec((tm, w), lambda i: (i, 0))
    return pl.pallas_call(
        _even_out_kernel,
        grid=(m // tm,),
        in_specs=[row(D_MODEL), row(A_V), row(B_W),
                  pl.BlockSpec((A_V, D_MODEL), lambda i: (0, 0)),
                  pl.BlockSpec((B_W, D_MODEL), lambda i: (A_V // B_W, 0))],
        out_specs=row(D_MODEL),
        out_shape=jax.ShapeDtypeStruct((m, D_MODEL), F32),
        compiler_params=_params("arbitrary"),
        name="even_out",
    )(x, ya, yb, w_out, w_out)


FFN_TM = 512
FFN_R = 8
FFN_CHUNK = 256


def _ffn_kernel(tiles_per_seq, final, h_ref, g_ref, wup_ref, cw_ref, cb_ref, wd_ref, fg_ref, o_ref,
                slab_ref, hn_ref, act_ref, halo_ref):
    tm = h_ref.shape[0]
    rows = tm // FFN_R
    nslab = D_MODEL // LANES
    ck = FFN_CHUNK
    group = lambda g: slice(g * rows, (g + 1) * rows)

    for s in range(nslab):
        slab_ref[s] = h_ref[:, s * LANES:(s + 1) * LANES]
    for g in range(FFN_R):
        x = jnp.concatenate([slab_ref[s, pl.ds(g, rows, stride=FFN_R), :] for s in range(nslab)],
                            axis=1)
        ms = jnp.mean(x * x, axis=-1, keepdims=True)
        hn_ref[group(g), :] = (x * lax.rsqrt(ms + EPS) * g_ref[...]).astype(BF16)

    @pl.when((pl.program_id(0) % tiles_per_seq) == 0)
    def _():
        halo_ref[...] = jnp.zeros_like(halo_ref)

    first_row = lax.broadcasted_iota(jnp.int32, (rows, ck), 0) == 0

    def shift_down(block, fill):
        return jnp.where(first_row, fill, pltpu.roll(block, 1, 0))

    def conv(cols, slot):
        u = jnp.dot(hn_ref[...], wup_ref[:, cols], preferred_element_type=F32)
        ug = [u[group(g), :] for g in range(FFN_R)]
        tails = halo_ref[slot]
        prev2 = shift_down(ug[FFN_R - 2], tails[SUBLANES - 1:SUBLANES, :])
        prev1 = shift_down(ug[FFN_R - 1], tails[2 * SUBLANES - 1:2 * SUBLANES, :])
        halo_ref[slot] = jnp.concatenate(
            [ug[FFN_R - 2][rows - SUBLANES:, :], ug[FFN_R - 1][rows - SUBLANES:, :]], axis=0)
        back1 = [prev1] + ug[:FFN_R - 1]
        back2 = [prev2, prev1] + ug[:FFN_R - 2]
        cw = cw_ref[:, cols]
        cb = cb_ref[:, cols]
        return [cw[0:1, :] * back2[g] + cw[1:2, :] * back1[g] + cw[2:3, :] * ug[g] + cb
                for g in range(FFN_R)]

    for c in range(D_FF // ck):
        gate = conv(slice(c * ck, (c + 1) * ck), 2 * c)
        value = conv(slice(D_FF + c * ck, D_FF + (c + 1) * ck), 2 * c + 1)
        for g in range(FFN_R):
            act_ref[group(g), c * ck:(c + 1) * ck] = (_silu(gate[g]) * value[g]).astype(BF16)

    down = jnp.dot(act_ref[...], wd_ref[...], preferred_element_type=F32)
    for g in range(FFN_R):
        for s in range(nslab):
            slab_ref[s, pl.ds(g, rows, stride=FFN_R), :] = down[group(g), s * LANES:(s + 1) * LANES]
    y = h_ref[...] + jnp.concatenate([slab_ref[s] for s in range(nslab)], axis=1)
    if final:
        ms = jnp.mean(y * y, axis=-1, keepdims=True)
        y = y * lax.rsqrt(ms + EPS) * fg_ref[...]
    o_ref[...] = y


def _ffn(h, g, w_up, conv_w, conv_b, w_down, final_g, seq, final, riders=()):
    m = h.shape[0]
    tm = FFN_TM
    assert D_FF % FFN_CHUNK == 0 and tm % (FFN_R * 2 * SUBLANES) == 0 and seq % tm == 0
    const = lambda shape: pl.BlockSpec(shape, lambda i: (0, 0), pipeline_mode=pl.Buffered(1))
    steps = m // tm
    in_specs = [
        pl.BlockSpec((tm, D_MODEL), lambda i: (i, 0)),
        const((1, D_MODEL)),
        const((D_MODEL, 2 * D_FF)), const((CONV_WIDTH, 2 * D_FF)), const((1, 2 * D_FF)),
        const((D_FF, D_MODEL)),
        const((1, D_MODEL)),
    ]
    rider_in, rider_out, rider_shapes = _rider_specs(riders, steps)
    return pl.pallas_call(
        _with_riders(functools.partial(_ffn_kernel, seq // tm, final), len(in_specs), 1, len(riders)),
        grid=(steps,),
        in_specs=in_specs + rider_in,
        out_specs=[pl.BlockSpec((tm, D_MODEL), lambda i: (i, 0))] + rider_out,
        out_shape=[jax.ShapeDtypeStruct((m, D_MODEL), F32)] + rider_shapes,
        scratch_shapes=[
            pltpu.VMEM((D_MODEL // LANES, tm, LANES), F32),
            pltpu.VMEM((tm, D_MODEL), BF16),
            pltpu.VMEM((tm, D_FF), BF16),
            pltpu.VMEM((2 * D_FF // FFN_CHUNK, 2 * SUBLANES, FFN_CHUNK), F32),
        ],
        compiler_params=_params("arbitrary"),
        name="conv_ffn",
    )(h, g, w_up, conv_w.astype(F32), conv_b.astype(F32).reshape(1, -1), w_down, final_g,
      *[a for a, _ in riders])


HGRN_GROUP = 128
HGRN_TM = 256
HGRN_PROJ_COLS = 256


def _hgrn_group(proj_ref, rows, lower_all, gain, causal, tri, st, fill):
    c = HGRN_CHUNK
    heads = range(HGRN_HEADS)
    hk = [slice(h * HGRN_DK, (h + 1) * HGRN_DK) for h in heads]
    hv = [slice(h * HGRN_DV, (h + 1) * HGRN_DV) for h in heads]
    chunks = [slice(n * c, (n + 1) * c) for n in range(HGRN_GROUP // c)]

    q = _silu(proj_ref[rows, 0:C_K])
    fg = lower_all + (1.0 - lower_all) * jax.nn.sigmoid(proj_ref[rows, C_K:2 * C_K])
    k = 1.0 - fg
    log_f = jnp.log(fg)
    hi = log_f.astype(BF16)
    rem = log_f - hi.astype(F32)
    mid = rem.astype(BF16)
    lo = (rem - mid.astype(F32)).astype(BF16)
    _emit(fill, 2)
    parts = jnp.dot(tri, jnp.concatenate([hi, mid, lo], axis=1), preferred_element_type=F32)
    b = parts[:, 0:C_K] + parts[:, C_K:2 * C_K] + parts[:, 2 * C_K:3 * C_K]
    v = proj_ref[rows, 2 * C_K:2 * C_K + C_V].astype(BF16)
    q_t = (q * jnp.exp(b)).astype(BF16)
    k_t = (k * jnp.exp(-b)).astype(BF16)
    _emit(fill, 2)
    scores = [lax.dot_general(q_t[:, hk[h]], k_t[:, hk[h]], NT_DIMS, preferred_element_type=F32)
              for h in heads]
    scores = [jnp.where(causal, s, 0.0).astype(BF16) for s in scores]
    intra = [jnp.dot(scores[h], v[:, hv[h]], preferred_element_type=F32) for h in heads]
    b_last = [b[cs.stop - 1:cs.stop, :] for cs in chunks]
    k_end = [(k[cs, :] * jnp.exp(bl - b[cs, :])).astype(BF16) for cs, bl in zip(chunks, b_last)]
    decay = [jnp.exp(bl) for bl in b_last]
    _emit(fill, 2)
    st = list(st)
    inter = [[] for _ in heads]
    for n, cs in enumerate(chunks):
        for h in heads:
            inter[h].append(lax.dot_general(q_t[cs, hk[h]], st[h].astype(BF16), NT_DIMS,
                                            preferred_element_type=F32))
            st[h] = st[h] * decay[n][:, hk[h]] + lax.dot_general(
                v[cs, hv[h]], k_end[n][:, hk[h]], TN_DIMS, preferred_element_type=F32)
    gate = gain * _silu(proj_ref[rows, 2 * C_K + C_V:2 * C_K + 2 * C_V])
    _emit(fill, 2)
    outs = []
    for h in heads:
        y = intra[h] + jnp.concatenate(inter[h], axis=0)
        yn = y * lax.rsqrt(jnp.mean(y * y, axis=-1, keepdims=True) + EPS)
        outs.append((yn * gate[:, hv[h]]).astype(BF16))
    return outs, st


def _hgrn_layer_kernel(layer, tiles_per_seq, h0_ref, hnext_ref, g_ref, w_ref, lbp_ref, on_ref, o_ref,
                       proj_a, proj_b, st_ref):
    t = pl.program_id(0)
    tm = o_ref.shape[0]
    grp = HGRN_GROUP

    def projection(h_ref, dst):
        x = h_ref[...]
        ms = jnp.mean(x * x, axis=-1, keepdims=True)
        hn = (x * lax.rsqrt(ms + EPS) * g_ref[...]).astype(BF16)

        def piece(c):
            cols = slice(c * HGRN_PROJ_COLS, (c + 1) * HGRN_PROJ_COLS)

            def run():
                dst[:, cols] = jnp.dot(hn, w_ref[:, cols], preferred_element_type=F32)
            return run

        return [piece(c) for c in range(ODD_IN // HGRN_PROJ_COLS)]

    @pl.when(t == 0)
    def _():
        for run in projection(h0_ref, proj_a):
            run()

    @pl.when(t % tiles_per_seq == 0)
    def _():
        st_ref[...] = jnp.zeros_like(st_ref)

    def step(src, dst):
        fill = iter(projection(hnext_ref, dst))
        lbp = lbp_ref[...]
        ex = jnp.exp(lbp - jnp.max(lbp, axis=0, keepdims=True))
        sm = ex / jnp.sum(ex, axis=0, keepdims=True)
        lower_all = jnp.sum(sm[0:layer + 1, :], axis=0, keepdims=True) - sm[0:1, :]
        row = lax.broadcasted_iota(jnp.int32, (grp, grp), 0)
        colm = lax.broadcasted_iota(jnp.int32, (grp, grp), 1)
        causal = jnp.logical_and(row // HGRN_CHUNK == colm // HGRN_CHUNK, row >= colm)
        tri = jnp.where(causal, 1.0, 0.0).astype(BF16)
        gain = on_ref[...]
        st = [st_ref[h] for h in range(HGRN_HEADS)]
        for r0 in range(0, tm, grp):
            outs, st = _hgrn_group(src, slice(r0, r0 + grp), lower_all, gain, causal, tri, st, fill)
            for h, y in enumerate(outs):
                o_ref[r0:r0 + grp, h * HGRN_DV:(h + 1) * HGRN_DV] = y
        for run in fill:
            run()
        for h in range(HGRN_HEADS):
            st_ref[h] = st[h]

    @pl.when(t % 2 == 0)
    def _():
        step(proj_a, proj_b)

    @pl.when(t % 2 == 1)
    def _():
        step(proj_b, proj_a)


def _hgrn_layer(h, g, w, hgrn_lb, out_norm, layer, seq):
    tokens, d = h.shape
    tm = HGRN_TM
    ntiles = tokens // tm
    const = lambda shape: pl.BlockSpec(shape, lambda t: (0, 0), pipeline_mode=pl.Buffered(1))
    return pl.pallas_call(
        functools.partial(_hgrn_layer_kernel, layer, seq // tm),
        grid=(ntiles,),
        in_specs=[
            const((tm, d)),
            pl.BlockSpec((tm, d), lambda t: (jnp.minimum(t + 1, ntiles - 1), 0)),
            const((1, d)), const(w.shape), const(hgrn_lb.shape), const((1, C_V)),
        ],
        out_specs=pl.BlockSpec((tm, C_V), lambda t: (t, 0)),
        out_shape=jax.ShapeDtypeStruct((tokens, C_V), BF16),
        scratch_shapes=[pltpu.VMEM((tm, ODD_IN), F32), pltpu.VMEM((tm, ODD_IN), F32),
                        pltpu.VMEM((HGRN_HEADS, HGRN_DV, HGRN_DK), F32)],
        compiler_params=_params("arbitrary"),
        name="hgrn_layer",
    )(h, h, g, w, hgrn_lb, out_norm)


def _matmul_residual_kernel(h_ref, y_ref, w_ref, o_ref):
    o_ref[...] = h_ref[...] + jnp.dot(y_ref[...], w_ref[...], preferred_element_type=F32)


def _matmul_residual(h, y, w, tm):
    m, n = h.shape
    k = y.shape[1]
    return pl.pallas_call(
        _matmul_residual_kernel,
        grid=(m // tm,),
        in_specs=[
            pl.BlockSpec((tm, n), lambda i: (i, 0)),
            pl.BlockSpec((tm, k), lambda i: (i, 0)),
            pl.BlockSpec((k, n), lambda i: (0, 0)),
        ],
        out_specs=pl.BlockSpec((tm, n), lambda i: (i, 0)),
        out_shape=jax.ShapeDtypeStruct((m, n), F32),
        compiler_params=_params("arbitrary"),
        name="matmul_residual",
    )(h, y, w)


def kernel(x, even_w_in, even_w_out, ret_norm, rel_bias, odd_w_in, odd_w_out, hgrn_lb, hgrn_norm,
           mix_norm, ffn_norm, ffn_w_up, ffn_conv_w, ffn_conv_b, ffn_w_down, final_norm):
    bn, seq, d = x.shape
    tokens = bn * seq
    row = lambda a: a.reshape(1, -1).astype(F32)
    h = x.reshape(tokens, d)
    final_g = row(final_norm)

    outs = _even_in(h, row(mix_norm[0]), even_w_in[0].astype(BF16), row(ret_norm[0]), seq,
                    riders=[(even_w_out, 0), (ffn_w_up, 0), (ffn_w_down, 0)])
    ya, qn, q4, q16, kn, k4, k16, vn, v4, v16, w_out0, w_up0, w_down0 = outs
    yb = _dilated((qn, kn, vn, q4, k4, v4, q16, k16, v16), rel_bias, bn, seq)
    h = _even_out(h, ya, yb.reshape(tokens, B_W), w_out0, tm=1024)
    h, w_in1, w_out1, w_up1, w_down1 = _ffn(
        h, row(ffn_norm[0]), w_up0, ffn_conv_w[0], ffn_conv_b[0], w_down0, final_g, seq, final=False,
        riders=[(odd_w_in, 0), (odd_w_out, 0), (ffn_w_up, 1), (ffn_w_down, 1)])

    y = _hgrn_layer(h, row(mix_norm[1]), w_in1, hgrn_lb.astype(F32), row(hgrn_norm[0]), 1, seq)
    h = _matmul_residual(h, y, w_out1, tm=1024)
    h, = _ffn(h, row(ffn_norm[1]), w_up1, ffn_conv_w[1], ffn_conv_b[1], w_down1, final_g, seq,
              final=True)
    return h.reshape(bn, seq, d)
```

```python
import functools
import math

import numpy as np
import jax
import jax.numpy as jnp
from jax import lax
from jax.experimental import pallas as pl
from jax.experimental.pallas import tpu as pltpu

F32 = jnp.float32
BF16 = jnp.bfloat16

D_MODEL = 1024
RET_HEADS = 8
RET_DK = 64
RET_DV = 128
RET_CHUNK = 128
ROPE_BASE = 10000.0
DIL_HEADS = 8
DIL_DH = 64
DIL_BRANCHES = ((128, 1), (512, 4), (2048, 16))
DIL_BLOCK = 128
HGRN_HEADS = 8
HGRN_DK = 128
HGRN_DV = 128
HGRN_CHUNK = 32
REL_BUCKETS = 32
REL_MAX_DIST = 2048
D_FF = 2816
CONV_WIDTH = 3
EPS = 1e-6

A_QK = RET_HEADS * RET_DK
A_V = RET_HEADS * RET_DV
B_W = DIL_HEADS * DIL_DH
EVEN_IN = 2 * A_QK + 2 * A_V + 3 * B_W
C_K = HGRN_HEADS * HGRN_DK
C_V = HGRN_HEADS * HGRN_DV
ODD_IN = 2 * C_K + 2 * C_V

LANES = 128
SUBLANES = 8
VMEM_LIMIT = 56 * 1024 * 1024
MASK_VALUE = -1e30

NT_DIMS = (((1,), (1,)), ((), ()))
TN_DIMS = (((0,), (0,)), ((), ()))


def _params(*semantics):
    return pltpu.CompilerParams(dimension_semantics=semantics, vmem_limit_bytes=VMEM_LIMIT)


def _silu(x):
    return x * jax.nn.sigmoid(x)


def _emit(fill, count):
    tokens = []
    for _ in range(count):
        thunk = next(fill, None)
        if thunk is not None:
            token = thunk()
            if token is not None:
                tokens.append(token)
    return tokens


def _anchor(tokens, zero_bits):
    acc = zero_bits
    for t in tokens:
        acc = acc & pltpu.bitcast(t, jnp.int32)
    return pltpu.bitcast(acc, F32)


BF16_SUBLANES = 2 * SUBLANES


def _rider_specs(weights, steps):
    in_specs, out_specs, out_shapes = [], [], []
    for arr, layer in weights:
        _, rows, cols = arr.shape
        rep = 1
        while (rows * rep) % (steps * BF16_SUBLANES):
            rep *= 2
        slab = rows * rep // steps
        slab_index = lambda i, rep=rep: jnp.minimum(i, steps - 1) // rep
        in_specs.append(pl.BlockSpec((None, slab, cols),
                                     lambda i, layer=layer, at=slab_index: (layer, at(i), 0)))
        out_specs.append(pl.BlockSpec((slab, cols), lambda i, at=slab_index: (at(i), 0)))
        out_shapes.append(jax.ShapeDtypeStruct((rows, cols), BF16))
    return in_specs, out_specs, out_shapes


def _with_riders(body, n_in, n_out, n_riders):
    def kernel_fn(*refs):
        ins, refs = refs[:n_in], refs[n_in:]
        rider_in, refs = refs[:n_riders], refs[n_riders:]
        outs, refs = refs[:n_out], refs[n_out:]
        rider_out, scratch = refs[:n_riders], refs[n_riders:]
        for src, dst in zip(rider_in, rider_out):
            dst[...] = src[...].astype(dst.dtype)
        body(*ins, *outs, *scratch)
    return kernel_fn


def _retention_tables(seq):
    h = jnp.arange(RET_HEADS, dtype=F32)
    log_g = jnp.log1p(-jnp.exp2(-5.0 - h))
    pos = jnp.arange(RET_CHUNK, dtype=F32)
    diff = pos[:, None] - pos[None, :]
    decay = jnp.where(diff >= 0, jnp.exp(jnp.maximum(diff, 0.0)[None] * log_g[:, None, None]), 0.0)
    q_scale = jnp.repeat(jnp.exp((pos + 1.0)[:, None] * log_g[None, :]), RET_DK, axis=1)
    k_scale = jnp.repeat(jnp.exp((RET_CHUNK - 1.0 - pos)[:, None] * log_g[None, :]), RET_DK, axis=1)
    chunk_decay = jnp.broadcast_to(jnp.exp(RET_CHUNK * log_g)[:, None, None], (RET_HEADS, 1, RET_DV))
    inv = ROPE_BASE ** (-jnp.arange(0, RET_DK, 2, dtype=F32) / RET_DK)
    ang = jnp.arange(seq, dtype=F32)[:, None] * inv[None, :]
    cos, sin = jnp.cos(ang), jnp.sin(ang)
    reps = LANES // RET_DK
    cos_t = jnp.tile(jnp.concatenate([cos, cos], axis=1), (1, reps))
    sin_t = jnp.tile(jnp.concatenate([-sin, sin], axis=1), (1, reps))
    return decay, q_scale, k_scale, chunk_decay, cos_t, sin_t


EVEN_TM = 512


def _even_in_kernel(tiles_per_seq, x_ref, g_ref, w_ref, cos_ref, sin_ref, qs_ref, ks_ref, dec_ref,
                    cd_ref, rn_ref, ya_ref, qn_ref, q4_ref, q16_ref, kn_ref, k4_ref, k16_ref,
                    vn_ref, v4_ref, v16_ref, slab_ref, st_ref):
    tm = x_ref.shape[0]
    chunk = RET_CHUNK

    @pl.when(pl.program_id(0) % tiles_per_seq == 0)
    def _():
        st_ref[...] = jnp.zeros_like(st_ref)

    x = x_ref[...]
    ms = jnp.mean(x * x, axis=-1, keepdims=True)
    hn = (x * lax.rsqrt(ms + EPS) * g_ref[...]).astype(BF16)

    def proj(c):
        return jnp.dot(hn, w_ref[:, c * B_W:(c + 1) * B_W], preferred_element_type=F32)

    cos = cos_ref[...]
    sin = sin_ref[...]
    lane = lax.broadcasted_iota(jnp.int32, (tm, LANES), 1)
    half = RET_DK // 2
    first_half = (lane % RET_DK) < half

    def rot(t):
        partner = jnp.where(first_half, pltpu.roll(t, LANES - half, 1), pltpu.roll(t, half, 1))
        return t * cos + partner * sin

    groups = [slice(s * LANES, (s + 1) * LANES) for s in range(B_W // LANES)]

    r = proj(0)
    q_rot = [rot(r[:, sl]) for sl in groups]
    q = [t.astype(BF16) for t in q_rot]
    q_in = [(t * qs_ref[:, sl]).astype(BF16) for t, sl in zip(q_rot, groups)]
    r = proj(1)
    k_rot = [rot(r[:, sl]) * (RET_DK ** -0.5) for sl in groups]
    k = [t.astype(BF16) for t in k_rot]
    k_out = [(t * ks_ref[:, sl]).astype(BF16) for t, sl in zip(k_rot, groups)]
    v = jnp.concatenate([proj(2), proj(3)], axis=1).astype(BF16)
    gate = rn_ref[...] * _silu(jnp.concatenate([proj(4), proj(5)], axis=1))

    def dilated_pieces(c, slot, nat_ref, r4_ref, r16_ref, scale):
        def project():
            r = proj(c)
            if scale is not None:
                r = r * scale
            nat_ref[...] = r.astype(BF16)
            for s, sl in enumerate(groups):
                slab_ref[slot, s] = r[:, sl]

        def by4():
            for s, sl in enumerate(groups):
                for rho in range(4):
                    r4_ref[rho, :, sl] = slab_ref[slot, s, pl.ds(rho, tm // 4, stride=4), :].astype(BF16)

        def by16():
            for s, sl in enumerate(groups):
                for rho in range(16):
                    r16_ref[rho, :, sl] = slab_ref[slot, s, pl.ds(rho, tm // 16, stride=16), :].astype(BF16)

        return [project, by4, by16]

    fill = iter(dilated_pieces(6, 0, qn_ref, q4_ref, q16_ref, DIL_DH ** -0.5)
                + dilated_pieces(7, 1, kn_ref, k4_ref, k16_ref, None)
                + dilated_pieces(8, 2, vn_ref, v4_ref, v16_ref, None))

    head0 = lax.broadcasted_iota(jnp.int32, (chunk, LANES), 1) < RET_DK
    heads = range(RET_HEADS)
    hs = [slice(h * RET_DV, (h + 1) * RET_DV) for h in heads]
    st = [st_ref[h] for h in heads]
    for ci in range(tm // chunk):
        rows = slice(ci * chunk, (ci + 1) * chunk)
        mask = [head0 if h % 2 == 0 else jnp.logical_not(head0) for h in heads]
        scores = [lax.dot_general(jnp.where(mask[h], q[h // 2][rows], 0), k[h // 2][rows], NT_DIMS,
                                  preferred_element_type=F32) * dec_ref[h] for h in heads]
        _emit(fill, 1)
        intra = [jnp.dot(scores[h].astype(BF16), v[rows, hs[h]], preferred_element_type=F32)
                 for h in heads]
        inter = [jnp.dot(jnp.where(mask[h], q_in[h // 2][rows], 0), st[h].astype(BF16),
                         preferred_element_type=F32) for h in heads]
        st = [st[h] * cd_ref[h] + lax.dot_general(k_out[h // 2][rows], v[rows, hs[h]], TN_DIMS,
                                                  preferred_element_type=F32) for h in heads]
        _emit(fill, 1)
        for h in heads:
            y = intra[h] + inter[h]
            yc = y - jnp.mean(y, axis=-1, keepdims=True)
            yn = yc * lax.rsqrt(jnp.mean(yc * yc, axis=-1, keepdims=True) + EPS)
            ya_ref[rows, hs[h]] = (yn * gate[rows, hs[h]]).astype(BF16)
        _emit(fill, 1)
    for run in fill:
        run()
    for h in heads:
        st_ref[h] = st[h]


def _even_in(x, g, w, ret_norm, seq, riders):
    tokens, d = x.shape
    bn = tokens // seq
    tm = EVEN_TM
    tps = seq // tm
    decay, q_scale, k_scale, chunk_decay, cos_t, sin_t = _retention_tables(seq)
    q_scale = jnp.tile(q_scale, (tm // RET_CHUNK, 1))
    k_scale = jnp.tile(k_scale, (tm // RET_CHUNK, 1))
    const = lambda shape: pl.BlockSpec(shape, lambda i: (0,) * len(shape), pipeline_mode=pl.Buffered(1))
    flat = lambda w_: pl.BlockSpec((tm, w_), lambda i: (i, 0))
    nat = pl.BlockSpec((None, tm, B_W), lambda i: (i // tps, i % tps, 0))
    by = lambda r: pl.BlockSpec((None, r, tm // r, B_W), lambda i: (i // tps, 0, i % tps, 0))
    sds = jax.ShapeDtypeStruct
    lay = [sds((bn, seq, B_W), BF16), sds((bn, 4, seq // 4, B_W), BF16),
           sds((bn, 16, seq // 16, B_W), BF16)]
    steps = tokens // tm
    in_specs = [
        flat(d), const((1, d)), const(w.shape),
        pl.BlockSpec((tm, LANES), lambda i: (i % tps, 0)),
        pl.BlockSpec((tm, LANES), lambda i: (i % tps, 0)),
        const((tm, A_QK)), const((tm, A_QK)),
        const(decay.shape), const(chunk_decay.shape), const((1, A_V)),
    ]
    out_specs = [flat(A_V)] + [nat, by(4), by(16)] * 3
    rider_in, rider_out, rider_shapes = _rider_specs(riders, steps)
    return pl.pallas_call(
        _with_riders(functools.partial(_even_in_kernel, tps), len(in_specs), len(out_specs), len(riders)),
        grid=(steps,),
        in_specs=in_specs + rider_in,
        out_specs=out_specs + rider_out,
        out_shape=[sds((tokens, A_V), BF16)] + lay * 3 + rider_shapes,
        scratch_shapes=[pltpu.VMEM((3, B_W // LANES, tm, LANES), F32),
                        pltpu.VMEM((RET_HEADS, 2 * RET_DK, RET_DV), F32)],
        compiler_params=_params("arbitrary"),
        name="even_in",
    )(x, g, w, cos_t, sin_t, q_scale, k_scale, decay, chunk_decay, ret_norm, *[a for a, _ in riders])


def _rel_bucket(dist):
    max_exact = REL_BUCKETS // 2
    d = dist.astype(F32)
    large = max_exact + (jnp.log(jnp.maximum(d, 1.0) / max_exact) / math.log(REL_MAX_DIST / max_exact)
                         * (REL_BUCKETS - max_exact)).astype(jnp.int32)
    large = jnp.minimum(large, REL_BUCKETS - 1)
    return jnp.where(dist < max_exact, dist, large)


DIL_UNROLL = 8


def _dilated_buckets():
    lb = DIL_BLOCK
    a_idx = jnp.arange(lb)[:, None]
    c_idx = jnp.arange(2 * lb)[None, :]
    dist = lb + a_idx - c_idx
    tables = []
    for window, dilation in DIL_BRANCHES:
        valid = (dist >= 0) & (dist <= window // dilation)
        tables.append(jnp.where(valid, _rel_bucket(jnp.maximum(dist, 0) * dilation), -1))
    return jnp.stack(tables).astype(jnp.int32)


def _dilated_kernel(rb_ref, bucket_ref, qn_ref, kn_ref, vn_ref, q4_ref, k4_ref, v4_ref, q16_ref,
                    k16_ref, v16_ref, y_ref, bias_ref, acc_ref, m_ref, l_ref):
    lb = DIL_BLOCK
    seq = qn_ref.shape[0]
    pair = pl.program_id(1)

    @pl.when(pl.program_id(0) == 0)
    def _():
        for br in range(len(DIL_BRANCHES)):
            bucket = bucket_ref[br]
            for a in range(2):
                h = 2 * pair + a
                t = jnp.full((lb, 2 * lb), MASK_VALUE, F32)
                for j in range(REL_BUCKETS):
                    t = jnp.where(bucket == j, rb_ref[j, h], t)
                bias_ref[br * DIL_HEADS + h] = t

    lane = lax.broadcasted_iota(jnp.int32, (lb, LANES), 1)
    head0 = lane < DIL_DH
    prev_cols = lax.broadcasted_iota(jnp.int32, (lb, 2 * lb), 1) < lb

    def attend(br, first, q, kp, kc, vp, vc):
        k = jnp.concatenate([kp, kc], axis=0)
        v = jnp.concatenate([vp, vc], axis=0)
        no_prev = jnp.where(jnp.logical_and(prev_cols, first), MASK_VALUE, 0.0)
        parts = []
        for a in range(2):
            mask = head0 if a == 0 else jnp.logical_not(head0)
            s = lax.dot_general(jnp.where(mask, q, 0), k, NT_DIMS, preferred_element_type=F32)
            s = s + bias_ref[br * DIL_HEADS + 2 * pair + a] + no_prev
            m = jnp.max(s, axis=-1, keepdims=True)
            e = jnp.exp(s - m)
            parts.append((jnp.dot(e.astype(BF16), v, preferred_element_type=F32), m,
                          jnp.sum(e, axis=-1, keepdims=True)))
        return tuple(jnp.where(head0, x0, x1) for x0, x1 in zip(*parts))

    def natural_block(n, carry):
        r0 = pl.multiple_of(n * lb, lb)
        rp = pl.multiple_of(jnp.maximum(n - 1, 0) * lb, lb)
        cur, prev = pl.ds(r0, lb), pl.ds(rp, lb)
        acc, m, l = attend(0, n == 0, qn_ref[cur, :], kn_ref[prev, :], kn_ref[cur, :],
                           vn_ref[prev, :], vn_ref[cur, :])
        acc_ref[cur, :] = acc
        m_ref[cur, :] = m
        l_ref[cur, :] = l
        return carry

    lax.fori_loop(0, seq // lb, natural_block, 0, unroll=DIL_UNROLL)

    def dilated_blocks(br, r, q_ref, k_ref, v_ref):
        nb = seq // (r * lb)

        def body(idx, carry):
            rho, n = idx // nb, idx % nb
            r0 = pl.multiple_of(n * lb, lb)
            rp = pl.multiple_of(jnp.maximum(n - 1, 0) * lb, lb)
            cur, prev = pl.ds(r0, lb), pl.ds(rp, lb)
            acc, m, l = attend(br, n == 0, q_ref[rho, cur, :], k_ref[rho, prev, :], k_ref[rho, cur, :],
                               v_ref[rho, prev, :], v_ref[rho, cur, :])
            rows = pl.ds(n * (lb * r) + rho, lb, stride=r)
            m_old = m_ref[rows, :]
            m_new = jnp.maximum(m_old, m)
            c_old = jnp.exp(m_old - m_new)
            c_cur = jnp.exp(m - m_new)
            acc_ref[rows, :] = acc_ref[rows, :] * c_old + acc * c_cur
            l_ref[rows, :] = l_ref[rows, :] * c_old + l * c_cur
            m_ref[rows, :] = m_new
            return carry

        lax.fori_loop(0, r * nb, body, 0, unroll=DIL_UNROLL)

    dilated_blocks(1, DIL_BRANCHES[1][1], q4_ref, k4_ref, v4_ref)
    dilated_blocks(2, DIL_BRANCHES[2][1], q16_ref, k16_ref, v16_ref)

    def finish(n, carry):
        rows = pl.ds(pl.multiple_of(n * lb, lb), lb)
        y_ref[rows, :] = (acc_ref[rows, :] / l_ref[rows, :]).astype(y_ref.dtype)
        return carry

    lax.fori_loop(0, seq // lb, finish, 0)


def _dilated(layouts, rel_bias, bn, seq):
    lb = DIL_BLOCK
    assert all(w // r <= lb and seq % (r * lb) == 0 for w, r in DIL_BRANCHES)
    assert tuple(r for _, r in DIL_BRANCHES) == (1, 4, 16)
    nat = pl.BlockSpec((None, seq, LANES), lambda b, p: (b, 0, p))
    by = lambda r: pl.BlockSpec((None, r, seq // r, LANES), lambda b, p: (b, 0, 0, p))
    nbr = len(DIL_BRANCHES)
    return pl.pallas_call(
        _dilated_kernel,
        grid=(bn, DIL_HEADS // 2),
        in_specs=[pl.BlockSpec(memory_space=pltpu.SMEM),
                  pl.BlockSpec((nbr, lb, 2 * lb), lambda b, p: (0, 0, 0))]
        + [nat] * 3 + [by(4)] * 3 + [by(16)] * 3,
        out_specs=nat,
        out_shape=jax.ShapeDtypeStruct((bn, seq, B_W), BF16),
        scratch_shapes=[pltpu.VMEM((nbr * DIL_HEADS, lb, 2 * lb), F32),
                        pltpu.VMEM((seq, LANES), F32), pltpu.VMEM((seq, LANES), F32),
                        pltpu.VMEM((seq, LANES), F32)],
        compiler_params=_params("arbitrary", "arbitrary"),
        name="dilated",
    )(rel_bias.astype(F32), _dilated_buckets(), *layouts)


def _even_out_kernel(x_ref, ya_ref, yb_ref, wa_ref, wb_ref, h_ref):
    acc = jnp.dot(ya_ref[...], wa_ref[...], preferred_element_type=F32)
    acc = acc + jnp.dot(yb_ref[...], wb_ref[...], preferred_element_type=F32)
    h_ref[...] = x_ref[...] + acc


def _even_out(x, ya, yb, w_out, tm):
    m = x.shape[0]
    row = lambda w: pl.BlockSpec((tm, w), lambda i: (i, 0))
    return pl.pallas_call(
        _even_out_kernel,
        grid=(m // tm,),
        in_specs=[row(D_MODEL), row(A_V), row(B_W),
                  pl.BlockSpec((A_V, D_MODEL), lambda i: (0, 0)),
                  pl.BlockSpec((B_W, D_MODEL), lambda i: (A_V // B_W, 0))],
        out_specs=row(D_MODEL),
        out_shape=jax.ShapeDtypeStruct((m, D_MODEL), F32),
        compiler_params=_params("arbitrary"),
        name="even_out",
    )(x, ya, yb, w_out, w_out)


FFN_TM = 512
FFN_R = 8
FFN_CHUNK = 256
FFN_DENSE_CHUNKS = 5


def _ffn_kernel(ntiles, tiles_per_seq, final, h0_ref, hnext_ref, hprev_ref, g_ref, wup_ref, cw_ref,
                cb_ref, wd_ref, fg_ref, zero_ref, where_ref, o_ref, slab_in, slab_out, hn_cur, hn_next,
                act_ref, down_ref, halo_ref):
    i = pl.program_id(0)
    tm = o_ref.shape[0]
    rows = tm // FFN_R
    nslab = D_MODEL // LANES
    ck = FFN_CHUNK
    group = lambda g: slice(g * rows, (g + 1) * rows)
    lanes = lambda s: slice(s * LANES, (s + 1) * LANES)
    anywhere = pl.multiple_of(where_ref[0], BF16_SUBLANES)

    def head_pieces(h_ref):
        def stage(s):
            def run():
                slab_in[s] = h_ref[:, lanes(s)]
                return slab_in[s, pl.ds(anywhere, SUBLANES), :]
            return run

        def normalise(g):
            def run():
                x = jnp.concatenate([slab_in[s, pl.ds(g, rows, stride=FFN_R), :] for s in range(nslab)],
                                    axis=1)
                ms = jnp.mean(x * x, axis=-1, keepdims=True)
                hn_next[group(g), :] = (x * lax.rsqrt(ms + EPS) * g_ref[...]).astype(BF16)
                return hn_next[pl.ds(anywhere, BF16_SUBLANES), 0:LANES].astype(F32)[0:SUBLANES]
            return run

        return [stage(s) for s in range(nslab)] + [normalise(g) for g in range(FFN_R)]

    def tail_pieces():
        def unpermute(g):
            def run():
                for s in range(nslab):
                    slab_out[s, pl.ds(g, rows, stride=FFN_R), :] = down_ref[group(g), lanes(s)]
                return slab_out[nslab - 1, pl.ds(anywhere, SUBLANES), :]
            return run

        def store(r0, r1):
            def run():
                y = hprev_ref[r0:r1, :] + jnp.concatenate([slab_out[s, r0:r1, :] for s in range(nslab)],
                                                          axis=1)
                if final:
                    ms = jnp.mean(y * y, axis=-1, keepdims=True)
                    y = y * lax.rsqrt(ms + EPS) * fg_ref[...]
                o_ref[r0:r1, :] = y
                return o_ref[pl.ds(anywhere, SUBLANES), 0:LANES]
            return run

        return ([unpermute(g) for g in range(FFN_R)]
                + [store(g * rows, (g + 1) * rows) for g in range(FFN_R)])

    @pl.when(i == 0)
    def _():
        for run in head_pieces(h0_ref):
            run()
        down_ref[...] = jnp.zeros_like(down_ref)
        halo_ref[...] = jnp.zeros_like(halo_ref)

    @pl.when(i < ntiles)
    def _():
        hn_cur[...] = hn_next[...]
        fill = iter(tail_pieces() + head_pieces(hnext_ref))
        seq_start = i % tiles_per_seq == 0
        first_row = lax.broadcasted_iota(jnp.int32, (rows, ck), 0) == 0

        def shift_down(block, fill_row):
            return jnp.where(first_row, fill_row, pltpu.roll(block, 1, 0))

        zero_bits = zero_ref[...]

        def conv(cols, slot, tokens):
            u = jnp.dot(hn_cur[...], wup_ref[:, cols], preferred_element_type=F32)
            ug = [u[group(g), :] for g in range(FFN_R)]
            tails = jnp.where(seq_start, 0.0, halo_ref[slot])
            prev2 = shift_down(ug[FFN_R - 2], tails[SUBLANES - 1:SUBLANES, :])
            prev1 = shift_down(ug[FFN_R - 1], tails[2 * SUBLANES - 1:2 * SUBLANES, :])
            halo_ref[slot] = jnp.concatenate(
                [ug[FFN_R - 2][rows - SUBLANES:, :], ug[FFN_R - 1][rows - SUBLANES:, :]], axis=0)
            back1 = [prev1] + ug[:FFN_R - 1]
            back2 = [prev2, prev1] + ug[:FFN_R - 2]
            cw = cw_ref[:, cols]
            cb = cb_ref[:, cols]
            if tokens:
                cb = cb + jnp.tile(_anchor(tokens, zero_bits)[0:1, :], (1, ck // LANES))
            return [cw[0:1, :] * back2[g] + cw[1:2, :] * back1[g] + cw[2:3, :] * ug[g] + cb
                    for g in range(FFN_R)]

        tokens = []
        nchunk = D_FF // ck
        for c in range(nchunk):
            per_point = 2 if c < FFN_DENSE_CHUNKS else 1
            gate = conv(slice(c * ck, (c + 1) * ck), 2 * c, tokens)
            tokens = _emit(fill, per_point)
            value = conv(slice(D_FF + c * ck, D_FF + (c + 1) * ck), 2 * c + 1, tokens)
            tokens = _emit(fill, per_point)
            for g in range(FFN_R):
                act_ref[group(g), c * ck:(c + 1) * ck] = (_silu(gate[g]) * value[g]).astype(BF16)
        down_ref[...] = jnp.dot(act_ref[...], wd_ref[...], preferred_element_type=F32)
        for run in fill:
            run()

    @pl.when(i == ntiles)
    def _():
        for run in tail_pieces():
            run()


def _ffn(h, g, w_up, conv_w, conv_b, w_down, final_g, seq, final, riders=()):
    m = h.shape[0]
    tm = FFN_TM
    assert D_FF % FFN_CHUNK == 0 and tm % (FFN_R * 2 * SUBLANES) == 0 and seq % tm == 0
    const = lambda shape: pl.BlockSpec(shape, lambda i: (0, 0), pipeline_mode=pl.Buffered(1))
    ntiles = m // tm
    tile = lambda at: pl.BlockSpec((tm, D_MODEL), lambda i: (at(i), 0))
    in_specs = [
        const((tm, D_MODEL)),
        tile(lambda i: jnp.minimum(i + 1, ntiles - 1)),
        tile(lambda i: jnp.maximum(i - 1, 0)),
        const((1, D_MODEL)),
        const((D_MODEL, 2 * D_FF)), const((CONV_WIDTH, 2 * D_FF)), const((1, 2 * D_FF)),
        const((D_FF, D_MODEL)),
        const((1, D_MODEL)),
        const((SUBLANES, LANES)),
        pl.BlockSpec(memory_space=pltpu.SMEM),
    ]
    rider_in, rider_out, rider_shapes = _rider_specs(riders, ntiles)
    return pl.pallas_call(
        _with_riders(functools.partial(_ffn_kernel, ntiles, seq // tm, final), len(in_specs), 1,
                     len(riders)),
        grid=(ntiles + 1,),
        in_specs=in_specs + rider_in,
        out_specs=[tile(lambda i: jnp.maximum(i - 1, 0))] + rider_out,
        out_shape=[jax.ShapeDtypeStruct((m, D_MODEL), F32)] + rider_shapes,
        scratch_shapes=[
            pltpu.VMEM((D_MODEL // LANES, tm, LANES), F32),
            pltpu.VMEM((D_MODEL // LANES, tm, LANES), F32),
            pltpu.VMEM((tm, D_MODEL), BF16),
            pltpu.VMEM((tm, D_MODEL), BF16),
            pltpu.VMEM((tm, D_FF), BF16),
            pltpu.VMEM((tm, D_MODEL), F32),
            pltpu.VMEM((2 * D_FF // FFN_CHUNK, 2 * SUBLANES, FFN_CHUNK), F32),
        ],
        compiler_params=_params("arbitrary"),
        name="conv_ffn",
    )(h, h, h, g, w_up, conv_w.astype(F32), conv_b.astype(F32).reshape(1, -1), w_down, final_g,
      jnp.zeros((SUBLANES, LANES), jnp.int32), jnp.zeros((1,), jnp.int32), *[a for a, _ in riders])


HGRN_GROUP = 128
HGRN_TM = 256
HGRN_PROJ_COLS = 256


def _hgrn_group(proj_ref, rows, lower_all, gain, causal, tri, st, fill):
    c = HGRN_CHUNK
    heads = range(HGRN_HEADS)
    hk = [slice(h * HGRN_DK, (h + 1) * HGRN_DK) for h in heads]
    hv = [slice(h * HGRN_DV, (h + 1) * HGRN_DV) for h in heads]
    chunks = [slice(n * c, (n + 1) * c) for n in range(HGRN_GROUP // c)]

    q = _silu(proj_ref[rows, 0:C_K])
    fg = lower_all + (1.0 - lower_all) * jax.nn.sigmoid(proj_ref[rows, C_K:2 * C_K])
    k = 1.0 - fg
    log_f = jnp.log(fg)
    hi = log_f.astype(BF16)
    rem = log_f - hi.astype(F32)
    mid = rem.astype(BF16)
    lo = (rem - mid.astype(F32)).astype(BF16)
    _emit(fill, 2)
    parts = jnp.dot(tri, jnp.concatenate([hi, mid, lo], axis=1), preferred_element_type=F32)
    b = parts[:, 0:C_K] + parts[:, C_K:2 * C_K] + parts[:, 2 * C_K:3 * C_K]
    v = proj_ref[rows, 2 * C_K:2 * C_K + C_V].astype(BF16)
    q_t = (q * jnp.exp(b)).astype(BF16)
    k_t = (k * jnp.exp(-b)).astype(BF16)
    _emit(fill, 2)
    scores = [lax.dot_general(q_t[:, hk[h]], k_t[:, hk[h]], NT_DIMS, preferred_element_type=F32)
              for h in heads]
    scores = [jnp.where(causal, s, 0.0).astype(BF16) for s in scores]
    intra = [jnp.dot(scores[h], v[:, hv[h]], preferred_element_type=F32) for h in heads]
    b_last = [b[cs.stop - 1:cs.stop, :] for cs in chunks]
    k_end = [(k[cs, :] * jnp.exp(bl - b[cs, :])).astype(BF16) for cs, bl in zip(chunks, b_last)]
    decay = [jnp.exp(bl) for bl in b_last]
    _emit(fill, 2)
    st = list(st)
    inter = [[] for _ in heads]
    for n, cs in enumerate(chunks):
        for h in heads:
            inter[h].append(lax.dot_general(q_t[cs, hk[h]], st[h].astype(BF16), NT_DIMS,
                                            preferred_element_type=F32))
            st[h] = st[h] * decay[n][:, hk[h]] + lax.dot_general(
                v[cs, hv[h]], k_end[n][:, hk[h]], TN_DIMS, preferred_element_type=F32)
    gate = gain * _silu(proj_ref[rows, 2 * C_K + C_V:2 * C_K + 2 * C_V])
    _emit(fill, 2)
    outs = []
    for h in heads:
        y = intra[h] + jnp.concatenate(inter[h], axis=0)
        yn = y * lax.rsqrt(jnp.mean(y * y, axis=-1, keepdims=True) + EPS)
        outs.append((yn * gate[:, hv[h]]).astype(BF16))
    return outs, st


def _hgrn_layer_kernel(layer, tiles_per_seq, h0_ref, hnext_ref, g_ref, w_ref, lbp_ref, on_ref, o_ref,
                       proj_a, proj_b, st_ref):
    t = pl.program_id(0)
    tm = o_ref.shape[0]
    grp = HGRN_GROUP

    def projection(h_ref, dst):
        x = h_ref[...]
        ms = jnp.mean(x * x, axis=-1, keepdims=True)
        hn = (x * lax.rsqrt(ms + EPS) * g_ref[...]).astype(BF16)

        def piece(c):
            cols = slice(c * HGRN_PROJ_COLS, (c + 1) * HGRN_PROJ_COLS)

            def run():
                dst[:, cols] = jnp.dot(hn, w_ref[:, cols], preferred_element_type=F32)
            return run

        return [piece(c) for c in range(ODD_IN // HGRN_PROJ_COLS)]

    @pl.when(t == 0)
    def _():
        for run in projection(h0_ref, proj_a):
            run()

    @pl.when(t % tiles_per_seq == 0)
    def _():
        st_ref[...] = jnp.zeros_like(st_ref)

    def step(src, dst):
        fill = iter(projection(hnext_ref, dst))
        lbp = lbp_ref[...]
        ex = jnp.exp(lbp - jnp.max(lbp, axis=0, keepdims=True))
        sm = ex / jnp.sum(ex, axis=0, keepdims=True)
        lower_all = jnp.sum(sm[0:layer + 1, :], axis=0, keepdims=True) - sm[0:1, :]
        row = lax.broadcasted_iota(jnp.int32, (grp, grp), 0)
        colm = lax.broadcasted_iota(jnp.int32, (grp, grp), 1)
        causal = jnp.logical_and(row // HGRN_CHUNK == colm // HGRN_CHUNK, row >= colm)
        tri = jnp.where(causal, 1.0, 0.0).astype(BF16)
        gain = on_ref[...]
        st = [st_ref[h] for h in range(HGRN_HEADS)]
        for r0 in range(0, tm, grp):
            outs, st = _hgrn_group(src, slice(r0, r0 + grp), lower_all, gain, causal, tri, st, fill)
            for h, y in enumerate(outs):
                o_ref[r0:r0 + grp, h * HGRN_DV:(h + 1) * HGRN_DV] = y
        for run in fill:
            run()
        for h in range(HGRN_HEADS):
            st_ref[h] = st[h]

    @pl.when(t % 2 == 0)
    def _():
        step(proj_a, proj_b)

    @pl.when(t % 2 == 1)
    def _():
        step(proj_b, proj_a)


def _hgrn_layer(h, g, w, hgrn_lb, out_norm, layer, seq):
    tokens, d = h.shape
    tm = HGRN_TM
    ntiles = tokens // tm
    const = lambda shape: pl.BlockSpec(shape, lambda t: (0, 0), pipeline_mode=pl.Buffered(1))
    return pl.pallas_call(
        functools.partial(_hgrn_layer_kernel, layer, seq // tm),
        grid=(ntiles,),
        in_specs=[
            const((tm, d)),
            pl.BlockSpec((tm, d), lambda t: (jnp.minimum(t + 1, ntiles - 1), 0)),
            const((1, d)), const(w.shape), const(hgrn_lb.shape), const((1, C_V)),
        ],
        out_specs=pl.BlockSpec((tm, C_V), lambda t: (t, 0)),
        out_shape=jax.ShapeDtypeStruct((tokens, C_V), BF16),
        scratch_shapes=[pltpu.VMEM((tm, ODD_IN), F32), pltpu.VMEM((tm, ODD_IN), F32),
                        pltpu.VMEM((HGRN_HEADS, HGRN_DV, HGRN_DK), F32)],
        compiler_params=_params("arbitrary"),
        name="hgrn_layer",
    )(h, h, g, w, hgrn_lb, out_norm)


def _matmul_residual_kernel(h_ref, y_ref, w_ref, o_ref):
    o_ref[...] = h_ref[...] + jnp.dot(y_ref[...], w_ref[...], preferred_element_type=F32)


def _matmul_residual(h, y, w, tm):
    m, n = h.shape
    k = y.shape[1]
    return pl.pallas_call(
        _matmul_residual_kernel,
        grid=(m // tm,),
        in_specs=[
            pl.BlockSpec((tm, n), lambda i: (i, 0)),
            pl.BlockSpec((tm, k), lambda i: (i, 0)),
            pl.BlockSpec((k, n), lambda i: (0, 0)),
        ],
        out_specs=pl.BlockSpec((tm, n), lambda i: (i, 0)),
        out_shape=jax.ShapeDtypeStruct((m, n), F32),
        compiler_params=_params("arbitrary"),
        name="matmul_residual",
    )(h, y, w)


def kernel(x, even_w_in, even_w_out, ret_norm, rel_bias, odd_w_in, odd_w_out, hgrn_lb, hgrn_norm,
           mix_norm, ffn_norm, ffn_w_up, ffn_conv_w, ffn_conv_b, ffn_w_down, final_norm):
    bn, seq, d = x.shape
    tokens = bn * seq
    row = lambda a: a.reshape(1, -1).astype(F32)
    h = x.reshape(tokens, d)
    final_g = row(final_norm)

    outs = _even_in(h, row(mix_norm[0]), even_w_in[0].astype(BF16), row(ret_norm[0]), seq,
                    riders=[(even_w_out, 0), (ffn_w_up, 0), (ffn_w_down, 0)])
    ya, qn, q4, q16, kn, k4, k16, vn, v4, v16, w_out0, w_up0, w_down0 = outs
    yb = _dilated((qn, kn, vn, q4, k4, v4, q16, k16, v16), rel_bias, bn, seq)
    h = _even_out(h, ya, yb.reshape(tokens, B_W), w_out0, tm=1024)
    h, w_in1, w_out1, w_up1, w_down1 = _ffn(
        h, row(ffn_norm[0]), w_up0, ffn_conv_w[0], ffn_conv_b[0], w_down0, final_g, seq, final=False,
        riders=[(odd_w_in, 0), (odd_w_out, 0), (ffn_w_up, 1), (ffn_w_down, 1)])

    y = _hgrn_layer(h, row(mix_norm[1]), w_in1, hgrn_lb.astype(F32), row(hgrn_norm[0]), 1, seq)
    h = _matmul_residual(h, y, w_out1, tm=1024)
    h, = _ffn(h, row(ffn_norm[1]), w_up1, ffn_conv_w[1], ffn_conv_b[1], w_down1, final_g, seq,
              final=True)
    return h.reshape(bn, seq, d)
```

```python
import functools
import math

import numpy as np
import jax
import jax.numpy as jnp
from jax import lax
from jax.experimental import pallas as pl
from jax.experimental.pallas import tpu as pltpu

F32 = jnp.float32
BF16 = jnp.bfloat16

D_MODEL = 1024
RET_HEADS = 8
RET_DK = 64
RET_DV = 128
RET_CHUNK = 128
ROPE_BASE = 10000.0
DIL_HEADS = 8
DIL_DH = 64
DIL_BRANCHES = ((128, 1), (512, 4), (2048, 16))
DIL_BLOCK = 128
HGRN_HEADS = 8
HGRN_DK = 128
HGRN_DV = 128
HGRN_CHUNK = 32
REL_BUCKETS = 32
REL_MAX_DIST = 2048
D_FF = 2816
CONV_WIDTH = 3
EPS = 1e-6

A_QK = RET_HEADS * RET_DK
A_V = RET_HEADS * RET_DV
B_W = DIL_HEADS * DIL_DH
EVEN_IN = 2 * A_QK + 2 * A_V + 3 * B_W
C_K = HGRN_HEADS * HGRN_DK
C_V = HGRN_HEADS * HGRN_DV
ODD_IN = 2 * C_K + 2 * C_V

LANES = 128
SUBLANES = 8
VMEM_LIMIT = 56 * 1024 * 1024
MASK_VALUE = -1e30

NT_DIMS = (((1,), (1,)), ((), ()))
TN_DIMS = (((0,), (0,)), ((), ()))


def _params(*semantics):
    return pltpu.CompilerParams(dimension_semantics=semantics, vmem_limit_bytes=VMEM_LIMIT)


def _silu(x):
    return x * jax.nn.sigmoid(x)


def _emit(fill, count):
    tokens = []
    for _ in range(count):
        thunk = next(fill, None)
        if thunk is not None:
            token = thunk()
            if token is not None:
                tokens.append(token)
    return tokens


def _anchor(tokens, zero_bits):
    acc = zero_bits
    for t in tokens:
        acc = acc & pltpu.bitcast(t, jnp.int32)
    return pltpu.bitcast(acc, F32)


BF16_SUBLANES = 2 * SUBLANES


def _rider_specs(weights, steps):
    in_specs, out_specs, out_shapes = [], [], []
    for arr, layer in weights:
        _, rows, cols = arr.shape
        rep = 1
        while (rows * rep) % (steps * BF16_SUBLANES):
            rep *= 2
        slab = rows * rep // steps
        slab_index = lambda i, rep=rep: jnp.minimum(i, steps - 1) // rep
        in_specs.append(pl.BlockSpec((None, slab, cols),
                                     lambda i, layer=layer, at=slab_index: (layer, at(i), 0)))
        out_specs.append(pl.BlockSpec((slab, cols), lambda i, at=slab_index: (at(i), 0)))
        out_shapes.append(jax.ShapeDtypeStruct((rows, cols), BF16))
    return in_specs, out_specs, out_shapes


def _with_riders(body, n_in, n_out, n_riders):
    def kernel_fn(*refs):
        ins, refs = refs[:n_in], refs[n_in:]
        rider_in, refs = refs[:n_riders], refs[n_riders:]
        outs, refs = refs[:n_out], refs[n_out:]
        rider_out, scratch = refs[:n_riders], refs[n_riders:]
        for src, dst in zip(rider_in, rider_out):
            dst[...] = src[...].astype(dst.dtype)
        body(*ins, *outs, *scratch)
    return kernel_fn


def _retention_tables(seq):
    h = jnp.arange(RET_HEADS, dtype=F32)
    log_g = jnp.log1p(-jnp.exp2(-5.0 - h))
    pos = jnp.arange(RET_CHUNK, dtype=F32)
    diff = pos[:, None] - pos[None, :]
    decay = jnp.where(diff >= 0, jnp.exp(jnp.maximum(diff, 0.0)[None] * log_g[:, None, None]), 0.0)
    q_scale = jnp.repeat(jnp.exp((pos + 1.0)[:, None] * log_g[None, :]), RET_DK, axis=1)
    k_scale = jnp.repeat(jnp.exp((RET_CHUNK - 1.0 - pos)[:, None] * log_g[None, :]), RET_DK, axis=1)
    chunk_decay = jnp.broadcast_to(jnp.exp(RET_CHUNK * log_g)[:, None, None], (RET_HEADS, 1, RET_DV))
    inv = ROPE_BASE ** (-jnp.arange(0, RET_DK, 2, dtype=F32) / RET_DK)
    ang = jnp.arange(seq, dtype=F32)[:, None] * inv[None, :]
    cos, sin = jnp.cos(ang), jnp.sin(ang)
    reps = LANES // RET_DK
    cos_t = jnp.tile(jnp.concatenate([cos, cos], axis=1), (1, reps))
    sin_t = jnp.tile(jnp.concatenate([-sin, sin], axis=1), (1, reps))
    return decay, q_scale, k_scale, chunk_decay, cos_t, sin_t


EVEN_TM = 512


def _even_in_kernel(tiles_per_seq, x_ref, g_ref, w_ref, cos_ref, sin_ref, qs_ref, ks_ref, dec_ref,
                    cd_ref, rn_ref, ya_ref, qn_ref, q4_ref, q16_ref, kn_ref, k4_ref, k16_ref,
                    vn_ref, v4_ref, v16_ref, slab_ref, st_ref):
    tm = x_ref.shape[0]
    chunk = RET_CHUNK

    @pl.when(pl.program_id(0) % tiles_per_seq == 0)
    def _():
        st_ref[...] = jnp.zeros_like(st_ref)

    x = x_ref[...]
    ms = jnp.mean(x * x, axis=-1, keepdims=True)
    hn = (x * lax.rsqrt(ms + EPS) * g_ref[...]).astype(BF16)

    def proj(c):
        return jnp.dot(hn, w_ref[:, c * B_W:(c + 1) * B_W], preferred_element_type=F32)

    cos = cos_ref[...]
    sin = sin_ref[...]
    lane = lax.broadcasted_iota(jnp.int32, (tm, LANES), 1)
    half = RET_DK // 2
    first_half = (lane % RET_DK) < half

    def rot(t):
        partner = jnp.where(first_half, pltpu.roll(t, LANES - half, 1), pltpu.roll(t, half, 1))
        return t * cos + partner * sin

    groups = [slice(s * LANES, (s + 1) * LANES) for s in range(B_W // LANES)]

    r = proj(0)
    q_rot = [rot(r[:, sl]) for sl in groups]
    q = [t.astype(BF16) for t in q_rot]
    q_in = [(t * qs_ref[:, sl]).astype(BF16) for t, sl in zip(q_rot, groups)]
    r = proj(1)
    k_rot = [rot(r[:, sl]) * (RET_DK ** -0.5) for sl in groups]
    k = [t.astype(BF16) for t in k_rot]
    k_out = [(t * ks_ref[:, sl]).astype(BF16) for t, sl in zip(k_rot, groups)]
    v = jnp.concatenate([proj(2), proj(3)], axis=1).astype(BF16)
    gate = rn_ref[...] * _silu(jnp.concatenate([proj(4), proj(5)], axis=1))

    def dilated_pieces(c, slot, nat_ref, r4_ref, r16_ref, scale):
        def project():
            r = proj(c)
            if scale is not None:
                r = r * scale
            nat_ref[...] = r.astype(BF16)
            for s, sl in enumerate(groups):
                slab_ref[slot, s] = r[:, sl]

        def by4():
            for s, sl in enumerate(groups):
                for rho in range(4):
                    r4_ref[rho, :, sl] = slab_ref[slot, s, pl.ds(rho, tm // 4, stride=4), :].astype(BF16)

        def by16():
            for s, sl in enumerate(groups):
                for rho in range(16):
                    r16_ref[rho, :, sl] = slab_ref[slot, s, pl.ds(rho, tm // 16, stride=16), :].astype(BF16)

        return [project, by4, by16]

    fill = iter(dilated_pieces(6, 0, qn_ref, q4_ref, q16_ref, DIL_DH ** -0.5)
                + dilated_pieces(7, 1, kn_ref, k4_ref, k16_ref, None)
                + dilated_pieces(8, 2, vn_ref, v4_ref, v16_ref, None))

    head0 = lax.broadcasted_iota(jnp.int32, (chunk, LANES), 1) < RET_DK
    heads = range(RET_HEADS)
    hs = [slice(h * RET_DV, (h + 1) * RET_DV) for h in heads]
    st = [st_ref[h] for h in heads]
    for ci in range(tm // chunk):
        rows = slice(ci * chunk, (ci + 1) * chunk)
        mask = [head0 if h % 2 == 0 else jnp.logical_not(head0) for h in heads]
        scores = [lax.dot_general(jnp.where(mask[h], q[h // 2][rows], 0), k[h // 2][rows], NT_DIMS,
                                  preferred_element_type=F32) * dec_ref[h] for h in heads]
        _emit(fill, 1)
        intra = [jnp.dot(scores[h].astype(BF16), v[rows, hs[h]], preferred_element_type=F32)
                 for h in heads]
        inter = [jnp.dot(jnp.where(mask[h], q_in[h // 2][rows], 0), st[h].astype(BF16),
                         preferred_element_type=F32) for h in heads]
        st = [st[h] * cd_ref[h] + lax.dot_general(k_out[h // 2][rows], v[rows, hs[h]], TN_DIMS,
                                                  preferred_element_type=F32) for h in heads]
        _emit(fill, 1)
        for h in heads:
            y = intra[h] + inter[h]
            yc = y - jnp.mean(y, axis=-1, keepdims=True)
            yn = yc * lax.rsqrt(jnp.mean(yc * yc, axis=-1, keepdims=True) + EPS)
            ya_ref[rows, hs[h]] = (yn * gate[rows, hs[h]]).astype(BF16)
        _emit(fill, 1)
    for run in fill:
        run()
    for h in heads:
        st_ref[h] = st[h]


def _even_in(x, g, w, ret_norm, seq, riders):
    tokens, d = x.shape
    bn = tokens // seq
    tm = EVEN_TM
    tps = seq // tm
    decay, q_scale, k_scale, chunk_decay, cos_t, sin_t = _retention_tables(seq)
    q_scale = jnp.tile(q_scale, (tm // RET_CHUNK, 1))
    k_scale = jnp.tile(k_scale, (tm // RET_CHUNK, 1))
    const = lambda shape: pl.BlockSpec(shape, lambda i: (0,) * len(shape), pipeline_mode=pl.Buffered(1))
    flat = lambda w_: pl.BlockSpec((tm, w_), lambda i: (i, 0))
    nat = pl.BlockSpec((None, tm, B_W), lambda i: (i // tps, i % tps, 0))
    by = lambda r: pl.BlockSpec((None, r, tm // r, B_W), lambda i: (i // tps, 0, i % tps, 0))
    sds = jax.ShapeDtypeStruct
    lay = [sds((bn, seq, B_W), BF16), sds((bn, 4, seq // 4, B_W), BF16),
           sds((bn, 16, seq // 16, B_W), BF16)]
    steps = tokens // tm
    in_specs = [
        flat(d), const((1, d)), const(w.shape),
        pl.BlockSpec((tm, LANES), lambda i: (i % tps, 0)),
        pl.BlockSpec((tm, LANES), lambda i: (i % tps, 0)),
        const((tm, A_QK)), const((tm, A_QK)),
        const(decay.shape), const(chunk_decay.shape), const((1, A_V)),
    ]
    out_specs = [flat(A_V)] + [nat, by(4), by(16)] * 3
    rider_in, rider_out, rider_shapes = _rider_specs(riders, steps)
    return pl.pallas_call(
        _with_riders(functools.partial(_even_in_kernel, tps), len(in_specs), len(out_specs), len(riders)),
        grid=(steps,),
        in_specs=in_specs + rider_in,
        out_specs=out_specs + rider_out,
        out_shape=[sds((tokens, A_V), BF16)] + lay * 3 + rider_shapes,
        scratch_shapes=[pltpu.VMEM((3, B_W // LANES, tm, LANES), F32),
                        pltpu.VMEM((RET_HEADS, 2 * RET_DK, RET_DV), F32)],
        compiler_params=_params("arbitrary"),
        name="even_in",
    )(x, g, w, cos_t, sin_t, q_scale, k_scale, decay, chunk_decay, ret_norm, *[a for a, _ in riders])


def _rel_bucket(dist):
    max_exact = REL_BUCKETS // 2
    d = dist.astype(F32)
    large = max_exact + (jnp.log(jnp.maximum(d, 1.0) / max_exact) / math.log(REL_MAX_DIST / max_exact)
                         * (REL_BUCKETS - max_exact)).astype(jnp.int32)
    large = jnp.minimum(large, REL_BUCKETS - 1)
    return jnp.where(dist < max_exact, dist, large)


DIL_UNROLL = 8


def _dilated_buckets():
    lb = DIL_BLOCK
    a_idx = jnp.arange(lb)[:, None]
    c_idx = jnp.arange(2 * lb)[None, :]
    dist = lb + a_idx - c_idx
    tables = []
    for window, dilation in DIL_BRANCHES:
        valid = (dist >= 0) & (dist <= window // dilation)
        tables.append(jnp.where(valid, _rel_bucket(jnp.maximum(dist, 0) * dilation), -1))
    return jnp.stack(tables).astype(jnp.int32)


def _dilated_kernel(rb_ref, bucket_ref, qn_ref, kn_ref, vn_ref, q4_ref, k4_ref, v4_ref, q16_ref,
                    k16_ref, v16_ref, y_ref, bias_ref, acc_ref, m_ref, l_ref):
    lb = DIL_BLOCK
    seq = qn_ref.shape[0]
    pair = pl.program_id(1)

    @pl.when(pl.program_id(0) == 0)
    def _():
        for br in range(len(DIL_BRANCHES)):
            bucket = bucket_ref[br]
            for a in range(2):
                h = 2 * pair + a
                t = jnp.full((lb, 2 * lb), MASK_VALUE, F32)
                for j in range(REL_BUCKETS):
                    t = jnp.where(bucket == j, rb_ref[j, h], t)
                bias_ref[br * DIL_HEADS + h] = t

    lane = lax.broadcasted_iota(jnp.int32, (lb, LANES), 1)
    head0 = lane < DIL_DH

    def attend(br, blocks):
        masks = (head0, jnp.logical_not(head0))
        bias = [bias_ref[br * DIL_HEADS + 2 * pair + a] for a in range(2)]
        ks = [jnp.concatenate([kp, kc], axis=0) for _, _, kp, kc, _, _ in blocks]
        vs = [jnp.concatenate([vp, vc], axis=0) for _, _, _, _, vp, vc in blocks]
        pens = [jnp.where(first, MASK_VALUE, 0.0) for first, *_ in blocks]
        scores = [[lax.dot_general(jnp.where(masks[a], blk[1], 0), k, NT_DIMS,
                                   preferred_element_type=F32) + bias[a] for a in range(2)]
                  for blk, k in zip(blocks, ks)]
        scores = [[jnp.concatenate([s[:, 0:lb] + pen, s[:, lb:]], axis=1) for s in pair_s]
                  for pair_s, pen in zip(scores, pens)]
        maxes = [[jnp.max(s, axis=-1, keepdims=True) for s in pair_s] for pair_s in scores]
        exps = [[jnp.exp(s - m) for s, m in zip(pair_s, pair_m)] for pair_s, pair_m in zip(scores, maxes)]
        sums = [[jnp.sum(e, axis=-1, keepdims=True) for e in pair_e] for pair_e in exps]
        outs = [[jnp.dot(e.astype(BF16), v, preferred_element_type=F32) for e in pair_e]
                for pair_e, v in zip(exps, vs)]
        return [tuple(jnp.where(head0, x[0], x[1]) for x in (o, m, l))
                for o, m, l in zip(outs, maxes, sums)]

    def block_rows(n):
        r0 = pl.multiple_of(n * lb, lb)
        rp = pl.multiple_of(jnp.maximum(n - 1, 0) * lb, lb)
        return pl.ds(r0, lb), pl.ds(rp, lb)

    def natural_blocks(it, carry):
        blocks, dests = [], []
        for j in range(DIL_UNROLL):
            n = it * DIL_UNROLL + j
            cur, prev = block_rows(n)
            blocks.append((n == 0, qn_ref[cur, :], kn_ref[prev, :], kn_ref[cur, :],
                           vn_ref[prev, :], vn_ref[cur, :]))
            dests.append(cur)
        for cur, (acc, m, l) in zip(dests, attend(0, blocks)):
            acc_ref[cur, :] = acc
            m_ref[cur, :] = m
            l_ref[cur, :] = l
        return carry

    lax.fori_loop(0, seq // (lb * DIL_UNROLL), natural_blocks, 0)

    def dilated_blocks(br, r, q_ref, k_ref, v_ref):
        nb = seq // (r * lb)

        def body(it, carry):
            blocks, dests = [], []
            for j in range(DIL_UNROLL):
                idx = it * DIL_UNROLL + j
                rho, n = idx // nb, idx % nb
                cur, prev = block_rows(n)
                blocks.append((n == 0, q_ref[rho, cur, :], k_ref[rho, prev, :], k_ref[rho, cur, :],
                               v_ref[rho, prev, :], v_ref[rho, cur, :]))
                dests.append(pl.ds(n * (lb * r) + rho, lb, stride=r))
            for rows, (acc, m, l) in zip(dests, attend(br, blocks)):
                m_old = m_ref[rows, :]
                m_new = jnp.maximum(m_old, m)
                c_old = jnp.exp(m_old - m_new)
                c_cur = jnp.exp(m - m_new)
                acc_ref[rows, :] = acc_ref[rows, :] * c_old + acc * c_cur
                l_ref[rows, :] = l_ref[rows, :] * c_old + l * c_cur
                m_ref[rows, :] = m_new
            return carry

        lax.fori_loop(0, r * nb // DIL_UNROLL, body, 0)

    dilated_blocks(1, DIL_BRANCHES[1][1], q4_ref, k4_ref, v4_ref)
    dilated_blocks(2, DIL_BRANCHES[2][1], q16_ref, k16_ref, v16_ref)

    def finish(n, carry):
        rows = pl.ds(pl.multiple_of(n * lb, lb), lb)
        y_ref[rows, :] = (acc_ref[rows, :] / l_ref[rows, :]).astype(y_ref.dtype)
        return carry

    lax.fori_loop(0, seq // lb, finish, 0)


def _dilated(layouts, rel_bias, bn, seq):
    lb = DIL_BLOCK
    assert all(w // r <= lb and seq % (r * lb) == 0 for w, r in DIL_BRANCHES)
    assert tuple(r for _, r in DIL_BRANCHES) == (1, 4, 16)
    nat = pl.BlockSpec((None, seq, LANES), lambda b, p: (b, 0, p))
    by = lambda r: pl.BlockSpec((None, r, seq // r, LANES), lambda b, p: (b, 0, 0, p))
    nbr = len(DIL_BRANCHES)
    return pl.pallas_call(
        _dilated_kernel,
        grid=(bn, DIL_HEADS // 2),
        in_specs=[pl.BlockSpec(memory_space=pltpu.SMEM),
                  pl.BlockSpec((nbr, lb, 2 * lb), lambda b, p: (0, 0, 0))]
        + [nat] * 3 + [by(4)] * 3 + [by(16)] * 3,
        out_specs=nat,
        out_shape=jax.ShapeDtypeStruct((bn, seq, B_W), BF16),
        scratch_shapes=[pltpu.VMEM((nbr * DIL_HEADS, lb, 2 * lb), F32),
                        pltpu.VMEM((seq, LANES), F32), pltpu.VMEM((seq, LANES), F32),
                        pltpu.VMEM((seq, LANES), F32)],
        compiler_params=_params("arbitrary", "arbitrary"),
        name="dilated",
    )(rel_bias.astype(F32), _dilated_buckets(), *layouts)


def _even_out_kernel(x_ref, ya_ref, yb_ref, wa_ref, wb_ref, h_ref):
    acc = jnp.dot(ya_ref[...], wa_ref[...], preferred_element_type=F32)
    acc = acc + jnp.dot(yb_ref[...], wb_ref[...], preferred_element_type=F32)
    h_ref[...] = x_ref[...] + acc


def _even_out(x, ya, yb, w_out, tm):
    m = x.shape[0]
    row = lambda w: pl.BlockSpec((tm, w), lambda i: (i, 0))
    return pl.pallas_call(
        _even_out_kernel,
        grid=(m // tm,),
        in_specs=[row(D_MODEL), row(A_V), row(B_W),
                  pl.BlockSpec((A_V, D_MODEL), lambda i: (0, 0)),
                  pl.BlockSpec((B_W, D_MODEL), lambda i: (A_V // B_W, 0))],
        out_specs=row(D_MODEL),
        out_shape=jax.ShapeDtypeStruct((m, D_MODEL), F32),
        compiler_params=_params("arbitrary"),
        name="even_out",
    )(x, ya, yb, w_out, w_out)


FFN_TM = 512
FFN_R = 8
FFN_CHUNK = 256
FFN_DENSE_CHUNKS = 5


def _ffn_kernel(ntiles, tiles_per_seq, final, h0_ref, hnext_ref, hprev_ref, g_ref, wup_ref, cw_ref,
                cb_ref, wd_ref, fg_ref, zero_ref, where_ref, o_ref, slab_in, slab_out, hn_cur, hn_next,
                act_ref, down_ref, halo_ref):
    i = pl.program_id(0)
    tm = o_ref.shape[0]
    rows = tm // FFN_R
    nslab = D_MODEL // LANES
    ck = FFN_CHUNK
    group = lambda g: slice(g * rows, (g + 1) * rows)
    lanes = lambda s: slice(s * LANES, (s + 1) * LANES)
    anywhere = pl.multiple_of(where_ref[0], BF16_SUBLANES)

    def head_pieces(h_ref):
        def stage(s):
            def run():
                slab_in[s] = h_ref[:, lanes(s)]
                return slab_in[s, pl.ds(anywhere, SUBLANES), :]
            return run

        def normalise(g):
            def run():
                x = jnp.concatenate([slab_in[s, pl.ds(g, rows, stride=FFN_R), :] for s in range(nslab)],
                                    axis=1)
                ms = jnp.mean(x * x, axis=-1, keepdims=True)
                hn_next[group(g), :] = (x * lax.rsqrt(ms + EPS) * g_ref[...]).astype(BF16)
                return hn_next[pl.ds(anywhere, BF16_SUBLANES), 0:LANES].astype(F32)[0:SUBLANES]
            return run

        return [stage(s) for s in range(nslab)] + [normalise(g) for g in range(FFN_R)]

    def tail_pieces():
        def unpermute(g):
            def run():
                for s in range(nslab):
                    slab_out[s, pl.ds(g, rows, stride=FFN_R), :] = down_ref[group(g), lanes(s)]
                return slab_out[nslab - 1, pl.ds(anywhere, SUBLANES), :]
            return run

        def store(r0, r1):
            def run():
                y = hprev_ref[r0:r1, :] + jnp.concatenate([slab_out[s, r0:r1, :] for s in range(nslab)],
                                                          axis=1)
                if final:
                    ms = jnp.mean(y * y, axis=-1, keepdims=True)
                    y = y * lax.rsqrt(ms + EPS) * fg_ref[...]
                o_ref[r0:r1, :] = y
                return o_ref[pl.ds(anywhere, SUBLANES), 0:LANES]
            return run

        return ([unpermute(g) for g in range(FFN_R)]
                + [store(g * rows, (g + 1) * rows) for g in range(FFN_R)])

    @pl.when(i == 0)
    def _():
        for run in head_pieces(h0_ref):
            run()
        down_ref[...] = jnp.zeros_like(down_ref)
        halo_ref[...] = jnp.zeros_like(halo_ref)

    @pl.when(i < ntiles)
    def _():
        hn_cur[...] = hn_next[...]
        fill = iter(tail_pieces() + head_pieces(hnext_ref))
        seq_start = i % tiles_per_seq == 0
        first_row = lax.broadcasted_iota(jnp.int32, (rows, ck), 0) == 0

        def shift_down(block, fill_row):
            return jnp.where(first_row, fill_row, pltpu.roll(block, 1, 0))

        zero_bits = zero_ref[...]

        def conv(cols, slot, tokens):
            u = jnp.dot(hn_cur[...], wup_ref[:, cols], preferred_element_type=F32)
            ug = [u[group(g), :] for g in range(FFN_R)]
            tails = jnp.where(seq_start, 0.0, halo_ref[slot])
            prev2 = shift_down(ug[FFN_R - 2], tails[SUBLANES - 1:SUBLANES, :])
            prev1 = shift_down(ug[FFN_R - 1], tails[2 * SUBLANES - 1:2 * SUBLANES, :])
            halo_ref[slot] = jnp.concatenate(
                [ug[FFN_R - 2][rows - SUBLANES:, :], ug[FFN_R - 1][rows - SUBLANES:, :]], axis=0)
            back1 = [prev1] + ug[:FFN_R - 1]
            back2 = [prev2, prev1] + ug[:FFN_R - 2]
            cw = cw_ref[:, cols]
            cb = cb_ref[:, cols]
            if tokens:
                cb = cb + jnp.tile(_anchor(tokens, zero_bits)[0:1, :], (1, ck // LANES))
            return [cw[0:1, :] * back2[g] + cw[1:2, :] * back1[g] + cw[2:3, :] * ug[g] + cb
                    for g in range(FFN_R)]

        tokens = []
        nchunk = D_FF // ck
        for c in range(nchunk):
            per_point = 2 if c < FFN_DENSE_CHUNKS else 1
            gate = conv(slice(c * ck, (c + 1) * ck), 2 * c, tokens)
            tokens = _emit(fill, per_point)
            value = conv(slice(D_FF + c * ck, D_FF + (c + 1) * ck), 2 * c + 1, tokens)
            tokens = _emit(fill, per_point)
            for g in range(FFN_R):
                act_ref[group(g), c * ck:(c + 1) * ck] = (_silu(gate[g]) * value[g]).astype(BF16)
        down_ref[...] = jnp.dot(act_ref[...], wd_ref[...], preferred_element_type=F32)
        for run in fill:
            run()

    @pl.when(i == ntiles)
    def _():
        for run in tail_pieces():
            run()


def _ffn(h, g, w_up, conv_w, conv_b, w_down, final_g, seq, final, riders=()):
    m = h.shape[0]
    tm = FFN_TM
    assert D_FF % FFN_CHUNK == 0 and tm % (FFN_R * 2 * SUBLANES) == 0 and seq % tm == 0
    const = lambda shape: pl.BlockSpec(shape, lambda i: (0, 0), pipeline_mode=pl.Buffered(1))
    ntiles = m // tm
    tile = lambda at: pl.BlockSpec((tm, D_MODEL), lambda i: (at(i), 0))
    in_specs = [
        const((tm, D_MODEL)),
        tile(lambda i: jnp.minimum(i + 1, ntiles - 1)),
        tile(lambda i: jnp.maximum(i - 1, 0)),
        const((1, D_MODEL)),
        const((D_MODEL, 2 * D_FF)), const((CONV_WIDTH, 2 * D_FF)), const((1, 2 * D_FF)),
        const((D_FF, D_MODEL)),
        const((1, D_MODEL)),
        const((SUBLANES, LANES)),
        pl.BlockSpec(memory_space=pltpu.SMEM),
    ]
    rider_in, rider_out, rider_shapes = _rider_specs(riders, ntiles)
    return pl.pallas_call(
        _with_riders(functools.partial(_ffn_kernel, ntiles, seq // tm, final), len(in_specs), 1,
                     len(riders)),
        grid=(ntiles + 1,),
        in_specs=in_specs + rider_in,
        out_specs=[tile(lambda i: jnp.maximum(i - 1, 0))] + rider_out,
        out_shape=[jax.ShapeDtypeStruct((m, D_MODEL), F32)] + rider_shapes,
        scratch_shapes=[
            pltpu.VMEM((D_MODEL // LANES, tm, LANES), F32),
            pltpu.VMEM((D_MODEL // LANES, tm, LANES), F32),
            pltpu.VMEM((tm, D_MODEL), BF16),
            pltpu.VMEM((tm, D_MODEL), BF16),
            pltpu.VMEM((tm, D_FF), BF16),
            pltpu.VMEM((tm, D_MODEL), F32),
            pltpu.VMEM((2 * D_FF // FFN_CHUNK, 2 * SUBLANES, FFN_CHUNK), F32),
        ],
        compiler_params=_params("arbitrary"),
        name="conv_ffn",
    )(h, h, h, g, w_up, conv_w.astype(F32), conv_b.astype(F32).reshape(1, -1), w_down, final_g,
      jnp.zeros((SUBLANES, LANES), jnp.int32), jnp.zeros((1,), jnp.int32), *[a for a, _ in riders])


HGRN_GROUP = 128
HGRN_TM = 256
HGRN_PROJ_COLS = 256


def _hgrn_group(proj_ref, rows, lower_all, gain, causal, tri, st, fill):
    c = HGRN_CHUNK
    heads = range(HGRN_HEADS)
    hk = [slice(h * HGRN_DK, (h + 1) * HGRN_DK) for h in heads]
    hv = [slice(h * HGRN_DV, (h + 1) * HGRN_DV) for h in heads]
    chunks = [slice(n * c, (n + 1) * c) for n in range(HGRN_GROUP // c)]

    q = _silu(proj_ref[rows, 0:C_K])
    fg = lower_all + (1.0 - lower_all) * jax.nn.sigmoid(proj_ref[rows, C_K:2 * C_K])
    k = 1.0 - fg
    log_f = jnp.log(fg)
    hi = log_f.astype(BF16)
    rem = log_f - hi.astype(F32)
    mid = rem.astype(BF16)
    lo = (rem - mid.astype(F32)).astype(BF16)
    _emit(fill, 2)
    parts = jnp.dot(tri, jnp.concatenate([hi, mid, lo], axis=1), preferred_element_type=F32)
    b = parts[:, 0:C_K] + parts[:, C_K:2 * C_K] + parts[:, 2 * C_K:3 * C_K]
    v = proj_ref[rows, 2 * C_K:2 * C_K + C_V].astype(BF16)
    q_t = (q * jnp.exp(b)).astype(BF16)
    k_t = (k * jnp.exp(-b)).astype(BF16)
    _emit(fill, 2)
    scores = [lax.dot_general(q_t[:, hk[h]], k_t[:, hk[h]], NT_DIMS, preferred_element_type=F32)
              for h in heads]
    scores = [jnp.where(causal, s, 0.0).astype(BF16) for s in scores]
    intra = [jnp.dot(scores[h], v[:, hv[h]], preferred_element_type=F32) for h in heads]
    b_last = [b[cs.stop - 1:cs.stop, :] for cs in chunks]
    k_end = [(k[cs, :] * jnp.exp(bl - b[cs, :])).astype(BF16) for cs, bl in zip(chunks, b_last)]
    decay = [jnp.exp(bl) for bl in b_last]
    _emit(fill, 2)
    st = list(st)
    inter = [[] for _ in heads]
    for n, cs in enumerate(chunks):
        for h in heads:
            inter[h].append(lax.dot_general(q_t[cs, hk[h]], st[h].astype(BF16), NT_DIMS,
                                            preferred_element_type=F32))
            st[h] = st[h] * decay[n][:, hk[h]] + lax.dot_general(
                v[cs, hv[h]], k_end[n][:, hk[h]], TN_DIMS, preferred_element_type=F32)
    gate = gain * _silu(proj_ref[rows, 2 * C_K + C_V:2 * C_K + 2 * C_V])
    _emit(fill, 2)
    outs = []
    for h in heads:
        y = intra[h] + jnp.concatenate(inter[h], axis=0)
        yn = y * lax.rsqrt(jnp.mean(y * y, axis=-1, keepdims=True) + EPS)
        outs.append((yn * gate[:, hv[h]]).astype(BF16))
    return outs, st


def _hgrn_layer_kernel(layer, tiles_per_seq, h0_ref, hnext_ref, g_ref, w_ref, lbp_ref, on_ref, o_ref,
                       proj_a, proj_b, st_ref):
    t = pl.program_id(0)
    tm = o_ref.shape[0]
    grp = HGRN_GROUP

    def projection(h_ref, dst):
        x = h_ref[...]
        ms = jnp.mean(x * x, axis=-1, keepdims=True)
        hn = (x * lax.rsqrt(ms + EPS) * g_ref[...]).astype(BF16)

        def piece(c):
            cols = slice(c * HGRN_PROJ_COLS, (c + 1) * HGRN_PROJ_COLS)

            def run():
                dst[:, cols] = jnp.dot(hn, w_ref[:, cols], preferred_element_type=F32)
            return run

        return [piece(c) for c in range(ODD_IN // HGRN_PROJ_COLS)]

    @pl.when(t == 0)
    def _():
        for run in projection(h0_ref, proj_a):
            run()

    @pl.when(t % tiles_per_seq == 0)
    def _():
        st_ref[...] = jnp.zeros_like(st_ref)

    def step(src, dst):
        fill = iter(projection(hnext_ref, dst))
        lbp = lbp_ref[...]
        ex = jnp.exp(lbp - jnp.max(lbp, axis=0, keepdims=True))
        sm = ex / jnp.sum(ex, axis=0, keepdims=True)
        lower_all = jnp.sum(sm[0:layer + 1, :], axis=0, keepdims=True) - sm[0:1, :]
        row = lax.broadcasted_iota(jnp.int32, (grp, grp), 0)
        colm = lax.broadcasted_iota(jnp.int32, (grp, grp), 1)
        causal = jnp.logical_and(row // HGRN_CHUNK == colm // HGRN_CHUNK, row >= colm)
        tri = jnp.where(causal, 1.0, 0.0).astype(BF16)
        gain = on_ref[...]
        st = [st_ref[h] for h in range(HGRN_HEADS)]
        for r0 in range(0, tm, grp):
            outs, st = _hgrn_group(src, slice(r0, r0 + grp), lower_all, gain, causal, tri, st, fill)
            for h, y in enumerate(outs):
                o_ref[r0:r0 + grp, h * HGRN_DV:(h + 1) * HGRN_DV] = y
        for run in fill:
            run()
        for h in range(HGRN_HEADS):
            st_ref[h] = st[h]

    @pl.when(t % 2 == 0)
    def _():
        step(proj_a, proj_b)

    @pl.when(t % 2 == 1)
    def _():
        step(proj_b, proj_a)


def _hgrn_layer(h, g, w, hgrn_lb, out_norm, layer, seq):
    tokens, d = h.shape
    tm = HGRN_TM
    ntiles = tokens // tm
    const = lambda shape: pl.BlockSpec(shape, lambda t: (0, 0), pipeline_mode=pl.Buffered(1))
    return pl.pallas_call(
        functools.partial(_hgrn_layer_kernel, layer, seq // tm),
        grid=(ntiles,),
        in_specs=[
            const((tm, d)),
            pl.BlockSpec((tm, d), lambda t: (jnp.minimum(t + 1, ntiles - 1), 0)),
            const((1, d)), const(w.shape), const(hgrn_lb.shape), const((1, C_V)),
        ],
        out_specs=pl.BlockSpec((tm, C_V), lambda t: (t, 0)),
        out_shape=jax.ShapeDtypeStruct((tokens, C_V), BF16),
        scratch_shapes=[pltpu.VMEM((tm, ODD_IN), F32), pltpu.VMEM((tm, ODD_IN), F32),
                        pltpu.VMEM((HGRN_HEADS, HGRN_DV, HGRN_DK), F32)],
        compiler_params=_params("arbitrary"),
        name="hgrn_layer",
    )(h, h, g, w, hgrn_lb, out_norm)


def _matmul_residual_kernel(h_ref, y_ref, w_ref, o_ref):
    o_ref[...] = h_ref[...] + jnp.dot(y_ref[...], w_ref[...], preferred_element_type=F32)


def _matmul_residual(h, y, w, tm):
    m, n = h.shape
    k = y.shape[1]
    return pl.pallas_call(
        _matmul_residual_kernel,
        grid=(m // tm,),
        in_specs=[
            pl.BlockSpec((tm, n), lambda i: (i, 0)),
            pl.BlockSpec((tm, k), lambda i: (i, 0)),
            pl.BlockSpec((k, n), lambda i: (0, 0)),
        ],
        out_specs=pl.BlockSpec((tm, n), lambda i: (i, 0)),
        out_shape=jax.ShapeDtypeStruct((m, n), F32),
        compiler_params=_params("arbitrary"),
        name="matmul_residual",
    )(h, y, w)


def kernel(x, even_w_in, even_w_out, ret_norm, rel_bias, odd_w_in, odd_w_out, hgrn_lb, hgrn_norm,
           mix_norm, ffn_norm, ffn_w_up, ffn_conv_w, ffn_conv_b, ffn_w_down, final_norm):
    bn, seq, d = x.shape
    tokens = bn * seq
    row = lambda a: a.reshape(1, -1).astype(F32)
    h = x.reshape(tokens, d)
    final_g = row(final_norm)

    outs = _even_in(h, row(mix_norm[0]), even_w_in[0].astype(BF16), row(ret_norm[0]), seq,
                    riders=[(even_w_out, 0), (ffn_w_up, 0), (ffn_w_down, 0)])
    ya, qn, q4, q16, kn, k4, k16, vn, v4, v16, w_out0, w_up0, w_down0 = outs
    yb = _dilated((qn, kn, vn, q4, k4, v4, q16, k16, v16), rel_bias, bn, seq)
    h = _even_out(h, ya, yb.reshape(tokens, B_W), w_out0, tm=1024)
    h, w_in1, w_out1, w_up1, w_down1 = _ffn(
        h, row(ffn_norm[0]), w_up0, ffn_conv_w[0], ffn_conv_b[0], w_down0, final_g, seq, final=False,
        riders=[(odd_w_in, 0), (odd_w_out, 0), (ffn_w_up, 1), (ffn_w_down, 1)])

    y = _hgrn_layer(h, row(mix_norm[1]), w_in1, hgrn_lb.astype(F32), row(hgrn_norm[0]), 1, seq)
    h = _matmul_residual(h, y, w_out1, tm=1024)
    h, = _ffn(h, row(ffn_norm[1]), w_up1, ffn_conv_w[1], ffn_conv_b[1], w_down1, final_g, seq,
              final=True)
    return h.reshape(bn, seq, d)
```

```python
import functools
import math

import numpy as np
import jax
import jax.numpy as jnp
from jax import lax
from jax.experimental import pallas as pl
from jax.experimental.pallas import tpu as pltpu

F32 = jnp.float32
BF16 = jnp.bfloat16

D_MODEL = 1024
RET_HEADS = 8
RET_DK = 64
RET_DV = 128
RET_CHUNK = 128
ROPE_BASE = 10000.0
DIL_HEADS = 8
DIL_DH = 64
DIL_BRANCHES = ((128, 1), (512, 4), (2048, 16))
DIL_BLOCK = 128
HGRN_HEADS = 8
HGRN_DK = 128
HGRN_DV = 128
HGRN_CHUNK = 32
REL_BUCKETS = 32
REL_MAX_DIST = 2048
D_FF = 2816
CONV_WIDTH = 3
EPS = 1e-6

A_QK = RET_HEADS * RET_DK
A_V = RET_HEADS * RET_DV
B_W = DIL_HEADS * DIL_DH
EVEN_IN = 2 * A_QK + 2 * A_V + 3 * B_W
C_K = HGRN_HEADS * HGRN_DK
C_V = HGRN_HEADS * HGRN_DV
ODD_IN = 2 * C_K + 2 * C_V

LANES = 128
SUBLANES = 8
VMEM_LIMIT = 56 * 1024 * 1024
MASK_VALUE = -1e30

NT_DIMS = (((1,), (1,)), ((), ()))
TN_DIMS = (((0,), (0,)), ((), ()))


def _params(*semantics):
    return pltpu.CompilerParams(dimension_semantics=semantics, vmem_limit_bytes=VMEM_LIMIT)


def _silu(x):
    return x * jax.nn.sigmoid(x)


def _emit(fill, count):
    tokens = []
    for _ in range(count):
        thunk = next(fill, None)
        if thunk is not None:
            token = thunk()
            if token is not None:
                tokens.append(token)
    return tokens


def _anchor(tokens, zero_bits):
    acc = zero_bits
    for t in tokens:
        acc = acc & pltpu.bitcast(t, jnp.int32)
    return pltpu.bitcast(acc, F32)


BF16_SUBLANES = 2 * SUBLANES


def _rider_specs(weights, steps):
    in_specs, out_specs, out_shapes = [], [], []
    for arr, layer in weights:
        _, rows, cols = arr.shape
        rep = 1
        while (rows * rep) % (steps * BF16_SUBLANES):
            rep *= 2
        slab = rows * rep // steps
        slab_index = lambda i, rep=rep: jnp.minimum(i, steps - 1) // rep
        in_specs.append(pl.BlockSpec((None, slab, cols),
                                     lambda i, layer=layer, at=slab_index: (layer, at(i), 0)))
        out_specs.append(pl.BlockSpec((slab, cols), lambda i, at=slab_index: (at(i), 0)))
        out_shapes.append(jax.ShapeDtypeStruct((rows, cols), BF16))
    return in_specs, out_specs, out_shapes


def _with_riders(body, n_in, n_out, n_riders):
    def kernel_fn(*refs):
        ins, refs = refs[:n_in], refs[n_in:]
        rider_in, refs = refs[:n_riders], refs[n_riders:]
        outs, refs = refs[:n_out], refs[n_out:]
        rider_out, scratch = refs[:n_riders], refs[n_riders:]
        for src, dst in zip(rider_in, rider_out):
            dst[...] = src[...].astype(dst.dtype)
        body(*ins, *outs, *scratch)
    return kernel_fn


def _retention_tables(seq):
    h = jnp.arange(RET_HEADS, dtype=F32)
    log_g = jnp.log1p(-jnp.exp2(-5.0 - h))
    pos = jnp.arange(RET_CHUNK, dtype=F32)
    diff = pos[:, None] - pos[None, :]
    decay = jnp.where(diff >= 0, jnp.exp(jnp.maximum(diff, 0.0)[None] * log_g[:, None, None]), 0.0)
    q_scale = jnp.repeat(jnp.exp((pos + 1.0)[:, None] * log_g[None, :]), RET_DK, axis=1)
    k_scale = jnp.repeat(jnp.exp((RET_CHUNK - 1.0 - pos)[:, None] * log_g[None, :]), RET_DK, axis=1)
    chunk_decay = jnp.broadcast_to(jnp.exp(RET_CHUNK * log_g)[:, None, None], (RET_HEADS, 1, RET_DV))
    inv = ROPE_BASE ** (-jnp.arange(0, RET_DK, 2, dtype=F32) / RET_DK)
    ang = jnp.arange(seq, dtype=F32)[:, None] * inv[None, :]
    cos, sin = jnp.cos(ang), jnp.sin(ang)
    reps = LANES // RET_DK
    cos_t = jnp.tile(jnp.concatenate([cos, cos], axis=1), (1, reps))
    sin_t = jnp.tile(jnp.concatenate([-sin, sin], axis=1), (1, reps))
    return decay, q_scale, k_scale, chunk_decay, cos_t, sin_t


EVEN_TM = 512


def _even_in_kernel(tiles_per_seq, x_ref, g_ref, w_ref, cos_ref, sin_ref, qs_ref, ks_ref, dec_ref,
                    cd_ref, rn_ref, ya_ref, q4_ref, q16_ref, k4_ref, k16_ref, v4_ref, v16_ref,
                    slab_ref, st_ref):
    tm = x_ref.shape[0]
    chunk = RET_CHUNK

    @pl.when(pl.program_id(0) % tiles_per_seq == 0)
    def _():
        st_ref[...] = jnp.zeros_like(st_ref)

    x = x_ref[...]
    ms = jnp.mean(x * x, axis=-1, keepdims=True)
    hn = (x * lax.rsqrt(ms + EPS) * g_ref[...]).astype(BF16)

    def proj(c):
        return jnp.dot(hn, w_ref[:, c * B_W:(c + 1) * B_W], preferred_element_type=F32)

    cos = cos_ref[...]
    sin = sin_ref[...]
    lane = lax.broadcasted_iota(jnp.int32, (tm, LANES), 1)
    half = RET_DK // 2
    first_half = (lane % RET_DK) < half

    def rot(t):
        partner = jnp.where(first_half, pltpu.roll(t, LANES - half, 1), pltpu.roll(t, half, 1))
        return t * cos + partner * sin

    groups = [slice(s * LANES, (s + 1) * LANES) for s in range(B_W // LANES)]

    r = proj(0)
    q_rot = [rot(r[:, sl]) for sl in groups]
    q = [t.astype(BF16) for t in q_rot]
    q_in = [(t * qs_ref[:, sl]).astype(BF16) for t, sl in zip(q_rot, groups)]
    r = proj(1)
    k_rot = [rot(r[:, sl]) * (RET_DK ** -0.5) for sl in groups]
    k = [t.astype(BF16) for t in k_rot]
    k_out = [(t * ks_ref[:, sl]).astype(BF16) for t, sl in zip(k_rot, groups)]
    v = jnp.concatenate([proj(2), proj(3)], axis=1).astype(BF16)
    gate = rn_ref[...] * _silu(jnp.concatenate([proj(4), proj(5)], axis=1))

    def dilated_pieces(c, slot, r4_ref, r16_ref, scale):
        def project():
            r = proj(c)
            if scale is not None:
                r = r * scale
            for s, sl in enumerate(groups):
                slab_ref[slot, s] = r[:, sl]

        def by4():
            for s, sl in enumerate(groups):
                for rho in range(4):
                    r4_ref[rho, :, sl] = slab_ref[slot, s, pl.ds(rho, tm // 4, stride=4), :].astype(BF16)

        def by16():
            for s, sl in enumerate(groups):
                for rho in range(16):
                    r16_ref[rho, :, sl] = slab_ref[slot, s, pl.ds(rho, tm // 16, stride=16), :].astype(BF16)

        return [project, by4, by16]

    fill = iter(dilated_pieces(6, 0, q4_ref, q16_ref, DIL_DH ** -0.5)
                + dilated_pieces(7, 1, k4_ref, k16_ref, None)
                + dilated_pieces(8, 2, v4_ref, v16_ref, None))

    head0 = lax.broadcasted_iota(jnp.int32, (chunk, LANES), 1) < RET_DK
    heads = range(RET_HEADS)
    hs = [slice(h * RET_DV, (h + 1) * RET_DV) for h in heads]
    st = [st_ref[h] for h in heads]
    for ci in range(tm // chunk):
        rows = slice(ci * chunk, (ci + 1) * chunk)
        mask = [head0 if h % 2 == 0 else jnp.logical_not(head0) for h in heads]
        scores = [lax.dot_general(jnp.where(mask[h], q[h // 2][rows], 0), k[h // 2][rows], NT_DIMS,
                                  preferred_element_type=F32) * dec_ref[h] for h in heads]
        _emit(fill, 1)
        intra = [jnp.dot(scores[h].astype(BF16), v[rows, hs[h]], preferred_element_type=F32)
                 for h in heads]
        inter = [jnp.dot(jnp.where(mask[h], q_in[h // 2][rows], 0), st[h].astype(BF16),
                         preferred_element_type=F32) for h in heads]
        st = [st[h] * cd_ref[h] + lax.dot_general(k_out[h // 2][rows], v[rows, hs[h]], TN_DIMS,
                                                  preferred_element_type=F32) for h in heads]
        _emit(fill, 1)
        for h in heads:
            y = intra[h] + inter[h]
            yc = y - jnp.mean(y, axis=-1, keepdims=True)
            yn = yc * lax.rsqrt(jnp.mean(yc * yc, axis=-1, keepdims=True) + EPS)
            ya_ref[rows, hs[h]] = (yn * gate[rows, hs[h]]).astype(BF16)
        _emit(fill, 1)
    for run in fill:
        run()
    for h in heads:
        st_ref[h] = st[h]


def _even_in(x, g, w, ret_norm, seq, riders):
    tokens, d = x.shape
    bn = tokens // seq
    tm = EVEN_TM
    tps = seq // tm
    decay, q_scale, k_scale, chunk_decay, cos_t, sin_t = _retention_tables(seq)
    q_scale = jnp.tile(q_scale, (tm // RET_CHUNK, 1))
    k_scale = jnp.tile(k_scale, (tm // RET_CHUNK, 1))
    const = lambda shape: pl.BlockSpec(shape, lambda i: (0,) * len(shape), pipeline_mode=pl.Buffered(1))
    flat = lambda w_: pl.BlockSpec((tm, w_), lambda i: (i, 0))
    by = lambda r: pl.BlockSpec((None, r, tm // r, B_W), lambda i: (i // tps, 0, i % tps, 0))
    sds = jax.ShapeDtypeStruct
    lay = [sds((bn, 4, seq // 4, B_W), BF16), sds((bn, 16, seq // 16, B_W), BF16)]
    steps = tokens // tm
    in_specs = [
        flat(d), const((1, d)), const(w.shape),
        pl.BlockSpec((tm, LANES), lambda i: (i % tps, 0)),
        pl.BlockSpec((tm, LANES), lambda i: (i % tps, 0)),
        const((tm, A_QK)), const((tm, A_QK)),
        const(decay.shape), const(chunk_decay.shape), const((1, A_V)),
    ]
    out_specs = [flat(A_V)] + [by(4), by(16)] * 3
    rider_in, rider_out, rider_shapes = _rider_specs(riders, steps)
    return pl.pallas_call(
        _with_riders(functools.partial(_even_in_kernel, tps), len(in_specs), len(out_specs), len(riders)),
        grid=(steps,),
        in_specs=in_specs + rider_in,
        out_specs=out_specs + rider_out,
        out_shape=[sds((tokens, A_V), BF16)] + lay * 3 + rider_shapes,
        scratch_shapes=[pltpu.VMEM((3, B_W // LANES, tm, LANES), F32),
                        pltpu.VMEM((RET_HEADS, 2 * RET_DK, RET_DV), F32)],
        compiler_params=_params("arbitrary"),
        name="even_in",
    )(x, g, w, cos_t, sin_t, q_scale, k_scale, decay, chunk_decay, ret_norm, *[a for a, _ in riders])


def _rel_bucket(dist):
    max_exact = REL_BUCKETS // 2
    d = dist.astype(F32)
    large = max_exact + (jnp.log(jnp.maximum(d, 1.0) / max_exact) / math.log(REL_MAX_DIST / max_exact)
                         * (REL_BUCKETS - max_exact)).astype(jnp.int32)
    large = jnp.minimum(large, REL_BUCKETS - 1)
    return jnp.where(dist < max_exact, dist, large)


DIL_BASE = 4
DIL_UNROLL = 8


def _dilated_buckets():
    lb = DIL_BLOCK
    slot = jnp.arange(lb)
    per = lb // DIL_BASE
    regrouped = DIL_BASE * (slot % per) + slot // per
    tables = []
    for (window, dilation), token in zip(DIL_BRANCHES, (regrouped, slot, slot)):
        a_idx = token[:, None]
        c_idx = jnp.concatenate([token, lb + token])[None, :]
        dist = lb + a_idx - c_idx
        valid = (dist >= 0) & (dist <= window // dilation)
        tables.append(jnp.where(valid, _rel_bucket(jnp.maximum(dist, 0) * dilation), -1))
    return jnp.stack(tables).astype(jnp.int32)


def _dilated_kernel(rb_ref, bucket_ref, q4_ref, k4_ref, v4_ref, q16_ref, k16_ref, v16_ref, y_ref,
                    bias_ref, acc_ref, m_ref, l_ref):
    lb = DIL_BLOCK
    seq = y_ref.shape[0]
    pair = pl.program_id(1)

    @pl.when(pl.program_id(0) == 0)
    def _():
        for br in range(len(DIL_BRANCHES)):
            bucket = bucket_ref[br]
            for a in range(2):
                h = 2 * pair + a
                t = jnp.full((lb, 2 * lb), MASK_VALUE, F32)
                for j in range(REL_BUCKETS):
                    t = jnp.where(bucket == j, rb_ref[j, h], t)
                bias_ref[br * DIL_HEADS + h] = t

    lane = lax.broadcasted_iota(jnp.int32, (lb, LANES), 1)
    head0 = lane < DIL_DH

    def attend(br, blocks):
        masks = (head0, jnp.logical_not(head0))
        bias = [bias_ref[br * DIL_HEADS + 2 * pair + a] for a in range(2)]
        ks = [jnp.concatenate([kp, kc], axis=0) for _, _, kp, kc, _, _ in blocks]
        vs = [jnp.concatenate([vp, vc], axis=0) for _, _, _, _, vp, vc in blocks]
        pens = [jnp.where(first, MASK_VALUE, 0.0) for first, *_ in blocks]
        scores = [[lax.dot_general(jnp.where(masks[a], blk[1], 0), k, NT_DIMS,
                                   preferred_element_type=F32) + bias[a] for a in range(2)]
                  for blk, k in zip(blocks, ks)]
        scores = [[jnp.concatenate([s[:, 0:lb] + pen, s[:, lb:]], axis=1) for s in pair_s]
                  for pair_s, pen in zip(scores, pens)]
        maxes = [[jnp.max(s, axis=-1, keepdims=True) for s in pair_s] for pair_s in scores]
        exps = [[jnp.exp(s - m) for s, m in zip(pair_s, pair_m)] for pair_s, pair_m in zip(scores, maxes)]
        sums = [[jnp.sum(e, axis=-1, keepdims=True) for e in pair_e] for pair_e in exps]
        outs = [[jnp.dot(e.astype(BF16), v, preferred_element_type=F32) for e in pair_e]
                for pair_e, v in zip(exps, vs)]
        return [tuple(jnp.where(head0, x[0], x[1]) for x in (o, m, l))
                for o, m, l in zip(outs, maxes, sums)]

    def block_rows(n):
        r0 = pl.multiple_of(n * lb, lb)
        rp = pl.multiple_of(jnp.maximum(n - 1, 0) * lb, lb)
        return pl.ds(r0, lb), pl.ds(rp, lb)

    base = DIL_BASE
    per = lb // base
    class_rows = seq // base

    def unit_blocks(it, carry):
        blocks, dests = [], []
        for j in range(DIL_UNROLL):
            n = it * DIL_UNROLL + j
            r0 = pl.multiple_of(n * per, per)
            rp = pl.multiple_of(jnp.maximum(n - 1, 0) * per, per)
            gather = lambda ref, at: jnp.concatenate([ref[rho, pl.ds(at, per), :] for rho in range(base)],
                                                     axis=0)
            blocks.append((n == 0, gather(q4_ref, r0), gather(k4_ref, rp), gather(k4_ref, r0),
                           gather(v4_ref, rp), gather(v4_ref, r0)))
            dests.append(r0)
        for r0, (acc, m, l) in zip(dests, attend(0, blocks)):
            for rho in range(base):
                rows = pl.ds(rho * class_rows + r0, per)
                piece = slice(rho * per, (rho + 1) * per)
                acc_ref[rows, :] = acc[piece]
                m_ref[rows, :] = m[piece]
                l_ref[rows, :] = l[piece]
        return carry

    lax.fori_loop(0, seq // (lb * DIL_UNROLL), unit_blocks, 0)

    def dilated_blocks(br, r, q_ref, k_ref, v_ref):
        nb = seq // (r * lb)
        sub = r // base

        def body(it, carry):
            blocks, dests = [], []
            for j in range(DIL_UNROLL):
                idx = it * DIL_UNROLL + j
                rho, n = idx // nb, idx % nb
                cur, prev = block_rows(n)
                blocks.append((n == 0, q_ref[rho, cur, :], k_ref[rho, prev, :], k_ref[rho, cur, :],
                               v_ref[rho, prev, :], v_ref[rho, cur, :]))
                start = (rho % base) * class_rows + n * (lb * sub) + rho // base
                dests.append(pl.ds(start, lb, stride=sub) if sub > 1
                             else pl.ds(pl.multiple_of(start, lb), lb))
            for rows, (acc, m, l) in zip(dests, attend(br, blocks)):
                m_old = m_ref[rows, :]
                m_new = jnp.maximum(m_old, m)
                c_old = jnp.exp(m_old - m_new)
                c_cur = jnp.exp(m - m_new)
                acc_ref[rows, :] = acc_ref[rows, :] * c_old + acc * c_cur
                l_ref[rows, :] = l_ref[rows, :] * c_old + l * c_cur
                m_ref[rows, :] = m_new
            return carry

        lax.fori_loop(0, r * nb // DIL_UNROLL, body, 0)

    dilated_blocks(1, DIL_BRANCHES[1][1], q4_ref, k4_ref, v4_ref)
    dilated_blocks(2, DIL_BRANCHES[2][1], q16_ref, k16_ref, v16_ref)

    def normalise(n, carry):
        for rho in range(base):
            rows = pl.ds(pl.multiple_of(rho * class_rows + n * lb, lb), lb)
            m_ref[pl.ds(n * (lb * base) + rho, lb, stride=base), :] = acc_ref[rows, :] / l_ref[rows, :]
        return carry

    lax.fori_loop(0, class_rows // lb, normalise, 0)

    def finish(n, carry):
        rows = pl.ds(pl.multiple_of(n * lb, lb), lb)
        y_ref[rows, :] = m_ref[rows, :].astype(y_ref.dtype)
        return carry

    lax.fori_loop(0, seq // lb, finish, 0)


def _dilated(layouts, rel_bias, bn, seq):
    lb = DIL_BLOCK
    assert all(w // r <= lb and seq % (r * lb) == 0 for w, r in DIL_BRANCHES)
    assert tuple(r for _, r in DIL_BRANCHES) == (1, 4, 16)
    nat = pl.BlockSpec((None, seq, LANES), lambda b, p: (b, 0, p))
    by = lambda r: pl.BlockSpec((None, r, seq // r, LANES), lambda b, p: (b, 0, 0, p))
    nbr = len(DIL_BRANCHES)
    return pl.pallas_call(
        _dilated_kernel,
        grid=(bn, DIL_HEADS // 2),
        in_specs=[pl.BlockSpec(memory_space=pltpu.SMEM),
                  pl.BlockSpec((nbr, lb, 2 * lb), lambda b, p: (0, 0, 0))]
        + [by(4)] * 3 + [by(16)] * 3,
        out_specs=nat,
        out_shape=jax.ShapeDtypeStruct((bn, seq, B_W), BF16),
        scratch_shapes=[pltpu.VMEM((nbr * DIL_HEADS, lb, 2 * lb), F32),
                        pltpu.VMEM((seq, LANES), F32), pltpu.VMEM((seq, LANES), F32),
                        pltpu.VMEM((seq, LANES), F32)],
        compiler_params=_params("arbitrary", "arbitrary"),
        name="dilated",
    )(rel_bias.astype(F32), _dilated_buckets(), *layouts)


def _even_out_kernel(x_ref, ya_ref, yb_ref, wa_ref, wb_ref, h_ref):
    acc = jnp.dot(ya_ref[...], wa_ref[...], preferred_element_type=F32)
    acc = acc + jnp.dot(yb_ref[...], wb_ref[...], preferred_element_type=F32)
    h_ref[...] = x_ref[...] + acc


def _even_out(x, ya, yb, w_out, tm):
    m = x.shape[0]
    row = lambda w: pl.BlockSpec((tm, w), lambda i: (i, 0))
    return pl.pallas_call(
        _even_out_kernel,
        grid=(m // tm,),
        in_specs=[row(D_MODEL), row(A_V), row(B_W),
                  pl.BlockSpec((A_V, D_MODEL), lambda i: (0, 0)),
                  pl.BlockSpec((B_W, D_MODEL), lambda i: (A_V // B_W, 0))],
        out_specs=row(D_MODEL),
        out_shape=jax.ShapeDtypeStruct((m, D_MODEL), F32),
        compiler_params=_params("arbitrary"),
        name="even_out",
    )(x, ya, yb, w_out, w_out)


FFN_TM = 512
FFN_R = 8
FFN_CHUNK = 256
FFN_DENSE_CHUNKS = 5


def _ffn_kernel(ntiles, tiles_per_seq, final, h0_ref, hnext_ref, hprev_ref, g_ref, wup_ref, cw_ref,
                cb_ref, wd_ref, fg_ref, zero_ref, where_ref, o_ref, slab_in, slab_out, hn_cur, hn_next,
                act_ref, down_ref, halo_ref):
    i = pl.program_id(0)
    tm = o_ref.shape[0]
    rows = tm // FFN_R
    nslab = D_MODEL // LANES
    ck = FFN_CHUNK
    group = lambda g: slice(g * rows, (g + 1) * rows)
    lanes = lambda s: slice(s * LANES, (s + 1) * LANES)
    anywhere = pl.multiple_of(where_ref[0], BF16_SUBLANES)

    def head_pieces(h_ref):
        def stage(s):
            def run():
                slab_in[s] = h_ref[:, lanes(s)]
                return slab_in[s, pl.ds(anywhere, SUBLANES), :]
            return run

        def normalise(g):
            def run():
                x = jnp.concatenate([slab_in[s, pl.ds(g, rows, stride=FFN_R), :] for s in range(nslab)],
                                    axis=1)
                ms = jnp.mean(x * x, axis=-1, keepdims=True)
                hn_next[group(g), :] = (x * lax.rsqrt(ms + EPS) * g_ref[...]).astype(BF16)
                return hn_next[pl.ds(anywhere, BF16_SUBLANES), 0:LANES].astype(F32)[0:SUBLANES]
            return run

        return [stage(s) for s in range(nslab)] + [normalise(g) for g in range(FFN_R)]

    def tail_pieces():
        def unpermute(g):
            def run():
                for s in range(nslab):
                    slab_out[s, pl.ds(g, rows, stride=FFN_R), :] = down_ref[group(g), lanes(s)]
                return slab_out[nslab - 1, pl.ds(anywhere, SUBLANES), :]
            return run

        def store(r0, r1):
            def run():
                y = hprev_ref[r0:r1, :] + jnp.concatenate([slab_out[s, r0:r1, :] for s in range(nslab)],
                                                          axis=1)
                if final:
                    ms = jnp.mean(y * y, axis=-1, keepdims=True)
                    y = y * lax.rsqrt(ms + EPS) * fg_ref[...]
                o_ref[r0:r1, :] = y
                return o_ref[pl.ds(anywhere, SUBLANES), 0:LANES]
            return run

        return ([unpermute(g) for g in range(FFN_R)]
                + [store(g * rows, (g + 1) * rows) for g in range(FFN_R)])

    @pl.when(i == 0)
    def _():
        for run in head_pieces(h0_ref):
            run()
        down_ref[...] = jnp.zeros_like(down_ref)
        halo_ref[...] = jnp.zeros_like(halo_ref)

    @pl.when(i < ntiles)
    def _():
        hn_cur[...] = hn_next[...]
        fill = iter(tail_pieces() + head_pieces(hnext_ref))
        seq_start = i % tiles_per_seq == 0
        first_row = lax.broadcasted_iota(jnp.int32, (rows, ck), 0) == 0

        def shift_down(block, fill_row):
            return jnp.where(first_row, fill_row, pltpu.roll(block, 1, 0))

        zero_bits = zero_ref[...]

        def conv(cols, slot, tokens):
            u = jnp.dot(hn_cur[...], wup_ref[:, cols], preferred_element_type=F32)
            ug = [u[group(g), :] for g in range(FFN_R)]
            tails = jnp.where(seq_start, 0.0, halo_ref[slot])
            prev2 = shift_down(ug[FFN_R - 2], tails[SUBLANES - 1:SUBLANES, :])
            prev1 = shift_down(ug[FFN_R - 1], tails[2 * SUBLANES - 1:2 * SUBLANES, :])
            halo_ref[slot] = jnp.concatenate(
                [ug[FFN_R - 2][rows - SUBLANES:, :], ug[FFN_R - 1][rows - SUBLANES:, :]], axis=0)
            back1 = [prev1] + ug[:FFN_R - 1]
            back2 = [prev2, prev1] + ug[:FFN_R - 2]
            cw = cw_ref[:, cols]
            cb = cb_ref[:, cols]
            if tokens:
                cb = cb + jnp.tile(_anchor(tokens, zero_bits)[0:1, :], (1, ck // LANES))
            return [cw[0:1, :] * back2[g] + cw[1:2, :] * back1[g] + cw[2:3, :] * ug[g] + cb
                    for g in range(FFN_R)]

        tokens = []
        nchunk = D_FF // ck
        for c in range(nchunk):
            per_point = 2 if c < FFN_DENSE_CHUNKS else 1
            gate = conv(slice(c * ck, (c + 1) * ck), 2 * c, tokens)
            tokens = _emit(fill, per_point)
            value = conv(slice(D_FF + c * ck, D_FF + (c + 1) * ck), 2 * c + 1, tokens)
            tokens = _emit(fill, per_point)
            for g in range(FFN_R):
                act_ref[group(g), c * ck:(c + 1) * ck] = (_silu(gate[g]) * value[g]).astype(BF16)
        down_ref[...] = jnp.dot(act_ref[...], wd_ref[...], preferred_element_type=F32)
        for run in fill:
            run()

    @pl.when(i == ntiles)
    def _():
        for run in tail_pieces():
            run()


def _ffn(h, g, w_up, conv_w, conv_b, w_down, final_g, seq, final, riders=()):
    m = h.shape[0]
    tm = FFN_TM
    assert D_FF % FFN_CHUNK == 0 and tm % (FFN_R * 2 * SUBLANES) == 0 and seq % tm == 0
    const = lambda shape: pl.BlockSpec(shape, lambda i: (0, 0), pipeline_mode=pl.Buffered(1))
    ntiles = m // tm
    tile = lambda at: pl.BlockSpec((tm, D_MODEL), lambda i: (at(i), 0))
    in_specs = [
        const((tm, D_MODEL)),
        tile(lambda i: jnp.minimum(i + 1, ntiles - 1)),
        tile(lambda i: jnp.maximum(i - 1, 0)),
        const((1, D_MODEL)),
        const((D_MODEL, 2 * D_FF)), const((CONV_WIDTH, 2 * D_FF)), const((1, 2 * D_FF)),
        const((D_FF, D_MODEL)),
        const((1, D_MODEL)),
        const((SUBLANES, LANES)),
        pl.BlockSpec(memory_space=pltpu.SMEM),
    ]
    rider_in, rider_out, rider_shapes = _rider_specs(riders, ntiles)
    return pl.pallas_call(
        _with_riders(functools.partial(_ffn_kernel, ntiles, seq // tm, final), len(in_specs), 1,
                     len(riders)),
        grid=(ntiles + 1,),
        in_specs=in_specs + rider_in,
        out_specs=[tile(lambda i: jnp.maximum(i - 1, 0))] + rider_out,
        out_shape=[jax.ShapeDtypeStruct((m, D_MODEL), F32)] + rider_shapes,
        scratch_shapes=[
            pltpu.VMEM((D_MODEL // LANES, tm, LANES), F32),
            pltpu.VMEM((D_MODEL // LANES, tm, LANES), F32),
            pltpu.VMEM((tm, D_MODEL), BF16),
            pltpu.VMEM((tm, D_MODEL), BF16),
            pltpu.VMEM((tm, D_FF), BF16),
            pltpu.VMEM((tm, D_MODEL), F32),
            pltpu.VMEM((2 * D_FF // FFN_CHUNK, 2 * SUBLANES, FFN_CHUNK), F32),
        ],
        compiler_params=_params("arbitrary"),
        name="conv_ffn",
    )(h, h, h, g, w_up, conv_w.astype(F32), conv_b.astype(F32).reshape(1, -1), w_down, final_g,
      jnp.zeros((SUBLANES, LANES), jnp.int32), jnp.zeros((1,), jnp.int32), *[a for a, _ in riders])


HGRN_GROUP = 128
HGRN_TM = 256
HGRN_PROJ_COLS = 256


def _hgrn_group(proj_ref, rows, lower_all, gain, causal, tri, st, fill):
    c = HGRN_CHUNK
    heads = range(HGRN_HEADS)
    hk = [slice(h * HGRN_DK, (h + 1) * HGRN_DK) for h in heads]
    hv = [slice(h * HGRN_DV, (h + 1) * HGRN_DV) for h in heads]
    chunks = [slice(n * c, (n + 1) * c) for n in range(HGRN_GROUP // c)]

    q = _silu(proj_ref[rows, 0:C_K])
    fg = lower_all + (1.0 - lower_all) * jax.nn.sigmoid(proj_ref[rows, C_K:2 * C_K])
    k = 1.0 - fg
    log_f = jnp.log(fg)
    hi = log_f.astype(BF16)
    rem = log_f - hi.astype(F32)
    mid = rem.astype(BF16)
    lo = (rem - mid.astype(F32)).astype(BF16)
    _emit(fill, 2)
    parts = jnp.dot(tri, jnp.concatenate([hi, mid, lo], axis=1), preferred_element_type=F32)
    b = parts[:, 0:C_K] + parts[:, C_K:2 * C_K] + parts[:, 2 * C_K:3 * C_K]
    v = proj_ref[rows, 2 * C_K:2 * C_K + C_V].astype(BF16)
    q_t = (q * jnp.exp(b)).astype(BF16)
    k_t = (k * jnp.exp(-b)).astype(BF16)
    _emit(fill, 2)
    scores = [lax.dot_general(q_t[:, hk[h]], k_t[:, hk[h]], NT_DIMS, preferred_element_type=F32)
              for h in heads]
    scores = [jnp.where(causal, s, 0.0).astype(BF16) for s in scores]
    intra = [jnp.dot(scores[h], v[:, hv[h]], preferred_element_type=F32) for h in heads]
    b_last = [b[cs.stop - 1:cs.stop, :] for cs in chunks]
    k_end = [(k[cs, :] * jnp.exp(bl - b[cs, :])).astype(BF16) for cs, bl in zip(chunks, b_last)]
    decay = [jnp.exp(bl) for bl in b_last]
    _emit(fill, 2)
    st = list(st)
    inter = [[] for _ in heads]
    for n, cs in enumerate(chunks):
        for h in heads:
            inter[h].append(lax.dot_general(q_t[cs, hk[h]], st[h].astype(BF16), NT_DIMS,
                                            preferred_element_type=F32))
            st[h] = st[h] * decay[n][:, hk[h]] + lax.dot_general(
                v[cs, hv[h]], k_end[n][:, hk[h]], TN_DIMS, preferred_element_type=F32)
    gate = gain * _silu(proj_ref[rows, 2 * C_K + C_V:2 * C_K + 2 * C_V])
    _emit(fill, 2)
    outs = []
    for h in heads:
        y = intra[h] + jnp.concatenate(inter[h], axis=0)
        yn = y * lax.rsqrt(jnp.mean(y * y, axis=-1, keepdims=True) + EPS)
        outs.append((yn * gate[:, hv[h]]).astype(BF16))
    return outs, st


def _hgrn_layer_kernel(layer, tiles_per_seq, h0_ref, hnext_ref, g_ref, w_ref, lbp_ref, on_ref, o_ref,
                       proj_a, proj_b, st_ref):
    t = pl.program_id(0)
    tm = o_ref.shape[0]
    grp = HGRN_GROUP

    def projection(h_ref, dst):
        x = h_ref[...]
        ms = jnp.mean(x * x, axis=-1, keepdims=True)
        hn = (x * lax.rsqrt(ms + EPS) * g_ref[...]).astype(BF16)

        def piece(c):
            cols = slice(c * HGRN_PROJ_COLS, (c + 1) * HGRN_PROJ_COLS)

            def run():
                dst[:, cols] = jnp.dot(hn, w_ref[:, cols], preferred_element_type=F32)
            return run

        return [piece(c) for c in range(ODD_IN // HGRN_PROJ_COLS)]

    @pl.when(t == 0)
    def _():
        for run in projection(h0_ref, proj_a):
            run()

    @pl.when(t % tiles_per_seq == 0)
    def _():
        st_ref[...] = jnp.zeros_like(st_ref)

    def step(src, dst):
        fill = iter(projection(hnext_ref, dst))
        lbp = lbp_ref[...]
        ex = jnp.exp(lbp - jnp.max(lbp, axis=0, keepdims=True))
        sm = ex / jnp.sum(ex, axis=0, keepdims=True)
        lower_all = jnp.sum(sm[0:layer + 1, :], axis=0, keepdims=True) - sm[0:1, :]
        row = lax.broadcasted_iota(jnp.int32, (grp, grp), 0)
        colm = lax.broadcasted_iota(jnp.int32, (grp, grp), 1)
        causal = jnp.logical_and(row // HGRN_CHUNK == colm // HGRN_CHUNK, row >= colm)
        tri = jnp.where(causal, 1.0, 0.0).astype(BF16)
        gain = on_ref[...]
        st = [st_ref[h] for h in range(HGRN_HEADS)]
        for r0 in range(0, tm, grp):
            outs, st = _hgrn_group(src, slice(r0, r0 + grp), lower_all, gain, causal, tri, st, fill)
            for h, y in enumerate(outs):
                o_ref[r0:r0 + grp, h * HGRN_DV:(h + 1) * HGRN_DV] = y
        for run in fill:
            run()
        for h in range(HGRN_HEADS):
            st_ref[h] = st[h]

    @pl.when(t % 2 == 0)
    def _():
        step(proj_a, proj_b)

    @pl.when(t % 2 == 1)
    def _():
        step(proj_b, proj_a)


def _hgrn_layer(h, g, w, hgrn_lb, out_norm, layer, seq):
    tokens, d = h.shape
    tm = HGRN_TM
    ntiles = tokens // tm
    const = lambda shape: pl.BlockSpec(shape, lambda t: (0, 0), pipeline_mode=pl.Buffered(1))
    return pl.pallas_call(
        functools.partial(_hgrn_layer_kernel, layer, seq // tm),
        grid=(ntiles,),
        in_specs=[
            const((tm, d)),
            pl.BlockSpec((tm, d), lambda t: (jnp.minimum(t + 1, ntiles - 1), 0)),
            const((1, d)), const(w.shape), const(hgrn_lb.shape), const((1, C_V)),
        ],
        out_specs=pl.BlockSpec((tm, C_V), lambda t: (t, 0)),
        out_shape=jax.ShapeDtypeStruct((tokens, C_V), BF16),
        scratch_shapes=[pltpu.VMEM((tm, ODD_IN), F32), pltpu.VMEM((tm, ODD_IN), F32),
                        pltpu.VMEM((HGRN_HEADS, HGRN_DV, HGRN_DK), F32)],
        compiler_params=_params("arbitrary"),
        name="hgrn_layer",
    )(h, h, g, w, hgrn_lb, out_norm)


def _matmul_residual_kernel(h_ref, y_ref, w_ref, o_ref):
    o_ref[...] = h_ref[...] + jnp.dot(y_ref[...], w_ref[...], preferred_element_type=F32)


def _matmul_residual(h, y, w, tm):
    m, n = h.shape
    k = y.shape[1]
    return pl.pallas_call(
        _matmul_residual_kernel,
        grid=(m // tm,),
        in_specs=[
            pl.BlockSpec((tm, n), lambda i: (i, 0)),
            pl.BlockSpec((tm, k), lambda i: (i, 0)),
            pl.BlockSpec((k, n), lambda i: (0, 0)),
        ],
        out_specs=pl.BlockSpec((tm, n), lambda i: (i, 0)),
        out_shape=jax.ShapeDtypeStruct((m, n), F32),
        compiler_params=_params("arbitrary"),
        name="matmul_residual",
    )(h, y, w)


def kernel(x, even_w_in, even_w_out, ret_norm, rel_bias, odd_w_in, odd_w_out, hgrn_lb, hgrn_norm,
           mix_norm, ffn_norm, ffn_w_up, ffn_conv_w, ffn_conv_b, ffn_w_down, final_norm):
    bn, seq, d = x.shape
    tokens = bn * seq
    row = lambda a: a.reshape(1, -1).astype(F32)
    h = x.reshape(tokens, d)
    final_g = row(final_norm)

    outs = _even_in(h, row(mix_norm[0]), even_w_in[0].astype(BF16), row(ret_norm[0]), seq,
                    riders=[(even_w_out, 0), (ffn_w_up, 0), (ffn_w_down, 0)])
    ya, q4, q16, k4, k16, v4, v16, w_out0, w_up0, w_down0 = outs
    yb = _dilated((q4, k4, v4, q16, k16, v16), rel_bias, bn, seq)
    h = _even_out(h, ya, yb.reshape(tokens, B_W), w_out0, tm=1024)
    h, w_in1, w_out1, w_up1, w_down1 = _ffn(
        h, row(ffn_norm[0]), w_up0, ffn_conv_w[0], ffn_conv_b[0], w_down0, final_g, seq, final=False,
        riders=[(odd_w_in, 0), (odd_w_out, 0), (ffn_w_up, 1), (ffn_w_down, 1)])

    y = _hgrn_layer(h, row(mix_norm[1]), w_in1, hgrn_lb.astype(F32), row(hgrn_norm[0]), 1, seq)
    h = _matmul_residual(h, y, w_out1, tm=1024)
    h, = _ffn(h, row(ffn_norm[1]), w_up1, ffn_conv_w[1], ffn_conv_b[1], w_down1, final_g, seq,
              final=True)
    return h.reshape(bn, seq, d)
```

```python
import functools
import math

import numpy as np
import jax
import jax.numpy as jnp
from jax import lax
from jax.experimental import pallas as pl
from jax.experimental.pallas import tpu as pltpu

F32 = jnp.float32
BF16 = jnp.bfloat16

D_MODEL = 1024
RET_HEADS = 8
RET_DK = 64
RET_DV = 128
RET_CHUNK = 128
ROPE_BASE = 10000.0
DIL_HEADS = 8
DIL_DH = 64
DIL_BRANCHES = ((128, 1), (512, 4), (2048, 16))
DIL_BLOCK = 128
HGRN_HEADS = 8
HGRN_DK = 128
HGRN_DV = 128
HGRN_CHUNK = 32
REL_BUCKETS = 32
REL_MAX_DIST = 2048
D_FF = 2816
CONV_WIDTH = 3
EPS = 1e-6

A_QK = RET_HEADS * RET_DK
A_V = RET_HEADS * RET_DV
B_W = DIL_HEADS * DIL_DH
EVEN_IN = 2 * A_QK + 2 * A_V + 3 * B_W
C_K = HGRN_HEADS * HGRN_DK
C_V = HGRN_HEADS * HGRN_DV
ODD_IN = 2 * C_K + 2 * C_V

LANES = 128
SUBLANES = 8
VMEM_LIMIT = 56 * 1024 * 1024
MASK_VALUE = -1e30

NT_DIMS = (((1,), (1,)), ((), ()))
TN_DIMS = (((0,), (0,)), ((), ()))


def _params(*semantics):
    return pltpu.CompilerParams(dimension_semantics=semantics, vmem_limit_bytes=VMEM_LIMIT)


def _silu(x):
    return x * jax.nn.sigmoid(x)


def _emit(fill, count):
    tokens = []
    for _ in range(count):
        thunk = next(fill, None)
        if thunk is not None:
            token = thunk()
            if token is not None:
                tokens.append(token)
    return tokens


def _anchor(tokens, zero_bits):
    acc = zero_bits
    for t in tokens:
        acc = acc & pltpu.bitcast(t, jnp.int32)
    return pltpu.bitcast(acc, F32)


BF16_SUBLANES = 2 * SUBLANES


def _rider_specs(weights, steps):
    in_specs, out_specs, out_shapes = [], [], []
    for arr, layer in weights:
        _, rows, cols = arr.shape
        rep = 1
        while (rows * rep) % (steps * BF16_SUBLANES):
            rep *= 2
        slab = rows * rep // steps
        slab_index = lambda i, rep=rep: jnp.minimum(i, steps - 1) // rep
        in_specs.append(pl.BlockSpec((None, slab, cols),
                                     lambda i, layer=layer, at=slab_index: (layer, at(i), 0)))
        out_specs.append(pl.BlockSpec((slab, cols), lambda i, at=slab_index: (at(i), 0)))
        out_shapes.append(jax.ShapeDtypeStruct((rows, cols), BF16))
    return in_specs, out_specs, out_shapes


def _with_riders(body, n_in, n_out, n_riders):
    def kernel_fn(*refs):
        ins, refs = refs[:n_in], refs[n_in:]
        rider_in, refs = refs[:n_riders], refs[n_riders:]
        outs, refs = refs[:n_out], refs[n_out:]
        rider_out, scratch = refs[:n_riders], refs[n_riders:]
        for src, dst in zip(rider_in, rider_out):
            dst[...] = src[...].astype(dst.dtype)
        body(*ins, *outs, *scratch)
    return kernel_fn


def _retention_tables(seq):
    h = jnp.arange(RET_HEADS, dtype=F32)
    log_g = jnp.log1p(-jnp.exp2(-5.0 - h))
    pos = jnp.arange(RET_CHUNK, dtype=F32)
    diff = pos[:, None] - pos[None, :]
    decay = jnp.where(diff >= 0, jnp.exp(jnp.maximum(diff, 0.0)[None] * log_g[:, None, None]), 0.0)
    q_scale = jnp.repeat(jnp.exp((pos + 1.0)[:, None] * log_g[None, :]), RET_DK, axis=1)
    k_scale = jnp.repeat(jnp.exp((RET_CHUNK - 1.0 - pos)[:, None] * log_g[None, :]), RET_DK, axis=1)
    chunk_decay = jnp.broadcast_to(jnp.exp(RET_CHUNK * log_g)[:, None, None], (RET_HEADS, 1, RET_DV))
    inv = ROPE_BASE ** (-jnp.arange(0, RET_DK, 2, dtype=F32) / RET_DK)
    ang = jnp.arange(seq, dtype=F32)[:, None] * inv[None, :]
    cos, sin = jnp.cos(ang), jnp.sin(ang)
    reps = LANES // RET_DK
    cos_t = jnp.tile(jnp.concatenate([cos, cos], axis=1), (1, reps))
    sin_t = jnp.tile(jnp.concatenate([-sin, sin], axis=1), (1, reps))
    return decay, q_scale, k_scale, chunk_decay, cos_t, sin_t


EVEN_TM = 512


def _even_in_kernel(tiles_per_seq, x_ref, g_ref, w_ref, cos_ref, sin_ref, qs_ref, ks_ref, dec_ref,
                    cd_ref, rn_ref, ya_ref, q4_ref, q16_ref, k4_ref, k16_ref, v4_ref, v16_ref,
                    slab_ref, slab4_ref, st_ref):
    tm = x_ref.shape[0]
    chunk = RET_CHUNK

    @pl.when(pl.program_id(0) % tiles_per_seq == 0)
    def _():
        st_ref[...] = jnp.zeros_like(st_ref)

    x = x_ref[...]
    ms = jnp.mean(x * x, axis=-1, keepdims=True)
    hn = (x * lax.rsqrt(ms + EPS) * g_ref[...]).astype(BF16)

    def proj(c):
        return jnp.dot(hn, w_ref[:, c * B_W:(c + 1) * B_W], preferred_element_type=F32)

    cos = cos_ref[...]
    sin = sin_ref[...]
    lane = lax.broadcasted_iota(jnp.int32, (tm, LANES), 1)
    half = RET_DK // 2
    first_half = (lane % RET_DK) < half

    def rot(t):
        partner = jnp.where(first_half, pltpu.roll(t, LANES - half, 1), pltpu.roll(t, half, 1))
        return t * cos + partner * sin

    groups = [slice(s * LANES, (s + 1) * LANES) for s in range(B_W // LANES)]

    r = proj(0)
    q_rot = [rot(r[:, sl]) for sl in groups]
    q = [t.astype(BF16) for t in q_rot]
    q_in = [(t * qs_ref[:, sl]).astype(BF16) for t, sl in zip(q_rot, groups)]
    r = proj(1)
    k_rot = [rot(r[:, sl]) * (RET_DK ** -0.5) for sl in groups]
    k = [t.astype(BF16) for t in k_rot]
    k_out = [(t * ks_ref[:, sl]).astype(BF16) for t, sl in zip(k_rot, groups)]
    v = jnp.concatenate([proj(2), proj(3)], axis=1).astype(BF16)
    gate = rn_ref[...] * _silu(jnp.concatenate([proj(4), proj(5)], axis=1))

    def dilated_pieces(c, slot, r4_ref, r16_ref, scale):
        def project():
            r = proj(c)
            if scale is not None:
                r = r * scale
            for s, sl in enumerate(groups):
                slab_ref[slot, s] = r[:, sl]

        def by4():
            for s, sl in enumerate(groups):
                for rho in range(4):
                    t = slab_ref[slot, s, pl.ds(rho, tm // 4, stride=4), :]
                    r4_ref[rho, :, sl] = t.astype(BF16)
                    slab4_ref[slot, s, rho * (tm // 4):(rho + 1) * (tm // 4), :] = t

        def by16():
            for s, sl in enumerate(groups):
                for rho4 in range(4):
                    for c in range(4):
                        r16_ref[rho4 + 4 * c, :, sl] = slab4_ref[
                            slot, s, pl.ds(rho4 * (tm // 4) + c, tm // 16, stride=4), :].astype(BF16)

        return [project, by4, by16]

    fill = iter(dilated_pieces(6, 0, q4_ref, q16_ref, DIL_DH ** -0.5)
                + dilated_pieces(7, 1, k4_ref, k16_ref, None)
                + dilated_pieces(8, 2, v4_ref, v16_ref, None))

    head0 = lax.broadcasted_iota(jnp.int32, (chunk, LANES), 1) < RET_DK
    heads = range(RET_HEADS)
    hs = [slice(h * RET_DV, (h + 1) * RET_DV) for h in heads]
    st = [st_ref[h] for h in heads]
    for ci in range(tm // chunk):
        rows = slice(ci * chunk, (ci + 1) * chunk)
        mask = [head0 if h % 2 == 0 else jnp.logical_not(head0) for h in heads]
        scores = [lax.dot_general(jnp.where(mask[h], q[h // 2][rows], 0), k[h // 2][rows], NT_DIMS,
                                  preferred_element_type=F32) * dec_ref[h] for h in heads]
        _emit(fill, 1)
        intra = [jnp.dot(scores[h].astype(BF16), v[rows, hs[h]], preferred_element_type=F32)
                 for h in heads]
        inter = [jnp.dot(jnp.where(mask[h], q_in[h // 2][rows], 0), st[h].astype(BF16),
                         preferred_element_type=F32) for h in heads]
        st = [st[h] * cd_ref[h] + lax.dot_general(k_out[h // 2][rows], v[rows, hs[h]], TN_DIMS,
                                                  preferred_element_type=F32) for h in heads]
        _emit(fill, 1)
        for h in heads:
            y = intra[h] + inter[h]
            yc = y - jnp.mean(y, axis=-1, keepdims=True)
            yn = yc * lax.rsqrt(jnp.mean(yc * yc, axis=-1, keepdims=True) + EPS)
            ya_ref[rows, hs[h]] = (yn * gate[rows, hs[h]]).astype(BF16)
        _emit(fill, 1)
    for run in fill:
        run()
    for h in heads:
        st_ref[h] = st[h]


def _even_in(x, g, w, ret_norm, seq, riders):
    tokens, d = x.shape
    bn = tokens // seq
    tm = EVEN_TM
    tps = seq // tm
    decay, q_scale, k_scale, chunk_decay, cos_t, sin_t = _retention_tables(seq)
    q_scale = jnp.tile(q_scale, (tm // RET_CHUNK, 1))
    k_scale = jnp.tile(k_scale, (tm // RET_CHUNK, 1))
    const = lambda shape: pl.BlockSpec(shape, lambda i: (0,) * len(shape), pipeline_mode=pl.Buffered(1))
    flat = lambda w_: pl.BlockSpec((tm, w_), lambda i: (i, 0))
    by = lambda r: pl.BlockSpec((None, r, tm // r, B_W), lambda i: (i // tps, 0, i % tps, 0))
    sds = jax.ShapeDtypeStruct
    lay = [sds((bn, 4, seq // 4, B_W), BF16), sds((bn, 16, seq // 16, B_W), BF16)]
    steps = tokens // tm
    in_specs = [
        flat(d), const((1, d)), const(w.shape),
        pl.BlockSpec((tm, LANES), lambda i: (i % tps, 0)),
        pl.BlockSpec((tm, LANES), lambda i: (i % tps, 0)),
        const((tm, A_QK)), const((tm, A_QK)),
        const(decay.shape), const(chunk_decay.shape), const((1, A_V)),
    ]
    out_specs = [flat(A_V)] + [by(4), by(16)] * 3
    rider_in, rider_out, rider_shapes = _rider_specs(riders, steps)
    return pl.pallas_call(
        _with_riders(functools.partial(_even_in_kernel, tps), len(in_specs), len(out_specs), len(riders)),
        grid=(steps,),
        in_specs=in_specs + rider_in,
        out_specs=out_specs + rider_out,
        out_shape=[sds((tokens, A_V), BF16)] + lay * 3 + rider_shapes,
        scratch_shapes=[pltpu.VMEM((3, B_W // LANES, tm, LANES), F32),
                        pltpu.VMEM((3, B_W // LANES, tm, LANES), F32),
                        pltpu.VMEM((RET_HEADS, 2 * RET_DK, RET_DV), F32)],
        compiler_params=_params("arbitrary"),
        name="even_in",
    )(x, g, w, cos_t, sin_t, q_scale, k_scale, decay, chunk_decay, ret_norm, *[a for a, _ in riders])


def _rel_bucket(dist):
    max_exact = REL_BUCKETS // 2
    d = dist.astype(F32)
    large = max_exact + (jnp.log(jnp.maximum(d, 1.0) / max_exact) / math.log(REL_MAX_DIST / max_exact)
                         * (REL_BUCKETS - max_exact)).astype(jnp.int32)
    large = jnp.minimum(large, REL_BUCKETS - 1)
    return jnp.where(dist < max_exact, dist, large)


DIL_BASE = 4
DIL_UNROLL = 8


def _dilated_buckets():
    lb = DIL_BLOCK
    slot = jnp.arange(lb)
    per = lb // DIL_BASE
    regrouped = DIL_BASE * (slot % per) + slot // per
    tables = []
    for (window, dilation), token in zip(DIL_BRANCHES, (regrouped, slot, slot)):
        a_idx = token[:, None]
        c_idx = jnp.concatenate([token, lb + token])[None, :]
        dist = lb + a_idx - c_idx
        valid = (dist >= 0) & (dist <= window // dilation)
        tables.append(jnp.where(valid, _rel_bucket(jnp.maximum(dist, 0) * dilation), -1))
    return jnp.stack(tables).astype(jnp.int32)


def _dilated_kernel(rb_ref, bucket_ref, q4_ref, k4_ref, v4_ref, q16_ref, k16_ref, v16_ref, y_ref,
                    bias_ref, acc_ref, m_ref, l_ref):
    lb = DIL_BLOCK
    seq = y_ref.shape[0]
    pair = pl.program_id(1)

    @pl.when(pl.program_id(0) == 0)
    def _():
        for br in range(len(DIL_BRANCHES)):
            bucket = bucket_ref[br]
            for a in range(2):
                h = 2 * pair + a
                t = jnp.full((lb, 2 * lb), MASK_VALUE, F32)
                for j in range(REL_BUCKETS):
                    t = jnp.where(bucket == j, rb_ref[j, h], t)
                bias_ref[br * DIL_HEADS + h] = t

    lane = lax.broadcasted_iota(jnp.int32, (lb, LANES), 1)
    head0 = lane < DIL_DH

    def attend(br, blocks):
        masks = (head0, jnp.logical_not(head0))
        bias = [bias_ref[br * DIL_HEADS + 2 * pair + a] for a in range(2)]
        ks = [jnp.concatenate([kp, kc], axis=0) for _, _, kp, kc, _, _ in blocks]
        vs = [jnp.concatenate([vp, vc], axis=0) for _, _, _, _, vp, vc in blocks]
        pens = [jnp.where(first, MASK_VALUE, 0.0) for first, *_ in blocks]
        scores = [[lax.dot_general(jnp.where(masks[a], blk[1], 0), k, NT_DIMS,
                                   preferred_element_type=F32) + bias[a] for a in range(2)]
                  for blk, k in zip(blocks, ks)]
        scores = [[jnp.concatenate([s[:, 0:lb] + pen, s[:, lb:]], axis=1) for s in pair_s]
                  for pair_s, pen in zip(scores, pens)]
        maxes = [[jnp.max(s, axis=-1, keepdims=True) for s in pair_s] for pair_s in scores]
        exps = [[jnp.exp(s - m) for s, m in zip(pair_s, pair_m)] for pair_s, pair_m in zip(scores, maxes)]
        sums = [[jnp.sum(e, axis=-1, keepdims=True) for e in pair_e] for pair_e in exps]
        outs = [[jnp.dot(e.astype(BF16), v, preferred_element_type=F32) for e in pair_e]
                for pair_e, v in zip(exps, vs)]
        return [tuple(jnp.where(head0, x[0], x[1]) for x in (o, m, l))
                for o, m, l in zip(outs, maxes, sums)]

    def block_rows(n):
        r0 = pl.multiple_of(n * lb, lb)
        rp = pl.multiple_of(jnp.maximum(n - 1, 0) * lb, lb)
        return pl.ds(r0, lb), pl.ds(rp, lb)

    base = DIL_BASE
    per = lb // base
    class_rows = seq // base

    def unit_blocks(it, carry):
        blocks, dests = [], []
        for j in range(DIL_UNROLL):
            n = it * DIL_UNROLL + j
            r0 = pl.multiple_of(n * per, per)
            rp = pl.multiple_of(jnp.maximum(n - 1, 0) * per, per)
            gather = lambda ref, at: jnp.concatenate([ref[rho, pl.ds(at, per), :] for rho in range(base)],
                                                     axis=0)
            blocks.append((n == 0, gather(q4_ref, r0), gather(k4_ref, rp), gather(k4_ref, r0),
                           gather(v4_ref, rp), gather(v4_ref, r0)))
            dests.append(r0)
        for r0, (acc, m, l) in zip(dests, attend(0, blocks)):
            for rho in range(base):
                rows = pl.ds(rho * class_rows + r0, per)
                piece = slice(rho * per, (rho + 1) * per)
                acc_ref[rows, :] = acc[piece]
                m_ref[rows, :] = m[piece]
                l_ref[rows, :] = l[piece]
        return carry

    lax.fori_loop(0, seq // (lb * DIL_UNROLL), unit_blocks, 0)

    def dilated_blocks(br, r, q_ref, k_ref, v_ref):
        nb = seq // (r * lb)
        sub = r // base

        def body(it, carry):
            blocks, dests = [], []
            for j in range(DIL_UNROLL):
                idx = it * DIL_UNROLL + j
                rho, n = idx // nb, idx % nb
                cur, prev = block_rows(n)
                blocks.append((n == 0, q_ref[rho, cur, :], k_ref[rho, prev, :], k_ref[rho, cur, :],
                               v_ref[rho, prev, :], v_ref[rho, cur, :]))
                start = (rho % base) * class_rows + n * (lb * sub) + rho // base
                dests.append(pl.ds(start, lb, stride=sub) if sub > 1
                             else pl.ds(pl.multiple_of(start, lb), lb))
            for rows, (acc, m, l) in zip(dests, attend(br, blocks)):
                m_old = m_ref[rows, :]
                m_new = jnp.maximum(m_old, m)
                c_old = jnp.exp(m_old - m_new)
                c_cur = jnp.exp(m - m_new)
                acc_ref[rows, :] = acc_ref[rows, :] * c_old + acc * c_cur
                l_ref[rows, :] = l_ref[rows, :] * c_old + l * c_cur
                m_ref[rows, :] = m_new
            return carry

        lax.fori_loop(0, r * nb // DIL_UNROLL, body, 0)

    dilated_blocks(1, DIL_BRANCHES[1][1], q4_ref, k4_ref, v4_ref)
    dilated_blocks(2, DIL_BRANCHES[2][1], q16_ref, k16_ref, v16_ref)

    def normalise(n, carry):
        for rho in range(base):
            rows = pl.ds(pl.multiple_of(rho * class_rows + n * lb, lb), lb)
            m_ref[pl.ds(n * (lb * base) + rho, lb, stride=base), :] = acc_ref[rows, :] / l_ref[rows, :]
        return carry

    lax.fori_loop(0, class_rows // lb, normalise, 0)

    def finish(n, carry):
        rows = pl.ds(pl.multiple_of(n * lb, lb), lb)
        y_ref[rows, :] = m_ref[rows, :].astype(y_ref.dtype)
        return carry

    lax.fori_loop(0, seq // lb, finish, 0)


def _dilated(layouts, rel_bias, bn, seq):
    lb = DIL_BLOCK
    assert all(w // r <= lb and seq % (r * lb) == 0 for w, r in DIL_BRANCHES)
    assert tuple(r for _, r in DIL_BRANCHES) == (1, 4, 16)
    nat = pl.BlockSpec((None, seq, LANES), lambda b, p: (b, 0, p))
    by = lambda r: pl.BlockSpec((None, r, seq // r, LANES), lambda b, p: (b, 0, 0, p))
    nbr = len(DIL_BRANCHES)
    return pl.pallas_call(
        _dilated_kernel,
        grid=(bn, DIL_HEADS // 2),
        in_specs=[pl.BlockSpec(memory_space=pltpu.SMEM),
                  pl.BlockSpec((nbr, lb, 2 * lb), lambda b, p: (0, 0, 0))]
        + [by(4)] * 3 + [by(16)] * 3,
        out_specs=nat,
        out_shape=jax.ShapeDtypeStruct((bn, seq, B_W), BF16),
        scratch_shapes=[pltpu.VMEM((nbr * DIL_HEADS, lb, 2 * lb), F32),
                        pltpu.VMEM((seq, LANES), F32), pltpu.VMEM((seq, LANES), F32),
                        pltpu.VMEM((seq, LANES), F32)],
        compiler_params=_params("arbitrary", "arbitrary"),
        name="dilated",
    )(rel_bias.astype(F32), _dilated_buckets(), *layouts)


def _even_out_kernel(x_ref, ya_ref, yb_ref, wa_ref, wb_ref, h_ref):
    acc = jnp.dot(ya_ref[...], wa_ref[...], preferred_element_type=F32)
    acc = acc + jnp.dot(yb_ref[...], wb_ref[...], preferred_element_type=F32)
    h_ref[...] = x_ref[...] + acc


def _even_out(x, ya, yb, w_out, tm):
    m = x.shape[0]
    row = lambda w: pl.BlockSpec((tm, w), lambda i: (i, 0))
    return pl.pallas_call(
        _even_out_kernel,
        grid=(m // tm,),
        in_specs=[row(D_MODEL), row(A_V), row(B_W),
                  pl.BlockSpec((A_V, D_MODEL), lambda i: (0, 0)),
                  pl.BlockSpec((B_W, D_MODEL), lambda i: (A_V // B_W, 0))],
        out_specs=row(D_MODEL),
        out_shape=jax.ShapeDtypeStruct((m, D_MODEL), F32),
        compiler_params=_params("arbitrary"),
        name="even_out",
    )(x, ya, yb, w_out, w_out)


FFN_TM = 512
FFN_R = 8
FFN_CHUNK = 256
FFN_DENSE_CHUNKS = 5


def _ffn_kernel(ntiles, tiles_per_seq, final, h0_ref, hnext_ref, hprev_ref, g_ref, wup_ref, cw_ref,
                cb_ref, wd_ref, fg_ref, zero_ref, where_ref, o_ref, slab_in, slab_out, hn_cur, hn_next,
                act_ref, down_ref, halo_ref):
    i = pl.program_id(0)
    tm = o_ref.shape[0]
    rows = tm // FFN_R
    nslab = D_MODEL // LANES
    ck = FFN_CHUNK
    group = lambda g: slice(g * rows, (g + 1) * rows)
    lanes = lambda s: slice(s * LANES, (s + 1) * LANES)
    anywhere = pl.multiple_of(where_ref[0], BF16_SUBLANES)

    def head_pieces(h_ref):
        def stage(s):
            def run():
                slab_in[s] = h_ref[:, lanes(s)]
                return slab_in[s, pl.ds(anywhere, SUBLANES), :]
            return run

        def normalise(g):
            def run():
                x = jnp.concatenate([slab_in[s, pl.ds(g, rows, stride=FFN_R), :] for s in range(nslab)],
                                    axis=1)
                ms = jnp.mean(x * x, axis=-1, keepdims=True)
                hn_next[group(g), :] = (x * lax.rsqrt(ms + EPS) * g_ref[...]).astype(BF16)
                return hn_next[pl.ds(anywhere, BF16_SUBLANES), 0:LANES].astype(F32)[0:SUBLANES]
            return run

        return [stage(s) for s in range(nslab)] + [normalise(g) for g in range(FFN_R)]

    def tail_pieces():
        def unpermute(g):
            def run():
                for s in range(nslab):
                    slab_out[s, pl.ds(g, rows, stride=FFN_R), :] = down_ref[group(g), lanes(s)]
                return slab_out[nslab - 1, pl.ds(anywhere, SUBLANES), :]
            return run

        def store(r0, r1):
            def run():
                y = hprev_ref[r0:r1, :] + jnp.concatenate([slab_out[s, r0:r1, :] for s in range(nslab)],
                                                          axis=1)
                if final:
                    ms = jnp.mean(y * y, axis=-1, keepdims=True)
                    y = y * lax.rsqrt(ms + EPS) * fg_ref[...]
                o_ref[r0:r1, :] = y
                return o_ref[pl.ds(anywhere, SUBLANES), 0:LANES]
            return run

        return ([unpermute(g) for g in range(FFN_R)]
                + [store(g * rows, (g + 1) * rows) for g in range(FFN_R)])

    @pl.when(i == 0)
    def _():
        for run in head_pieces(h0_ref):
            run()
        down_ref[...] = jnp.zeros_like(down_ref)
        halo_ref[...] = jnp.zeros_like(halo_ref)

    @pl.when(i < ntiles)
    def _():
        hn_cur[...] = hn_next[...]
        fill = iter(tail_pieces() + head_pieces(hnext_ref))
        seq_start = i % tiles_per_seq == 0
        first_row = lax.broadcasted_iota(jnp.int32, (rows, ck), 0) == 0

        def shift_down(block, fill_row):
            return jnp.where(first_row, fill_row, pltpu.roll(block, 1, 0))

        zero_bits = zero_ref[...]

        def conv(cols, slot, tokens):
            u = jnp.dot(hn_cur[...], wup_ref[:, cols], preferred_element_type=F32)
            ug = [u[group(g), :] for g in range(FFN_R)]
            tails = jnp.where(seq_start, 0.0, halo_ref[slot])
            prev2 = shift_down(ug[FFN_R - 2], tails[SUBLANES - 1:SUBLANES, :])
            prev1 = shift_down(ug[FFN_R - 1], tails[2 * SUBLANES - 1:2 * SUBLANES, :])
            halo_ref[slot] = jnp.concatenate(
                [ug[FFN_R - 2][rows - SUBLANES:, :], ug[FFN_R - 1][rows - SUBLANES:, :]], axis=0)
            back1 = [prev1] + ug[:FFN_R - 1]
            back2 = [prev2, prev1] + ug[:FFN_R - 2]
            cw = cw_ref[:, cols]
            cb = cb_ref[:, cols]
            if tokens:
                cb = cb + jnp.tile(_anchor(tokens, zero_bits)[0:1, :], (1, ck // LANES))
            return [cw[0:1, :] * back2[g] + cw[1:2, :] * back1[g] + cw[2:3, :] * ug[g] + cb
                    for g in range(FFN_R)]

        tokens = []
        nchunk = D_FF // ck
        for c in range(nchunk):
            per_point = 2 if c < FFN_DENSE_CHUNKS else 1
            gate = conv(slice(c * ck, (c + 1) * ck), 2 * c, tokens)
            tokens = _emit(fill, per_point)
            value = conv(slice(D_FF + c * ck, D_FF + (c + 1) * ck), 2 * c + 1, tokens)
            tokens = _emit(fill, per_point)
            for g in range(FFN_R):
                act_ref[group(g), c * ck:(c + 1) * ck] = (_silu(gate[g]) * value[g]).astype(BF16)
        down_ref[...] = jnp.dot(act_ref[...], wd_ref[...], preferred_element_type=F32)
        for run in fill:
            run()

    @pl.when(i == ntiles)
    def _():
        for run in tail_pieces():
            run()


def _ffn(h, g, w_up, conv_w, conv_b, w_down, final_g, seq, final, riders=()):
    m = h.shape[0]
    tm = FFN_TM
    assert D_FF % FFN_CHUNK == 0 and tm % (FFN_R * 2 * SUBLANES) == 0 and seq % tm == 0
    const = lambda shape: pl.BlockSpec(shape, lambda i: (0, 0), pipeline_mode=pl.Buffered(1))
    ntiles = m // tm
    tile = lambda at: pl.BlockSpec((tm, D_MODEL), lambda i: (at(i), 0))
    in_specs = [
        const((tm, D_MODEL)),
        tile(lambda i: jnp.minimum(i + 1, ntiles - 1)),
        tile(lambda i: jnp.maximum(i - 1, 0)),
        const((1, D_MODEL)),
        const((D_MODEL, 2 * D_FF)), const((CONV_WIDTH, 2 * D_FF)), const((1, 2 * D_FF)),
        const((D_FF, D_MODEL)),
        const((1, D_MODEL)),
        const((SUBLANES, LANES)),
        pl.BlockSpec(memory_space=pltpu.SMEM),
    ]
    rider_in, rider_out, rider_shapes = _rider_specs(riders, ntiles)
    return pl.pallas_call(
        _with_riders(functools.partial(_ffn_kernel, ntiles, seq // tm, final), len(in_specs), 1,
                     len(riders)),
        grid=(ntiles + 1,),
        in_specs=in_specs + rider_in,
        out_specs=[tile(lambda i: jnp.maximum(i - 1, 0))] + rider_out,
        out_shape=[jax.ShapeDtypeStruct((m, D_MODEL), F32)] + rider_shapes,
        scratch_shapes=[
            pltpu.VMEM((D_MODEL // LANES, tm, LANES), F32),
            pltpu.VMEM((D_MODEL // LANES, tm, LANES), F32),
            pltpu.VMEM((tm, D_MODEL), BF16),
            pltpu.VMEM((tm, D_MODEL), BF16),
            pltpu.VMEM((tm, D_FF), BF16),
            pltpu.VMEM((tm, D_MODEL), F32),
            pltpu.VMEM((2 * D_FF // FFN_CHUNK, 2 * SUBLANES, FFN_CHUNK), F32),
        ],
        compiler_params=_params("arbitrary"),
        name="conv_ffn",
    )(h, h, h, g, w_up, conv_w.astype(F32), conv_b.astype(F32).reshape(1, -1), w_down, final_g,
      jnp.zeros((SUBLANES, LANES), jnp.int32), jnp.zeros((1,), jnp.int32), *[a for a, _ in riders])


HGRN_GROUP = 128
HGRN_TM = 256
HGRN_PROJ_COLS = 256


def _hgrn_group(proj_ref, rows, lower_all, gain, causal, tri, st, fill):
    c = HGRN_CHUNK
    heads = range(HGRN_HEADS)
    hk = [slice(h * HGRN_DK, (h + 1) * HGRN_DK) for h in heads]
    hv = [slice(h * HGRN_DV, (h + 1) * HGRN_DV) for h in heads]
    chunks = [slice(n * c, (n + 1) * c) for n in range(HGRN_GROUP // c)]

    q = _silu(proj_ref[rows, 0:C_K])
    fg = lower_all + (1.0 - lower_all) * jax.nn.sigmoid(proj_ref[rows, C_K:2 * C_K])
    k = 1.0 - fg
    log_f = jnp.log(fg)
    hi = log_f.astype(BF16)
    lo = (log_f - hi.astype(F32)).astype(BF16)
    _emit(fill, 2)
    parts = jnp.dot(tri, jnp.concatenate([hi, lo], axis=1), preferred_element_type=F32)
    b = parts[:, 0:C_K] + parts[:, C_K:2 * C_K]
    v = proj_ref[rows, 2 * C_K:2 * C_K + C_V].astype(BF16)
    q_t = (q * jnp.exp(b)).astype(BF16)
    k_t = (k * jnp.exp(-b)).astype(BF16)
    _emit(fill, 2)
    scores = [lax.dot_general(q_t[:, hk[h]], k_t[:, hk[h]], NT_DIMS, preferred_element_type=F32)
              for h in heads]
    scores = [jnp.where(causal, s, 0.0).astype(BF16) for s in scores]
    intra = [jnp.dot(scores[h], v[:, hv[h]], preferred_element_type=F32) for h in heads]
    b_last = [b[cs.stop - 1:cs.stop, :] for cs in chunks]
    k_end = [(k[cs, :] * jnp.exp(bl - b[cs, :])).astype(BF16) for cs, bl in zip(chunks, b_last)]
    decay = [jnp.exp(bl) for bl in b_last]
    _emit(fill, 2)
    st = list(st)
    inter = [[] for _ in heads]
    for n, cs in enumerate(chunks):
        for h in heads:
            inter[h].append(lax.dot_general(q_t[cs, hk[h]], st[h].astype(BF16), NT_DIMS,
                                            preferred_element_type=F32))
            st[h] = st[h] * decay[n][:, hk[h]] + lax.dot_general(
                v[cs, hv[h]], k_end[n][:, hk[h]], TN_DIMS, preferred_element_type=F32)
    gate = gain * _silu(proj_ref[rows, 2 * C_K + C_V:2 * C_K + 2 * C_V])
    _emit(fill, 2)
    outs = []
    for h in heads:
        y = intra[h] + jnp.concatenate(inter[h], axis=0)
        yn = y * lax.rsqrt(jnp.mean(y * y, axis=-1, keepdims=True) + EPS)
        outs.append((yn * gate[:, hv[h]]).astype(BF16))
    return outs, st


def _hgrn_layer_kernel(layer, tiles_per_seq, h0_ref, hnext_ref, g_ref, w_ref, lbp_ref, on_ref, o_ref,
                       proj_a, proj_b, st_ref):
    t = pl.program_id(0)
    tm = o_ref.shape[0]
    grp = HGRN_GROUP

    def projection(h_ref, dst):
        x = h_ref[...]
        ms = jnp.mean(x * x, axis=-1, keepdims=True)
        hn = (x * lax.rsqrt(ms + EPS) * g_ref[...]).astype(BF16)

        def piece(c):
            cols = slice(c * HGRN_PROJ_COLS, (c + 1) * HGRN_PROJ_COLS)

            def run():
                dst[:, cols] = jnp.dot(hn, w_ref[:, cols], preferred_element_type=F32)
            return run

        return [piece(c) for c in range(ODD_IN // HGRN_PROJ_COLS)]

    @pl.when(t == 0)
    def _():
        for run in projection(h0_ref, proj_a):
            run()

    @pl.when(t % tiles_per_seq == 0)
    def _():
        st_ref[...] = jnp.zeros_like(st_ref)

    def step(src, dst):
        fill = iter(projection(hnext_ref, dst))
        lbp = lbp_ref[...]
        ex = jnp.exp(lbp - jnp.max(lbp, axis=0, keepdims=True))
        sm = ex / jnp.sum(ex, axis=0, keepdims=True)
        lower_all = jnp.sum(sm[0:layer + 1, :], axis=0, keepdims=True) - sm[0:1, :]
        row = lax.broadcasted_iota(jnp.int32, (grp, grp), 0)
        colm = lax.broadcasted_iota(jnp.int32, (grp, grp), 1)
        causal = jnp.logical_and(row // HGRN_CHUNK == colm // HGRN_CHUNK, row >= colm)
        tri = jnp.where(causal, 1.0, 0.0).astype(BF16)
        gain = on_ref[...]
        st = [st_ref[h] for h in range(HGRN_HEADS)]
        for r0 in range(0, tm, grp):
            outs, st = _hgrn_group(src, slice(r0, r0 + grp), lower_all, gain, causal, tri, st, fill)
            for h, y in enumerate(outs):
                o_ref[r0:r0 + grp, h * HGRN_DV:(h + 1) * HGRN_DV] = y
        for run in fill:
            run()
        for h in range(HGRN_HEADS):
            st_ref[h] = st[h]

    @pl.when(t % 2 == 0)
    def _():
        step(proj_a, proj_b)

    @pl.when(t % 2 == 1)
    def _():
        step(proj_b, proj_a)


def _hgrn_layer(h, g, w, hgrn_lb, out_norm, layer, seq):
    tokens, d = h.shape
    tm = HGRN_TM
    ntiles = tokens // tm
    const = lambda shape: pl.BlockSpec(shape, lambda t: (0, 0), pipeline_mode=pl.Buffered(1))
    return pl.pallas_call(
        functools.partial(_hgrn_layer_kernel, layer, seq // tm),
        grid=(ntiles,),
        in_specs=[
            const((tm, d)),
            pl.BlockSpec((tm, d), lambda t: (jnp.minimum(t + 1, ntiles - 1), 0)),
            const((1, d)), const(w.shape), const(hgrn_lb.shape), const((1, C_V)),
        ],
        out_specs=pl.BlockSpec((tm, C_V), lambda t: (t, 0)),
        out_shape=jax.ShapeDtypeStruct((tokens, C_V), BF16),
        scratch_shapes=[pltpu.VMEM((tm, ODD_IN), F32), pltpu.VMEM((tm, ODD_IN), F32),
                        pltpu.VMEM((HGRN_HEADS, HGRN_DV, HGRN_DK), F32)],
        compiler_params=_params("arbitrary"),
        name="hgrn_layer",
    )(h, h, g, w, hgrn_lb, out_norm)


def _matmul_residual_kernel(h_ref, y_ref, w_ref, o_ref):
    o_ref[...] = h_ref[...] + jnp.dot(y_ref[...], w_ref[...], preferred_element_type=F32)


def _matmul_residual(h, y, w, tm):
    m, n = h.shape
    k = y.shape[1]
    return pl.pallas_call(
        _matmul_residual_kernel,
        grid=(m // tm,),
        in_specs=[
            pl.BlockSpec((tm, n), lambda i: (i, 0)),
            pl.BlockSpec((tm, k), lambda i: (i, 0)),
            pl.BlockSpec((k, n), lambda i: (0, 0)),
        ],
        out_specs=pl.BlockSpec((tm, n), lambda i: (i, 0)),
        out_shape=jax.ShapeDtypeStruct((m, n), F32),
        compiler_params=_params("arbitrary"),
        name="matmul_residual",
    )(h, y, w)


def kernel(x, even_w_in, even_w_out, ret_norm, rel_bias, odd_w_in, odd_w_out, hgrn_lb, hgrn_norm,
           mix_norm, ffn_norm, ffn_w_up, ffn_conv_w, ffn_conv_b, ffn_w_down, final_norm):
    bn, seq, d = x.shape
    tokens = bn * seq
    row = lambda a: a.reshape(1, -1).astype(F32)
    h = x.reshape(tokens, d)
    final_g = row(final_norm)

    outs = _even_in(h, row(mix_norm[0]), even_w_in[0].astype(BF16), row(ret_norm[0]), seq,
                    riders=[(even_w_out, 0), (ffn_w_up, 0), (ffn_w_down, 0)])
    ya, q4, q16, k4, k16, v4, v16, w_out0, w_up0, w_down0 = outs
    yb = _dilated((q4, k4, v4, q16, k16, v16), rel_bias, bn, seq)
    h = _even_out(h, ya, yb.reshape(tokens, B_W), w_out0, tm=1024)
    h, w_in1, w_out1, w_up1, w_down1 = _ffn(
        h, row(ffn_norm[0]), w_up0, ffn_conv_w[0], ffn_conv_b[0], w_down0, final_g, seq, final=False,
        riders=[(odd_w_in, 0), (odd_w_out, 0), (ffn_w_up, 1), (ffn_w_down, 1)])

    y = _hgrn_layer(h, row(mix_norm[1]), w_in1, hgrn_lb.astype(F32), row(hgrn_norm[0]), 1, seq)
    h = _matmul_residual(h, y, w_out1, tm=1024)
    h, = _ffn(h, row(ffn_norm[1]), w_up1, ffn_conv_w[1], ffn_conv_b[1], w_down1, final_g, seq,
              final=True)
    return h.reshape(bn, seq, d)
```

```python
import functools
import math

import numpy as np
import jax
import jax.numpy as jnp
from jax import lax
from jax.experimental import pallas as pl
from jax.experimental.pallas import tpu as pltpu

F32 = jnp.float32
BF16 = jnp.bfloat16

D_MODEL = 1024
RET_HEADS = 8
RET_DK = 64
RET_DV = 128
RET_CHUNK = 128
ROPE_BASE = 10000.0
DIL_HEADS = 8
DIL_DH = 64
DIL_BRANCHES = ((128, 1), (512, 4), (2048, 16))
DIL_BLOCK = 128
HGRN_HEADS = 8
HGRN_DK = 128
HGRN_DV = 128
HGRN_CHUNK = 32
REL_BUCKETS = 32
REL_MAX_DIST = 2048
D_FF = 2816
CONV_WIDTH = 3
EPS = 1e-6

A_QK = RET_HEADS * RET_DK
A_V = RET_HEADS * RET_DV
B_W = DIL_HEADS * DIL_DH
EVEN_IN = 2 * A_QK + 2 * A_V + 3 * B_W
C_K = HGRN_HEADS * HGRN_DK
C_V = HGRN_HEADS * HGRN_DV
ODD_IN = 2 * C_K + 2 * C_V

LANES = 128
SUBLANES = 8
VMEM_LIMIT = 56 * 1024 * 1024
MASK_VALUE = -1e30

NT_DIMS = (((1,), (1,)), ((), ()))
TN_DIMS = (((0,), (0,)), ((), ()))


def _params(*semantics):
    return pltpu.CompilerParams(dimension_semantics=semantics, vmem_limit_bytes=VMEM_LIMIT)


def _silu(x):
    return x * jax.nn.sigmoid(x)


def _emit(fill, count):
    tokens = []
    for _ in range(count):
        thunk = next(fill, None)
        if thunk is not None:
            token = thunk()
            if token is not None:
                tokens.append(token)
    return tokens


def _anchor(tokens, zero_bits):
    acc = zero_bits
    for t in tokens:
        acc = acc & pltpu.bitcast(t, jnp.int32)
    return pltpu.bitcast(acc, F32)


BF16_SUBLANES = 2 * SUBLANES


def _rider_specs(weights, steps):
    in_specs, out_specs, out_shapes = [], [], []
    for arr, layer in weights:
        _, rows, cols = arr.shape
        rep = 1
        while (rows * rep) % (steps * BF16_SUBLANES):
            rep *= 2
        slab = rows * rep // steps
        slab_index = lambda i, rep=rep: jnp.minimum(i, steps - 1) // rep
        in_specs.append(pl.BlockSpec((None, slab, cols),
                                     lambda i, layer=layer, at=slab_index: (layer, at(i), 0)))
        out_specs.append(pl.BlockSpec((slab, cols), lambda i, at=slab_index: (at(i), 0)))
        out_shapes.append(jax.ShapeDtypeStruct((rows, cols), BF16))
    return in_specs, out_specs, out_shapes


def _with_riders(body, n_in, n_out, n_riders):
    def kernel_fn(*refs):
        ins, refs = refs[:n_in], refs[n_in:]
        rider_in, refs = refs[:n_riders], refs[n_riders:]
        outs, refs = refs[:n_out], refs[n_out:]
        rider_out, scratch = refs[:n_riders], refs[n_riders:]
        for src, dst in zip(rider_in, rider_out):
            dst[...] = src[...].astype(dst.dtype)
        body(*ins, *outs, *scratch)
    return kernel_fn


def _retention_tables(seq):
    h = jnp.arange(RET_HEADS, dtype=F32)
    log_g = jnp.log1p(-jnp.exp2(-5.0 - h))
    pos = jnp.arange(RET_CHUNK, dtype=F32)
    diff = pos[:, None] - pos[None, :]
    decay = jnp.where(diff >= 0, jnp.exp(jnp.maximum(diff, 0.0)[None] * log_g[:, None, None]), 0.0)
    q_scale = jnp.repeat(jnp.exp((pos + 1.0)[:, None] * log_g[None, :]), RET_DK, axis=1)
    k_scale = jnp.repeat(jnp.exp((RET_CHUNK - 1.0 - pos)[:, None] * log_g[None, :]), RET_DK, axis=1)
    chunk_decay = jnp.broadcast_to(jnp.exp(RET_CHUNK * log_g)[:, None, None], (RET_HEADS, 1, RET_DV))
    inv = ROPE_BASE ** (-jnp.arange(0, RET_DK, 2, dtype=F32) / RET_DK)
    ang = jnp.arange(seq, dtype=F32)[:, None] * inv[None, :]
    cos, sin = jnp.cos(ang), jnp.sin(ang)
    reps = LANES // RET_DK
    cos_t = jnp.tile(jnp.concatenate([cos, cos], axis=1), (1, reps))
    sin_t = jnp.tile(jnp.concatenate([-sin, sin], axis=1), (1, reps))
    return decay, q_scale, k_scale, chunk_decay, cos_t, sin_t


EVEN_TM = 512


def _even_in_kernel(tiles_per_seq, x_ref, g_ref, w_ref, cos_ref, sin_ref, qs_ref, ks_ref, dec_ref,
                    cd_ref, rn_ref, ya_ref, q4_ref, q16_ref, k4_ref, k16_ref, v4_ref, v16_ref,
                    slab_ref, slab4_ref, st_ref):
    tm = x_ref.shape[0]
    chunk = RET_CHUNK

    @pl.when(pl.program_id(0) % tiles_per_seq == 0)
    def _():
        st_ref[...] = jnp.zeros_like(st_ref)

    x = x_ref[...]
    ms = jnp.mean(x * x, axis=-1, keepdims=True)
    hn = (x * lax.rsqrt(ms + EPS) * g_ref[...]).astype(BF16)

    def proj(c):
        return jnp.dot(hn, w_ref[:, c * B_W:(c + 1) * B_W], preferred_element_type=F32)

    cos = cos_ref[...]
    sin = sin_ref[...]
    lane = lax.broadcasted_iota(jnp.int32, (tm, LANES), 1)
    half = RET_DK // 2
    first_half = (lane % RET_DK) < half

    def rot(t):
        partner = jnp.where(first_half, pltpu.roll(t, LANES - half, 1), pltpu.roll(t, half, 1))
        return t * cos + partner * sin

    groups = [slice(s * LANES, (s + 1) * LANES) for s in range(B_W // LANES)]

    r = proj(0)
    q_rot = [rot(r[:, sl]) for sl in groups]
    q = [t.astype(BF16) for t in q_rot]
    q_in = [(t * qs_ref[:, sl]).astype(BF16) for t, sl in zip(q_rot, groups)]
    r = proj(1)
    k_rot = [rot(r[:, sl]) * (RET_DK ** -0.5) for sl in groups]
    k = [t.astype(BF16) for t in k_rot]
    k_out = [(t * ks_ref[:, sl]).astype(BF16) for t, sl in zip(k_rot, groups)]
    v = jnp.concatenate([proj(2), proj(3)], axis=1).astype(BF16)
    gate = rn_ref[...] * _silu(jnp.concatenate([proj(4), proj(5)], axis=1))

    def dilated_pieces(c, slot, r4_ref, r16_ref, scale):
        def project():
            r = proj(c)
            if scale is not None:
                r = r * scale
            for s, sl in enumerate(groups):
                slab_ref[slot, s] = r[:, sl]

        def by4():
            for s, sl in enumerate(groups):
                for rho in range(4):
                    t = slab_ref[slot, s, pl.ds(rho, tm // 4, stride=4), :]
                    r4_ref[rho, :, sl] = t.astype(BF16)
                    slab4_ref[slot, s, rho * (tm // 4):(rho + 1) * (tm // 4), :] = t

        def by16():
            for s, sl in enumerate(groups):
                for rho4 in range(4):
                    for c in range(4):
                        r16_ref[rho4 + 4 * c, :, sl] = slab4_ref[
                            slot, s, pl.ds(rho4 * (tm // 4) + c, tm // 16, stride=4), :].astype(BF16)

        return [project, by4, by16]

    fill = iter(dilated_pieces(6, 0, q4_ref, q16_ref, DIL_DH ** -0.5)
                + dilated_pieces(7, 1, k4_ref, k16_ref, None)
                + dilated_pieces(8, 2, v4_ref, v16_ref, None))

    head0 = lax.broadcasted_iota(jnp.int32, (chunk, LANES), 1) < RET_DK
    heads = range(RET_HEADS)
    hs = [slice(h * RET_DV, (h + 1) * RET_DV) for h in heads]
    st = [st_ref[h] for h in heads]
    for ci in range(tm // chunk):
        rows = slice(ci * chunk, (ci + 1) * chunk)
        mask = [head0 if h % 2 == 0 else jnp.logical_not(head0) for h in heads]
        scores = [lax.dot_general(jnp.where(mask[h], q[h // 2][rows], 0), k[h // 2][rows], NT_DIMS,
                                  preferred_element_type=F32) * dec_ref[h] for h in heads]
        _emit(fill, 1)
        intra = [jnp.dot(scores[h].astype(BF16), v[rows, hs[h]], preferred_element_type=F32)
                 for h in heads]
        inter = [jnp.dot(jnp.where(mask[h], q_in[h // 2][rows], 0), st[h].astype(BF16),
                         preferred_element_type=F32) for h in heads]
        st = [st[h] * cd_ref[h] + lax.dot_general(k_out[h // 2][rows], v[rows, hs[h]], TN_DIMS,
                                                  preferred_element_type=F32) for h in heads]
        _emit(fill, 1)
        for h in heads:
            y = intra[h] + inter[h]
            yc = y - jnp.mean(y, axis=-1, keepdims=True)
            yn = yc * lax.rsqrt(jnp.mean(yc * yc, axis=-1, keepdims=True) + EPS)
            ya_ref[rows, hs[h]] = (yn * gate[rows, hs[h]]).astype(BF16)
        _emit(fill, 1)
    for run in fill:
        run()
    for h in heads:
        st_ref[h] = st[h]


def _even_in(x, g, w, ret_norm, seq, riders):
    tokens, d = x.shape
    bn = tokens // seq
    tm = EVEN_TM
    tps = seq // tm
    decay, q_scale, k_scale, chunk_decay, cos_t, sin_t = _retention_tables(seq)
    q_scale = jnp.tile(q_scale, (tm // RET_CHUNK, 1))
    k_scale = jnp.tile(k_scale, (tm // RET_CHUNK, 1))
    const = lambda shape: pl.BlockSpec(shape, lambda i: (0,) * len(shape), pipeline_mode=pl.Buffered(1))
    flat = lambda w_: pl.BlockSpec((tm, w_), lambda i: (i, 0))
    by = lambda r: pl.BlockSpec((None, r, tm // r, B_W), lambda i: (i // tps, 0, i % tps, 0))
    sds = jax.ShapeDtypeStruct
    lay = [sds((bn, 4, seq // 4, B_W), BF16), sds((bn, 16, seq // 16, B_W), BF16)]
    steps = tokens // tm
    in_specs = [
        flat(d), const((1, d)), const(w.shape),
        pl.BlockSpec((tm, LANES), lambda i: (i % tps, 0)),
        pl.BlockSpec((tm, LANES), lambda i: (i % tps, 0)),
        const((tm, A_QK)), const((tm, A_QK)),
        const(decay.shape), const(chunk_decay.shape), const((1, A_V)),
    ]
    out_specs = [flat(A_V)] + [by(4), by(16)] * 3
    rider_in, rider_out, rider_shapes = _rider_specs(riders, steps)
    return pl.pallas_call(
        _with_riders(functools.partial(_even_in_kernel, tps), len(in_specs), len(out_specs), len(riders)),
        grid=(steps,),
        in_specs=in_specs + rider_in,
        out_specs=out_specs + rider_out,
        out_shape=[sds((tokens, A_V), BF16)] + lay * 3 + rider_shapes,
        scratch_shapes=[pltpu.VMEM((3, B_W // LANES, tm, LANES), F32),
                        pltpu.VMEM((3, B_W // LANES, tm, LANES), F32),
                        pltpu.VMEM((RET_HEADS, 2 * RET_DK, RET_DV), F32)],
        compiler_params=_params("arbitrary"),
        name="even_in",
    )(x, g, w, cos_t, sin_t, q_scale, k_scale, decay, chunk_decay, ret_norm, *[a for a, _ in riders])


def _rel_bucket(dist):
    max_exact = REL_BUCKETS // 2
    d = dist.astype(F32)
    large = max_exact + (jnp.log(jnp.maximum(d, 1.0) / max_exact) / math.log(REL_MAX_DIST / max_exact)
                         * (REL_BUCKETS - max_exact)).astype(jnp.int32)
    large = jnp.minimum(large, REL_BUCKETS - 1)
    return jnp.where(dist < max_exact, dist, large)


DIL_BASE = 4
DIL_UNROLL = 8


def _dilated_buckets():
    lb = DIL_BLOCK
    slot = jnp.arange(lb)
    per = lb // DIL_BASE
    regrouped = DIL_BASE * (slot % per) + slot // per
    tables = []
    for (window, dilation), token in zip(DIL_BRANCHES, (regrouped, slot, slot)):
        a_idx = token[:, None]
        c_idx = jnp.concatenate([token, lb + token])[None, :]
        dist = lb + a_idx - c_idx
        valid = (dist >= 0) & (dist <= window // dilation)
        tables.append(jnp.where(valid, _rel_bucket(jnp.maximum(dist, 0) * dilation), -1))
    return jnp.stack(tables).astype(jnp.int32)


def _dilated_kernel(rb_ref, bucket_ref, q4_ref, k4_ref, v4_ref, q16_ref, k16_ref, v16_ref, y_ref,
                    bias_ref, acc_ref, m_ref, l_ref):
    lb = DIL_BLOCK
    seq = y_ref.shape[0]
    pair = pl.program_id(1)

    @pl.when(pl.program_id(0) == 0)
    def _():
        for br in range(len(DIL_BRANCHES)):
            bucket = bucket_ref[br]
            for a in range(2):
                h = 2 * pair + a
                t = jnp.full((lb, 2 * lb), MASK_VALUE, F32)
                for j in range(REL_BUCKETS):
                    t = jnp.where(bucket == j, rb_ref[j, h], t)
                bias_ref[br * DIL_HEADS + h] = t

    lane = lax.broadcasted_iota(jnp.int32, (lb, LANES), 1)
    head0 = lane < DIL_DH

    def attend(br, blocks):
        masks = (head0, jnp.logical_not(head0))
        bias = [bias_ref[br * DIL_HEADS + 2 * pair + a] for a in range(2)]
        ks = [jnp.concatenate([kp, kc], axis=0) for _, _, kp, kc, _, _ in blocks]
        vs = [jnp.concatenate([vp, vc], axis=0) for _, _, _, _, vp, vc in blocks]
        pens = [jnp.where(first, MASK_VALUE, 0.0) for first, *_ in blocks]
        scores = [[lax.dot_general(jnp.where(masks[a], blk[1], 0), k, NT_DIMS,
                                   preferred_element_type=F32) + bias[a] for a in range(2)]
                  for blk, k in zip(blocks, ks)]
        scores = [[jnp.concatenate([s[:, 0:lb] + pen, s[:, lb:]], axis=1) for s in pair_s]
                  for pair_s, pen in zip(scores, pens)]
        maxes = [[jnp.max(s, axis=-1, keepdims=True) for s in pair_s] for pair_s in scores]
        exps = [[jnp.exp(s - m) for s, m in zip(pair_s, pair_m)] for pair_s, pair_m in zip(scores, maxes)]
        sums = [[jnp.sum(e, axis=-1, keepdims=True) for e in pair_e] for pair_e in exps]
        outs = [[jnp.dot(e.astype(BF16), v, preferred_element_type=F32) for e in pair_e]
                for pair_e, v in zip(exps, vs)]
        return [tuple(jnp.where(head0, x[0], x[1]) for x in (o, m, l))
                for o, m, l in zip(outs, maxes, sums)]

    def block_rows(n):
        r0 = pl.multiple_of(n * lb, lb)
        rp = pl.multiple_of(jnp.maximum(n - 1, 0) * lb, lb)
        return pl.ds(r0, lb), pl.ds(rp, lb)

    base = DIL_BASE
    per = lb // base
    class_rows = seq // base

    def unit_blocks(it, carry):
        blocks, dests = [], []
        for j in range(DIL_UNROLL):
            n = it * DIL_UNROLL + j
            r0 = pl.multiple_of(n * per, per)
            rp = pl.multiple_of(jnp.maximum(n - 1, 0) * per, per)
            gather = lambda ref, at: jnp.concatenate([ref[rho, pl.ds(at, per), :] for rho in range(base)],
                                                     axis=0)
            blocks.append((n == 0, gather(q4_ref, r0), gather(k4_ref, rp), gather(k4_ref, r0),
                           gather(v4_ref, rp), gather(v4_ref, r0)))
            dests.append(r0)
        for r0, (acc, m, l) in zip(dests, attend(0, blocks)):
            for rho in range(base):
                rows = pl.ds(rho * class_rows + r0, per)
                piece = slice(rho * per, (rho + 1) * per)
                acc_ref[rows, :] = acc[piece]
                m_ref[rows, :] = m[piece]
                l_ref[rows, :] = l[piece]
        return carry

    lax.fori_loop(0, seq // (lb * DIL_UNROLL), unit_blocks, 0)

    def dilated_blocks(br, r, q_ref, k_ref, v_ref):
        nb = seq // (r * lb)
        sub = r // base

        def body(it, carry):
            blocks, dests = [], []
            for j in range(DIL_UNROLL):
                idx = it * DIL_UNROLL + j
                rho, n = idx // nb, idx % nb
                cur, prev = block_rows(n)
                blocks.append((n == 0, q_ref[rho, cur, :], k_ref[rho, prev, :], k_ref[rho, cur, :],
                               v_ref[rho, prev, :], v_ref[rho, cur, :]))
                start = (rho % base) * class_rows + n * (lb * sub) + rho // base
                dests.append(pl.ds(start, lb, stride=sub) if sub > 1
                             else pl.ds(pl.multiple_of(start, lb), lb))
            for rows, (acc, m, l) in zip(dests, attend(br, blocks)):
                m_old = m_ref[rows, :]
                m_new = jnp.maximum(m_old, m)
                c_old = jnp.exp(m_old - m_new)
                c_cur = jnp.exp(m - m_new)
                acc_ref[rows, :] = acc_ref[rows, :] * c_old + acc * c_cur
                l_ref[rows, :] = l_ref[rows, :] * c_old + l * c_cur
                m_ref[rows, :] = m_new
            return carry

        lax.fori_loop(0, r * nb // DIL_UNROLL, body, 0)

    dilated_blocks(1, DIL_BRANCHES[1][1], q4_ref, k4_ref, v4_ref)
    dilated_blocks(2, DIL_BRANCHES[2][1], q16_ref, k16_ref, v16_ref)

    def normalise(n, carry):
        for rho in range(base):
            rows = pl.ds(pl.multiple_of(rho * class_rows + n * lb, lb), lb)
            m_ref[pl.ds(n * (lb * base) + rho, lb, stride=base), :] = acc_ref[rows, :] / l_ref[rows, :]
        return carry

    lax.fori_loop(0, class_rows // lb, normalise, 0)

    def finish(n, carry):
        rows = pl.ds(pl.multiple_of(n * lb, lb), lb)
        y_ref[rows, :] = m_ref[rows, :].astype(y_ref.dtype)
        return carry

    lax.fori_loop(0, seq // lb, finish, 0)


def _dilated(layouts, rel_bias, bn, seq):
    lb = DIL_BLOCK
    assert all(w // r <= lb and seq % (r * lb) == 0 for w, r in DIL_BRANCHES)
    assert tuple(r for _, r in DIL_BRANCHES) == (1, 4, 16)
    nat = pl.BlockSpec((None, seq, LANES), lambda b, p: (b, 0, p))
    by = lambda r: pl.BlockSpec((None, r, seq // r, LANES), lambda b, p: (b, 0, 0, p))
    nbr = len(DIL_BRANCHES)
    return pl.pallas_call(
        _dilated_kernel,
        grid=(bn, DIL_HEADS // 2),
        in_specs=[pl.BlockSpec(memory_space=pltpu.SMEM),
                  pl.BlockSpec((nbr, lb, 2 * lb), lambda b, p: (0, 0, 0))]
        + [by(4)] * 3 + [by(16)] * 3,
        out_specs=nat,
        out_shape=jax.ShapeDtypeStruct((bn, seq, B_W), BF16),
        scratch_shapes=[pltpu.VMEM((nbr * DIL_HEADS, lb, 2 * lb), F32),
                        pltpu.VMEM((seq, LANES), F32), pltpu.VMEM((seq, LANES), F32),
                        pltpu.VMEM((seq, LANES), F32)],
        compiler_params=_params("arbitrary", "arbitrary"),
        name="dilated",
    )(rel_bias.astype(F32), _dilated_buckets(), *layouts)


FFN_TM = 512
FFN_R = 8
FFN_CHUNK = 256
FFN_DENSE_CHUNKS = 5


def _ffn_kernel(ntiles, tiles_per_seq, final, nparts, *refs):
    x_ref, y_refs, wo_refs = refs[0], refs[1:1 + nparts], refs[1 + nparts:1 + 2 * nparts]
    (g_ref, wup_ref, cw_ref, cb_ref, wd_ref, fg_ref, zero_ref, where_ref, o_ref, slab_in, slab_out,
     hn_cur, hn_next, res_ref, act_ref, down_ref, halo_ref) = refs[1 + 2 * nparts:]
    i = pl.program_id(0)
    tm = o_ref.shape[0]
    rows = tm // FFN_R
    nslab = D_MODEL // LANES
    ck = FFN_CHUNK
    group = lambda g: slice(g * rows, (g + 1) * rows)
    lanes = lambda s: slice(s * LANES, (s + 1) * LANES)
    anywhere = pl.multiple_of(where_ref[0], BF16_SUBLANES)

    def head_pieces():
        wide = 2 * LANES

        def project(p):
            def run():
                cols = slice(p * wide, (p + 1) * wide)
                h = x_ref[:, cols]
                for y_ref, wo_ref in zip(y_refs, wo_refs):
                    h = h + jnp.dot(y_ref[...], wo_ref[:, cols], preferred_element_type=F32)
                slab_in[2 * p] = h[:, 0:LANES]
                slab_in[2 * p + 1] = h[:, LANES:wide]
                return slab_in[2 * p + 1, pl.ds(anywhere, SUBLANES), :]
            return run

        def normalise(g):
            def run():
                x = jnp.concatenate([slab_in[s, pl.ds(g, rows, stride=FFN_R), :] for s in range(nslab)],
                                    axis=1)
                res_ref[i % 2, group(g), :] = x
                ms = jnp.mean(x * x, axis=-1, keepdims=True)
                hn_next[group(g), :] = (x * lax.rsqrt(ms + EPS) * g_ref[...]).astype(BF16)
                return hn_next[pl.ds(anywhere, BF16_SUBLANES), 0:LANES].astype(F32)[0:SUBLANES]
            return run

        return [project(p) for p in range(D_MODEL // wide)] + [normalise(g) for g in range(FFN_R)]

    def tail_pieces():
        def unpermute(g):
            def run():
                for s in range(nslab):
                    slab_out[s, pl.ds(g, rows, stride=FFN_R), :] = down_ref[group(g), lanes(s)]
                return slab_out[nslab - 1, pl.ds(anywhere, SUBLANES), :]
            return run

        def store(r0, r1):
            def run():
                y = jnp.concatenate([slab_out[s, r0:r1, :] for s in range(nslab)], axis=1)
                if final:
                    ms = jnp.mean(y * y, axis=-1, keepdims=True)
                    y = y * lax.rsqrt(ms + EPS) * fg_ref[...]
                o_ref[r0:r1, :] = y
                return o_ref[pl.ds(anywhere, SUBLANES), 0:LANES]
            return run

        return ([unpermute(g) for g in range(FFN_R)]
                + [store(g * rows, (g + 1) * rows) for g in range(FFN_R)])

    @pl.when(i == 0)
    def _():
        for run in head_pieces():
            run()
        down_ref[...] = jnp.zeros_like(down_ref)
        halo_ref[...] = jnp.zeros_like(halo_ref)

    @pl.when(jnp.logical_and(i >= 1, i <= ntiles))
    def _():
        hn_cur[...] = hn_next[...]
        fill = iter(tail_pieces() + head_pieces())
        seq_start = (i - 1) % tiles_per_seq == 0
        first_row = lax.broadcasted_iota(jnp.int32, (rows, ck), 0) == 0

        def shift_down(block, fill_row):
            return jnp.where(first_row, fill_row, pltpu.roll(block, 1, 0))

        zero_bits = zero_ref[...]

        def conv(cols, slot, tokens):
            u = jnp.dot(hn_cur[...], wup_ref[:, cols], preferred_element_type=F32)
            ug = [u[group(g), :] for g in range(FFN_R)]
            tails = jnp.where(seq_start, 0.0, halo_ref[slot])
            prev2 = shift_down(ug[FFN_R - 2], tails[SUBLANES - 1:SUBLANES, :])
            prev1 = shift_down(ug[FFN_R - 1], tails[2 * SUBLANES - 1:2 * SUBLANES, :])
            halo_ref[slot] = jnp.concatenate(
                [ug[FFN_R - 2][rows - SUBLANES:, :], ug[FFN_R - 1][rows - SUBLANES:, :]], axis=0)
            back1 = [prev1] + ug[:FFN_R - 1]
            back2 = [prev2, prev1] + ug[:FFN_R - 2]
            cw = cw_ref[:, cols]
            cb = cb_ref[:, cols]
            if tokens:
                cb = cb + jnp.tile(_anchor(tokens, zero_bits)[0:1, :], (1, ck // LANES))
            return [cw[0:1, :] * back2[g] + cw[1:2, :] * back1[g] + cw[2:3, :] * ug[g] + cb
                    for g in range(FFN_R)]

        tokens = []
        nchunk = D_FF // ck
        for c in range(nchunk):
            per_point = 2 if c < FFN_DENSE_CHUNKS else 1
            gate = conv(slice(c * ck, (c + 1) * ck), 2 * c, tokens)
            tokens = _emit(fill, per_point)
            value = conv(slice(D_FF + c * ck, D_FF + (c + 1) * ck), 2 * c + 1, tokens)
            tokens = _emit(fill, per_point)
            for g in range(FFN_R):
                act_ref[group(g), c * ck:(c + 1) * ck] = (_silu(gate[g]) * value[g]).astype(BF16)
        down_ref[...] = res_ref[(i - 1) % 2] + jnp.dot(act_ref[...], wd_ref[...],
                                                       preferred_element_type=F32)
        for run in fill:
            run()

    @pl.when(i == ntiles + 1)
    def _():
        for run in tail_pieces():
            run()


def _ffn(x, mixed, g, w_up, conv_w, conv_b, w_down, final_g, seq, final, riders=()):
    m = x.shape[0]
    tm = FFN_TM
    assert D_FF % FFN_CHUNK == 0 and tm % (FFN_R * 2 * SUBLANES) == 0 and seq % tm == 0
    const = lambda shape, at=(0, 0): pl.BlockSpec(shape, lambda i: at, pipeline_mode=pl.Buffered(1))
    ntiles = m // tm
    tile = lambda width, at: pl.BlockSpec((tm, width), lambda i: (at(i), 0))
    head_tile = lambda i: jnp.minimum(i, ntiles - 1)
    in_specs = (
        [tile(D_MODEL, head_tile)]
        + [tile(y.shape[1], head_tile) for y, _, _ in mixed]
        + [const((y.shape[1], D_MODEL), (row0 // y.shape[1], 0)) for y, _, row0 in mixed]
        + [const((1, D_MODEL)),
           const((D_MODEL, 2 * D_FF)), const((CONV_WIDTH, 2 * D_FF)), const((1, 2 * D_FF)),
           const((D_FF, D_MODEL)),
           const((1, D_MODEL)),
           const((SUBLANES, LANES)),
           pl.BlockSpec(memory_space=pltpu.SMEM)])
    rider_in, rider_out, rider_shapes = _rider_specs(riders, ntiles)
    return pl.pallas_call(
        _with_riders(functools.partial(_ffn_kernel, ntiles, seq // tm, final, len(mixed)),
                     len(in_specs), 1, len(riders)),
        grid=(ntiles + 2,),
        in_specs=in_specs + rider_in,
        out_specs=[tile(D_MODEL, lambda i: jnp.maximum(i - 2, 0))] + rider_out,
        out_shape=[jax.ShapeDtypeStruct((m, D_MODEL), F32)] + rider_shapes,
        scratch_shapes=[
            pltpu.VMEM((D_MODEL // LANES, tm, LANES), F32),
            pltpu.VMEM((D_MODEL // LANES, tm, LANES), F32),
            pltpu.VMEM((tm, D_MODEL), BF16),
            pltpu.VMEM((tm, D_MODEL), BF16),
            pltpu.VMEM((2, tm, D_MODEL), F32),
            pltpu.VMEM((tm, D_FF), BF16),
            pltpu.VMEM((tm, D_MODEL), F32),
            pltpu.VMEM((2 * D_FF // FFN_CHUNK, 2 * SUBLANES, FFN_CHUNK), F32),
        ],
        compiler_params=_params("arbitrary"),
        name="conv_ffn",
    )(x, *[y for y, _, _ in mixed], *[wo for _, wo, _ in mixed], g, w_up, conv_w.astype(F32),
      conv_b.astype(F32).reshape(1, -1), w_down, final_g, jnp.zeros((SUBLANES, LANES), jnp.int32),
      jnp.zeros((1,), jnp.int32), *[a for a, _ in riders])


HGRN_GROUP = 128
HGRN_TM = 256
HGRN_PROJ_COLS = 256
HGRN_EMIT = 2


def _hgrn_group(proj_ref, rows, lower_all, gain, causal, tri, st, fill):
    c = HGRN_CHUNK
    heads = range(HGRN_HEADS)
    hk = [slice(h * HGRN_DK, (h + 1) * HGRN_DK) for h in heads]
    hv = [slice(h * HGRN_DV, (h + 1) * HGRN_DV) for h in heads]
    chunks = [slice(n * c, (n + 1) * c) for n in range(HGRN_GROUP // c)]

    q = _silu(proj_ref[rows, 0:C_K])
    fg = lower_all + (1.0 - lower_all) * jax.nn.sigmoid(proj_ref[rows, C_K:2 * C_K])
    k = 1.0 - fg
    log_f = jnp.log(fg)
    hi = log_f.astype(BF16)
    lo = (log_f - hi.astype(F32)).astype(BF16)
    _emit(fill, HGRN_EMIT)
    parts = jnp.dot(tri, jnp.concatenate([hi, lo], axis=1), preferred_element_type=F32)
    b = parts[:, 0:C_K] + parts[:, C_K:2 * C_K]
    v = proj_ref[rows, 2 * C_K:2 * C_K + C_V].astype(BF16)
    q_t = (q * jnp.exp(b)).astype(BF16)
    k_t = (k * jnp.exp(-b)).astype(BF16)
    _emit(fill, HGRN_EMIT)
    scores = [lax.dot_general(q_t[:, hk[h]], k_t[:, hk[h]], NT_DIMS, preferred_element_type=F32)
              for h in heads]
    scores = [jnp.where(causal, s, 0.0).astype(BF16) for s in scores]
    intra = [jnp.dot(scores[h], v[:, hv[h]], preferred_element_type=F32) for h in heads]
    b_last = [b[cs.stop - 1:cs.stop, :] for cs in chunks]
    k_end = [(k[cs, :] * jnp.exp(bl - b[cs, :])).astype(BF16) for cs, bl in zip(chunks, b_last)]
    decay = [jnp.exp(bl) for bl in b_last]
    _emit(fill, HGRN_EMIT)
    st = list(st)
    inter = [[] for _ in heads]
    for n, cs in enumerate(chunks):
        for h in heads:
            inter[h].append(lax.dot_general(q_t[cs, hk[h]], st[h].astype(BF16), NT_DIMS,
                                            preferred_element_type=F32))
            st[h] = st[h] * decay[n][:, hk[h]] + lax.dot_general(
                v[cs, hv[h]], k_end[n][:, hk[h]], TN_DIMS, preferred_element_type=F32)
    gate = gain * _silu(proj_ref[rows, 2 * C_K + C_V:2 * C_K + 2 * C_V])
    _emit(fill, HGRN_EMIT)
    outs = []
    for h in heads:
        y = intra[h] + jnp.concatenate(inter[h], axis=0)
        yn = y * lax.rsqrt(jnp.mean(y * y, axis=-1, keepdims=True) + EPS)
        outs.append((yn * gate[:, hv[h]]).astype(BF16))
    return outs, st


def _hgrn_layer_kernel(layer, tiles_per_seq, h0_ref, hnext_ref, g_ref, w_ref, lbp_ref, on_ref, o_ref,
                       proj_a, proj_b, st_ref):
    t = pl.program_id(0)
    tm = o_ref.shape[0]
    grp = HGRN_GROUP

    def projection(h_ref, dst):
        x = h_ref[...]
        ms = jnp.mean(x * x, axis=-1, keepdims=True)
        hn = (x * lax.rsqrt(ms + EPS) * g_ref[...]).astype(BF16)

        def piece(c):
            cols = slice(c * HGRN_PROJ_COLS, (c + 1) * HGRN_PROJ_COLS)

            def run():
                dst[:, cols] = jnp.dot(hn, w_ref[:, cols], preferred_element_type=F32)
            return run

        return [piece(c) for c in range(ODD_IN // HGRN_PROJ_COLS)]

    @pl.when(t == 0)
    def _():
        for run in projection(h0_ref, proj_a):
            run()

    @pl.when(t % tiles_per_seq == 0)
    def _():
        st_ref[...] = jnp.zeros_like(st_ref)

    def step(src, dst):
        fill = iter(projection(hnext_ref, dst))
        lbp = lbp_ref[...]
        ex = jnp.exp(lbp - jnp.max(lbp, axis=0, keepdims=True))
        sm = ex / jnp.sum(ex, axis=0, keepdims=True)
        lower_all = jnp.sum(sm[0:layer + 1, :], axis=0, keepdims=True) - sm[0:1, :]
        row = lax.broadcasted_iota(jnp.int32, (grp, grp), 0)
        colm = lax.broadcasted_iota(jnp.int32, (grp, grp), 1)
        causal = jnp.logical_and(row // HGRN_CHUNK == colm // HGRN_CHUNK, row >= colm)
        tri = jnp.where(causal, 1.0, 0.0).astype(BF16)
        gain = on_ref[...]
        st = [st_ref[h] for h in range(HGRN_HEADS)]
        for r0 in range(0, tm, grp):
            outs, st = _hgrn_group(src, slice(r0, r0 + grp), lower_all, gain, causal, tri, st, fill)
            for h, y in enumerate(outs):
                o_ref[r0:r0 + grp, h * HGRN_DV:(h + 1) * HGRN_DV] = y
        for run in fill:
            run()
        for h in range(HGRN_HEADS):
            st_ref[h] = st[h]

    @pl.when(t % 2 == 0)
    def _():
        step(proj_a, proj_b)

    @pl.when(t % 2 == 1)
    def _():
        step(proj_b, proj_a)


def _hgrn_layer(h, g, w, hgrn_lb, out_norm, layer, seq):
    tokens, d = h.shape
    tm = HGRN_TM
    ntiles = tokens // tm
    const = lambda shape: pl.BlockSpec(shape, lambda t: (0, 0), pipeline_mode=pl.Buffered(1))
    return pl.pallas_call(
        functools.partial(_hgrn_layer_kernel, layer, seq // tm),
        grid=(ntiles,),
        in_specs=[
            const((tm, d)),
            pl.BlockSpec((tm, d), lambda t: (jnp.minimum(t + 1, ntiles - 1), 0)),
            const((1, d)), const(w.shape), const(hgrn_lb.shape), const((1, C_V)),
        ],
        out_specs=pl.BlockSpec((tm, C_V), lambda t: (t, 0)),
        out_shape=jax.ShapeDtypeStruct((tokens, C_V), BF16),
        scratch_shapes=[pltpu.VMEM((tm, ODD_IN), F32), pltpu.VMEM((tm, ODD_IN), F32),
                        pltpu.VMEM((HGRN_HEADS, HGRN_DV, HGRN_DK), F32)],
        compiler_params=_params("arbitrary"),
        name="hgrn_layer",
    )(h, h, g, w, hgrn_lb, out_norm)


def kernel(x, even_w_in, even_w_out, ret_norm, rel_bias, odd_w_in, odd_w_out, hgrn_lb, hgrn_norm,
           mix_norm, ffn_norm, ffn_w_up, ffn_conv_w, ffn_conv_b, ffn_w_down, final_norm):
    bn, seq, d = x.shape
    tokens = bn * seq
    row = lambda a: a.reshape(1, -1).astype(F32)
    h = x.reshape(tokens, d)
    final_g = row(final_norm)

    outs = _even_in(h, row(mix_norm[0]), even_w_in[0].astype(BF16), row(ret_norm[0]), seq,
                    riders=[(even_w_out, 0), (ffn_w_up, 0), (ffn_w_down, 0)])
    ya, q4, q16, k4, k16, v4, v16, w_out0, w_up0, w_down0 = outs
    yb = _dilated((q4, k4, v4, q16, k16, v16), rel_bias, bn, seq).reshape(tokens, B_W)
    h, w_in1, w_out1, w_up1, w_down1 = _ffn(
        h, [(ya, w_out0, 0), (yb, w_out0, A_V)], row(ffn_norm[0]), w_up0, ffn_conv_w[0],
        ffn_conv_b[0], w_down0, final_g, seq, final=False,
        riders=[(odd_w_in, 0), (odd_w_out, 0), (ffn_w_up, 1), (ffn_w_down, 1)])

    y = _hgrn_layer(h, row(mix_norm[1]), w_in1, hgrn_lb.astype(F32), row(hgrn_norm[0]), 1, seq)
    h, = _ffn(h, [(y, w_out1, 0)], row(ffn_norm[1]), w_up1, ffn_conv_w[1], ffn_conv_b[1], w_down1,
              final_g, seq, final=True)
    return h.reshape(bn, seq, d)
```

```python
import functools
import math

import numpy as np
import jax
import jax.numpy as jnp
from jax import lax
from jax.experimental import pallas as pl
from jax.experimental.pallas import tpu as pltpu

F32 = jnp.float32
BF16 = jnp.bfloat16

D_MODEL = 1024
RET_HEADS = 8
RET_DK = 64
RET_DV = 128
RET_CHUNK = 128
ROPE_BASE = 10000.0
DIL_HEADS = 8
DIL_DH = 64
DIL_BRANCHES = ((128, 1), (512, 4), (2048, 16))
DIL_BLOCK = 128
HGRN_HEADS = 8
HGRN_DK = 128
HGRN_DV = 128
HGRN_CHUNK = 32
REL_BUCKETS = 32
REL_MAX_DIST = 2048
D_FF = 2816
CONV_WIDTH = 3
EPS = 1e-6

A_QK = RET_HEADS * RET_DK
A_V = RET_HEADS * RET_DV
B_W = DIL_HEADS * DIL_DH
EVEN_IN = 2 * A_QK + 2 * A_V + 3 * B_W
C_K = HGRN_HEADS * HGRN_DK
C_V = HGRN_HEADS * HGRN_DV
ODD_IN = 2 * C_K + 2 * C_V

LANES = 128
SUBLANES = 8
VMEM_LIMIT = 56 * 1024 * 1024
MASK_VALUE = -1e30

NT_DIMS = (((1,), (1,)), ((), ()))
TN_DIMS = (((0,), (0,)), ((), ()))


def _params(*semantics):
    return pltpu.CompilerParams(dimension_semantics=semantics, vmem_limit_bytes=VMEM_LIMIT)


def _silu(x):
    return x * jax.nn.sigmoid(x)


def _emit(fill, count):
    tokens = []
    for _ in range(count):
        thunk = next(fill, None)
        if thunk is not None:
            token = thunk()
            if token is not None:
                tokens.append(token)
    return tokens


def _anchor(tokens, zero_bits):
    acc = zero_bits
    for t in tokens:
        acc = acc & pltpu.bitcast(t, jnp.int32)
    return pltpu.bitcast(acc, F32)


BF16_SUBLANES = 2 * SUBLANES


def _rider_specs(weights, steps):
    in_specs, out_specs, out_shapes = [], [], []
    for arr, layer in weights:
        _, rows, cols = arr.shape
        rep = 1
        while (rows * rep) % (steps * BF16_SUBLANES):
            rep *= 2
        slab = rows * rep // steps
        slab_index = lambda i, rep=rep: jnp.minimum(i, steps - 1) // rep
        in_specs.append(pl.BlockSpec((None, slab, cols),
                                     lambda i, layer=layer, at=slab_index: (layer, at(i), 0)))
        out_specs.append(pl.BlockSpec((slab, cols), lambda i, at=slab_index: (at(i), 0)))
        out_shapes.append(jax.ShapeDtypeStruct((rows, cols), BF16))
    return in_specs, out_specs, out_shapes


def _with_riders(body, n_in, n_out, n_riders):
    def kernel_fn(*refs):
        ins, refs = refs[:n_in], refs[n_in:]
        rider_in, refs = refs[:n_riders], refs[n_riders:]
        outs, refs = refs[:n_out], refs[n_out:]
        rider_out, scratch = refs[:n_riders], refs[n_riders:]
        for src, dst in zip(rider_in, rider_out):
            dst[...] = src[...].astype(dst.dtype)
        body(*ins, *outs, *scratch)
    return kernel_fn


def _retention_tables(seq):
    h = jnp.arange(RET_HEADS, dtype=F32)
    log_g = jnp.log1p(-jnp.exp2(-5.0 - h))
    pos = jnp.arange(RET_CHUNK, dtype=F32)
    diff = pos[:, None] - pos[None, :]
    decay = jnp.where(diff >= 0, jnp.exp(jnp.maximum(diff, 0.0)[None] * log_g[:, None, None]), 0.0)
    q_scale = jnp.repeat(jnp.exp((pos + 1.0)[:, None] * log_g[None, :]), RET_DK, axis=1)
    k_scale = jnp.repeat(jnp.exp((RET_CHUNK - 1.0 - pos)[:, None] * log_g[None, :]), RET_DK, axis=1)
    chunk_decay = jnp.broadcast_to(jnp.exp(RET_CHUNK * log_g)[:, None, None], (RET_HEADS, 1, RET_DV))
    inv = ROPE_BASE ** (-jnp.arange(0, RET_DK, 2, dtype=F32) / RET_DK)
    ang = jnp.arange(seq, dtype=F32)[:, None] * inv[None, :]
    cos, sin = jnp.cos(ang), jnp.sin(ang)
    reps = LANES // RET_DK
    cos_t = jnp.tile(jnp.concatenate([cos, cos], axis=1), (1, reps))
    sin_t = jnp.tile(jnp.concatenate([-sin, sin], axis=1), (1, reps))
    return decay, q_scale, k_scale, chunk_decay, cos_t, sin_t


EVEN_TM = 512


def _even_in_kernel(tiles_per_seq, x_ref, g_ref, w_ref, cos_ref, sin_ref, qs_ref, ks_ref, dec_ref,
                    cd_ref, rn_ref, ya_ref, q4_ref, q16_ref, k4_ref, k16_ref, v4_ref, v16_ref,
                    slab_ref, slab4_ref, st_ref):
    tm = x_ref.shape[0]
    chunk = RET_CHUNK

    @pl.when(pl.program_id(0) % tiles_per_seq == 0)
    def _():
        st_ref[...] = jnp.zeros_like(st_ref)

    x = x_ref[...]
    ms = jnp.mean(x * x, axis=-1, keepdims=True)
    hn = (x * lax.rsqrt(ms + EPS) * g_ref[...]).astype(BF16)

    def proj(c):
        return jnp.dot(hn, w_ref[:, c * B_W:(c + 1) * B_W], preferred_element_type=F32)

    cos = cos_ref[...]
    sin = sin_ref[...]
    lane = lax.broadcasted_iota(jnp.int32, (tm, LANES), 1)
    half = RET_DK // 2
    first_half = (lane % RET_DK) < half

    def rot(t):
        partner = jnp.where(first_half, pltpu.roll(t, LANES - half, 1), pltpu.roll(t, half, 1))
        return t * cos + partner * sin

    groups = [slice(s * LANES, (s + 1) * LANES) for s in range(B_W // LANES)]

    r = proj(0)
    q_rot = [rot(r[:, sl]) for sl in groups]
    q = [t.astype(BF16) for t in q_rot]
    q_in = [(t * qs_ref[:, sl]).astype(BF16) for t, sl in zip(q_rot, groups)]
    r = proj(1)
    k_rot = [rot(r[:, sl]) * (RET_DK ** -0.5) for sl in groups]
    k = [t.astype(BF16) for t in k_rot]
    k_out = [(t * ks_ref[:, sl]).astype(BF16) for t, sl in zip(k_rot, groups)]
    v = jnp.concatenate([proj(2), proj(3)], axis=1).astype(BF16)
    gate = rn_ref[...] * _silu(jnp.concatenate([proj(4), proj(5)], axis=1))

    def dilated_pieces(c, slot, r4_ref, r16_ref, scale):
        def project():
            r = proj(c)
            if scale is not None:
                r = r * scale
            for s, sl in enumerate(groups):
                slab_ref[slot, s] = r[:, sl]

        def by4():
            for s, sl in enumerate(groups):
                for rho in range(4):
                    t = slab_ref[slot, s, pl.ds(rho, tm // 4, stride=4), :]
                    r4_ref[rho, :, sl] = t.astype(BF16)
                    slab4_ref[slot, s, rho * (tm // 4):(rho + 1) * (tm // 4), :] = t

        def by16():
            for s, sl in enumerate(groups):
                for rho4 in range(4):
                    for c in range(4):
                        r16_ref[rho4 + 4 * c, :, sl] = slab4_ref[
                            slot, s, pl.ds(rho4 * (tm // 4) + c, tm // 16, stride=4), :].astype(BF16)

        return [project, by4, by16]

    fill = iter(dilated_pieces(6, 0, q4_ref, q16_ref, DIL_DH ** -0.5)
                + dilated_pieces(7, 1, k4_ref, k16_ref, None)
                + dilated_pieces(8, 2, v4_ref, v16_ref, None))

    head0 = lax.broadcasted_iota(jnp.int32, (chunk, LANES), 1) < RET_DK
    heads = range(RET_HEADS)
    hs = [slice(h * RET_DV, (h + 1) * RET_DV) for h in heads]
    st = [st_ref[h] for h in heads]
    for ci in range(tm // chunk):
        rows = slice(ci * chunk, (ci + 1) * chunk)
        mask = [head0 if h % 2 == 0 else jnp.logical_not(head0) for h in heads]
        scores = [lax.dot_general(jnp.where(mask[h], q[h // 2][rows], 0), k[h // 2][rows], NT_DIMS,
                                  preferred_element_type=F32) * dec_ref[h] for h in heads]
        _emit(fill, 1)
        intra = [jnp.dot(scores[h].astype(BF16), v[rows, hs[h]], preferred_element_type=F32)
                 for h in heads]
        inter = [jnp.dot(jnp.where(mask[h], q_in[h // 2][rows], 0), st[h].astype(BF16),
                         preferred_element_type=F32) for h in heads]
        st = [st[h] * cd_ref[h] + lax.dot_general(k_out[h // 2][rows], v[rows, hs[h]], TN_DIMS,
                                                  preferred_element_type=F32) for h in heads]
        _emit(fill, 1)
        for h in heads:
            y = intra[h] + inter[h]
            yc = y - jnp.mean(y, axis=-1, keepdims=True)
            yn = yc * lax.rsqrt(jnp.mean(yc * yc, axis=-1, keepdims=True) + EPS)
            ya_ref[rows, hs[h]] = (yn * gate[rows, hs[h]]).astype(BF16)
        _emit(fill, 1)
    for run in fill:
        run()
    for h in heads:
        st_ref[h] = st[h]


def _even_in(x, g, w, ret_norm, seq, riders):
    tokens, d = x.shape
    bn = tokens // seq
    tm = EVEN_TM
    tps = seq // tm
    decay, q_scale, k_scale, chunk_decay, cos_t, sin_t = _retention_tables(seq)
    q_scale = jnp.tile(q_scale, (tm // RET_CHUNK, 1))
    k_scale = jnp.tile(k_scale, (tm // RET_CHUNK, 1))
    const = lambda shape: pl.BlockSpec(shape, lambda i: (0,) * len(shape), pipeline_mode=pl.Buffered(1))
    flat = lambda w_: pl.BlockSpec((tm, w_), lambda i: (i, 0))
    by = lambda r: pl.BlockSpec((None, r, tm // r, B_W), lambda i: (i // tps, 0, i % tps, 0))
    sds = jax.ShapeDtypeStruct
    lay = [sds((bn, 4, seq // 4, B_W), BF16), sds((bn, 16, seq // 16, B_W), BF16)]
    steps = tokens // tm
    in_specs = [
        flat(d), const((1, d)), const(w.shape),
        pl.BlockSpec((tm, LANES), lambda i: (i % tps, 0)),
        pl.BlockSpec((tm, LANES), lambda i: (i % tps, 0)),
        const((tm, A_QK)), const((tm, A_QK)),
        const(decay.shape), const(chunk_decay.shape), const((1, A_V)),
    ]
    out_specs = [flat(A_V)] + [by(4), by(16)] * 3
    rider_in, rider_out, rider_shapes = _rider_specs(riders, steps)
    return pl.pallas_call(
        _with_riders(functools.partial(_even_in_kernel, tps), len(in_specs), len(out_specs), len(riders)),
        grid=(steps,),
        in_specs=in_specs + rider_in,
        out_specs=out_specs + rider_out,
        out_shape=[sds((tokens, A_V), BF16)] + lay * 3 + rider_shapes,
        scratch_shapes=[pltpu.VMEM((3, B_W // LANES, tm, LANES), F32),
                        pltpu.VMEM((3, B_W // LANES, tm, LANES), F32),
                        pltpu.VMEM((RET_HEADS, 2 * RET_DK, RET_DV), F32)],
        compiler_params=_params("arbitrary"),
        name="even_in",
    )(x, g, w, cos_t, sin_t, q_scale, k_scale, decay, chunk_decay, ret_norm, *[a for a, _ in riders])


def _rel_bucket(dist):
    max_exact = REL_BUCKETS // 2
    d = dist.astype(F32)
    large = max_exact + (jnp.log(jnp.maximum(d, 1.0) / max_exact) / math.log(REL_MAX_DIST / max_exact)
                         * (REL_BUCKETS - max_exact)).astype(jnp.int32)
    large = jnp.minimum(large, REL_BUCKETS - 1)
    return jnp.where(dist < max_exact, dist, large)


DIL_BASE = 4
DIL_UNROLL = 8


def _dilated_buckets():
    lb = DIL_BLOCK
    slot = jnp.arange(lb)
    per = lb // DIL_BASE
    regrouped = DIL_BASE * (slot % per) + slot // per
    tables = []
    for (window, dilation), token in zip(DIL_BRANCHES, (regrouped, slot, slot)):
        a_idx = token[:, None]
        c_idx = jnp.concatenate([token, lb + token])[None, :]
        dist = lb + a_idx - c_idx
        valid = (dist >= 0) & (dist <= window // dilation)
        tables.append(jnp.where(valid, _rel_bucket(jnp.maximum(dist, 0) * dilation), -1))
    return jnp.stack(tables).astype(jnp.int32)


def _dilated_kernel(rb_ref, bucket_ref, q4_ref, k4_ref, v4_ref, q16_ref, k16_ref, v16_ref, y_ref,
                    bias_ref, acc_ref, m_ref, l_ref):
    lb = DIL_BLOCK
    seq = y_ref.shape[0]
    pair = pl.program_id(1)

    @pl.when(pl.program_id(0) == 0)
    def _():
        for br in range(len(DIL_BRANCHES)):
            bucket = bucket_ref[br]
            for a in range(2):
                h = 2 * pair + a
                t = jnp.full((lb, 2 * lb), MASK_VALUE, F32)
                for j in range(REL_BUCKETS):
                    t = jnp.where(bucket == j, rb_ref[j, h], t)
                bias_ref[br * DIL_HEADS + h] = t

    lane = lax.broadcasted_iota(jnp.int32, (lb, LANES), 1)
    head0 = lane < DIL_DH

    def attend(br, blocks):
        masks = (head0, jnp.logical_not(head0))
        bias = [bias_ref[br * DIL_HEADS + 2 * pair + a] for a in range(2)]
        ks = [jnp.concatenate([kp, kc], axis=0) for _, _, kp, kc, _, _ in blocks]
        vs = [jnp.concatenate([vp, vc], axis=0) for _, _, _, _, vp, vc in blocks]
        pens = [jnp.where(first, MASK_VALUE, 0.0) for first, *_ in blocks]
        scores = [[lax.dot_general(jnp.where(masks[a], blk[1], 0), k, NT_DIMS,
                                   preferred_element_type=F32) + bias[a] for a in range(2)]
                  for blk, k in zip(blocks, ks)]
        scores = [[jnp.concatenate([s[:, 0:lb] + pen, s[:, lb:]], axis=1) for s in pair_s]
                  for pair_s, pen in zip(scores, pens)]
        maxes = [[jnp.max(s, axis=-1, keepdims=True) for s in pair_s] for pair_s in scores]
        exps = [[jnp.exp(s - m) for s, m in zip(pair_s, pair_m)] for pair_s, pair_m in zip(scores, maxes)]
        sums = [[jnp.sum(e, axis=-1, keepdims=True) for e in pair_e] for pair_e in exps]
        outs = [[jnp.dot(e.astype(BF16), v, preferred_element_type=F32) for e in pair_e]
                for pair_e, v in zip(exps, vs)]
        return [tuple(jnp.where(head0, x[0], x[1]) for x in (o, m, l))
                for o, m, l in zip(outs, maxes, sums)]

    def block_rows(n):
        r0 = pl.multiple_of(n * lb, lb)
        rp = pl.multiple_of(jnp.maximum(n - 1, 0) * lb, lb)
        return pl.ds(r0, lb), pl.ds(rp, lb)

    base = DIL_BASE
    per = lb // base
    class_rows = seq // base

    def unit_blocks(it, carry):
        blocks, dests = [], []
        for j in range(DIL_UNROLL):
            n = it * DIL_UNROLL + j
            r0 = pl.multiple_of(n * per, per)
            rp = pl.multiple_of(jnp.maximum(n - 1, 0) * per, per)
            gather = lambda ref, at: jnp.concatenate([ref[rho, pl.ds(at, per), :] for rho in range(base)],
                                                     axis=0)
            blocks.append((n == 0, gather(q4_ref, r0), gather(k4_ref, rp), gather(k4_ref, r0),
                           gather(v4_ref, rp), gather(v4_ref, r0)))
            dests.append(r0)
        for r0, (acc, m, l) in zip(dests, attend(0, blocks)):
            for rho in range(base):
                rows = pl.ds(rho * class_rows + r0, per)
                piece = slice(rho * per, (rho + 1) * per)
                acc_ref[rows, :] = acc[piece]
                m_ref[rows, :] = m[piece]
                l_ref[rows, :] = l[piece]
        return carry

    lax.fori_loop(0, seq // (lb * DIL_UNROLL), unit_blocks, 0)

    def dilated_blocks(br, r, q_ref, k_ref, v_ref):
        nb = seq // (r * lb)
        sub = r // base

        def body(it, carry):
            blocks, dests = [], []
            for j in range(DIL_UNROLL):
                idx = it * DIL_UNROLL + j
                rho, n = idx // nb, idx % nb
                cur, prev = block_rows(n)
                blocks.append((n == 0, q_ref[rho, cur, :], k_ref[rho, prev, :], k_ref[rho, cur, :],
                               v_ref[rho, prev, :], v_ref[rho, cur, :]))
                start = (rho % base) * class_rows + n * (lb * sub) + rho // base
                dests.append(pl.ds(start, lb, stride=sub) if sub > 1
                             else pl.ds(pl.multiple_of(start, lb), lb))
            for rows, (acc, m, l) in zip(dests, attend(br, blocks)):
                m_old = m_ref[rows, :]
                m_new = jnp.maximum(m_old, m)
                c_old = jnp.exp(m_old - m_new)
                c_cur = jnp.exp(m - m_new)
                acc_ref[rows, :] = acc_ref[rows, :] * c_old + acc * c_cur
                l_ref[rows, :] = l_ref[rows, :] * c_old + l * c_cur
                m_ref[rows, :] = m_new
            return carry

        lax.fori_loop(0, r * nb // DIL_UNROLL, body, 0)

    dilated_blocks(1, DIL_BRANCHES[1][1], q4_ref, k4_ref, v4_ref)
    dilated_blocks(2, DIL_BRANCHES[2][1], q16_ref, k16_ref, v16_ref)

    def normalise(n, carry):
        for rho in range(base):
            rows = pl.ds(pl.multiple_of(rho * class_rows + n * lb, lb), lb)
            m_ref[pl.ds(n * (lb * base) + rho, lb, stride=base), :] = acc_ref[rows, :] / l_ref[rows, :]
        return carry

    lax.fori_loop(0, class_rows // lb, normalise, 0)

    def finish(n, carry):
        rows = pl.ds(pl.multiple_of(n * lb, lb), lb)
        y_ref[rows, :] = m_ref[rows, :].astype(y_ref.dtype)
        return carry

    lax.fori_loop(0, seq // lb, finish, 0)


def _dilated(layouts, rel_bias, bn, seq):
    lb = DIL_BLOCK
    assert all(w // r <= lb and seq % (r * lb) == 0 for w, r in DIL_BRANCHES)
    assert tuple(r for _, r in DIL_BRANCHES) == (1, 4, 16)
    nat = pl.BlockSpec((None, seq, LANES), lambda b, p: (b, 0, p))
    by = lambda r: pl.BlockSpec((None, r, seq // r, LANES), lambda b, p: (b, 0, 0, p))
    nbr = len(DIL_BRANCHES)
    return pl.pallas_call(
        _dilated_kernel,
        grid=(bn, DIL_HEADS // 2),
        in_specs=[pl.BlockSpec(memory_space=pltpu.SMEM),
                  pl.BlockSpec((nbr, lb, 2 * lb), lambda b, p: (0, 0, 0))]
        + [by(4)] * 3 + [by(16)] * 3,
        out_specs=nat,
        out_shape=jax.ShapeDtypeStruct((bn, seq, B_W), BF16),
        scratch_shapes=[pltpu.VMEM((nbr * DIL_HEADS, lb, 2 * lb), F32),
                        pltpu.VMEM((seq, LANES), F32), pltpu.VMEM((seq, LANES), F32),
                        pltpu.VMEM((seq, LANES), F32)],
        compiler_params=_params("arbitrary", "arbitrary"),
        name="dilated",
    )(rel_bias.astype(F32), _dilated_buckets(), *layouts)


FFN_TM = 512
FFN_R = 8
FFN_CHUNK = 256
FFN_DENSE_CHUNKS = 5


def _ffn_kernel(ntiles, tiles_per_seq, final, nparts, *refs):
    x_ref, y_refs, wo_refs = refs[0], refs[1:1 + nparts], refs[1 + nparts:1 + 2 * nparts]
    (g_ref, wup_ref, cw_ref, cb_ref, wd_ref, fg_ref, zero_ref, where_ref, o_ref, slab_in, slab_out,
     hn_cur, hn_next, res_ref, act_ref, down_ref, halo_ref) = refs[1 + 2 * nparts:]
    i = pl.program_id(0)
    tm = o_ref.shape[0]
    rows = tm // FFN_R
    nslab = D_MODEL // LANES
    ck = FFN_CHUNK
    group = lambda g: slice(g * rows, (g + 1) * rows)
    lanes = lambda s: slice(s * LANES, (s + 1) * LANES)
    anywhere = pl.multiple_of(where_ref[0], BF16_SUBLANES)

    def head_pieces():
        wide = 2 * LANES

        def project(p):
            def run():
                cols = slice(p * wide, (p + 1) * wide)
                h = x_ref[:, cols]
                for y_ref, wo_ref in zip(y_refs, wo_refs):
                    h = h + jnp.dot(y_ref[...], wo_ref[:, cols], preferred_element_type=F32)
                slab_in[2 * p] = h[:, 0:LANES]
                slab_in[2 * p + 1] = h[:, LANES:wide]
                return slab_in[2 * p + 1, pl.ds(anywhere, SUBLANES), :]
            return run

        def normalise(g):
            def run():
                x = jnp.concatenate([slab_in[s, pl.ds(g, rows, stride=FFN_R), :] for s in range(nslab)],
                                    axis=1)
                res_ref[i % 2, group(g), :] = x
                ms = jnp.mean(x * x, axis=-1, keepdims=True)
                hn_next[group(g), :] = (x * lax.rsqrt(ms + EPS) * g_ref[...]).astype(BF16)
                return hn_next[pl.ds(anywhere, BF16_SUBLANES), 0:LANES].astype(F32)[0:SUBLANES]
            return run

        return [project(p) for p in range(D_MODEL // wide)] + [normalise(g) for g in range(FFN_R)]

    def tail_pieces():
        def unpermute(g):
            def run():
                for s in range(nslab):
                    slab_out[s, pl.ds(g, rows, stride=FFN_R), :] = down_ref[group(g), lanes(s)]
                return slab_out[nslab - 1, pl.ds(anywhere, SUBLANES), :]
            return run

        def store(r0, r1):
            def run():
                y = jnp.concatenate([slab_out[s, r0:r1, :] for s in range(nslab)], axis=1)
                if final:
                    ms = jnp.mean(y * y, axis=-1, keepdims=True)
                    y = y * lax.rsqrt(ms + EPS) * fg_ref[...]
                o_ref[r0:r1, :] = y
                return o_ref[pl.ds(anywhere, SUBLANES), 0:LANES]
            return run

        return ([unpermute(g) for g in range(FFN_R)]
                + [store(g * rows, (g + 1) * rows) for g in range(FFN_R)])

    @pl.when(i == 0)
    def _():
        for run in head_pieces():
            run()
        down_ref[...] = jnp.zeros_like(down_ref)
        halo_ref[...] = jnp.zeros_like(halo_ref)

    @pl.when(jnp.logical_and(i >= 1, i <= ntiles))
    def _():
        hn_cur[...] = hn_next[...]
        fill = iter(tail_pieces() + head_pieces())
        seq_start = (i - 1) % tiles_per_seq == 0
        first_row = lax.broadcasted_iota(jnp.int32, (rows, ck), 0) == 0

        def shift_down(block, fill_row):
            return jnp.where(first_row, fill_row, pltpu.roll(block, 1, 0))

        zero_bits = zero_ref[...]

        def conv(cols, slot, tokens):
            u = jnp.dot(hn_cur[...], wup_ref[:, cols], preferred_element_type=F32)
            ug = [u[group(g), :] for g in range(FFN_R)]
            tails = jnp.where(seq_start, 0.0, halo_ref[slot])
            prev2 = shift_down(ug[FFN_R - 2], tails[SUBLANES - 1:SUBLANES, :])
            prev1 = shift_down(ug[FFN_R - 1], tails[2 * SUBLANES - 1:2 * SUBLANES, :])
            halo_ref[slot] = jnp.concatenate(
                [ug[FFN_R - 2][rows - SUBLANES:, :], ug[FFN_R - 1][rows - SUBLANES:, :]], axis=0)
            back1 = [prev1] + ug[:FFN_R - 1]
            back2 = [prev2, prev1] + ug[:FFN_R - 2]
            cw = cw_ref[:, cols]
            cb = cb_ref[:, cols]
            if tokens:
                cb = cb + jnp.tile(_anchor(tokens, zero_bits)[0:1, :], (1, ck // LANES))
            return [cw[0:1, :] * back2[g] + cw[1:2, :] * back1[g] + cw[2:3, :] * ug[g] + cb
                    for g in range(FFN_R)]

        tokens = []
        nchunk = D_FF // ck
        for c in range(nchunk):
            per_point = 2 if c < FFN_DENSE_CHUNKS else 1
            gate = conv(slice(c * ck, (c + 1) * ck), 2 * c, tokens)
            tokens = _emit(fill, per_point)
            value = conv(slice(D_FF + c * ck, D_FF + (c + 1) * ck), 2 * c + 1, tokens)
            tokens = _emit(fill, per_point)
            for g in range(FFN_R):
                act_ref[group(g), c * ck:(c + 1) * ck] = (_silu(gate[g]) * value[g]).astype(BF16)
        down_ref[...] = res_ref[(i - 1) % 2] + jnp.dot(act_ref[...], wd_ref[...],
                                                       preferred_element_type=F32)
        for run in fill:
            run()

    @pl.when(i == ntiles + 1)
    def _():
        for run in tail_pieces():
            run()


def _ffn(x, mixed, g, w_up, conv_w, conv_b, w_down, final_g, seq, final, riders=()):
    m = x.shape[0]
    tm = FFN_TM
    assert D_FF % FFN_CHUNK == 0 and tm % (FFN_R * 2 * SUBLANES) == 0 and seq % tm == 0
    const = lambda shape, at=(0, 0): pl.BlockSpec(shape, lambda i: at, pipeline_mode=pl.Buffered(1))
    ntiles = m // tm
    tile = lambda width, at: pl.BlockSpec((tm, width), lambda i: (at(i), 0))
    head_tile = lambda i: jnp.minimum(i, ntiles - 1)
    in_specs = (
        [tile(D_MODEL, head_tile)]
        + [tile(y.shape[1], head_tile) for y, _, _ in mixed]
        + [const((y.shape[1], D_MODEL), (row0 // y.shape[1], 0)) for y, _, row0 in mixed]
        + [const((1, D_MODEL)),
           const((D_MODEL, 2 * D_FF)), const((CONV_WIDTH, 2 * D_FF)), const((1, 2 * D_FF)),
           const((D_FF, D_MODEL)),
           const((1, D_MODEL)),
           const((SUBLANES, LANES)),
           pl.BlockSpec(memory_space=pltpu.SMEM)])
    rider_in, rider_out, rider_shapes = _rider_specs(riders, ntiles)
    return pl.pallas_call(
        _with_riders(functools.partial(_ffn_kernel, ntiles, seq // tm, final, len(mixed)),
                     len(in_specs), 1, len(riders)),
        grid=(ntiles + 2,),
        in_specs=in_specs + rider_in,
        out_specs=[tile(D_MODEL, lambda i: jnp.maximum(i - 2, 0))] + rider_out,
        out_shape=[jax.ShapeDtypeStruct((m, D_MODEL), F32)] + rider_shapes,
        scratch_shapes=[
            pltpu.VMEM((D_MODEL // LANES, tm, LANES), F32),
            pltpu.VMEM((D_MODEL // LANES, tm, LANES), F32),
            pltpu.VMEM((tm, D_MODEL), BF16),
            pltpu.VMEM((tm, D_MODEL), BF16),
            pltpu.VMEM((2, tm, D_MODEL), F32),
            pltpu.VMEM((tm, D_FF), BF16),
            pltpu.VMEM((tm, D_MODEL), F32),
            pltpu.VMEM((2 * D_FF // FFN_CHUNK, 2 * SUBLANES, FFN_CHUNK), F32),
        ],
        compiler_params=_params("arbitrary"),
        name="conv_ffn",
    )(x, *[y for y, _, _ in mixed], *[wo for _, wo, _ in mixed], g, w_up, conv_w.astype(F32),
      conv_b.astype(F32).reshape(1, -1), w_down, final_g, jnp.zeros((SUBLANES, LANES), jnp.int32),
      jnp.zeros((1,), jnp.int32), *[a for a, _ in riders])


HGRN_GROUP = 128
HGRN_TM = 256
HGRN_PROJ_COLS = 256
HGRN_EMIT = 2


def _hgrn_groups(proj_ref, row_slices, lower_all, gain, causal, tri, st, fill):
    c = HGRN_CHUNK
    heads = range(HGRN_HEADS)
    groups = range(len(row_slices))
    hk = [slice(h * HGRN_DK, (h + 1) * HGRN_DK) for h in heads]
    hv = [slice(h * HGRN_DV, (h + 1) * HGRN_DV) for h in heads]
    chunks = [slice(n * c, (n + 1) * c) for n in range(HGRN_GROUP // c)]

    q = [_silu(proj_ref[rows, 0:C_K]) for rows in row_slices]
    fg = [lower_all + (1.0 - lower_all) * jax.nn.sigmoid(proj_ref[rows, C_K:2 * C_K])
          for rows in row_slices]
    k = [1.0 - f for f in fg]
    split = []
    for f in fg:
        log_f = jnp.log(f)
        hi = log_f.astype(BF16)
        rem = log_f - hi.astype(F32)
        mid = rem.astype(BF16)
        split.append(jnp.concatenate([hi, mid, (rem - mid.astype(F32)).astype(BF16)], axis=1))
    _emit(fill, HGRN_EMIT)
    parts = [jnp.dot(tri, s, preferred_element_type=F32) for s in split]
    b = [p[:, 0:C_K] + p[:, C_K:2 * C_K] + p[:, 2 * C_K:3 * C_K] for p in parts]
    v = [proj_ref[rows, 2 * C_K:2 * C_K + C_V].astype(BF16) for rows in row_slices]
    q_t = [(q[g] * jnp.exp(b[g])).astype(BF16) for g in groups]
    k_t = [(k[g] * jnp.exp(-b[g])).astype(BF16) for g in groups]
    _emit(fill, HGRN_EMIT)
    scores = [[lax.dot_general(q_t[g][:, hk[h]], k_t[g][:, hk[h]], NT_DIMS, preferred_element_type=F32)
               for h in heads] for g in groups]
    scores = [[jnp.where(causal, s, 0.0).astype(BF16) for s in per_group] for per_group in scores]
    intra = [[jnp.dot(scores[g][h], v[g][:, hv[h]], preferred_element_type=F32) for h in heads]
             for g in groups]
    _emit(fill, HGRN_EMIT)
    b_last = [[b[g][cs.stop - 1:cs.stop, :] for cs in chunks] for g in groups]
    k_end = [[(k[g][cs, :] * jnp.exp(bl - b[g][cs, :])).astype(BF16) for cs, bl in zip(chunks, b_last[g])]
             for g in groups]
    decay = [[jnp.exp(bl) for bl in b_last[g]] for g in groups]
    gate = [gain * _silu(proj_ref[rows, 2 * C_K + C_V:2 * C_K + 2 * C_V]) for rows in row_slices]
    _emit(fill, HGRN_EMIT)
    st = list(st)
    outs = []
    for g in groups:
        inter = [[] for _ in heads]
        for n, cs in enumerate(chunks):
            for h in heads:
                inter[h].append(lax.dot_general(q_t[g][cs, hk[h]], st[h].astype(BF16), NT_DIMS,
                                                preferred_element_type=F32))
                st[h] = st[h] * decay[g][n][:, hk[h]] + lax.dot_general(
                    v[g][cs, hv[h]], k_end[g][n][:, hk[h]], TN_DIMS, preferred_element_type=F32)
        _emit(fill, HGRN_EMIT)
        per_head = []
        for h in heads:
            y = intra[g][h] + jnp.concatenate(inter[h], axis=0)
            yn = y * lax.rsqrt(jnp.mean(y * y, axis=-1, keepdims=True) + EPS)
            per_head.append((yn * gate[g][:, hv[h]]).astype(BF16))
        outs.append(per_head)
        _emit(fill, HGRN_EMIT)
    return outs, st


def _hgrn_layer_kernel(layer, tiles_per_seq, h0_ref, hnext_ref, g_ref, w_ref, lbp_ref, on_ref, o_ref,
                       proj_a, proj_b, st_ref):
    t = pl.program_id(0)
    tm = o_ref.shape[0]
    grp = HGRN_GROUP

    def projection(h_ref, dst):
        x = h_ref[...]
        ms = jnp.mean(x * x, axis=-1, keepdims=True)
        hn = (x * lax.rsqrt(ms + EPS) * g_ref[...]).astype(BF16)

        def piece(c):
            cols = slice(c * HGRN_PROJ_COLS, (c + 1) * HGRN_PROJ_COLS)

            def run():
                dst[:, cols] = jnp.dot(hn, w_ref[:, cols], preferred_element_type=F32)
            return run

        return [piece(c) for c in range(ODD_IN // HGRN_PROJ_COLS)]

    @pl.when(t == 0)
    def _():
        for run in projection(h0_ref, proj_a):
            run()

    @pl.when(t % tiles_per_seq == 0)
    def _():
        st_ref[...] = jnp.zeros_like(st_ref)

    def step(src, dst):
        fill = iter(projection(hnext_ref, dst))
        lbp = lbp_ref[...]
        ex = jnp.exp(lbp - jnp.max(lbp, axis=0, keepdims=True))
        sm = ex / jnp.sum(ex, axis=0, keepdims=True)
        lower_all = jnp.sum(sm[0:layer + 1, :], axis=0, keepdims=True) - sm[0:1, :]
        row = lax.broadcasted_iota(jnp.int32, (grp, grp), 0)
        colm = lax.broadcasted_iota(jnp.int32, (grp, grp), 1)
        causal = jnp.logical_and(row // HGRN_CHUNK == colm // HGRN_CHUNK, row >= colm)
        tri = jnp.where(causal, 1.0, 0.0).astype(BF16)
        gain = on_ref[...]
        st = [st_ref[h] for h in range(HGRN_HEADS)]
        row_slices = [slice(r0, r0 + grp) for r0 in range(0, tm, grp)]
        outs, st = _hgrn_groups(src, row_slices, lower_all, gain, causal, tri, st, fill)
        for rows, per_head in zip(row_slices, outs):
            for h, y in enumerate(per_head):
                o_ref[rows, h * HGRN_DV:(h + 1) * HGRN_DV] = y
        for run in fill:
            run()
        for h in range(HGRN_HEADS):
            st_ref[h] = st[h]

    @pl.when(t % 2 == 0)
    def _():
        step(proj_a, proj_b)

    @pl.when(t % 2 == 1)
    def _():
        step(proj_b, proj_a)


def _hgrn_layer(h, g, w, hgrn_lb, out_norm, layer, seq):
    tokens, d = h.shape
    tm = HGRN_TM
    ntiles = tokens // tm
    const = lambda shape: pl.BlockSpec(shape, lambda t: (0, 0), pipeline_mode=pl.Buffered(1))
    return pl.pallas_call(
        functools.partial(_hgrn_layer_kernel, layer, seq // tm),
        grid=(ntiles,),
        in_specs=[
            const((tm, d)),
            pl.BlockSpec((tm, d), lambda t: (jnp.minimum(t + 1, ntiles - 1), 0)),
            const((1, d)), const(w.shape), const(hgrn_lb.shape), const((1, C_V)),
        ],
        out_specs=pl.BlockSpec((tm, C_V), lambda t: (t, 0)),
        out_shape=jax.ShapeDtypeStruct((tokens, C_V), BF16),
        scratch_shapes=[pltpu.VMEM((tm, ODD_IN), F32), pltpu.VMEM((tm, ODD_IN), F32),
                        pltpu.VMEM((HGRN_HEADS, HGRN_DV, HGRN_DK), F32)],
        compiler_params=_params("arbitrary"),
        name="hgrn_layer",
    )(h, h, g, w, hgrn_lb, out_norm)


def kernel(x, even_w_in, even_w_out, ret_norm, rel_bias, odd_w_in, odd_w_out, hgrn_lb, hgrn_norm,
           mix_norm, ffn_norm, ffn_w_up, ffn_conv_w, ffn_conv_b, ffn_w_down, final_norm):
    bn, seq, d = x.shape
    tokens = bn * seq
    row = lambda a: a.reshape(1, -1).astype(F32)
    h = x.reshape(tokens, d)
    final_g = row(final_norm)

    outs = _even_in(h, row(mix_norm[0]), even_w_in[0].astype(BF16), row(ret_norm[0]), seq,
                    riders=[(even_w_out, 0), (ffn_w_up, 0), (ffn_w_down, 0)])
    ya, q4, q16, k4, k16, v4, v16, w_out0, w_up0, w_down0 = outs
    yb = _dilated((q4, k4, v4, q16, k16, v16), rel_bias, bn, seq).reshape(tokens, B_W)
    h, w_in1, w_out1, w_up1, w_down1 = _ffn(
        h, [(ya, w_out0, 0), (yb, w_out0, A_V)], row(ffn_norm[0]), w_up0, ffn_conv_w[0],
        ffn_conv_b[0], w_down0, final_g, seq, final=False,
        riders=[(odd_w_in, 0), (odd_w_out, 0), (ffn_w_up, 1), (ffn_w_down, 1)])

    y = _hgrn_layer(h, row(mix_norm[1]), w_in1, hgrn_lb.astype(F32), row(hgrn_norm[0]), 1, seq)
    h, = _ffn(h, [(y, w_out1, 0)], row(ffn_norm[1]), w_up1, ffn_conv_w[1], ffn_conv_b[1], w_down1,
              final_g, seq, final=True)
    return h.reshape(bn, seq, d)
```

```python
import functools
import math

import jax
import jax.numpy as jnp
from jax import lax
from jax.experimental import pallas as pl
from jax.experimental.pallas import tpu as pltpu

F32 = jnp.float32
BF16 = jnp.bfloat16

D_MODEL = 1024
RET_HEADS = 8
RET_DK = 64
RET_DV = 128
RET_CHUNK = 128
ROPE_BASE = 10000.0
DIL_HEADS = 8
DIL_DH = 64
DIL_BRANCHES = ((128, 1), (512, 4), (2048, 16))
DIL_BLOCK = 128
DIL_MID, DIL_WIDE = DIL_BRANCHES[1][1], DIL_BRANCHES[2][1]
HGRN_HEADS = 8
HGRN_DK = 128
HGRN_DV = 128
HGRN_CHUNK = 32
REL_BUCKETS = 32
REL_MAX_DIST = 2048
D_FF = 2816
CONV_WIDTH = 3
EPS = 1e-6

A_QK = RET_HEADS * RET_DK
A_V = RET_HEADS * RET_DV
B_W = DIL_HEADS * DIL_DH
EVEN_IN = 2 * A_QK + 2 * A_V + 3 * B_W
C_K = HGRN_HEADS * HGRN_DK
C_V = HGRN_HEADS * HGRN_DV
ODD_IN = 2 * C_K + 2 * C_V

LANES = 128
SUBLANES = 8
VMEM_LIMIT = 56 * 1024 * 1024
MASK_VALUE = -1e30

NT_DIMS = (((1,), (1,)), ((), ()))
TN_DIMS = (((0,), (0,)), ((), ()))


def _params(*semantics):
    return pltpu.CompilerParams(dimension_semantics=semantics, vmem_limit_bytes=VMEM_LIMIT)


def _silu(x):
    return x * jax.nn.sigmoid(x)


def _emit(fill, count):
    tokens = []
    for _ in range(count):
        thunk = next(fill, None)
        if thunk is not None:
            token = thunk()
            if token is not None:
                tokens.append(token)
    return tokens


def _anchor(tokens, zero_bits):
    acc = zero_bits
    for t in tokens:
        acc = acc & pltpu.bitcast(t, jnp.int32)
    return pltpu.bitcast(acc, F32)


BF16_SUBLANES = 2 * SUBLANES


def _rider_specs(weights, steps):
    in_specs, out_specs, out_shapes = [], [], []
    for arr, layer in weights:
        _, rows, cols = arr.shape
        rep = 1
        while (rows * rep) % (steps * BF16_SUBLANES):
            rep *= 2
        slab = rows * rep // steps
        slab_index = lambda i, rep=rep: jnp.minimum(i, steps - 1) // rep
        in_specs.append(pl.BlockSpec((None, slab, cols),
                                     lambda i, layer=layer, at=slab_index: (layer, at(i), 0)))
        out_specs.append(pl.BlockSpec((slab, cols), lambda i, at=slab_index: (at(i), 0)))
        out_shapes.append(jax.ShapeDtypeStruct((rows, cols), BF16))
    return in_specs, out_specs, out_shapes


def _with_riders(body, n_in, n_out, n_riders):
    def kernel_fn(*refs):
        ins, refs = refs[:n_in], refs[n_in:]
        rider_in, refs = refs[:n_riders], refs[n_riders:]
        outs, refs = refs[:n_out], refs[n_out:]
        rider_out, scratch = refs[:n_riders], refs[n_riders:]
        for src, dst in zip(rider_in, rider_out):
            dst[...] = src[...].astype(dst.dtype)
        body(*ins, *outs, *scratch)
    return kernel_fn


def _retention_tables(seq):
    h = jnp.arange(RET_HEADS, dtype=F32)
    log_g = jnp.log1p(-jnp.exp2(-5.0 - h))
    pos = jnp.arange(RET_CHUNK, dtype=F32)
    diff = pos[:, None] - pos[None, :]
    decay = jnp.where(diff >= 0, jnp.exp(jnp.maximum(diff, 0.0)[None] * log_g[:, None, None]), 0.0)
    q_scale = jnp.repeat(jnp.exp((pos + 1.0)[:, None] * log_g[None, :]), RET_DK, axis=1)
    k_scale = jnp.repeat(jnp.exp((RET_CHUNK - 1.0 - pos)[:, None] * log_g[None, :]), RET_DK, axis=1)
    chunk_decay = jnp.broadcast_to(jnp.exp(RET_CHUNK * log_g)[:, None, None], (RET_HEADS, 1, RET_DV))
    inv = ROPE_BASE ** (-jnp.arange(0, RET_DK, 2, dtype=F32) / RET_DK)
    ang = jnp.arange(seq, dtype=F32)[:, None] * inv[None, :]
    cos, sin = jnp.cos(ang), jnp.sin(ang)
    reps = LANES // RET_DK
    cos_t = jnp.tile(jnp.concatenate([cos, cos], axis=1), (1, reps))
    sin_t = jnp.tile(jnp.concatenate([-sin, sin], axis=1), (1, reps))
    return decay, q_scale, k_scale, chunk_decay, cos_t, sin_t


EVEN_TM = 512


def _even_in_kernel(tiles_per_seq, x_ref, g_ref, w_ref, cos_ref, sin_ref, qs_ref, ks_ref, dec_ref,
                    cd_ref, rn_ref, ya_ref, q4_ref, q16_ref, k4_ref, k16_ref, v4_ref, v16_ref,
                    slab_ref, slab4_ref, st_ref):
    tm = x_ref.shape[0]
    chunk = RET_CHUNK

    @pl.when(pl.program_id(0) % tiles_per_seq == 0)
    def _():
        st_ref[...] = jnp.zeros_like(st_ref)

    x = x_ref[...]
    ms = jnp.mean(x * x, axis=-1, keepdims=True)
    hn = (x * lax.rsqrt(ms + EPS) * g_ref[...]).astype(BF16)

    def proj(c):
        return jnp.dot(hn, w_ref[:, c * B_W:(c + 1) * B_W], preferred_element_type=F32)

    cos = cos_ref[...]
    sin = sin_ref[...]
    lane = lax.broadcasted_iota(jnp.int32, (tm, LANES), 1)
    half = RET_DK // 2
    first_half = (lane % RET_DK) < half

    def rot(t):
        partner = jnp.where(first_half, pltpu.roll(t, LANES - half, 1), pltpu.roll(t, half, 1))
        return t * cos + partner * sin

    groups = [slice(s * LANES, (s + 1) * LANES) for s in range(B_W // LANES)]

    r = proj(0)
    q_rot = [rot(r[:, sl]) for sl in groups]
    q = [t.astype(BF16) for t in q_rot]
    q_in = [(t * qs_ref[:, sl]).astype(BF16) for t, sl in zip(q_rot, groups)]
    r = proj(1)
    k_rot = [rot(r[:, sl]) * (RET_DK ** -0.5) for sl in groups]
    k = [t.astype(BF16) for t in k_rot]
    k_out = [(t * ks_ref[:, sl]).astype(BF16) for t, sl in zip(k_rot, groups)]
    v = jnp.concatenate([proj(2), proj(3)], axis=1).astype(BF16)
    gate = rn_ref[...] * _silu(jnp.concatenate([proj(4), proj(5)], axis=1))

    def dilated_pieces(c, slot, r4_ref, r16_ref, scale):
        def project():
            r = proj(c)
            if scale is not None:
                r = r * scale
            for s, sl in enumerate(groups):
                slab_ref[slot, s] = r[:, sl]

        mid, wide = DIL_MID, DIL_WIDE
        sub = wide // mid
        per = tm // mid

        def by4():
            for s, sl in enumerate(groups):
                for rho in range(mid):
                    t = slab_ref[slot, s, pl.ds(rho, per, stride=mid), :]
                    r4_ref[rho, :, sl] = t.astype(BF16)
                    slab4_ref[slot, s, rho * per:(rho + 1) * per, :] = t

        def by16():
            for s, sl in enumerate(groups):
                for rho in range(mid):
                    for c in range(sub):
                        r16_ref[rho + mid * c, :, sl] = slab4_ref[
                            slot, s, pl.ds(rho * per + c, tm // wide, stride=sub), :].astype(BF16)

        return [project, by4, by16]

    fill = iter(dilated_pieces(6, 0, q4_ref, q16_ref, DIL_DH ** -0.5)
                + dilated_pieces(7, 1, k4_ref, k16_ref, None)
                + dilated_pieces(8, 2, v4_ref, v16_ref, None))

    head0 = lax.broadcasted_iota(jnp.int32, (chunk, LANES), 1) < RET_DK
    heads = range(RET_HEADS)
    hs = [slice(h * RET_DV, (h + 1) * RET_DV) for h in heads]
    st = [st_ref[h] for h in heads]
    for ci in range(tm // chunk):
        rows = slice(ci * chunk, (ci + 1) * chunk)
        mask = [head0 if h % 2 == 0 else jnp.logical_not(head0) for h in heads]
        scores = [lax.dot_general(jnp.where(mask[h], q[h // 2][rows], 0), k[h // 2][rows], NT_DIMS,
                                  preferred_element_type=F32) * dec_ref[h] for h in heads]
        _emit(fill, 1)
        intra = [jnp.dot(scores[h].astype(BF16), v[rows, hs[h]], preferred_element_type=F32)
                 for h in heads]
        inter = [jnp.dot(jnp.where(mask[h], q_in[h // 2][rows], 0), st[h].astype(BF16),
                         preferred_element_type=F32) for h in heads]
        st = [st[h] * cd_ref[h] + lax.dot_general(k_out[h // 2][rows], v[rows, hs[h]], TN_DIMS,
                                                  preferred_element_type=F32) for h in heads]
        _emit(fill, 1)
        for h in heads:
            y = intra[h] + inter[h]
            yc = y - jnp.mean(y, axis=-1, keepdims=True)
            yn = yc * lax.rsqrt(jnp.mean(yc * yc, axis=-1, keepdims=True) + EPS)
            ya_ref[rows, hs[h]] = (yn * gate[rows, hs[h]]).astype(BF16)
        _emit(fill, 1)
    for run in fill:
        run()
    for h in heads:
        st_ref[h] = st[h]


def _even_in(x, g, w, ret_norm, seq, riders):
    tokens, d = x.shape
    bn = tokens // seq
    tm = EVEN_TM
    tps = seq // tm
    decay, q_scale, k_scale, chunk_decay, cos_t, sin_t = _retention_tables(seq)
    q_scale = jnp.tile(q_scale, (tm // RET_CHUNK, 1))
    k_scale = jnp.tile(k_scale, (tm // RET_CHUNK, 1))
    const = lambda shape: pl.BlockSpec(shape, lambda i: (0,) * len(shape), pipeline_mode=pl.Buffered(1))
    flat = lambda w_: pl.BlockSpec((tm, w_), lambda i: (i, 0))
    by = lambda r: pl.BlockSpec((None, r, tm // r, B_W), lambda i: (i // tps, 0, i % tps, 0))
    sds = jax.ShapeDtypeStruct
    lay = [sds((bn, r, seq // r, B_W), BF16) for r in (DIL_MID, DIL_WIDE)]
    steps = tokens // tm
    in_specs = [
        flat(d), const((1, d)), const(w.shape),
        pl.BlockSpec((tm, LANES), lambda i: (i % tps, 0)),
        pl.BlockSpec((tm, LANES), lambda i: (i % tps, 0)),
        const((tm, A_QK)), const((tm, A_QK)),
        const(decay.shape), const(chunk_decay.shape), const((1, A_V)),
    ]
    out_specs = [flat(A_V)] + [by(DIL_MID), by(DIL_WIDE)] * 3
    rider_in, rider_out, rider_shapes = _rider_specs(riders, steps)
    return pl.pallas_call(
        _with_riders(functools.partial(_even_in_kernel, tps), len(in_specs), len(out_specs), len(riders)),
        grid=(steps,),
        in_specs=in_specs + rider_in,
        out_specs=out_specs + rider_out,
        out_shape=[sds((tokens, A_V), BF16)] + lay * 3 + rider_shapes,
        scratch_shapes=[pltpu.VMEM((3, B_W // LANES, tm, LANES), F32),
                        pltpu.VMEM((3, B_W // LANES, tm, LANES), F32),
                        pltpu.VMEM((RET_HEADS, 2 * RET_DK, RET_DV), F32)],
        compiler_params=_params("arbitrary"),
        name="even_in",
    )(x, g, w, cos_t, sin_t, q_scale, k_scale, decay, chunk_decay, ret_norm, *[a for a, _ in riders])


def _rel_bucket(dist):
    max_exact = REL_BUCKETS // 2
    d = dist.astype(F32)
    large = max_exact + (jnp.log(jnp.maximum(d, 1.0) / max_exact) / math.log(REL_MAX_DIST / max_exact)
                         * (REL_BUCKETS - max_exact)).astype(jnp.int32)
    large = jnp.minimum(large, REL_BUCKETS - 1)
    return jnp.where(dist < max_exact, dist, large)


DIL_BASE = DIL_MID
DIL_UNROLL = 8


def _dilated_buckets():
    lb = DIL_BLOCK
    slot = jnp.arange(lb)
    per = lb // DIL_BASE
    regrouped = DIL_BASE * (slot % per) + slot // per
    tables = []
    for (window, dilation), token in zip(DIL_BRANCHES, (regrouped, slot, slot)):
        a_idx = token[:, None]
        c_idx = jnp.concatenate([token, lb + token])[None, :]
        dist = lb + a_idx - c_idx
        valid = (dist >= 0) & (dist <= window // dilation)
        tables.append(jnp.where(valid, _rel_bucket(jnp.maximum(dist, 0) * dilation), -1))
    return jnp.stack(tables).astype(jnp.int32)


def _dilated_kernel(rb_ref, bucket_ref, q4_ref, k4_ref, v4_ref, q16_ref, k16_ref, v16_ref, y_ref,
                    bias_ref, acc_ref, m_ref, l_ref):
    lb = DIL_BLOCK
    seq = y_ref.shape[0]
    pair = pl.program_id(1)

    @pl.when(pl.program_id(0) == 0)
    def _():
        for br in range(len(DIL_BRANCHES)):
            bucket = bucket_ref[br]
            for a in range(2):
                h = 2 * pair + a
                t = jnp.full((lb, 2 * lb), MASK_VALUE, F32)
                for j in range(REL_BUCKETS):
                    t = jnp.where(bucket == j, rb_ref[j, h], t)
                bias_ref[br * DIL_HEADS + h] = t

    lane = lax.broadcasted_iota(jnp.int32, (lb, LANES), 1)
    head0 = lane < DIL_DH

    def attend(br, blocks):
        masks = (head0, jnp.logical_not(head0))
        bias = [bias_ref[br * DIL_HEADS + 2 * pair + a] for a in range(2)]
        ks = [jnp.concatenate([kp, kc], axis=0) for _, _, kp, kc, _, _ in blocks]
        vs = [jnp.concatenate([vp, vc], axis=0) for _, _, _, _, vp, vc in blocks]
        pens = [jnp.where(first, MASK_VALUE, 0.0) for first, *_ in blocks]
        scores = [[lax.dot_general(jnp.where(masks[a], blk[1], 0), k, NT_DIMS,
                                   preferred_element_type=F32) + bias[a] for a in range(2)]
                  for blk, k in zip(blocks, ks)]
        scores = [[jnp.concatenate([s[:, 0:lb] + pen, s[:, lb:]], axis=1) for s in pair_s]
                  for pair_s, pen in zip(scores, pens)]
        maxes = [[jnp.max(s, axis=-1, keepdims=True) for s in pair_s] for pair_s in scores]
        exps = [[jnp.exp(s - m) for s, m in zip(pair_s, pair_m)] for pair_s, pair_m in zip(scores, maxes)]
        sums = [[jnp.sum(e, axis=-1, keepdims=True) for e in pair_e] for pair_e in exps]
        outs = [[jnp.dot(e.astype(BF16), v, preferred_element_type=F32) for e in pair_e]
                for pair_e, v in zip(exps, vs)]
        return [tuple(jnp.where(head0, x[0], x[1]) for x in (o, m, l))
                for o, m, l in zip(outs, maxes, sums)]

    def block_rows(n):
        r0 = pl.multiple_of(n * lb, lb)
        rp = pl.multiple_of(jnp.maximum(n - 1, 0) * lb, lb)
        return pl.ds(r0, lb), pl.ds(rp, lb)

    base = DIL_BASE
    per = lb // base
    class_rows = seq // base

    def unit_blocks(it, carry):
        blocks, dests = [], []
        for j in range(DIL_UNROLL):
            n = it * DIL_UNROLL + j
            r0 = pl.multiple_of(n * per, per)
            rp = pl.multiple_of(jnp.maximum(n - 1, 0) * per, per)
            gather = lambda ref, at: jnp.concatenate([ref[rho, pl.ds(at, per), :] for rho in range(base)],
                                                     axis=0)
            blocks.append((n == 0, gather(q4_ref, r0), gather(k4_ref, rp), gather(k4_ref, r0),
                           gather(v4_ref, rp), gather(v4_ref, r0)))
            dests.append(r0)
        for r0, (acc, m, l) in zip(dests, attend(0, blocks)):
            for rho in range(base):
                rows = pl.ds(rho * class_rows + r0, per)
                piece = slice(rho * per, (rho + 1) * per)
                acc_ref[rows, :] = acc[piece]
                m_ref[rows, :] = m[piece]
                l_ref[rows, :] = l[piece]
        return carry

    lax.fori_loop(0, seq // (lb * DIL_UNROLL), unit_blocks, 0)

    def dilated_blocks(br, r, q_ref, k_ref, v_ref):
        nb = seq // (r * lb)
        sub = r // base

        def body(it, carry):
            blocks, dests = [], []
            for j in range(DIL_UNROLL):
                idx = it * DIL_UNROLL + j
                rho, n = idx // nb, idx % nb
                cur, prev = block_rows(n)
                blocks.append((n == 0, q_ref[rho, cur, :], k_ref[rho, prev, :], k_ref[rho, cur, :],
                               v_ref[rho, prev, :], v_ref[rho, cur, :]))
                start = (rho % base) * class_rows + n * (lb * sub) + rho // base
                dests.append(pl.ds(start, lb, stride=sub) if sub > 1
                             else pl.ds(pl.multiple_of(start, lb), lb))
            for rows, (acc, m, l) in zip(dests, attend(br, blocks)):
                m_old = m_ref[rows, :]
                m_new = jnp.maximum(m_old, m)
                c_old = jnp.exp(m_old - m_new)
                c_cur = jnp.exp(m - m_new)
                acc_ref[rows, :] = acc_ref[rows, :] * c_old + acc * c_cur
                l_ref[rows, :] = l_ref[rows, :] * c_old + l * c_cur
                m_ref[rows, :] = m_new
            return carry

        lax.fori_loop(0, r * nb // DIL_UNROLL, body, 0)

    dilated_blocks(1, DIL_BRANCHES[1][1], q4_ref, k4_ref, v4_ref)
    dilated_blocks(2, DIL_BRANCHES[2][1], q16_ref, k16_ref, v16_ref)

    def normalise(n, carry):
        for rho in range(base):
            rows = pl.ds(pl.multiple_of(rho * class_rows + n * lb, lb), lb)
            m_ref[pl.ds(n * (lb * base) + rho, lb, stride=base), :] = acc_ref[rows, :] / l_ref[rows, :]
        return carry

    lax.fori_loop(0, class_rows // lb, normalise, 0)

    def finish(n, carry):
        rows = pl.ds(pl.multiple_of(n * lb, lb), lb)
        y_ref[rows, :] = m_ref[rows, :].astype(y_ref.dtype)
        return carry

    lax.fori_loop(0, seq // lb, finish, 0)


def _dilated(layouts, rel_bias, bn, seq):
    lb = DIL_BLOCK
    assert all(w // r <= lb and seq % (r * lb) == 0 for w, r in DIL_BRANCHES)
    assert DIL_BRANCHES[0][1] == 1 and DIL_WIDE % DIL_MID == 0
    nat = pl.BlockSpec((None, seq, LANES), lambda b, p: (b, 0, p))
    by = lambda r: pl.BlockSpec((None, r, seq // r, LANES), lambda b, p: (b, 0, 0, p))
    nbr = len(DIL_BRANCHES)
    return pl.pallas_call(
        _dilated_kernel,
        grid=(bn, DIL_HEADS // 2),
        in_specs=[pl.BlockSpec(memory_space=pltpu.SMEM),
                  pl.BlockSpec((nbr, lb, 2 * lb), lambda b, p: (0, 0, 0))]
        + [by(DIL_MID)] * 3 + [by(DIL_WIDE)] * 3,
        out_specs=nat,
        out_shape=jax.ShapeDtypeStruct((bn, seq, B_W), BF16),
        scratch_shapes=[pltpu.VMEM((nbr * DIL_HEADS, lb, 2 * lb), F32),
                        pltpu.VMEM((seq, LANES), F32), pltpu.VMEM((seq, LANES), F32),
                        pltpu.VMEM((seq, LANES), F32)],
        compiler_params=_params("arbitrary", "arbitrary"),
        name="dilated",
    )(rel_bias.astype(F32), _dilated_buckets(), *layouts)


FFN_TM = 512
FFN_R = 8
FFN_CHUNK = 256
FFN_DENSE_CHUNKS = 5


def _ffn_kernel(ntiles, tiles_per_seq, final, nparts, *refs):
    x_ref, y_refs, wo_refs = refs[0], refs[1:1 + nparts], refs[1 + nparts:1 + 2 * nparts]
    (g_ref, wup_ref, cw_ref, cb_ref, wd_ref, fg_ref, zero_ref, where_ref, o_ref, slab_in, slab_out,
     hn_cur, hn_next, res_ref, act_ref, down_ref, halo_ref) = refs[1 + 2 * nparts:]
    i = pl.program_id(0)
    tm = o_ref.shape[0]
    rows = tm // FFN_R
    nslab = D_MODEL // LANES
    ck = FFN_CHUNK
    group = lambda g: slice(g * rows, (g + 1) * rows)
    lanes = lambda s: slice(s * LANES, (s + 1) * LANES)
    anywhere = pl.multiple_of(where_ref[0], BF16_SUBLANES)

    def head_pieces():
        wide = 2 * LANES

        def project(p):
            def run():
                cols = slice(p * wide, (p + 1) * wide)
                h = x_ref[:, cols]
                for y_ref, wo_ref in zip(y_refs, wo_refs):
                    h = h + jnp.dot(y_ref[...], wo_ref[:, cols], preferred_element_type=F32)
                slab_in[2 * p] = h[:, 0:LANES]
                slab_in[2 * p + 1] = h[:, LANES:wide]
                return slab_in[2 * p + 1, pl.ds(anywhere, SUBLANES), :]
            return run

        def normalise(g):
            def run():
                x = jnp.concatenate([slab_in[s, pl.ds(g, rows, stride=FFN_R), :] for s in range(nslab)],
                                    axis=1)
                res_ref[i % 2, group(g), :] = x
                ms = jnp.mean(x * x, axis=-1, keepdims=True)
                hn_next[group(g), :] = (x * lax.rsqrt(ms + EPS) * g_ref[...]).astype(BF16)
                return hn_next[pl.ds(anywhere, BF16_SUBLANES), 0:LANES].astype(F32)[0:SUBLANES]
            return run

        return [project(p) for p in range(D_MODEL // wide)] + [normalise(g) for g in range(FFN_R)]

    def tail_pieces():
        def unpermute(g):
            def run():
                for s in range(nslab):
                    slab_out[s, pl.ds(g, rows, stride=FFN_R), :] = down_ref[group(g), lanes(s)]
                return slab_out[nslab - 1, pl.ds(anywhere, SUBLANES), :]
            return run

        def store(r0, r1):
            def run():
                y = jnp.concatenate([slab_out[s, r0:r1, :] for s in range(nslab)], axis=1)
                if final:
                    ms = jnp.mean(y * y, axis=-1, keepdims=True)
                    y = y * lax.rsqrt(ms + EPS) * fg_ref[...]
                o_ref[r0:r1, :] = y
                return o_ref[pl.ds(anywhere, SUBLANES), 0:LANES]
            return run

        return ([unpermute(g) for g in range(FFN_R)]
                + [store(g * rows, (g + 1) * rows) for g in range(FFN_R)])

    @pl.when(i == 0)
    def _():
        for run in head_pieces():
            run()
        down_ref[...] = jnp.zeros_like(down_ref)
        halo_ref[...] = jnp.zeros_like(halo_ref)

    @pl.when(jnp.logical_and(i >= 1, i <= ntiles))
    def _():
        hn_cur[...] = hn_next[...]
        fill = iter(tail_pieces() + head_pieces())
        seq_start = (i - 1) % tiles_per_seq == 0
        first_row = lax.broadcasted_iota(jnp.int32, (rows, ck), 0) == 0

        def shift_down(block, fill_row):
            return jnp.where(first_row, fill_row, pltpu.roll(block, 1, 0))

        zero_bits = zero_ref[...]

        def conv(cols, slot, tokens):
            u = jnp.dot(hn_cur[...], wup_ref[:, cols], preferred_element_type=F32)
            ug = [u[group(g), :] for g in range(FFN_R)]
            tails = jnp.where(seq_start, 0.0, halo_ref[slot])
            prev2 = shift_down(ug[FFN_R - 2], tails[SUBLANES - 1:SUBLANES, :])
            prev1 = shift_down(ug[FFN_R - 1], tails[2 * SUBLANES - 1:2 * SUBLANES, :])
            halo_ref[slot] = jnp.concatenate(
                [ug[FFN_R - 2][rows - SUBLANES:, :], ug[FFN_R - 1][rows - SUBLANES:, :]], axis=0)
            back1 = [prev1] + ug[:FFN_R - 1]
            back2 = [prev2, prev1] + ug[:FFN_R - 2]
            cw = cw_ref[:, cols]
            cb = cb_ref[:, cols]
            if tokens:
                cb = cb + jnp.tile(_anchor(tokens, zero_bits)[0:1, :], (1, ck // LANES))
            return [cw[0:1, :] * back2[g] + cw[1:2, :] * back1[g] + cw[2:3, :] * ug[g] + cb
                    for g in range(FFN_R)]

        tokens = []
        nchunk = D_FF // ck
        for c in range(nchunk):
            per_point = 2 if c < FFN_DENSE_CHUNKS else 1
            gate = conv(slice(c * ck, (c + 1) * ck), 2 * c, tokens)
            tokens = _emit(fill, per_point)
            value = conv(slice(D_FF + c * ck, D_FF + (c + 1) * ck), 2 * c + 1, tokens)
            tokens = _emit(fill, per_point)
            for g in range(FFN_R):
                act_ref[group(g), c * ck:(c + 1) * ck] = (_silu(gate[g]) * value[g]).astype(BF16)
        down_ref[...] = res_ref[(i - 1) % 2] + jnp.dot(act_ref[...], wd_ref[...],
                                                       preferred_element_type=F32)
        for run in fill:
            run()

    @pl.when(i == ntiles + 1)
    def _():
        for run in tail_pieces():
            run()


def _ffn(x, mixed, g, w_up, conv_w, conv_b, w_down, final_g, seq, final, riders=()):
    m = x.shape[0]
    tm = FFN_TM
    assert D_FF % FFN_CHUNK == 0 and tm % (FFN_R * 2 * SUBLANES) == 0 and seq % tm == 0
    const = lambda shape, at=(0, 0): pl.BlockSpec(shape, lambda i: at, pipeline_mode=pl.Buffered(1))
    ntiles = m // tm
    tile = lambda width, at: pl.BlockSpec((tm, width), lambda i: (at(i), 0))
    head_tile = lambda i: jnp.minimum(i, ntiles - 1)
    in_specs = (
        [tile(D_MODEL, head_tile)]
        + [tile(y.shape[1], head_tile) for y, _, _ in mixed]
        + [const((y.shape[1], D_MODEL), (row0 // y.shape[1], 0)) for y, _, row0 in mixed]
        + [const((1, D_MODEL)),
           const((D_MODEL, 2 * D_FF)), const((CONV_WIDTH, 2 * D_FF)), const((1, 2 * D_FF)),
           const((D_FF, D_MODEL)),
           const((1, D_MODEL)),
           const((SUBLANES, LANES)),
           pl.BlockSpec(memory_space=pltpu.SMEM)])
    rider_in, rider_out, rider_shapes = _rider_specs(riders, ntiles)
    return pl.pallas_call(
        _with_riders(functools.partial(_ffn_kernel, ntiles, seq // tm, final, len(mixed)),
                     len(in_specs), 1, len(riders)),
        grid=(ntiles + 2,),
        in_specs=in_specs + rider_in,
        out_specs=[tile(D_MODEL, lambda i: jnp.maximum(i - 2, 0))] + rider_out,
        out_shape=[jax.ShapeDtypeStruct((m, D_MODEL), F32)] + rider_shapes,
        scratch_shapes=[
            pltpu.VMEM((D_MODEL // LANES, tm, LANES), F32),
            pltpu.VMEM((D_MODEL // LANES, tm, LANES), F32),
            pltpu.VMEM((tm, D_MODEL), BF16),
            pltpu.VMEM((tm, D_MODEL), BF16),
            pltpu.VMEM((2, tm, D_MODEL), F32),
            pltpu.VMEM((tm, D_FF), BF16),
            pltpu.VMEM((tm, D_MODEL), F32),
            pltpu.VMEM((2 * D_FF // FFN_CHUNK, 2 * SUBLANES, FFN_CHUNK), F32),
        ],
        compiler_params=_params("arbitrary"),
        name="conv_ffn",
    )(x, *[y for y, _, _ in mixed], *[wo for _, wo, _ in mixed], g, w_up, conv_w.astype(F32),
      conv_b.astype(F32).reshape(1, -1), w_down, final_g, jnp.zeros((SUBLANES, LANES), jnp.int32),
      jnp.zeros((1,), jnp.int32), *[a for a, _ in riders])


HGRN_GROUP = 128
HGRN_TM = 256
HGRN_PROJ_COLS = 256
HGRN_EMIT = 2


def _hgrn_groups(proj_ref, row_slices, lower_all, gain, causal, tri, st, fill):
    c = HGRN_CHUNK
    heads = range(HGRN_HEADS)
    groups = range(len(row_slices))
    hk = [slice(h * HGRN_DK, (h + 1) * HGRN_DK) for h in heads]
    hv = [slice(h * HGRN_DV, (h + 1) * HGRN_DV) for h in heads]
    chunks = [slice(n * c, (n + 1) * c) for n in range(HGRN_GROUP // c)]

    q = [_silu(proj_ref[rows, 0:C_K]) for rows in row_slices]
    fg = [lower_all + (1.0 - lower_all) * jax.nn.sigmoid(proj_ref[rows, C_K:2 * C_K])
          for rows in row_slices]
    k = [1.0 - f for f in fg]
    split = []
    for f in fg:
        log_f = jnp.log(f)
        hi = log_f.astype(BF16)
        rem = log_f - hi.astype(F32)
        mid = rem.astype(BF16)
        split.append(jnp.concatenate([hi, mid, (rem - mid.astype(F32)).astype(BF16)], axis=1))
    _emit(fill, HGRN_EMIT)
    parts = [jnp.dot(tri, s, preferred_element_type=F32) for s in split]
    b = [p[:, 0:C_K] + p[:, C_K:2 * C_K] + p[:, 2 * C_K:3 * C_K] for p in parts]
    v = [proj_ref[rows, 2 * C_K:2 * C_K + C_V].astype(BF16) for rows in row_slices]
    q_t = [(q[g] * jnp.exp(b[g])).astype(BF16) for g in groups]
    k_t = [(k[g] * jnp.exp(-b[g])).astype(BF16) for g in groups]
    _emit(fill, HGRN_EMIT)
    scores = [[lax.dot_general(q_t[g][:, hk[h]], k_t[g][:, hk[h]], NT_DIMS, preferred_element_type=F32)
               for h in heads] for g in groups]
    scores = [[jnp.where(causal, s, 0.0).astype(BF16) for s in per_group] for per_group in scores]
    intra = [[jnp.dot(scores[g][h], v[g][:, hv[h]], preferred_element_type=F32) for h in heads]
             for g in groups]
    _emit(fill, HGRN_EMIT)
    b_last = [[b[g][cs.stop - 1:cs.stop, :] for cs in chunks] for g in groups]
    k_end = [[(k[g][cs, :] * jnp.exp(bl - b[g][cs, :])).astype(BF16) for cs, bl in zip(chunks, b_last[g])]
             for g in groups]
    decay = [[jnp.exp(bl) for bl in b_last[g]] for g in groups]
    gate = [gain * _silu(proj_ref[rows, 2 * C_K + C_V:2 * C_K + 2 * C_V]) for rows in row_slices]
    _emit(fill, HGRN_EMIT)
    st = list(st)
    outs = []
    for g in groups:
        inter = [[] for _ in heads]
        for n, cs in enumerate(chunks):
            for h in heads:
                inter[h].append(lax.dot_general(q_t[g][cs, hk[h]], st[h].astype(BF16), NT_DIMS,
                                                preferred_element_type=F32))
                st[h] = st[h] * decay[g][n][:, hk[h]] + lax.dot_general(
                    v[g][cs, hv[h]], k_end[g][n][:, hk[h]], TN_DIMS, preferred_element_type=F32)
        _emit(fill, HGRN_EMIT)
        per_head = []
        for h in heads:
            y = intra[g][h] + jnp.concatenate(inter[h], axis=0)
            yn = y * lax.rsqrt(jnp.mean(y * y, axis=-1, keepdims=True) + EPS)
            per_head.append((yn * gate[g][:, hv[h]]).astype(BF16))
        outs.append(per_head)
        _emit(fill, HGRN_EMIT)
    return outs, st


def _hgrn_layer_kernel(layer, tiles_per_seq, h0_ref, hnext_ref, g_ref, w_ref, lbp_ref, on_ref, o_ref,
                       proj_a, proj_b, st_ref):
    t = pl.program_id(0)
    tm = o_ref.shape[0]
    grp = HGRN_GROUP

    def projection(h_ref, dst):
        x = h_ref[...]
        ms = jnp.mean(x * x, axis=-1, keepdims=True)
        hn = (x * lax.rsqrt(ms + EPS) * g_ref[...]).astype(BF16)

        def piece(c):
            cols = slice(c * HGRN_PROJ_COLS, (c + 1) * HGRN_PROJ_COLS)

            def run():
                dst[:, cols] = jnp.dot(hn, w_ref[:, cols], preferred_element_type=F32)
            return run

        return [piece(c) for c in range(ODD_IN // HGRN_PROJ_COLS)]

    @pl.when(t == 0)
    def _():
        for run in projection(h0_ref, proj_a):
            run()

    @pl.when(t % tiles_per_seq == 0)
    def _():
        st_ref[...] = jnp.zeros_like(st_ref)

    def step(src, dst):
        fill = iter(projection(hnext_ref, dst))
        lbp = lbp_ref[...]
        ex = jnp.exp(lbp - jnp.max(lbp, axis=0, keepdims=True))
        sm = ex / jnp.sum(ex, axis=0, keepdims=True)
        lower_all = jnp.sum(sm[0:layer + 1, :], axis=0, keepdims=True) - sm[0:1, :]
        row = lax.broadcasted_iota(jnp.int32, (grp, grp), 0)
        colm = lax.broadcasted_iota(jnp.int32, (grp, grp), 1)
        causal = jnp.logical_and(row // HGRN_CHUNK == colm // HGRN_CHUNK, row >= colm)
        tri = jnp.where(causal, 1.0, 0.0).astype(BF16)
        gain = on_ref[...]
        st = [st_ref[h] for h in range(HGRN_HEADS)]
        row_slices = [slice(r0, r0 + grp) for r0 in range(0, tm, grp)]
        outs, st = _hgrn_groups(src, row_slices, lower_all, gain, causal, tri, st, fill)
        for rows, per_head in zip(row_slices, outs):
            for h, y in enumerate(per_head):
                o_ref[rows, h * HGRN_DV:(h + 1) * HGRN_DV] = y
        for run in fill:
            run()
        for h in range(HGRN_HEADS):
            st_ref[h] = st[h]

    @pl.when(t % 2 == 0)
    def _():
        step(proj_a, proj_b)

    @pl.when(t % 2 == 1)
    def _():
        step(proj_b, proj_a)


def _hgrn_layer(h, g, w, hgrn_lb, out_norm, layer, seq):
    tokens, d = h.shape
    tm = HGRN_TM
    ntiles = tokens // tm
    const = lambda shape: pl.BlockSpec(shape, lambda t: (0, 0), pipeline_mode=pl.Buffered(1))
    return pl.pallas_call(
        functools.partial(_hgrn_layer_kernel, layer, seq // tm),
        grid=(ntiles,),
        in_specs=[
            const((tm, d)),
            pl.BlockSpec((tm, d), lambda t: (jnp.minimum(t + 1, ntiles - 1), 0)),
            const((1, d)), const(w.shape), const(hgrn_lb.shape), const((1, C_V)),
        ],
        out_specs=pl.BlockSpec((tm, C_V), lambda t: (t, 0)),
        out_shape=jax.ShapeDtypeStruct((tokens, C_V), BF16),
        scratch_shapes=[pltpu.VMEM((tm, ODD_IN), F32), pltpu.VMEM((tm, ODD_IN), F32),
                        pltpu.VMEM((HGRN_HEADS, HGRN_DV, HGRN_DK), F32)],
        compiler_params=_params("arbitrary"),
        name="hgrn_layer",
    )(h, h, g, w, hgrn_lb, out_norm)


def kernel(x, even_w_in, even_w_out, ret_norm, rel_bias, odd_w_in, odd_w_out, hgrn_lb, hgrn_norm,
           mix_norm, ffn_norm, ffn_w_up, ffn_conv_w, ffn_conv_b, ffn_w_down, final_norm):
    bn, seq, d = x.shape
    tokens = bn * seq
    row = lambda a: a.reshape(1, -1).astype(F32)
    h = x.reshape(tokens, d)
    final_g = row(final_norm)

    outs = _even_in(h, row(mix_norm[0]), even_w_in[0].astype(BF16), row(ret_norm[0]), seq,
                    riders=[(even_w_out, 0), (ffn_w_up, 0), (ffn_w_down, 0)])
    ya, q4, q16, k4, k16, v4, v16, w_out0, w_up0, w_down0 = outs
    yb = _dilated((q4, k4, v4, q16, k16, v16), rel_bias, bn, seq).reshape(tokens, B_W)
    h, w_in1, w_out1, w_up1, w_down1 = _ffn(
        h, [(ya, w_out0, 0), (yb, w_out0, A_V)], row(ffn_norm[0]), w_up0, ffn_conv_w[0],
        ffn_conv_b[0], w_down0, final_g, seq, final=False,
        riders=[(odd_w_in, 0), (odd_w_out, 0), (ffn_w_up, 1), (ffn_w_down, 1)])

    y = _hgrn_layer(h, row(mix_norm[1]), w_in1, hgrn_lb.astype(F32), row(hgrn_norm[0]), 1, seq)
    h, = _ffn(h, [(y, w_out1, 0)], row(ffn_norm[1]), w_up1, ffn_conv_w[1], ffn_conv_b[1], w_down1,
              final_g, seq, final=True)
    return h.reshape(bn, seq, d)
```

```python
import functools
import math

import jax
import jax.numpy as jnp
from jax import lax
from jax.experimental import pallas as pl
from jax.experimental.pallas import tpu as pltpu

F32 = jnp.float32
BF16 = jnp.bfloat16

D_MODEL = 1024
RET_HEADS = 8
RET_DK = 64
RET_DV = 128
RET_CHUNK = 128
ROPE_BASE = 10000.0
DIL_HEADS = 8
DIL_DH = 64
DIL_BRANCHES = ((128, 1), (512, 4), (2048, 16))
DIL_BLOCK = 128
DIL_MID, DIL_WIDE = DIL_BRANCHES[1][1], DIL_BRANCHES[2][1]
HGRN_HEADS = 8
HGRN_DK = 128
HGRN_DV = 128
HGRN_CHUNK = 32
REL_BUCKETS = 32
REL_MAX_DIST = 2048
D_FF = 2816
CONV_WIDTH = 3
EPS = 1e-6

A_QK = RET_HEADS * RET_DK
A_V = RET_HEADS * RET_DV
B_W = DIL_HEADS * DIL_DH
EVEN_IN = 2 * A_QK + 2 * A_V + 3 * B_W
C_K = HGRN_HEADS * HGRN_DK
C_V = HGRN_HEADS * HGRN_DV
ODD_IN = 2 * C_K + 2 * C_V

LANES = 128
SUBLANES = 8
VMEM_LIMIT = 56 * 1024 * 1024
MASK_VALUE = -1e30

NT_DIMS = (((1,), (1,)), ((), ()))
TN_DIMS = (((0,), (0,)), ((), ()))


def _params(*semantics):
    return pltpu.CompilerParams(dimension_semantics=semantics, vmem_limit_bytes=VMEM_LIMIT)


def _silu(x):
    return x * jax.nn.sigmoid(x)


def _emit(fill, count):
    tokens = []
    for _ in range(count):
        thunk = next(fill, None)
        if thunk is not None:
            token = thunk()
            if token is not None:
                tokens.append(token)
    return tokens


def _anchor(tokens, zero_bits):
    acc = zero_bits
    for t in tokens:
        acc = acc & pltpu.bitcast(t, jnp.int32)
    return pltpu.bitcast(acc, F32)


BF16_SUBLANES = 2 * SUBLANES


def _rider_specs(weights, steps):
    in_specs, out_specs, out_shapes = [], [], []
    for arr, layer in weights:
        _, rows, cols = arr.shape
        rep = 1
        while (rows * rep) % (steps * BF16_SUBLANES):
            rep *= 2
        slab = rows * rep // steps
        slab_index = lambda i, rep=rep: jnp.minimum(i, steps - 1) // rep
        in_specs.append(pl.BlockSpec((None, slab, cols),
                                     lambda i, layer=layer, at=slab_index: (layer, at(i), 0)))
        out_specs.append(pl.BlockSpec((slab, cols), lambda i, at=slab_index: (at(i), 0)))
        out_shapes.append(jax.ShapeDtypeStruct((rows, cols), BF16))
    return in_specs, out_specs, out_shapes


def _with_riders(body, n_in, n_out, n_riders):
    def kernel_fn(*refs):
        ins, refs = refs[:n_in], refs[n_in:]
        rider_in, refs = refs[:n_riders], refs[n_riders:]
        outs, refs = refs[:n_out], refs[n_out:]
        rider_out, scratch = refs[:n_riders], refs[n_riders:]
        for src, dst in zip(rider_in, rider_out):
            dst[...] = src[...].astype(dst.dtype)
        body(*ins, *outs, *scratch)
    return kernel_fn


def _retention_tables(seq):
    h = jnp.arange(RET_HEADS, dtype=F32)
    log_g = jnp.log1p(-jnp.exp2(-5.0 - h))
    pos = jnp.arange(RET_CHUNK, dtype=F32)
    diff = pos[:, None] - pos[None, :]
    decay = jnp.where(diff >= 0, jnp.exp(jnp.maximum(diff, 0.0)[None] * log_g[:, None, None]), 0.0)
    q_scale = jnp.repeat(jnp.exp((pos + 1.0)[:, None] * log_g[None, :]), RET_DK, axis=1)
    k_scale = jnp.repeat(jnp.exp((RET_CHUNK - 1.0 - pos)[:, None] * log_g[None, :]), RET_DK, axis=1)
    chunk_decay = jnp.broadcast_to(jnp.exp(RET_CHUNK * log_g)[:, None, None], (RET_HEADS, 1, RET_DV))
    inv = ROPE_BASE ** (-jnp.arange(0, RET_DK, 2, dtype=F32) / RET_DK)
    ang = jnp.arange(seq, dtype=F32)[:, None] * inv[None, :]
    cos, sin = jnp.cos(ang), jnp.sin(ang)
    reps = LANES // RET_DK
    cos_t = jnp.tile(jnp.concatenate([cos, cos], axis=1), (1, reps))
    sin_t = jnp.tile(jnp.concatenate([-sin, sin], axis=1), (1, reps))
    return decay, q_scale, k_scale, chunk_decay, cos_t, sin_t


EVEN_TM = 512


def _even_in_kernel(tiles_per_seq, x_ref, g_ref, w_ref, cos_ref, sin_ref, qs_ref, ks_ref, dec_ref,
                    cd_ref, rn_ref, ya_ref, q4_ref, q16_ref, k4_ref, k16_ref, v4_ref, v16_ref,
                    slab_ref, slab4_ref, st_ref):
    tm = x_ref.shape[0]
    chunk = RET_CHUNK

    @pl.when(pl.program_id(0) % tiles_per_seq == 0)
    def _():
        st_ref[...] = jnp.zeros_like(st_ref)

    x = x_ref[...]
    ms = jnp.mean(x * x, axis=-1, keepdims=True)
    hn = (x * lax.rsqrt(ms + EPS) * g_ref[...]).astype(BF16)

    def proj(c):
        return jnp.dot(hn, w_ref[:, c * B_W:(c + 1) * B_W], preferred_element_type=F32)

    cos = cos_ref[...]
    sin = sin_ref[...]
    lane = lax.broadcasted_iota(jnp.int32, (tm, LANES), 1)
    half = RET_DK // 2
    first_half = (lane % RET_DK) < half

    def rot(t):
        partner = jnp.where(first_half, pltpu.roll(t, LANES - half, 1), pltpu.roll(t, half, 1))
        return t * cos + partner * sin

    groups = [slice(s * LANES, (s + 1) * LANES) for s in range(B_W // LANES)]

    r = proj(0)
    q_rot = [rot(r[:, sl]) for sl in groups]
    q = [t.astype(BF16) for t in q_rot]
    q_in = [(t * qs_ref[:, sl]).astype(BF16) for t, sl in zip(q_rot, groups)]
    r = proj(1)
    k_rot = [rot(r[:, sl]) * (RET_DK ** -0.5) for sl in groups]
    k = [t.astype(BF16) for t in k_rot]
    k_out = [(t * ks_ref[:, sl]).astype(BF16) for t, sl in zip(k_rot, groups)]
    v = jnp.concatenate([proj(2), proj(3)], axis=1).astype(BF16)
    gate = rn_ref[...] * _silu(jnp.concatenate([proj(4), proj(5)], axis=1))

    def dilated_pieces(c, slot, r4_ref, r16_ref, scale):
        def project():
            r = proj(c)
            if scale is not None:
                r = r * scale
            for s, sl in enumerate(groups):
                slab_ref[slot, s] = r[:, sl]

        mid, wide = DIL_MID, DIL_WIDE
        sub = wide // mid
        per = tm // mid

        def by4():
            for s, sl in enumerate(groups):
                for rho in range(mid):
                    t = slab_ref[slot, s, pl.ds(rho, per, stride=mid), :]
                    r4_ref[rho, :, sl] = t.astype(BF16)
                    slab4_ref[slot, s, rho * per:(rho + 1) * per, :] = t

        def by16():
            for s, sl in enumerate(groups):
                for rho in range(mid):
                    for c in range(sub):
                        r16_ref[rho + mid * c, :, sl] = slab4_ref[
                            slot, s, pl.ds(rho * per + c, tm // wide, stride=sub), :].astype(BF16)

        return [project, by4, by16]

    fill = iter(dilated_pieces(6, 0, q4_ref, q16_ref, DIL_DH ** -0.5)
                + dilated_pieces(7, 1, k4_ref, k16_ref, None)
                + dilated_pieces(8, 2, v4_ref, v16_ref, None))

    head0 = lax.broadcasted_iota(jnp.int32, (chunk, LANES), 1) < RET_DK
    heads = range(RET_HEADS)
    hs = [slice(h * RET_DV, (h + 1) * RET_DV) for h in heads]
    st = [st_ref[h] for h in heads]
    for ci in range(tm // chunk):
        rows = slice(ci * chunk, (ci + 1) * chunk)
        mask = [head0 if h % 2 == 0 else jnp.logical_not(head0) for h in heads]
        scores = [lax.dot_general(jnp.where(mask[h], q[h // 2][rows], 0), k[h // 2][rows], NT_DIMS,
                                  preferred_element_type=F32) * dec_ref[h] for h in heads]
        _emit(fill, 1)
        intra = [jnp.dot(scores[h].astype(BF16), v[rows, hs[h]], preferred_element_type=F32)
                 for h in heads]
        inter = [jnp.dot(jnp.where(mask[h], q_in[h // 2][rows], 0), st[h].astype(BF16),
                         preferred_element_type=F32) for h in heads]
        st = [st[h] * cd_ref[h] + lax.dot_general(k_out[h // 2][rows], v[rows, hs[h]], TN_DIMS,
                                                  preferred_element_type=F32) for h in heads]
        _emit(fill, 1)
        for h in heads:
            y = intra[h] + inter[h]
            yc = y - jnp.mean(y, axis=-1, keepdims=True)
            yn = yc * lax.rsqrt(jnp.mean(yc * yc, axis=-1, keepdims=True) + EPS)
            ya_ref[rows, hs[h]] = (yn * gate[rows, hs[h]]).astype(BF16)
        _emit(fill, 1)
    for run in fill:
        run()
    for h in heads:
        st_ref[h] = st[h]


def _even_in(x, g, w, ret_norm, seq, riders):
    tokens, d = x.shape
    bn = tokens // seq
    tm = EVEN_TM
    tps = seq // tm
    decay, q_scale, k_scale, chunk_decay, cos_t, sin_t = _retention_tables(seq)
    q_scale = jnp.tile(q_scale, (tm // RET_CHUNK, 1))
    k_scale = jnp.tile(k_scale, (tm // RET_CHUNK, 1))
    const = lambda shape: pl.BlockSpec(shape, lambda i: (0,) * len(shape), pipeline_mode=pl.Buffered(1))
    flat = lambda w_: pl.BlockSpec((tm, w_), lambda i: (i, 0))
    by = lambda r: pl.BlockSpec((None, r, tm // r, B_W), lambda i: (i // tps, 0, i % tps, 0))
    sds = jax.ShapeDtypeStruct
    lay = [sds((bn, r, seq // r, B_W), BF16) for r in (DIL_MID, DIL_WIDE)]
    steps = tokens // tm
    in_specs = [
        flat(d), const((1, d)), const(w.shape),
        pl.BlockSpec((tm, LANES), lambda i: (i % tps, 0)),
        pl.BlockSpec((tm, LANES), lambda i: (i % tps, 0)),
        const((tm, A_QK)), const((tm, A_QK)),
        const(decay.shape), const(chunk_decay.shape), const((1, A_V)),
    ]
    out_specs = [flat(A_V)] + [by(DIL_MID), by(DIL_WIDE)] * 3
    rider_in, rider_out, rider_shapes = _rider_specs(riders, steps)
    return pl.pallas_call(
        _with_riders(functools.partial(_even_in_kernel, tps), len(in_specs), len(out_specs), len(riders)),
        grid=(steps,),
        in_specs=in_specs + rider_in,
        out_specs=out_specs + rider_out,
        out_shape=[sds((tokens, A_V), BF16)] + lay * 3 + rider_shapes,
        scratch_shapes=[pltpu.VMEM((3, B_W // LANES, tm, LANES), F32),
                        pltpu.VMEM((3, B_W // LANES, tm, LANES), F32),
                        pltpu.VMEM((RET_HEADS, 2 * RET_DK, RET_DV), F32)],
        compiler_params=_params("arbitrary"),
        name="even_in",
    )(x, g, w, cos_t, sin_t, q_scale, k_scale, decay, chunk_decay, ret_norm, *[a for a, _ in riders])


def _rel_bucket(dist):
    max_exact = REL_BUCKETS // 2
    d = dist.astype(F32)
    large = max_exact + (jnp.log(jnp.maximum(d, 1.0) / max_exact) / math.log(REL_MAX_DIST / max_exact)
                         * (REL_BUCKETS - max_exact)).astype(jnp.int32)
    large = jnp.minimum(large, REL_BUCKETS - 1)
    return jnp.where(dist < max_exact, dist, large)


DIL_BASE = DIL_MID
DIL_UNROLL = 8


def _dilated_buckets():
    lb = DIL_BLOCK
    slot = jnp.arange(lb)
    per = lb // DIL_BASE
    regrouped = DIL_BASE * (slot % per) + slot // per
    tables = []
    for (window, dilation), token in zip(DIL_BRANCHES, (regrouped, slot, slot)):
        a_idx = token[:, None]
        c_idx = jnp.concatenate([token, lb + token])[None, :]
        dist = lb + a_idx - c_idx
        valid = (dist >= 0) & (dist <= window // dilation)
        tables.append(jnp.where(valid, _rel_bucket(jnp.maximum(dist, 0) * dilation), -1))
    return jnp.stack(tables).astype(jnp.int32)


def _dilated_kernel(rb_ref, bucket_ref, q4_ref, k4_ref, v4_ref, q16_ref, k16_ref, v16_ref, y_ref,
                    bias_ref, acc_ref, m_ref, l_ref):
    lb = DIL_BLOCK
    seq = y_ref.shape[0]
    pair = pl.program_id(1)

    @pl.when(pl.program_id(0) == 0)
    def _():
        for br in range(len(DIL_BRANCHES)):
            bucket = bucket_ref[br]
            for a in range(2):
                h = 2 * pair + a
                t = jnp.full((lb, 2 * lb), MASK_VALUE, F32)
                for j in range(REL_BUCKETS):
                    t = jnp.where(bucket == j, rb_ref[j, h], t)
                bias_ref[br * DIL_HEADS + h] = t

    lane = lax.broadcasted_iota(jnp.int32, (lb, LANES), 1)
    head0 = lane < DIL_DH

    def attend(br, blocks):
        masks = (head0, jnp.logical_not(head0))
        bias = [bias_ref[br * DIL_HEADS + 2 * pair + a] for a in range(2)]
        ks = [jnp.concatenate([kp, kc], axis=0) for _, _, kp, kc, _, _ in blocks]
        vs = [jnp.concatenate([vp, vc], axis=0) for _, _, _, _, vp, vc in blocks]
        pens = [jnp.where(first, MASK_VALUE, 0.0) for first, *_ in blocks]
        scores = [[lax.dot_general(jnp.where(masks[a], blk[1], 0), k, NT_DIMS,
                                   preferred_element_type=F32) + bias[a] for a in range(2)]
                  for blk, k in zip(blocks, ks)]
        scores = [[jnp.concatenate([s[:, 0:lb] + pen, s[:, lb:]], axis=1) for s in pair_s]
                  for pair_s, pen in zip(scores, pens)]
        maxes = [[jnp.max(s, axis=-1, keepdims=True) for s in pair_s] for pair_s in scores]
        exps = [[jnp.exp(s - m) for s, m in zip(pair_s, pair_m)] for pair_s, pair_m in zip(scores, maxes)]
        sums = [[jnp.sum(e, axis=-1, keepdims=True) for e in pair_e] for pair_e in exps]
        outs = [[jnp.dot(e.astype(BF16), v, preferred_element_type=F32) for e in pair_e]
                for pair_e, v in zip(exps, vs)]
        return [tuple(jnp.where(head0, x[0], x[1]) for x in (o, m, l))
                for o, m, l in zip(outs, maxes, sums)]

    def block_rows(n):
        r0 = pl.multiple_of(n * lb, lb)
        rp = pl.multiple_of(jnp.maximum(n - 1, 0) * lb, lb)
        return pl.ds(r0, lb), pl.ds(rp, lb)

    base = DIL_BASE
    per = lb // base
    class_rows = seq // base

    def unit_blocks(it, carry):
        blocks, dests = [], []
        for j in range(DIL_UNROLL):
            n = it * DIL_UNROLL + j
            r0 = pl.multiple_of(n * per, per)
            rp = pl.multiple_of(jnp.maximum(n - 1, 0) * per, per)
            gather = lambda ref, at: jnp.concatenate([ref[rho, pl.ds(at, per), :] for rho in range(base)],
                                                     axis=0)
            blocks.append((n == 0, gather(q4_ref, r0), gather(k4_ref, rp), gather(k4_ref, r0),
                           gather(v4_ref, rp), gather(v4_ref, r0)))
            dests.append(r0)
        for r0, (acc, m, l) in zip(dests, attend(0, blocks)):
            for rho in range(base):
                rows = pl.ds(rho * class_rows + r0, per)
                piece = slice(rho * per, (rho + 1) * per)
                acc_ref[rows, :] = acc[piece]
                m_ref[rows, :] = m[piece]
                l_ref[rows, :] = l[piece]
        return carry

    lax.fori_loop(0, seq // (lb * DIL_UNROLL), unit_blocks, 0)

    def dilated_blocks(br, r, q_ref, k_ref, v_ref):
        nb = seq // (r * lb)
        sub = r // base

        def body(it, carry):
            blocks, dests = [], []
            for j in range(DIL_UNROLL):
                idx = it * DIL_UNROLL + j
                rho, n = idx // nb, idx % nb
                cur, prev = block_rows(n)
                blocks.append((n == 0, q_ref[rho, cur, :], k_ref[rho, prev, :], k_ref[rho, cur, :],
                               v_ref[rho, prev, :], v_ref[rho, cur, :]))
                start = (rho % base) * class_rows + n * (lb * sub) + rho // base
                dests.append(pl.ds(start, lb, stride=sub) if sub > 1
                             else pl.ds(pl.multiple_of(start, lb), lb))
            for rows, (acc, m, l) in zip(dests, attend(br, blocks)):
                m_old = m_ref[rows, :]
                m_new = jnp.maximum(m_old, m)
                c_old = jnp.exp(m_old - m_new)
                c_cur = jnp.exp(m - m_new)
                acc_ref[rows, :] = acc_ref[rows, :] * c_old + acc * c_cur
                l_ref[rows, :] = l_ref[rows, :] * c_old + l * c_cur
                m_ref[rows, :] = m_new
            return carry

        lax.fori_loop(0, r * nb // DIL_UNROLL, body, 0)

    dilated_blocks(1, DIL_BRANCHES[1][1], q4_ref, k4_ref, v4_ref)
    dilated_blocks(2, DIL_BRANCHES[2][1], q16_ref, k16_ref, v16_ref)

    def normalise(n, carry):
        for rho in range(base):
            rows = pl.ds(pl.multiple_of(rho * class_rows + n * lb, lb), lb)
            m_ref[pl.ds(n * (lb * base) + rho, lb, stride=base), :] = acc_ref[rows, :] / l_ref[rows, :]
        return carry

    lax.fori_loop(0, class_rows // lb, normalise, 0)

    def finish(n, carry):
        rows = pl.ds(pl.multiple_of(n * lb, lb), lb)
        y_ref[rows, :] = m_ref[rows, :].astype(y_ref.dtype)
        return carry

    lax.fori_loop(0, seq // lb, finish, 0)


def _dilated(layouts, rel_bias, bn, seq):
    lb = DIL_BLOCK
    assert all(w // r <= lb and seq % (r * lb) == 0 for w, r in DIL_BRANCHES)
    assert DIL_BRANCHES[0][1] == 1 and DIL_WIDE % DIL_MID == 0
    nat = pl.BlockSpec((None, seq, LANES), lambda b, p: (b, 0, p))
    by = lambda r: pl.BlockSpec((None, r, seq // r, LANES), lambda b, p: (b, 0, 0, p))
    nbr = len(DIL_BRANCHES)
    return pl.pallas_call(
        _dilated_kernel,
        grid=(bn, DIL_HEADS // 2),
        in_specs=[pl.BlockSpec(memory_space=pltpu.SMEM),
                  pl.BlockSpec((nbr, lb, 2 * lb), lambda b, p: (0, 0, 0))]
        + [by(DIL_MID)] * 3 + [by(DIL_WIDE)] * 3,
        out_specs=nat,
        out_shape=jax.ShapeDtypeStruct((bn, seq, B_W), BF16),
        scratch_shapes=[pltpu.VMEM((nbr * DIL_HEADS, lb, 2 * lb), F32),
                        pltpu.VMEM((seq, LANES), F32), pltpu.VMEM((seq, LANES), F32),
                        pltpu.VMEM((seq, LANES), F32)],
        compiler_params=_params("arbitrary", "arbitrary"),
        name="dilated",
    )(rel_bias.astype(F32), _dilated_buckets(), *layouts)


FFN_TM = 512
FFN_R = 8
FFN_CHUNK = 256
FFN_HEAD_COLS = 512
FFN_DENSE_CHUNKS = 5


def _ffn_kernel(ntiles, tiles_per_seq, final, nparts, *refs):
    x_ref, y_refs, wo_refs = refs[0], refs[1:1 + nparts], refs[1 + nparts:1 + 2 * nparts]
    (g_ref, wup_ref, cw_ref, cb_ref, wd_ref, fg_ref, zero_ref, where_ref, o_ref, slab_in, slab_out,
     hn_cur, hn_next, res_ref, act_ref, down_ref, halo_ref) = refs[1 + 2 * nparts:]
    i = pl.program_id(0)
    tm = o_ref.shape[0]
    rows = tm // FFN_R
    nslab = D_MODEL // LANES
    ck = FFN_CHUNK
    group = lambda g: slice(g * rows, (g + 1) * rows)
    lanes = lambda s: slice(s * LANES, (s + 1) * LANES)
    anywhere = pl.multiple_of(where_ref[0], BF16_SUBLANES)

    def head_pieces():
        wide = FFN_HEAD_COLS
        per = wide // LANES

        def project(p):
            def run():
                cols = slice(p * wide, (p + 1) * wide)
                h = x_ref[:, cols]
                for y_ref, wo_ref in zip(y_refs, wo_refs):
                    h = h + jnp.dot(y_ref[...], wo_ref[:, cols], preferred_element_type=F32)
                for s in range(per):
                    slab_in[per * p + s] = h[:, s * LANES:(s + 1) * LANES]
                return slab_in[per * p + per - 1, pl.ds(anywhere, SUBLANES), :]
            return run

        def normalise(g):
            def run():
                x = jnp.concatenate([slab_in[s, pl.ds(g, rows, stride=FFN_R), :] for s in range(nslab)],
                                    axis=1)
                res_ref[i % 2, group(g), :] = x
                ms = jnp.mean(x * x, axis=-1, keepdims=True)
                hn_next[group(g), :] = (x * lax.rsqrt(ms + EPS) * g_ref[...]).astype(BF16)
                return hn_next[pl.ds(anywhere, BF16_SUBLANES), 0:LANES].astype(F32)[0:SUBLANES]
            return run

        return [project(p) for p in range(D_MODEL // wide)] + [normalise(g) for g in range(FFN_R)]

    def tail_pieces():
        def unpermute(g):
            def run():
                for s in range(nslab):
                    slab_out[s, pl.ds(g, rows, stride=FFN_R), :] = down_ref[group(g), lanes(s)]
                return slab_out[nslab - 1, pl.ds(anywhere, SUBLANES), :]
            return run

        def store(r0, r1):
            def run():
                y = jnp.concatenate([slab_out[s, r0:r1, :] for s in range(nslab)], axis=1)
                if final:
                    ms = jnp.mean(y * y, axis=-1, keepdims=True)
                    y = y * lax.rsqrt(ms + EPS) * fg_ref[...]
                o_ref[r0:r1, :] = y
                return o_ref[pl.ds(anywhere, SUBLANES), 0:LANES]
            return run

        return ([unpermute(g) for g in range(FFN_R)]
                + [store(g * rows, (g + 1) * rows) for g in range(FFN_R)])

    @pl.when(i == 0)
    def _():
        for run in head_pieces():
            run()
        down_ref[...] = jnp.zeros_like(down_ref)
        halo_ref[...] = jnp.zeros_like(halo_ref)

    @pl.when(jnp.logical_and(i >= 1, i <= ntiles))
    def _():
        hn_cur[...] = hn_next[...]
        fill = iter(tail_pieces() + head_pieces())
        seq_start = (i - 1) % tiles_per_seq == 0
        first_row = lax.broadcasted_iota(jnp.int32, (rows, ck), 0) == 0

        def shift_down(block, fill_row):
            return jnp.where(first_row, fill_row, pltpu.roll(block, 1, 0))

        zero_bits = zero_ref[...]

        def conv(cols, slot, tokens):
            u = jnp.dot(hn_cur[...], wup_ref[:, cols], preferred_element_type=F32)
            ug = [u[group(g), :] for g in range(FFN_R)]
            tails = jnp.where(seq_start, 0.0, halo_ref[slot])
            prev2 = shift_down(ug[FFN_R - 2], tails[SUBLANES - 1:SUBLANES, :])
            prev1 = shift_down(ug[FFN_R - 1], tails[2 * SUBLANES - 1:2 * SUBLANES, :])
            halo_ref[slot] = jnp.concatenate(
                [ug[FFN_R - 2][rows - SUBLANES:, :], ug[FFN_R - 1][rows - SUBLANES:, :]], axis=0)
            back1 = [prev1] + ug[:FFN_R - 1]
            back2 = [prev2, prev1] + ug[:FFN_R - 2]
            cw = cw_ref[:, cols]
            cb = cb_ref[:, cols]
            if tokens:
                cb = cb + jnp.tile(_anchor(tokens, zero_bits)[0:1, :], (1, ck // LANES))
            return [cw[0:1, :] * back2[g] + cw[1:2, :] * back1[g] + cw[2:3, :] * ug[g] + cb
                    for g in range(FFN_R)]

        tokens = []
        nchunk = D_FF // ck
        for c in range(nchunk):
            per_point = 2 if c < FFN_DENSE_CHUNKS else 1
            gate = conv(slice(c * ck, (c + 1) * ck), 2 * c, tokens)
            tokens = _emit(fill, per_point)
            value = conv(slice(D_FF + c * ck, D_FF + (c + 1) * ck), 2 * c + 1, tokens)
            tokens = _emit(fill, per_point)
            for g in range(FFN_R):
                act_ref[group(g), c * ck:(c + 1) * ck] = (_silu(gate[g]) * value[g]).astype(BF16)
        down_ref[...] = res_ref[(i - 1) % 2] + jnp.dot(act_ref[...], wd_ref[...],
                                                       preferred_element_type=F32)
        for run in fill:
            run()

    @pl.when(i == ntiles + 1)
    def _():
        for run in tail_pieces():
            run()


def _ffn(x, mixed, g, w_up, conv_w, conv_b, w_down, final_g, seq, final, riders=()):
    m = x.shape[0]
    tm = FFN_TM
    assert D_FF % FFN_CHUNK == 0 and tm % (FFN_R * 2 * SUBLANES) == 0 and seq % tm == 0
    const = lambda shape, at=(0, 0): pl.BlockSpec(shape, lambda i: at, pipeline_mode=pl.Buffered(1))
    ntiles = m // tm
    tile = lambda width, at: pl.BlockSpec((tm, width), lambda i: (at(i), 0))
    head_tile = lambda i: jnp.minimum(i, ntiles - 1)
    in_specs = (
        [tile(D_MODEL, head_tile)]
        + [tile(y.shape[1], head_tile) for y, _, _ in mixed]
        + [const((y.shape[1], D_MODEL), (row0 // y.shape[1], 0)) for y, _, row0 in mixed]
        + [const((1, D_MODEL)),
           const((D_MODEL, 2 * D_FF)), const((CONV_WIDTH, 2 * D_FF)), const((1, 2 * D_FF)),
           const((D_FF, D_MODEL)),
           const((1, D_MODEL)),
           const((SUBLANES, LANES)),
           pl.BlockSpec(memory_space=pltpu.SMEM)])
    rider_in, rider_out, rider_shapes = _rider_specs(riders, ntiles)
    return pl.pallas_call(
        _with_riders(functools.partial(_ffn_kernel, ntiles, seq // tm, final, len(mixed)),
                     len(in_specs), 1, len(riders)),
        grid=(ntiles + 2,),
        in_specs=in_specs + rider_in,
        out_specs=[tile(D_MODEL, lambda i: jnp.maximum(i - 2, 0))] + rider_out,
        out_shape=[jax.ShapeDtypeStruct((m, D_MODEL), F32)] + rider_shapes,
        scratch_shapes=[
            pltpu.VMEM((D_MODEL // LANES, tm, LANES), F32),
            pltpu.VMEM((D_MODEL // LANES, tm, LANES), F32),
            pltpu.VMEM((tm, D_MODEL), BF16),
            pltpu.VMEM((tm, D_MODEL), BF16),
            pltpu.VMEM((2, tm, D_MODEL), F32),
            pltpu.VMEM((tm, D_FF), BF16),
            pltpu.VMEM((tm, D_MODEL), F32),
            pltpu.VMEM((2 * D_FF // FFN_CHUNK, 2 * SUBLANES, FFN_CHUNK), F32),
        ],
        compiler_params=_params("arbitrary"),
        name="conv_ffn",
    )(x, *[y for y, _, _ in mixed], *[wo for _, wo, _ in mixed], g, w_up, conv_w.astype(F32),
      conv_b.astype(F32).reshape(1, -1), w_down, final_g, jnp.zeros((SUBLANES, LANES), jnp.int32),
      jnp.zeros((1,), jnp.int32), *[a for a, _ in riders])


HGRN_GROUP = 128
HGRN_TM = 256
HGRN_PROJ_COLS = 256
HGRN_EMIT = 2


def _hgrn_groups(proj_ref, row_slices, lower_all, gain, causal, tri, st, fill):
    c = HGRN_CHUNK
    heads = range(HGRN_HEADS)
    groups = range(len(row_slices))
    hk = [slice(h * HGRN_DK, (h + 1) * HGRN_DK) for h in heads]
    hv = [slice(h * HGRN_DV, (h + 1) * HGRN_DV) for h in heads]
    chunks = [slice(n * c, (n + 1) * c) for n in range(HGRN_GROUP // c)]

    q = [_silu(proj_ref[rows, 0:C_K]) for rows in row_slices]
    fg = [lower_all + (1.0 - lower_all) * jax.nn.sigmoid(proj_ref[rows, C_K:2 * C_K])
          for rows in row_slices]
    k = [1.0 - f for f in fg]
    split = []
    for f in fg:
        log_f = jnp.log(f)
        hi = log_f.astype(BF16)
        rem = log_f - hi.astype(F32)
        mid = rem.astype(BF16)
        split.append(jnp.concatenate([hi, mid, (rem - mid.astype(F32)).astype(BF16)], axis=1))
    _emit(fill, HGRN_EMIT)
    parts = [jnp.dot(tri, s, preferred_element_type=F32) for s in split]
    b = [p[:, 0:C_K] + p[:, C_K:2 * C_K] + p[:, 2 * C_K:3 * C_K] for p in parts]
    v = [proj_ref[rows, 2 * C_K:2 * C_K + C_V].astype(BF16) for rows in row_slices]
    q_t = [(q[g] * jnp.exp(b[g])).astype(BF16) for g in groups]
    k_t = [(k[g] * jnp.exp(-b[g])).astype(BF16) for g in groups]
    _emit(fill, HGRN_EMIT)
    scores = [[lax.dot_general(q_t[g][:, hk[h]], k_t[g][:, hk[h]], NT_DIMS, preferred_element_type=F32)
               for h in heads] for g in groups]
    scores = [[jnp.where(causal, s, 0.0).astype(BF16) for s in per_group] for per_group in scores]
    intra = [[jnp.dot(scores[g][h], v[g][:, hv[h]], preferred_element_type=F32) for h in heads]
             for g in groups]
    _emit(fill, HGRN_EMIT)
    b_last = [[b[g][cs.stop - 1:cs.stop, :] for cs in chunks] for g in groups]
    k_end = [[(k[g][cs, :] * jnp.exp(bl - b[g][cs, :])).astype(BF16) for cs, bl in zip(chunks, b_last[g])]
             for g in groups]
    decay = [[jnp.exp(bl) for bl in b_last[g]] for g in groups]
    gate = [gain * _silu(proj_ref[rows, 2 * C_K + C_V:2 * C_K + 2 * C_V]) for rows in row_slices]
    _emit(fill, HGRN_EMIT)
    st = list(st)
    outs = []
    for g in groups:
        inter = [[] for _ in heads]
        for n, cs in enumerate(chunks):
            for h in heads:
                inter[h].append(lax.dot_general(q_t[g][cs, hk[h]], st[h].astype(BF16), NT_DIMS,
                                                preferred_element_type=F32))
                st[h] = st[h] * decay[g][n][:, hk[h]] + lax.dot_general(
                    v[g][cs, hv[h]], k_end[g][n][:, hk[h]], TN_DIMS, preferred_element_type=F32)
        _emit(fill, HGRN_EMIT)
        per_head = []
        for h in heads:
            y = intra[g][h] + jnp.concatenate(inter[h], axis=0)
            yn = y * lax.rsqrt(jnp.mean(y * y, axis=-1, keepdims=True) + EPS)
            per_head.append((yn * gate[g][:, hv[h]]).astype(BF16))
        outs.append(per_head)
        _emit(fill, HGRN_EMIT)
    return outs, st


def _hgrn_layer_kernel(layer, tiles_per_seq, h0_ref, hnext_ref, g_ref, w_ref, lbp_ref, on_ref, o_ref,
                       proj_a, proj_b, st_ref):
    t = pl.program_id(0)
    tm = o_ref.shape[0]
    grp = HGRN_GROUP

    def projection(h_ref, dst):
        x = h_ref[...]
        ms = jnp.mean(x * x, axis=-1, keepdims=True)
        hn = (x * lax.rsqrt(ms + EPS) * g_ref[...]).astype(BF16)

        def piece(c):
            cols = slice(c * HGRN_PROJ_COLS, (c + 1) * HGRN_PROJ_COLS)

            def run():
                dst[:, cols] = jnp.dot(hn, w_ref[:, cols], preferred_element_type=F32)
            return run

        return [piece(c) for c in range(ODD_IN // HGRN_PROJ_COLS)]

    @pl.when(t == 0)
    def _():
        for run in projection(h0_ref, proj_a):
            run()

    @pl.when(t % tiles_per_seq == 0)
    def _():
        st_ref[...] = jnp.zeros_like(st_ref)

    def step(src, dst):
        fill = iter(projection(hnext_ref, dst))
        lbp = lbp_ref[...]
        ex = jnp.exp(lbp - jnp.max(lbp, axis=0, keepdims=True))
        sm = ex / jnp.sum(ex, axis=0, keepdims=True)
        lower_all = jnp.sum(sm[0:layer + 1, :], axis=0, keepdims=True) - sm[0:1, :]
        row = lax.broadcasted_iota(jnp.int32, (grp, grp), 0)
        colm = lax.broadcasted_iota(jnp.int32, (grp, grp), 1)
        causal = jnp.logical_and(row // HGRN_CHUNK == colm // HGRN_CHUNK, row >= colm)
        tri = jnp.where(causal, 1.0, 0.0).astype(BF16)
        gain = on_ref[...]
        st = [st_ref[h] for h in range(HGRN_HEADS)]
        row_slices = [slice(r0, r0 + grp) for r0 in range(0, tm, grp)]
        outs, st = _hgrn_groups(src, row_slices, lower_all, gain, causal, tri, st, fill)
        for rows, per_head in zip(row_slices, outs):
            for h, y in enumerate(per_head):
                o_ref[rows, h * HGRN_DV:(h + 1) * HGRN_DV] = y
        for run in fill:
            run()
        for h in range(HGRN_HEADS):
            st_ref[h] = st[h]

    @pl.when(t % 2 == 0)
    def _():
        step(proj_a, proj_b)

    @pl.when(t % 2 == 1)
    def _():
        step(proj_b, proj_a)


def _hgrn_layer(h, g, w, hgrn_lb, out_norm, layer, seq):
    tokens, d = h.shape
    tm = HGRN_TM
    ntiles = tokens // tm
    const = lambda shape: pl.BlockSpec(shape, lambda t: (0, 0), pipeline_mode=pl.Buffered(1))
    return pl.pallas_call(
        functools.partial(_hgrn_layer_kernel, layer, seq // tm),
        grid=(ntiles,),
        in_specs=[
            const((tm, d)),
            pl.BlockSpec((tm, d), lambda t: (jnp.minimum(t + 1, ntiles - 1), 0)),
            const((1, d)), const(w.shape), const(hgrn_lb.shape), const((1, C_V)),
        ],
        out_specs=pl.BlockSpec((tm, C_V), lambda t: (t, 0)),
        out_shape=jax.ShapeDtypeStruct((tokens, C_V), BF16),
        scratch_shapes=[pltpu.VMEM((tm, ODD_IN), F32), pltpu.VMEM((tm, ODD_IN), F32),
                        pltpu.VMEM((HGRN_HEADS, HGRN_DV, HGRN_DK), F32)],
        compiler_params=_params("arbitrary"),
        name="hgrn_layer",
    )(h, h, g, w, hgrn_lb, out_norm)


def kernel(x, even_w_in, even_w_out, ret_norm, rel_bias, odd_w_in, odd_w_out, hgrn_lb, hgrn_norm,
           mix_norm, ffn_norm, ffn_w_up, ffn_conv_w, ffn_conv_b, ffn_w_down, final_norm):
    bn, seq, d = x.shape
    tokens = bn * seq
    row = lambda a: a.reshape(1, -1).astype(F32)
    h = x.reshape(tokens, d)
    final_g = row(final_norm)

    outs = _even_in(h, row(mix_norm[0]), even_w_in[0].astype(BF16), row(ret_norm[0]), seq,
                    riders=[(even_w_out, 0), (ffn_w_up, 0), (ffn_w_down, 0)])
    ya, q4, q16, k4, k16, v4, v16, w_out0, w_up0, w_down0 = outs
    yb = _dilated((q4, k4, v4, q16, k16, v16), rel_bias, bn, seq).reshape(tokens, B_W)
    h, w_in1, w_out1, w_up1, w_down1 = _ffn(
        h, [(ya, w_out0, 0), (yb, w_out0, A_V)], row(ffn_norm[0]), w_up0, ffn_conv_w[0],
        ffn_conv_b[0], w_down0, final_g, seq, final=False,
        riders=[(odd_w_in, 0), (odd_w_out, 0), (ffn_w_up, 1), (ffn_w_down, 1)])

    y = _hgrn_layer(h, row(mix_norm[1]), w_in1, hgrn_lb.astype(F32), row(hgrn_norm[0]), 1, seq)
    h, = _ffn(h, [(y, w_out1, 0)], row(ffn_norm[1]), w_up1, ffn_conv_w[1], ffn_conv_b[1], w_down1,
              final_g, seq, final=True)
    return h.reshape(bn, seq, d)
```

```python
import functools
import math

import jax
import jax.numpy as jnp
from jax import lax
from jax.experimental import pallas as pl
from jax.experimental.pallas import tpu as pltpu

F32 = jnp.float32
BF16 = jnp.bfloat16

D_MODEL = 1024
RET_HEADS = 8
RET_DK = 64
RET_DV = 128
RET_CHUNK = 128
ROPE_BASE = 10000.0
DIL_HEADS = 8
DIL_DH = 64
DIL_BRANCHES = ((128, 1), (512, 4), (2048, 16))
DIL_BLOCK = 128
DIL_MID, DIL_WIDE = DIL_BRANCHES[1][1], DIL_BRANCHES[2][1]
HGRN_HEADS = 8
HGRN_DK = 128
HGRN_DV = 128
HGRN_CHUNK = 32
REL_BUCKETS = 32
REL_MAX_DIST = 2048
D_FF = 2816
CONV_WIDTH = 3
EPS = 1e-6

A_QK = RET_HEADS * RET_DK
A_V = RET_HEADS * RET_DV
B_W = DIL_HEADS * DIL_DH
EVEN_IN = 2 * A_QK + 2 * A_V + 3 * B_W
C_K = HGRN_HEADS * HGRN_DK
C_V = HGRN_HEADS * HGRN_DV
ODD_IN = 2 * C_K + 2 * C_V

LANES = 128
SUBLANES = 8
VMEM_LIMIT = 56 * 1024 * 1024
MASK_VALUE = -1e30

NT_DIMS = (((1,), (1,)), ((), ()))
TN_DIMS = (((0,), (0,)), ((), ()))


def _params(*semantics):
    return pltpu.CompilerParams(dimension_semantics=semantics, vmem_limit_bytes=VMEM_LIMIT)


def _silu(x):
    return x * jax.nn.sigmoid(x)


def _emit(fill, count):
    tokens = []
    for _ in range(count):
        thunk = next(fill, None)
        if thunk is not None:
            token = thunk()
            if token is not None:
                tokens.append(token)
    return tokens


def _anchor(tokens, zero_bits):
    acc = zero_bits
    for t in tokens:
        acc = acc & pltpu.bitcast(t, jnp.int32)
    return pltpu.bitcast(acc, F32)


BF16_SUBLANES = 2 * SUBLANES


def _rider_specs(weights, steps):
    in_specs, out_specs, out_shapes = [], [], []
    for arr, layer in weights:
        _, rows, cols = arr.shape
        rep = 1
        while (rows * rep) % (steps * BF16_SUBLANES):
            rep *= 2
        slab = rows * rep // steps
        slab_index = lambda i, rep=rep: jnp.minimum(i, steps - 1) // rep
        in_specs.append(pl.BlockSpec((None, slab, cols),
                                     lambda i, layer=layer, at=slab_index: (layer, at(i), 0)))
        out_specs.append(pl.BlockSpec((slab, cols), lambda i, at=slab_index: (at(i), 0)))
        out_shapes.append(jax.ShapeDtypeStruct((rows, cols), BF16))
    return in_specs, out_specs, out_shapes


def _with_riders(body, n_in, n_out, n_riders):
    def kernel_fn(*refs):
        ins, refs = refs[:n_in], refs[n_in:]
        rider_in, refs = refs[:n_riders], refs[n_riders:]
        outs, refs = refs[:n_out], refs[n_out:]
        rider_out, scratch = refs[:n_riders], refs[n_riders:]
        for src, dst in zip(rider_in, rider_out):
            dst[...] = src[...].astype(dst.dtype)
        body(*ins, *outs, *scratch)
    return kernel_fn


def _retention_tables(seq):
    h = jnp.arange(RET_HEADS, dtype=F32)
    log_g = jnp.log1p(-jnp.exp2(-5.0 - h))
    pos = jnp.arange(RET_CHUNK, dtype=F32)
    diff = pos[:, None] - pos[None, :]
    decay = jnp.where(diff >= 0, jnp.exp(jnp.maximum(diff, 0.0)[None] * log_g[:, None, None]), 0.0)
    q_scale = jnp.repeat(jnp.exp((pos + 1.0)[:, None] * log_g[None, :]), RET_DK, axis=1)
    k_scale = jnp.repeat(jnp.exp((RET_CHUNK - 1.0 - pos)[:, None] * log_g[None, :]), RET_DK, axis=1)
    chunk_decay = jnp.broadcast_to(jnp.exp(RET_CHUNK * log_g)[:, None, None], (RET_HEADS, 1, RET_DV))
    inv = ROPE_BASE ** (-jnp.arange(0, RET_DK, 2, dtype=F32) / RET_DK)
    ang = jnp.arange(seq, dtype=F32)[:, None] * inv[None, :]
    cos, sin = jnp.cos(ang), jnp.sin(ang)
    reps = LANES // RET_DK
    cos_t = jnp.tile(jnp.concatenate([cos, cos], axis=1), (1, reps))
    sin_t = jnp.tile(jnp.concatenate([-sin, sin], axis=1), (1, reps))
    return decay, q_scale, k_scale, chunk_decay, cos_t, sin_t


EVEN_TM = 512


def _even_in_kernel(tiles_per_seq, x_ref, g_ref, w_ref, cos_ref, sin_ref, qs_ref, ks_ref, dec_ref,
                    cd_ref, rn_ref, ya_ref, q4_ref, q16_ref, k4_ref, k16_ref, v4_ref, v16_ref,
                    slab_ref, slab4_ref, st_ref):
    tm = x_ref.shape[0]
    chunk = RET_CHUNK

    @pl.when(pl.program_id(0) % tiles_per_seq == 0)
    def _():
        st_ref[...] = jnp.zeros_like(st_ref)

    x = x_ref[...]
    ms = jnp.mean(x * x, axis=-1, keepdims=True)
    hn = (x * lax.rsqrt(ms + EPS) * g_ref[...]).astype(BF16)

    def proj(c):
        return jnp.dot(hn, w_ref[:, c * B_W:(c + 1) * B_W], preferred_element_type=F32)

    cos = cos_ref[...]
    sin = sin_ref[...]
    lane = lax.broadcasted_iota(jnp.int32, (tm, LANES), 1)
    half = RET_DK // 2
    first_half = (lane % RET_DK) < half

    def rot(t):
        partner = jnp.where(first_half, pltpu.roll(t, LANES - half, 1), pltpu.roll(t, half, 1))
        return t * cos + partner * sin

    groups = [slice(s * LANES, (s + 1) * LANES) for s in range(B_W // LANES)]

    r = proj(0)
    q_rot = [rot(r[:, sl]) for sl in groups]
    q = [t.astype(BF16) for t in q_rot]
    q_in = [(t * qs_ref[:, sl]).astype(BF16) for t, sl in zip(q_rot, groups)]
    r = proj(1)
    k_rot = [rot(r[:, sl]) * (RET_DK ** -0.5) for sl in groups]
    k = [t.astype(BF16) for t in k_rot]
    k_out = [(t * ks_ref[:, sl]).astype(BF16) for t, sl in zip(k_rot, groups)]
    v = jnp.concatenate([proj(2), proj(3)], axis=1).astype(BF16)
    gate = rn_ref[...] * _silu(jnp.concatenate([proj(4), proj(5)], axis=1))

    def dilated_pieces(c, slot, r4_ref, r16_ref, scale):
        def project():
            r = proj(c)
            if scale is not None:
                r = r * scale
            for s, sl in enumerate(groups):
                slab_ref[slot, s] = r[:, sl]

        mid, wide = DIL_MID, DIL_WIDE
        sub = wide // mid
        per = tm // mid

        def by4():
            for s, sl in enumerate(groups):
                for rho in range(mid):
                    t = slab_ref[slot, s, pl.ds(rho, per, stride=mid), :]
                    r4_ref[rho, :, sl] = t.astype(BF16)
                    slab4_ref[slot, s, rho * per:(rho + 1) * per, :] = t

        def by16():
            for s, sl in enumerate(groups):
                for rho in range(mid):
                    for c in range(sub):
                        r16_ref[rho + mid * c, :, sl] = slab4_ref[
                            slot, s, pl.ds(rho * per + c, tm // wide, stride=sub), :].astype(BF16)

        return [project, by4, by16]

    fill = iter(dilated_pieces(6, 0, q4_ref, q16_ref, DIL_DH ** -0.5)
                + dilated_pieces(7, 1, k4_ref, k16_ref, None)
                + dilated_pieces(8, 2, v4_ref, v16_ref, None))

    head0 = lax.broadcasted_iota(jnp.int32, (chunk, LANES), 1) < RET_DK
    heads = range(RET_HEADS)
    hs = [slice(h * RET_DV, (h + 1) * RET_DV) for h in heads]
    st = [st_ref[h] for h in heads]
    for ci in range(tm // chunk):
        rows = slice(ci * chunk, (ci + 1) * chunk)
        mask = [head0 if h % 2 == 0 else jnp.logical_not(head0) for h in heads]
        scores = [lax.dot_general(jnp.where(mask[h], q[h // 2][rows], 0), k[h // 2][rows], NT_DIMS,
                                  preferred_element_type=F32) * dec_ref[h] for h in heads]
        _emit(fill, 1)
        intra = [jnp.dot(scores[h].astype(BF16), v[rows, hs[h]], preferred_element_type=F32)
                 for h in heads]
        inter = [jnp.dot(jnp.where(mask[h], q_in[h // 2][rows], 0), st[h].astype(BF16),
                         preferred_element_type=F32) for h in heads]
        st = [st[h] * cd_ref[h] + lax.dot_general(k_out[h // 2][rows], v[rows, hs[h]], TN_DIMS,
                                                  preferred_element_type=F32) for h in heads]
        _emit(fill, 1)
        for h in heads:
            y = intra[h] + inter[h]
            yc = y - jnp.mean(y, axis=-1, keepdims=True)
            yn = yc * lax.rsqrt(jnp.mean(yc * yc, axis=-1, keepdims=True) + EPS)
            ya_ref[rows, hs[h]] = (yn * gate[rows, hs[h]]).astype(BF16)
        _emit(fill, 1)
    for run in fill:
        run()
    for h in heads:
        st_ref[h] = st[h]


def _even_in(x, g, w, ret_norm, seq, riders):
    tokens, d = x.shape
    bn = tokens // seq
    tm = EVEN_TM
    tps = seq // tm
    decay, q_scale, k_scale, chunk_decay, cos_t, sin_t = _retention_tables(seq)
    q_scale = jnp.tile(q_scale, (tm // RET_CHUNK, 1))
    k_scale = jnp.tile(k_scale, (tm // RET_CHUNK, 1))
    const = lambda shape: pl.BlockSpec(shape, lambda i: (0,) * len(shape), pipeline_mode=pl.Buffered(1))
    flat = lambda w_: pl.BlockSpec((tm, w_), lambda i: (i, 0))
    by = lambda r: pl.BlockSpec((None, r, tm // r, B_W), lambda i: (i // tps, 0, i % tps, 0))
    sds = jax.ShapeDtypeStruct
    lay = [sds((bn, r, seq // r, B_W), BF16) for r in (DIL_MID, DIL_WIDE)]
    steps = tokens // tm
    in_specs = [
        flat(d), const((1, d)), const(w.shape),
        pl.BlockSpec((tm, LANES), lambda i: (i % tps, 0)),
        pl.BlockSpec((tm, LANES), lambda i: (i % tps, 0)),
        const((tm, A_QK)), const((tm, A_QK)),
        const(decay.shape), const(chunk_decay.shape), const((1, A_V)),
    ]
    out_specs = [flat(A_V)] + [by(DIL_MID), by(DIL_WIDE)] * 3
    rider_in, rider_out, rider_shapes = _rider_specs(riders, steps)
    return pl.pallas_call(
        _with_riders(functools.partial(_even_in_kernel, tps), len(in_specs), len(out_specs), len(riders)),
        grid=(steps,),
        in_specs=in_specs + rider_in,
        out_specs=out_specs + rider_out,
        out_shape=[sds((tokens, A_V), BF16)] + lay * 3 + rider_shapes,
        scratch_shapes=[pltpu.VMEM((3, B_W // LANES, tm, LANES), F32),
                        pltpu.VMEM((3, B_W // LANES, tm, LANES), F32),
                        pltpu.VMEM((RET_HEADS, 2 * RET_DK, RET_DV), F32)],
        compiler_params=_params("arbitrary"),
        name="even_in",
    )(x, g, w, cos_t, sin_t, q_scale, k_scale, decay, chunk_decay, ret_norm, *[a for a, _ in riders])


def _rel_bucket(dist):
    max_exact = REL_BUCKETS // 2
    d = dist.astype(F32)
    large = max_exact + (jnp.log(jnp.maximum(d, 1.0) / max_exact) / math.log(REL_MAX_DIST / max_exact)
                         * (REL_BUCKETS - max_exact)).astype(jnp.int32)
    large = jnp.minimum(large, REL_BUCKETS - 1)
    return jnp.where(dist < max_exact, dist, large)


DIL_BASE = DIL_MID
DIL_UNROLL = 8


def _dilated_buckets():
    lb = DIL_BLOCK
    slot = jnp.arange(lb)
    per = lb // DIL_BASE
    regrouped = DIL_BASE * (slot % per) + slot // per
    tables = []
    for (window, dilation), token in zip(DIL_BRANCHES, (regrouped, slot, slot)):
        a_idx = token[:, None]
        c_idx = jnp.concatenate([token, lb + token])[None, :]
        dist = lb + a_idx - c_idx
        valid = (dist >= 0) & (dist <= window // dilation)
        tables.append(jnp.where(valid, _rel_bucket(jnp.maximum(dist, 0) * dilation), -1))
    return jnp.stack(tables).astype(jnp.int32)


def _dilated_kernel(rb_ref, bucket_ref, q4_ref, k4_ref, v4_ref, q16_ref, k16_ref, v16_ref, y_ref,
                    bias_ref, acc_ref, m_ref, l_ref):
    lb = DIL_BLOCK
    seq = y_ref.shape[0]
    pair = pl.program_id(1)

    @pl.when(pl.program_id(0) == 0)
    def _():
        for br in range(len(DIL_BRANCHES)):
            bucket = bucket_ref[br]
            for a in range(2):
                h = 2 * pair + a
                t = jnp.full((lb, 2 * lb), MASK_VALUE, F32)
                for j in range(REL_BUCKETS):
                    t = jnp.where(bucket == j, rb_ref[j, h], t)
                bias_ref[br * DIL_HEADS + h] = t

    lane = lax.broadcasted_iota(jnp.int32, (lb, LANES), 1)
    head0 = lane < DIL_DH

    def attend(br, blocks, with_prev):
        masks = (head0, jnp.logical_not(head0))
        bias = [bias_ref[br * DIL_HEADS + 2 * pair + a] for a in range(2)]
        if with_prev:
            ks = [jnp.concatenate([kp, kc], axis=0) for _, kp, kc, _, _ in blocks]
            vs = [jnp.concatenate([vp, vc], axis=0) for _, _, _, vp, vc in blocks]
        else:
            bias = [t[:, lb:2 * lb] for t in bias]
            ks = [kc for _, _, kc, _, _ in blocks]
            vs = [vc for _, _, _, _, vc in blocks]
        scores = [[lax.dot_general(jnp.where(masks[a], blk[0], 0), k, NT_DIMS,
                                   preferred_element_type=F32) + bias[a] for a in range(2)]
                  for blk, k in zip(blocks, ks)]
        maxes = [[jnp.max(s, axis=-1, keepdims=True) for s in pair_s] for pair_s in scores]
        exps = [[jnp.exp(s - m) for s, m in zip(pair_s, pair_m)] for pair_s, pair_m in zip(scores, maxes)]
        sums = [[jnp.sum(e, axis=-1, keepdims=True) for e in pair_e] for pair_e in exps]
        outs = [[jnp.dot(e.astype(BF16), v, preferred_element_type=F32) for e in pair_e]
                for pair_e, v in zip(exps, vs)]
        return [tuple(jnp.where(head0, x[0], x[1]) for x in (o, m, l))
                for o, m, l in zip(outs, maxes, sums)]

    def aligned(x, m):
        return x if isinstance(x, int) else pl.multiple_of(x, m)

    def sweep(count, run):
        full = count // DIL_UNROLL
        if full == 1:
            run(list(range(DIL_UNROLL)))
        elif full > 1:
            def body(it, carry):
                run([it * DIL_UNROLL + j for j in range(DIL_UNROLL)])
                return carry
            lax.fori_loop(0, full, body, 0)
        rest = list(range(full * DIL_UNROLL, count))
        if rest:
            run(rest)

    base = DIL_BASE
    per = lb // base
    class_rows = seq // base

    def unit_blocks(ns, with_prev):
        blocks, dests = [], []
        for n in ns:
            r0 = aligned(n * per, per)
            gather = lambda ref, at: jnp.concatenate([ref[rho, pl.ds(at, per), :] for rho in range(base)],
                                                     axis=0)
            if with_prev:
                rp = aligned((n - 1) * per, per)
                blocks.append((gather(q4_ref, r0), gather(k4_ref, rp), gather(k4_ref, r0),
                               gather(v4_ref, rp), gather(v4_ref, r0)))
            else:
                blocks.append((gather(q4_ref, r0), None, gather(k4_ref, r0), None, gather(v4_ref, r0)))
            dests.append(r0)
        for r0, (acc, m, l) in zip(dests, attend(0, blocks, with_prev)):
            for rho in range(base):
                rows = pl.ds(rho * class_rows + r0, per)
                piece = slice(rho * per, (rho + 1) * per)
                acc_ref[rows, :] = acc[piece]
                m_ref[rows, :] = m[piece]
                l_ref[rows, :] = l[piece]

    nb1 = seq // lb
    sweep(1, lambda js: unit_blocks(js, False))
    sweep(nb1 - 1, lambda js: unit_blocks([1 + j for j in js], True))

    def dilated_blocks(br, r, q_ref, k_ref, v_ref, ids, with_prev):
        sub = r // base
        blocks, dests = [], []
        for rho, n in ids:
            cur = pl.ds(aligned(n * lb, lb), lb)
            if with_prev:
                prev = pl.ds(aligned((n - 1) * lb, lb), lb)
                blocks.append((q_ref[rho, cur, :], k_ref[rho, prev, :], k_ref[rho, cur, :],
                               v_ref[rho, prev, :], v_ref[rho, cur, :]))
            else:
                blocks.append((q_ref[rho, cur, :], None, k_ref[rho, cur, :], None, v_ref[rho, cur, :]))
            start = (rho % base) * class_rows + n * (lb * sub) + rho // base
            dests.append(pl.ds(start, lb, stride=sub) if sub > 1 else pl.ds(aligned(start, lb), lb))
        for rows, (acc, m, l) in zip(dests, attend(br, blocks, with_prev)):
            m_old = m_ref[rows, :]
            m_new = jnp.maximum(m_old, m)
            c_old = jnp.exp(m_old - m_new)
            c_cur = jnp.exp(m - m_new)
            acc_ref[rows, :] = acc_ref[rows, :] * c_old + acc * c_cur
            l_ref[rows, :] = l_ref[rows, :] * c_old + l * c_cur
            m_ref[rows, :] = m_new

    for br, refs in ((1, (q4_ref, k4_ref, v4_ref)), (2, (q16_ref, k16_ref, v16_ref))):
        r = DIL_BRANCHES[br][1]
        nb = seq // (r * lb)
        sweep(r, lambda js, br=br, r=r, refs=refs: dilated_blocks(
            br, r, *refs, [(j, 0) for j in js], False))
        sweep(r * (nb - 1), lambda js, br=br, r=r, refs=refs: dilated_blocks(
            br, r, *refs, [(j % r, 1 + j // r) for j in js], True))

    def normalise(n, carry):
        for rho in range(base):
            rows = pl.ds(pl.multiple_of(rho * class_rows + n * lb, lb), lb)
            m_ref[pl.ds(n * (lb * base) + rho, lb, stride=base), :] = acc_ref[rows, :] / l_ref[rows, :]
        return carry

    lax.fori_loop(0, class_rows // lb, normalise, 0)

    def finish(n, carry):
        rows = pl.ds(pl.multiple_of(n * lb, lb), lb)
        y_ref[rows, :] = m_ref[rows, :].astype(y_ref.dtype)
        return carry

    lax.fori_loop(0, seq // lb, finish, 0)


def _dilated(layouts, rel_bias, bn, seq):
    lb = DIL_BLOCK
    assert all(w // r <= lb and seq % (r * lb) == 0 for w, r in DIL_BRANCHES)
    assert DIL_BRANCHES[0][1] == 1 and DIL_WIDE % DIL_MID == 0
    nat = pl.BlockSpec((None, seq, LANES), lambda b, p: (b, 0, p))
    by = lambda r: pl.BlockSpec((None, r, seq // r, LANES), lambda b, p: (b, 0, 0, p))
    nbr = len(DIL_BRANCHES)
    return pl.pallas_call(
        _dilated_kernel,
        grid=(bn, DIL_HEADS // 2),
        in_specs=[pl.BlockSpec(memory_space=pltpu.SMEM),
                  pl.BlockSpec((nbr, lb, 2 * lb), lambda b, p: (0, 0, 0))]
        + [by(DIL_MID)] * 3 + [by(DIL_WIDE)] * 3,
        out_specs=nat,
        out_shape=jax.ShapeDtypeStruct((bn, seq, B_W), BF16),
        scratch_shapes=[pltpu.VMEM((nbr * DIL_HEADS, lb, 2 * lb), F32),
                        pltpu.VMEM((seq, LANES), F32), pltpu.VMEM((seq, LANES), F32),
                        pltpu.VMEM((seq, LANES), F32)],
        compiler_params=_params("arbitrary", "arbitrary"),
        name="dilated",
    )(rel_bias.astype(F32), _dilated_buckets(), *layouts)


FFN_TM = 512
FFN_R = 8
FFN_CHUNK = 256
FFN_HEAD_COLS = 512
FFN_DENSE_CHUNKS = 5


def _ffn_kernel(ntiles, tiles_per_seq, final, nparts, *refs):
    x_ref, y_refs, wo_refs = refs[0], refs[1:1 + nparts], refs[1 + nparts:1 + 2 * nparts]
    (g_ref, wup_ref, cw_ref, cb_ref, wd_ref, fg_ref, zero_ref, where_ref, o_ref, slab_in, slab_out,
     hn_cur, hn_next, res_ref, act_ref, down_ref, halo_ref) = refs[1 + 2 * nparts:]
    i = pl.program_id(0)
    tm = o_ref.shape[0]
    rows = tm // FFN_R
    nslab = D_MODEL // LANES
    ck = FFN_CHUNK
    group = lambda g: slice(g * rows, (g + 1) * rows)
    lanes = lambda s: slice(s * LANES, (s + 1) * LANES)
    anywhere = pl.multiple_of(where_ref[0], BF16_SUBLANES)

    def head_pieces():
        wide = FFN_HEAD_COLS
        per = wide // LANES

        def project(p):
            def run():
                cols = slice(p * wide, (p + 1) * wide)
                h = x_ref[:, cols]
                for y_ref, wo_ref in zip(y_refs, wo_refs):
                    h = h + jnp.dot(y_ref[...], wo_ref[:, cols], preferred_element_type=F32)
                for s in range(per):
                    slab_in[per * p + s] = h[:, s * LANES:(s + 1) * LANES]
                return slab_in[per * p + per - 1, pl.ds(anywhere, SUBLANES), :]
            return run

        def normalise(g):
            def run():
                x = jnp.concatenate([slab_in[s, pl.ds(g, rows, stride=FFN_R), :] for s in range(nslab)],
                                    axis=1)
                res_ref[i % 2, group(g), :] = x
                ms = jnp.mean(x * x, axis=-1, keepdims=True)
                hn_next[group(g), :] = (x * lax.rsqrt(ms + EPS) * g_ref[...]).astype(BF16)
                return hn_next[pl.ds(anywhere, BF16_SUBLANES), 0:LANES].astype(F32)[0:SUBLANES]
            return run

        return [project(p) for p in range(D_MODEL // wide)] + [normalise(g) for g in range(FFN_R)]

    def tail_pieces():
        def unpermute(g):
            def run():
                for s in range(nslab):
                    slab_out[s, pl.ds(g, rows, stride=FFN_R), :] = down_ref[group(g), lanes(s)]
                return slab_out[nslab - 1, pl.ds(anywhere, SUBLANES), :]
            return run

        def store(r0, r1):
            def run():
                y = jnp.concatenate([slab_out[s, r0:r1, :] for s in range(nslab)], axis=1)
                if final:
                    ms = jnp.mean(y * y, axis=-1, keepdims=True)
                    y = y * lax.rsqrt(ms + EPS) * fg_ref[...]
                o_ref[r0:r1, :] = y
                return o_ref[pl.ds(anywhere, SUBLANES), 0:LANES]
            return run

        return ([unpermute(g) for g in range(FFN_R)]
                + [store(g * rows, (g + 1) * rows) for g in range(FFN_R)])

    @pl.when(i == 0)
    def _():
        for run in head_pieces():
            run()
        down_ref[...] = jnp.zeros_like(down_ref)
        halo_ref[...] = jnp.zeros_like(halo_ref)

    @pl.when(jnp.logical_and(i >= 1, i <= ntiles))
    def _():
        hn_cur[...] = hn_next[...]
        fill = iter(tail_pieces() + head_pieces())
        seq_start = (i - 1) % tiles_per_seq == 0
        first_row = lax.broadcasted_iota(jnp.int32, (rows, ck), 0) == 0

        def shift_down(block, fill_row):
            return jnp.where(first_row, fill_row, pltpu.roll(block, 1, 0))

        zero_bits = zero_ref[...]

        def conv(cols, slot, tokens):
            u = jnp.dot(hn_cur[...], wup_ref[:, cols], preferred_element_type=F32)
            ug = [u[group(g), :] for g in range(FFN_R)]
            tails = jnp.where(seq_start, 0.0, halo_ref[slot])
            prev2 = shift_down(ug[FFN_R - 2], tails[SUBLANES - 1:SUBLANES, :])
            prev1 = shift_down(ug[FFN_R - 1], tails[2 * SUBLANES - 1:2 * SUBLANES, :])
            halo_ref[slot] = jnp.concatenate(
                [ug[FFN_R - 2][rows - SUBLANES:, :], ug[FFN_R - 1][rows - SUBLANES:, :]], axis=0)
            back1 = [prev1] + ug[:FFN_R - 1]
            back2 = [prev2, prev1] + ug[:FFN_R - 2]
            cw = cw_ref[:, cols]
            cb = cb_ref[:, cols]
            if tokens:
                cb = cb + jnp.tile(_anchor(tokens, zero_bits)[0:1, :], (1, ck // LANES))
            return [cw[0:1, :] * back2[g] + cw[1:2, :] * back1[g] + cw[2:3, :] * ug[g] + cb
                    for g in range(FFN_R)]

        tokens = []
        nchunk = D_FF // ck
        for c in range(nchunk):
            per_point = 2 if c < FFN_DENSE_CHUNKS else 1
            gate = conv(slice(c * ck, (c + 1) * ck), 2 * c, tokens)
            tokens = _emit(fill, per_point)
            value = conv(slice(D_FF + c * ck, D_FF + (c + 1) * ck), 2 * c + 1, tokens)
            tokens = _emit(fill, per_point)
            for g in range(FFN_R):
                act_ref[group(g), c * ck:(c + 1) * ck] = (_silu(gate[g]) * value[g]).astype(BF16)
        down_ref[...] = res_ref[(i - 1) % 2] + jnp.dot(act_ref[...], wd_ref[...],
                                                       preferred_element_type=F32)
        for run in fill:
            run()

    @pl.when(i == ntiles + 1)
    def _():
        for run in tail_pieces():
            run()


def _ffn(x, mixed, g, w_up, conv_w, conv_b, w_down, final_g, seq, final, riders=()):
    m = x.shape[0]
    tm = FFN_TM
    assert D_FF % FFN_CHUNK == 0 and tm % (FFN_R * 2 * SUBLANES) == 0 and seq % tm == 0
    const = lambda shape, at=(0, 0): pl.BlockSpec(shape, lambda i: at, pipeline_mode=pl.Buffered(1))
    ntiles = m // tm
    tile = lambda width, at: pl.BlockSpec((tm, width), lambda i: (at(i), 0))
    head_tile = lambda i: jnp.minimum(i, ntiles - 1)
    in_specs = (
        [tile(D_MODEL, head_tile)]
        + [tile(y.shape[1], head_tile) for y, _, _ in mixed]
        + [const((y.shape[1], D_MODEL), (row0 // y.shape[1], 0)) for y, _, row0 in mixed]
        + [const((1, D_MODEL)),
           const((D_MODEL, 2 * D_FF)), const((CONV_WIDTH, 2 * D_FF)), const((1, 2 * D_FF)),
           const((D_FF, D_MODEL)),
           const((1, D_MODEL)),
           const((SUBLANES, LANES)),
           pl.BlockSpec(memory_space=pltpu.SMEM)])
    rider_in, rider_out, rider_shapes = _rider_specs(riders, ntiles)
    return pl.pallas_call(
        _with_riders(functools.partial(_ffn_kernel, ntiles, seq // tm, final, len(mixed)),
                     len(in_specs), 1, len(riders)),
        grid=(ntiles + 2,),
        in_specs=in_specs + rider_in,
        out_specs=[tile(D_MODEL, lambda i: jnp.maximum(i - 2, 0))] + rider_out,
        out_shape=[jax.ShapeDtypeStruct((m, D_MODEL), F32)] + rider_shapes,
        scratch_shapes=[
            pltpu.VMEM((D_MODEL // LANES, tm, LANES), F32),
            pltpu.VMEM((D_MODEL // LANES, tm, LANES), F32),
            pltpu.VMEM((tm, D_MODEL), BF16),
            pltpu.VMEM((tm, D_MODEL), BF16),
            pltpu.VMEM((2, tm, D_MODEL), F32),
            pltpu.VMEM((tm, D_FF), BF16),
            pltpu.VMEM((tm, D_MODEL), F32),
            pltpu.VMEM((2 * D_FF // FFN_CHUNK, 2 * SUBLANES, FFN_CHUNK), F32),
        ],
        compiler_params=_params("arbitrary"),
        name="conv_ffn",
    )(x, *[y for y, _, _ in mixed], *[wo for _, wo, _ in mixed], g, w_up, conv_w.astype(F32),
      conv_b.astype(F32).reshape(1, -1), w_down, final_g, jnp.zeros((SUBLANES, LANES), jnp.int32),
      jnp.zeros((1,), jnp.int32), *[a for a, _ in riders])


HGRN_GROUP = 128
HGRN_TM = 256
HGRN_PROJ_COLS = 256
HGRN_EMIT = 2


def _hgrn_groups(proj_ref, row_slices, lower_all, gain, causal, tri, st, fill):
    c = HGRN_CHUNK
    heads = range(HGRN_HEADS)
    groups = range(len(row_slices))
    hk = [slice(h * HGRN_DK, (h + 1) * HGRN_DK) for h in heads]
    hv = [slice(h * HGRN_DV, (h + 1) * HGRN_DV) for h in heads]
    chunks = [slice(n * c, (n + 1) * c) for n in range(HGRN_GROUP // c)]

    q = [_silu(proj_ref[rows, 0:C_K]) for rows in row_slices]
    fg = [lower_all + (1.0 - lower_all) * jax.nn.sigmoid(proj_ref[rows, C_K:2 * C_K])
          for rows in row_slices]
    k = [1.0 - f for f in fg]
    split = []
    for f in fg:
        log_f = jnp.log(f)
        hi = log_f.astype(BF16)
        rem = log_f - hi.astype(F32)
        mid = rem.astype(BF16)
        split.append(jnp.concatenate([hi, mid, (rem - mid.astype(F32)).astype(BF16)], axis=1))
    _emit(fill, HGRN_EMIT)
    parts = [jnp.dot(tri, s, preferred_element_type=F32) for s in split]
    b = [p[:, 0:C_K] + p[:, C_K:2 * C_K] + p[:, 2 * C_K:3 * C_K] for p in parts]
    v = [proj_ref[rows, 2 * C_K:2 * C_K + C_V].astype(BF16) for rows in row_slices]
    q_t = [(q[g] * jnp.exp(b[g])).astype(BF16) for g in groups]
    k_t = [(k[g] * jnp.exp(-b[g])).astype(BF16) for g in groups]
    _emit(fill, HGRN_EMIT)
    scores = [[lax.dot_general(q_t[g][:, hk[h]], k_t[g][:, hk[h]], NT_DIMS, preferred_element_type=F32)
               for h in heads] for g in groups]
    scores = [[jnp.where(causal, s, 0.0).astype(BF16) for s in per_group] for per_group in scores]
    intra = [[jnp.dot(scores[g][h], v[g][:, hv[h]], preferred_element_type=F32) for h in heads]
             for g in groups]
    _emit(fill, HGRN_EMIT)
    b_last = [[b[g][cs.stop - 1:cs.stop, :] for cs in chunks] for g in groups]
    k_end = [[(k[g][cs, :] * jnp.exp(bl - b[g][cs, :])).astype(BF16) for cs, bl in zip(chunks, b_last[g])]
             for g in groups]
    decay = [[jnp.exp(bl) for bl in b_last[g]] for g in groups]
    gate = [gain * _silu(proj_ref[rows, 2 * C_K + C_V:2 * C_K + 2 * C_V]) for rows in row_slices]
    _emit(fill, HGRN_EMIT)
    st = list(st)
    outs = []
    for g in groups:
        inter = [[] for _ in heads]
        for n, cs in enumerate(chunks):
            for h in heads:
                inter[h].append(lax.dot_general(q_t[g][cs, hk[h]], st[h].astype(BF16), NT_DIMS,
                                                preferred_element_type=F32))
                st[h] = st[h] * decay[g][n][:, hk[h]] + lax.dot_general(
                    v[g][cs, hv[h]], k_end[g][n][:, hk[h]], TN_DIMS, preferred_element_type=F32)
        _emit(fill, HGRN_EMIT)
        per_head = []
        for h in heads:
            y = intra[g][h] + jnp.concatenate(inter[h], axis=0)
            yn = y * lax.rsqrt(jnp.mean(y * y, axis=-1, keepdims=True) + EPS)
            per_head.append((yn * gate[g][:, hv[h]]).astype(BF16))
        outs.append(per_head)
        _emit(fill, HGRN_EMIT)
    return outs, st


def _hgrn_layer_kernel(layer, tiles_per_seq, h0_ref, hnext_ref, g_ref, w_ref, lbp_ref, on_ref, o_ref,
                       proj_a, proj_b, st_ref):
    t = pl.program_id(0)
    tm = o_ref.shape[0]
    grp = HGRN_GROUP

    def projection(h_ref, dst):
        x = h_ref[...]
        ms = jnp.mean(x * x, axis=-1, keepdims=True)
        hn = (x * lax.rsqrt(ms + EPS) * g_ref[...]).astype(BF16)

        def piece(c):
            cols = slice(c * HGRN_PROJ_COLS, (c + 1) * HGRN_PROJ_COLS)

            def run():
                dst[:, cols] = jnp.dot(hn, w_ref[:, cols], preferred_element_type=F32)
            return run

        return [piece(c) for c in range(ODD_IN // HGRN_PROJ_COLS)]

    @pl.when(t == 0)
    def _():
        for run in projection(h0_ref, proj_a):
            run()

    @pl.when(t % tiles_per_seq == 0)
    def _():
        st_ref[...] = jnp.zeros_like(st_ref)

    def step(src, dst):
        fill = iter(projection(hnext_ref, dst))
        lbp = lbp_ref[...]
        ex = jnp.exp(lbp - jnp.max(lbp, axis=0, keepdims=True))
        sm = ex / jnp.sum(ex, axis=0, keepdims=True)
        lower_all = jnp.sum(sm[0:layer + 1, :], axis=0, keepdims=True) - sm[0:1, :]
        row = lax.broadcasted_iota(jnp.int32, (grp, grp), 0)
        colm = lax.broadcasted_iota(jnp.int32, (grp, grp), 1)
        causal = jnp.logical_and(row // HGRN_CHUNK == colm // HGRN_CHUNK, row >= colm)
        tri = jnp.where(causal, 1.0, 0.0).astype(BF16)
        gain = on_ref[...]
        st = [st_ref[h] for h in range(HGRN_HEADS)]
        row_slices = [slice(r0, r0 + grp) for r0 in range(0, tm, grp)]
        outs, st = _hgrn_groups(src, row_slices, lower_all, gain, causal, tri, st, fill)
        for rows, per_head in zip(row_slices, outs):
            for h, y in enumerate(per_head):
                o_ref[rows, h * HGRN_DV:(h + 1) * HGRN_DV] = y
        for run in fill:
            run()
        for h in range(HGRN_HEADS):
            st_ref[h] = st[h]

    @pl.when(t % 2 == 0)
    def _():
        step(proj_a, proj_b)

    @pl.when(t % 2 == 1)
    def _():
        step(proj_b, proj_a)


def _hgrn_layer(h, g, w, hgrn_lb, out_norm, layer, seq):
    tokens, d = h.shape
    tm = HGRN_TM
    ntiles = tokens // tm
    const = lambda shape: pl.BlockSpec(shape, lambda t: (0, 0), pipeline_mode=pl.Buffered(1))
    return pl.pallas_call(
        functools.partial(_hgrn_layer_kernel, layer, seq // tm),
        grid=(ntiles,),
        in_specs=[
            const((tm, d)),
            pl.BlockSpec((tm, d), lambda t: (jnp.minimum(t + 1, ntiles - 1), 0)),
            const((1, d)), const(w.shape), const(hgrn_lb.shape), const((1, C_V)),
        ],
        out_specs=pl.BlockSpec((tm, C_V), lambda t: (t, 0)),
        out_shape=jax.ShapeDtypeStruct((tokens, C_V), BF16),
        scratch_shapes=[pltpu.VMEM((tm, ODD_IN), F32), pltpu.VMEM((tm, ODD_IN), F32),
                        pltpu.VMEM((HGRN_HEADS, HGRN_DV, HGRN_DK), F32)],
        compiler_params=_params("arbitrary"),
        name="hgrn_layer",
    )(h, h, g, w, hgrn_lb, out_norm)


def kernel(x, even_w_in, even_w_out, ret_norm, rel_bias, odd_w_in, odd_w_out, hgrn_lb, hgrn_norm,
           mix_norm, ffn_norm, ffn_w_up, ffn_conv_w, ffn_conv_b, ffn_w_down, final_norm):
    bn, seq, d = x.shape
    tokens = bn * seq
    row = lambda a: a.reshape(1, -1).astype(F32)
    h = x.reshape(tokens, d)
    final_g = row(final_norm)

    outs = _even_in(h, row(mix_norm[0]), even_w_in[0].astype(BF16), row(ret_norm[0]), seq,
                    riders=[(even_w_out, 0), (ffn_w_up, 0), (ffn_w_down, 0)])
    ya, q4, q16, k4, k16, v4, v16, w_out0, w_up0, w_down0 = outs
    yb = _dilated((q4, k4, v4, q16, k16, v16), rel_bias, bn, seq).reshape(tokens, B_W)
    h, w_in1, w_out1, w_up1, w_down1 = _ffn(
        h, [(ya, w_out0, 0), (yb, w_out0, A_V)], row(ffn_norm[0]), w_up0, ffn_conv_w[0],
        ffn_conv_b[0], w_down0, final_g, seq, final=False,
        riders=[(odd_w_in, 0), (odd_w_out, 0), (ffn_w_up, 1), (ffn_w_down, 1)])

    y = _hgrn_layer(h, row(mix_norm[1]), w_in1, hgrn_lb.astype(F32), row(hgrn_norm[0]), 1, seq)
    h, = _ffn(h, [(y, w_out1, 0)], row(ffn_norm[1]), w_up1, ffn_conv_w[1], ffn_conv_b[1], w_down1,
              final_g, seq, final=True)
    return h.reshape(bn, seq, d)
```

```python
import functools
import math

import jax
import jax.numpy as jnp
from jax import lax
from jax.experimental import pallas as pl
from jax.experimental.pallas import tpu as pltpu

F32 = jnp.float32
BF16 = jnp.bfloat16

D_MODEL = 1024
RET_HEADS = 8
RET_DK = 64
RET_DV = 128
RET_CHUNK = 128
ROPE_BASE = 10000.0
DIL_HEADS = 8
DIL_DH = 64
DIL_BRANCHES = ((128, 1), (512, 4), (2048, 16))
DIL_BLOCK = 128
DIL_MID, DIL_WIDE = DIL_BRANCHES[1][1], DIL_BRANCHES[2][1]
HGRN_HEADS = 8
HGRN_DK = 128
HGRN_DV = 128
HGRN_CHUNK = 32
REL_BUCKETS = 32
REL_MAX_DIST = 2048
D_FF = 2816
CONV_WIDTH = 3
EPS = 1e-6

A_QK = RET_HEADS * RET_DK
A_V = RET_HEADS * RET_DV
B_W = DIL_HEADS * DIL_DH
EVEN_IN = 2 * A_QK + 2 * A_V + 3 * B_W
C_K = HGRN_HEADS * HGRN_DK
C_V = HGRN_HEADS * HGRN_DV
ODD_IN = 2 * C_K + 2 * C_V

LANES = 128
SUBLANES = 8
VMEM_LIMIT = 56 * 1024 * 1024
MASK_VALUE = -1e30

NT_DIMS = (((1,), (1,)), ((), ()))
TN_DIMS = (((0,), (0,)), ((), ()))


def _params(*semantics):
    return pltpu.CompilerParams(dimension_semantics=semantics, vmem_limit_bytes=VMEM_LIMIT)


def _silu(x):
    return x * jax.nn.sigmoid(x)


def _emit(fill, count):
    tokens = []
    for _ in range(count):
        thunk = next(fill, None)
        if thunk is not None:
            token = thunk()
            if token is not None:
                tokens.append(token)
    return tokens


def _anchor(tokens, zero_bits):
    acc = zero_bits
    for t in tokens:
        acc = acc & pltpu.bitcast(t, jnp.int32)
    return pltpu.bitcast(acc, F32)


BF16_SUBLANES = 2 * SUBLANES


def _rider_specs(weights, steps):
    in_specs, out_specs, out_shapes = [], [], []
    for arr, layer in weights:
        _, rows, cols = arr.shape
        rep = 1
        while (rows * rep) % (steps * BF16_SUBLANES):
            rep *= 2
        slab = rows * rep // steps
        slab_index = lambda i, rep=rep: jnp.minimum(i, steps - 1) // rep
        in_specs.append(pl.BlockSpec((None, slab, cols),
                                     lambda i, layer=layer, at=slab_index: (layer, at(i), 0)))
        out_specs.append(pl.BlockSpec((slab, cols), lambda i, at=slab_index: (at(i), 0)))
        out_shapes.append(jax.ShapeDtypeStruct((rows, cols), BF16))
    return in_specs, out_specs, out_shapes


def _with_riders(body, n_in, n_out, n_riders):
    def kernel_fn(*refs):
        ins, refs = refs[:n_in], refs[n_in:]
        rider_in, refs = refs[:n_riders], refs[n_riders:]
        outs, refs = refs[:n_out], refs[n_out:]
        rider_out, scratch = refs[:n_riders], refs[n_riders:]
        for src, dst in zip(rider_in, rider_out):
            dst[...] = src[...].astype(dst.dtype)
        body(*ins, *outs, *scratch)
    return kernel_fn


def _retention_tables(seq):
    h = jnp.arange(RET_HEADS, dtype=F32)
    log_g = jnp.log1p(-jnp.exp2(-5.0 - h))
    pos = jnp.arange(RET_CHUNK, dtype=F32)
    diff = pos[:, None] - pos[None, :]
    decay = jnp.where(diff >= 0, jnp.exp(jnp.maximum(diff, 0.0)[None] * log_g[:, None, None]), 0.0)
    q_scale = jnp.repeat(jnp.exp((pos + 1.0)[:, None] * log_g[None, :]), RET_DK, axis=1)
    k_scale = jnp.repeat(jnp.exp((RET_CHUNK - 1.0 - pos)[:, None] * log_g[None, :]), RET_DK, axis=1)
    chunk_decay = jnp.broadcast_to(jnp.exp(RET_CHUNK * log_g)[:, None, None], (RET_HEADS, 1, RET_DV))
    inv = ROPE_BASE ** (-jnp.arange(0, RET_DK, 2, dtype=F32) / RET_DK)
    ang = jnp.arange(seq, dtype=F32)[:, None] * inv[None, :]
    cos, sin = jnp.cos(ang), jnp.sin(ang)
    reps = LANES // RET_DK
    cos_t = jnp.tile(jnp.concatenate([cos, cos], axis=1), (1, reps))
    sin_t = jnp.tile(jnp.concatenate([-sin, sin], axis=1), (1, reps))
    return decay, q_scale, k_scale, chunk_decay, cos_t, sin_t


EVEN_TM = 512


def _even_in_kernel(tiles_per_seq, x_ref, g_ref, w_ref, cos_ref, sin_ref, qs_ref, ks_ref, dec_ref,
                    cd_ref, rn_ref, ya_ref, q4_ref, q16_ref, k4_ref, k16_ref, v4_ref, v16_ref,
                    slab_ref, slab4_ref, st_ref):
    tm = x_ref.shape[0]
    chunk = RET_CHUNK

    @pl.when(pl.program_id(0) % tiles_per_seq == 0)
    def _():
        st_ref[...] = jnp.zeros_like(st_ref)

    x = x_ref[...]
    ms = jnp.mean(x * x, axis=-1, keepdims=True)
    hn = (x * lax.rsqrt(ms + EPS) * g_ref[...]).astype(BF16)

    def proj(c):
        return jnp.dot(hn, w_ref[:, c * B_W:(c + 1) * B_W], preferred_element_type=F32)

    cos = cos_ref[...]
    sin = sin_ref[...]
    lane = lax.broadcasted_iota(jnp.int32, (tm, LANES), 1)
    half = RET_DK // 2
    first_half = (lane % RET_DK) < half

    def rot(t):
        partner = jnp.where(first_half, pltpu.roll(t, LANES - half, 1), pltpu.roll(t, half, 1))
        return t * cos + partner * sin

    groups = [slice(s * LANES, (s + 1) * LANES) for s in range(B_W // LANES)]

    r = proj(0)
    q_rot = [rot(r[:, sl]) for sl in groups]
    q = [t.astype(BF16) for t in q_rot]
    q_in = [(t * qs_ref[:, sl]).astype(BF16) for t, sl in zip(q_rot, groups)]
    r = proj(1)
    k_rot = [rot(r[:, sl]) * (RET_DK ** -0.5) for sl in groups]
    k = [t.astype(BF16) for t in k_rot]
    k_out = [(t * ks_ref[:, sl]).astype(BF16) for t, sl in zip(k_rot, groups)]
    v = jnp.concatenate([proj(2), proj(3)], axis=1).astype(BF16)
    gate = rn_ref[...] * _silu(jnp.concatenate([proj(4), proj(5)], axis=1))

    def dilated_pieces(c, slot, r4_ref, r16_ref, scale):
        def project():
            r = proj(c)
            if scale is not None:
                r = r * scale
            for s, sl in enumerate(groups):
                slab_ref[slot, s] = r[:, sl]

        mid, wide = DIL_MID, DIL_WIDE
        sub = wide // mid
        per = tm // mid

        def by4():
            for s, sl in enumerate(groups):
                for rho in range(mid):
                    t = slab_ref[slot, s, pl.ds(rho, per, stride=mid), :]
                    r4_ref[rho, :, sl] = t.astype(BF16)
                    slab4_ref[slot, s, rho * per:(rho + 1) * per, :] = t

        def by16():
            for s, sl in enumerate(groups):
                for rho in range(mid):
                    for c in range(sub):
                        r16_ref[rho + mid * c, :, sl] = slab4_ref[
                            slot, s, pl.ds(rho * per + c, tm // wide, stride=sub), :].astype(BF16)

        return [project, by4, by16]

    fill = iter(dilated_pieces(6, 0, q4_ref, q16_ref, DIL_DH ** -0.5)
                + dilated_pieces(7, 1, k4_ref, k16_ref, None)
                + dilated_pieces(8, 2, v4_ref, v16_ref, None))

    head0 = lax.broadcasted_iota(jnp.int32, (chunk, LANES), 1) < RET_DK
    heads = range(RET_HEADS)
    hs = [slice(h * RET_DV, (h + 1) * RET_DV) for h in heads]
    st = [st_ref[h] for h in heads]
    for ci in range(tm // chunk):
        rows = slice(ci * chunk, (ci + 1) * chunk)
        mask = [head0 if h % 2 == 0 else jnp.logical_not(head0) for h in heads]
        scores = [lax.dot_general(jnp.where(mask[h], q[h // 2][rows], 0), k[h // 2][rows], NT_DIMS,
                                  preferred_element_type=F32) * dec_ref[h] for h in heads]
        _emit(fill, 1)
        intra = [jnp.dot(scores[h].astype(BF16), v[rows, hs[h]], preferred_element_type=F32)
                 for h in heads]
        inter = [jnp.dot(jnp.where(mask[h], q_in[h // 2][rows], 0), st[h].astype(BF16),
                         preferred_element_type=F32) for h in heads]
        st = [st[h] * cd_ref[h] + lax.dot_general(k_out[h // 2][rows], v[rows, hs[h]], TN_DIMS,
                                                  preferred_element_type=F32) for h in heads]
        _emit(fill, 1)
        for h in heads:
            y = intra[h] + inter[h]
            yc = y - jnp.mean(y, axis=-1, keepdims=True)
            yn = yc * lax.rsqrt(jnp.mean(yc * yc, axis=-1, keepdims=True) + EPS)
            ya_ref[rows, hs[h]] = (yn * gate[rows, hs[h]]).astype(BF16)
        _emit(fill, 1)
    for run in fill:
        run()
    for h in heads:
        st_ref[h] = st[h]


def _even_in(x, g, w, ret_norm, seq, riders):
    tokens, d = x.shape
    bn = tokens // seq
    tm = EVEN_TM
    tps = seq // tm
    decay, q_scale, k_scale, chunk_decay, cos_t, sin_t = _retention_tables(seq)
    q_scale = jnp.tile(q_scale, (tm // RET_CHUNK, 1))
    k_scale = jnp.tile(k_scale, (tm // RET_CHUNK, 1))
    const = lambda shape: pl.BlockSpec(shape, lambda i: (0,) * len(shape), pipeline_mode=pl.Buffered(1))
    flat = lambda w_: pl.BlockSpec((tm, w_), lambda i: (i, 0))
    by = lambda r: pl.BlockSpec((None, r, tm // r, B_W), lambda i: (i // tps, 0, i % tps, 0))
    sds = jax.ShapeDtypeStruct
    lay = [sds((bn, r, seq // r, B_W), BF16) for r in (DIL_MID, DIL_WIDE)]
    steps = tokens // tm
    in_specs = [
        flat(d), const((1, d)), const(w.shape),
        pl.BlockSpec((tm, LANES), lambda i: (i % tps, 0)),
        pl.BlockSpec((tm, LANES), lambda i: (i % tps, 0)),
        const((tm, A_QK)), const((tm, A_QK)),
        const(decay.shape), const(chunk_decay.shape), const((1, A_V)),
    ]
    out_specs = [flat(A_V)] + [by(DIL_MID), by(DIL_WIDE)] * 3
    rider_in, rider_out, rider_shapes = _rider_specs(riders, steps)
    return pl.pallas_call(
        _with_riders(functools.partial(_even_in_kernel, tps), len(in_specs), len(out_specs), len(riders)),
        grid=(steps,),
        in_specs=in_specs + rider_in,
        out_specs=out_specs + rider_out,
        out_shape=[sds((tokens, A_V), BF16)] + lay * 3 + rider_shapes,
        scratch_shapes=[pltpu.VMEM((3, B_W // LANES, tm, LANES), F32),
                        pltpu.VMEM((3, B_W // LANES, tm, LANES), F32),
                        pltpu.VMEM((RET_HEADS, 2 * RET_DK, RET_DV), F32)],
        compiler_params=_params("arbitrary"),
        name="even_in",
    )(x, g, w, cos_t, sin_t, q_scale, k_scale, decay, chunk_decay, ret_norm, *[a for a, _ in riders])


def _rel_bucket(dist):
    max_exact = REL_BUCKETS // 2
    d = dist.astype(F32)
    large = max_exact + (jnp.log(jnp.maximum(d, 1.0) / max_exact) / math.log(REL_MAX_DIST / max_exact)
                         * (REL_BUCKETS - max_exact)).astype(jnp.int32)
    large = jnp.minimum(large, REL_BUCKETS - 1)
    return jnp.where(dist < max_exact, dist, large)


DIL_BASE = DIL_MID
DIL_UNROLL = 8


def _dilated_buckets():
    lb = DIL_BLOCK
    slot = jnp.arange(lb)
    per = lb // DIL_BASE
    regrouped = DIL_BASE * (slot % per) + slot // per
    tables = []
    for (window, dilation), token in zip(DIL_BRANCHES, (regrouped, slot, slot)):
        a_idx = token[:, None]
        c_idx = jnp.concatenate([token, lb + token])[None, :]
        dist = lb + a_idx - c_idx
        valid = (dist >= 0) & (dist <= window // dilation)
        tables.append(jnp.where(valid, _rel_bucket(jnp.maximum(dist, 0) * dilation), -1))
    return jnp.stack(tables).astype(jnp.int32)


def _dilated_kernel(rb_ref, bucket_ref, q4_ref, k4_ref, v4_ref, q16_ref, k16_ref, v16_ref, y_ref,
                    bias_ref, acc_ref, m_ref, l_ref):
    lb = DIL_BLOCK
    seq = y_ref.shape[0]
    pair = pl.program_id(1)

    @pl.when(pl.program_id(0) == 0)
    def _():
        for br in range(len(DIL_BRANCHES)):
            bucket = bucket_ref[br]
            for a in range(2):
                h = 2 * pair + a
                t = jnp.full((lb, 2 * lb), MASK_VALUE, F32)
                for j in range(REL_BUCKETS):
                    t = jnp.where(bucket == j, rb_ref[j, h], t)
                bias_ref[br * DIL_HEADS + h] = t

    lane = lax.broadcasted_iota(jnp.int32, (lb, LANES), 1)
    head0 = lane < DIL_DH

    def attend(br, blocks):
        masks = (head0, jnp.logical_not(head0))
        both = [bias_ref[br * DIL_HEADS + 2 * pair + a] for a in range(2)]
        cur_only = [t[:, lb:2 * lb] for t in both]
        bias = [cur_only if kp is None else both for _, kp, _, _, _ in blocks]
        ks = [kc if kp is None else jnp.concatenate([kp, kc], axis=0) for _, kp, kc, _, _ in blocks]
        vs = [vc if vp is None else jnp.concatenate([vp, vc], axis=0) for _, _, _, vp, vc in blocks]
        scores = [[lax.dot_general(jnp.where(masks[a], blk[0], 0), k, NT_DIMS,
                                   preferred_element_type=F32) + b[a] for a in range(2)]
                  for blk, k, b in zip(blocks, ks, bias)]
        maxes = [[jnp.max(s, axis=-1, keepdims=True) for s in pair_s] for pair_s in scores]
        exps = [[jnp.exp(s - m) for s, m in zip(pair_s, pair_m)] for pair_s, pair_m in zip(scores, maxes)]
        sums = [[jnp.sum(e, axis=-1, keepdims=True) for e in pair_e] for pair_e in exps]
        outs = [[jnp.dot(e.astype(BF16), v, preferred_element_type=F32) for e in pair_e]
                for pair_e, v in zip(exps, vs)]
        return [tuple(jnp.where(head0, x[0], x[1]) for x in (o, m, l))
                for o, m, l in zip(outs, maxes, sums)]

    def aligned(x, m):
        return x if isinstance(x, int) else pl.multiple_of(x, m)

    def is_first(n):
        return isinstance(n, int) and n == 0

    def sweep(count, item_of, run, extra=()):
        full = count // DIL_UNROLL
        if full == 1:
            run([item_of(j) for j in range(DIL_UNROLL)])
        elif full > 1:
            def body(it, carry):
                run([item_of(it * DIL_UNROLL + j) for j in range(DIL_UNROLL)])
                return carry
            lax.fori_loop(0, full, body, 0)
        rest = [item_of(j) for j in range(full * DIL_UNROLL, count)] + list(extra)
        for at in range(0, len(rest), DIL_UNROLL):
            run(rest[at:at + DIL_UNROLL])

    base = DIL_BASE
    per = lb // base
    class_rows = seq // base

    def unit_blocks(ns):
        blocks, dests = [], []
        for n in ns:
            r0 = aligned(n * per, per)
            gather = lambda ref, at: jnp.concatenate([ref[rho, pl.ds(at, per), :] for rho in range(base)],
                                                     axis=0)
            if is_first(n):
                blocks.append((gather(q4_ref, r0), None, gather(k4_ref, r0), None, gather(v4_ref, r0)))
            else:
                rp = aligned((n - 1) * per, per)
                blocks.append((gather(q4_ref, r0), gather(k4_ref, rp), gather(k4_ref, r0),
                               gather(v4_ref, rp), gather(v4_ref, r0)))
            dests.append(r0)
        for r0, (acc, m, l) in zip(dests, attend(0, blocks)):
            for rho in range(base):
                rows = pl.ds(rho * class_rows + r0, per)
                piece = slice(rho * per, (rho + 1) * per)
                acc_ref[rows, :] = acc[piece]
                m_ref[rows, :] = m[piece]
                l_ref[rows, :] = l[piece]

    sweep(seq // lb - 1, lambda j: 1 + j, unit_blocks, extra=[0])

    def dilated_blocks(br, r, q_ref, k_ref, v_ref, ids):
        sub = r // base
        blocks, dests = [], []
        for rho, n in ids:
            cur = pl.ds(aligned(n * lb, lb), lb)
            if is_first(n):
                blocks.append((q_ref[rho, cur, :], None, k_ref[rho, cur, :], None, v_ref[rho, cur, :]))
            else:
                prev = pl.ds(aligned((n - 1) * lb, lb), lb)
                blocks.append((q_ref[rho, cur, :], k_ref[rho, prev, :], k_ref[rho, cur, :],
                               v_ref[rho, prev, :], v_ref[rho, cur, :]))
            start = (rho % base) * class_rows + n * (lb * sub) + rho // base
            dests.append(pl.ds(start, lb, stride=sub) if sub > 1 else pl.ds(aligned(start, lb), lb))
        for rows, (acc, m, l) in zip(dests, attend(br, blocks)):
            m_old = m_ref[rows, :]
            m_new = jnp.maximum(m_old, m)
            c_old = jnp.exp(m_old - m_new)
            c_cur = jnp.exp(m - m_new)
            acc_ref[rows, :] = acc_ref[rows, :] * c_old + acc * c_cur
            l_ref[rows, :] = l_ref[rows, :] * c_old + l * c_cur
            m_ref[rows, :] = m_new

    for br, refs in ((1, (q4_ref, k4_ref, v4_ref)), (2, (q16_ref, k16_ref, v16_ref))):
        r = DIL_BRANCHES[br][1]
        nb = seq // (r * lb)
        run = lambda ids, br=br, r=r, refs=refs: dilated_blocks(br, r, *refs, ids)
        firsts = [(rho, 0) for rho in range(r)]
        later = lambda j, r=r: (j % r, 1 + j // r)
        if r < DIL_UNROLL:
            sweep(r * (nb - 1), later, run, extra=firsts)
        else:
            sweep(r, lambda j: (j, 0), run)
            sweep(r * (nb - 1), later, run)

    def normalise(n, carry):
        for rho in range(base):
            rows = pl.ds(pl.multiple_of(rho * class_rows + n * lb, lb), lb)
            m_ref[pl.ds(n * (lb * base) + rho, lb, stride=base), :] = acc_ref[rows, :] / l_ref[rows, :]
        return carry

    lax.fori_loop(0, class_rows // lb, normalise, 0)

    def finish(n, carry):
        rows = pl.ds(pl.multiple_of(n * lb, lb), lb)
        y_ref[rows, :] = m_ref[rows, :].astype(y_ref.dtype)
        return carry

    lax.fori_loop(0, seq // lb, finish, 0)


def _dilated(layouts, rel_bias, bn, seq):
    lb = DIL_BLOCK
    assert all(w // r <= lb and seq % (r * lb) == 0 for w, r in DIL_BRANCHES)
    assert DIL_BRANCHES[0][1] == 1 and DIL_WIDE % DIL_MID == 0
    nat = pl.BlockSpec((None, seq, LANES), lambda b, p: (b, 0, p))
    by = lambda r: pl.BlockSpec((None, r, seq // r, LANES), lambda b, p: (b, 0, 0, p))
    nbr = len(DIL_BRANCHES)
    return pl.pallas_call(
        _dilated_kernel,
        grid=(bn, DIL_HEADS // 2),
        in_specs=[pl.BlockSpec(memory_space=pltpu.SMEM),
                  pl.BlockSpec((nbr, lb, 2 * lb), lambda b, p: (0, 0, 0))]
        + [by(DIL_MID)] * 3 + [by(DIL_WIDE)] * 3,
        out_specs=nat,
        out_shape=jax.ShapeDtypeStruct((bn, seq, B_W), BF16),
        scratch_shapes=[pltpu.VMEM((nbr * DIL_HEADS, lb, 2 * lb), F32),
                        pltpu.VMEM((seq, LANES), F32), pltpu.VMEM((seq, LANES), F32),
                        pltpu.VMEM((seq, LANES), F32)],
        compiler_params=_params("arbitrary", "arbitrary"),
        name="dilated",
    )(rel_bias.astype(F32), _dilated_buckets(), *layouts)


FFN_TM = 512
FFN_R = 8
FFN_CHUNK = 256
FFN_HEAD_COLS = 512
FFN_DENSE_CHUNKS = 5


def _ffn_kernel(ntiles, tiles_per_seq, final, nparts, *refs):
    x_ref, y_refs, wo_refs = refs[0], refs[1:1 + nparts], refs[1 + nparts:1 + 2 * nparts]
    (g_ref, wup_ref, cw_ref, cb_ref, wd_ref, fg_ref, zero_ref, where_ref, o_ref, slab_in, slab_out,
     hn_cur, hn_next, res_ref, act_ref, down_ref, halo_ref) = refs[1 + 2 * nparts:]
    i = pl.program_id(0)
    tm = o_ref.shape[0]
    rows = tm // FFN_R
    nslab = D_MODEL // LANES
    ck = FFN_CHUNK
    group = lambda g: slice(g * rows, (g + 1) * rows)
    lanes = lambda s: slice(s * LANES, (s + 1) * LANES)
    anywhere = pl.multiple_of(where_ref[0], BF16_SUBLANES)

    def head_pieces():
        wide = FFN_HEAD_COLS
        per = wide // LANES

        def project(p):
            def run():
                cols = slice(p * wide, (p + 1) * wide)
                h = x_ref[:, cols]
                for y_ref, wo_ref in zip(y_refs, wo_refs):
                    h = h + jnp.dot(y_ref[...], wo_ref[:, cols], preferred_element_type=F32)
                for s in range(per):
                    slab_in[per * p + s] = h[:, s * LANES:(s + 1) * LANES]
                return slab_in[per * p + per - 1, pl.ds(anywhere, SUBLANES), :]
            return run

        def normalise(g):
            def run():
                x = jnp.concatenate([slab_in[s, pl.ds(g, rows, stride=FFN_R), :] for s in range(nslab)],
                                    axis=1)
                res_ref[i % 2, group(g), :] = x
                ms = jnp.mean(x * x, axis=-1, keepdims=True)
                hn_next[group(g), :] = (x * lax.rsqrt(ms + EPS) * g_ref[...]).astype(BF16)
                return hn_next[pl.ds(anywhere, BF16_SUBLANES), 0:LANES].astype(F32)[0:SUBLANES]
            return run

        return [project(p) for p in range(D_MODEL // wide)] + [normalise(g) for g in range(FFN_R)]

    def tail_pieces():
        def unpermute(g):
            def run():
                for s in range(nslab):
                    slab_out[s, pl.ds(g, rows, stride=FFN_R), :] = down_ref[group(g), lanes(s)]
                return slab_out[nslab - 1, pl.ds(anywhere, SUBLANES), :]
            return run

        def store(r0, r1):
            def run():
                y = jnp.concatenate([slab_out[s, r0:r1, :] for s in range(nslab)], axis=1)
                if final:
                    ms = jnp.mean(y * y, axis=-1, keepdims=True)
                    y = y * lax.rsqrt(ms + EPS) * fg_ref[...]
                o_ref[r0:r1, :] = y
                return o_ref[pl.ds(anywhere, SUBLANES), 0:LANES]
            return run

        return ([unpermute(g) for g in range(FFN_R)]
                + [store(g * rows, (g + 1) * rows) for g in range(FFN_R)])

    @pl.when(i == 0)
    def _():
        for run in head_pieces():
            run()
        down_ref[...] = jnp.zeros_like(down_ref)
        halo_ref[...] = jnp.zeros_like(halo_ref)

    @pl.when(jnp.logical_and(i >= 1, i <= ntiles))
    def _():
        hn_cur[...] = hn_next[...]
        fill = iter(tail_pieces() + head_pieces())
        seq_start = (i - 1) % tiles_per_seq == 0
        first_row = lax.broadcasted_iota(jnp.int32, (rows, ck), 0) == 0

        def shift_down(block, fill_row):
            return jnp.where(first_row, fill_row, pltpu.roll(block, 1, 0))

        zero_bits = zero_ref[...]

        def conv(cols, slot, tokens):
            u = jnp.dot(hn_cur[...], wup_ref[:, cols], preferred_element_type=F32)
            ug = [u[group(g), :] for g in range(FFN_R)]
            tails = jnp.where(seq_start, 0.0, halo_ref[slot])
            prev2 = shift_down(ug[FFN_R - 2], tails[SUBLANES - 1:SUBLANES, :])
            prev1 = shift_down(ug[FFN_R - 1], tails[2 * SUBLANES - 1:2 * SUBLANES, :])
            halo_ref[slot] = jnp.concatenate(
                [ug[FFN_R - 2][rows - SUBLANES:, :], ug[FFN_R - 1][rows - SUBLANES:, :]], axis=0)
            back1 = [prev1] + ug[:FFN_R - 1]
            back2 = [prev2, prev1] + ug[:FFN_R - 2]
            cw = cw_ref[:, cols]
            cb = cb_ref[:, cols]
            if tokens:
                cb = cb + jnp.tile(_anchor(tokens, zero_bits)[0:1, :], (1, ck // LANES))
            return [cw[0:1, :] * back2[g] + cw[1:2, :] * back1[g] + cw[2:3, :] * ug[g] + cb
                    for g in range(FFN_R)]

        tokens = []
        nchunk = D_FF // ck
        for c in range(nchunk):
            per_point = 2 if c < FFN_DENSE_CHUNKS else 1
            gate = conv(slice(c * ck, (c + 1) * ck), 2 * c, tokens)
            tokens = _emit(fill, per_point)
            value = conv(slice(D_FF + c * ck, D_FF + (c + 1) * ck), 2 * c + 1, tokens)
            tokens = _emit(fill, per_point)
            for g in range(FFN_R):
                act_ref[group(g), c * ck:(c + 1) * ck] = (_silu(gate[g]) * value[g]).astype(BF16)
        down_ref[...] = res_ref[(i - 1) % 2] + jnp.dot(act_ref[...], wd_ref[...],
                                                       preferred_element_type=F32)
        for run in fill:
            run()

    @pl.when(i == ntiles + 1)
    def _():
        for run in tail_pieces():
            run()


def _ffn(x, mixed, g, w_up, conv_w, conv_b, w_down, final_g, seq, final, riders=()):
    m = x.shape[0]
    tm = FFN_TM
    assert D_FF % FFN_CHUNK == 0 and tm % (FFN_R * 2 * SUBLANES) == 0 and seq % tm == 0
    const = lambda shape, at=(0, 0): pl.BlockSpec(shape, lambda i: at, pipeline_mode=pl.Buffered(1))
    ntiles = m // tm
    tile = lambda width, at: pl.BlockSpec((tm, width), lambda i: (at(i), 0))
    head_tile = lambda i: jnp.minimum(i, ntiles - 1)
    in_specs = (
        [tile(D_MODEL, head_tile)]
        + [tile(y.shape[1], head_tile) for y, _, _ in mixed]
        + [const((y.shape[1], D_MODEL), (row0 // y.shape[1], 0)) for y, _, row0 in mixed]
        + [const((1, D_MODEL)),
           const((D_MODEL, 2 * D_FF)), const((CONV_WIDTH, 2 * D_FF)), const((1, 2 * D_FF)),
           const((D_FF, D_MODEL)),
           const((1, D_MODEL)),
           const((SUBLANES, LANES)),
           pl.BlockSpec(memory_space=pltpu.SMEM)])
    rider_in, rider_out, rider_shapes = _rider_specs(riders, ntiles)
    return pl.pallas_call(
        _with_riders(functools.partial(_ffn_kernel, ntiles, seq // tm, final, len(mixed)),
                     len(in_specs), 1, len(riders)),
        grid=(ntiles + 2,),
        in_specs=in_specs + rider_in,
        out_specs=[tile(D_MODEL, lambda i: jnp.maximum(i - 2, 0))] + rider_out,
        out_shape=[jax.ShapeDtypeStruct((m, D_MODEL), F32)] + rider_shapes,
        scratch_shapes=[
            pltpu.VMEM((D_MODEL // LANES, tm, LANES), F32),
            pltpu.VMEM((D_MODEL // LANES, tm, LANES), F32),
            pltpu.VMEM((tm, D_MODEL), BF16),
            pltpu.VMEM((tm, D_MODEL), BF16),
            pltpu.VMEM((2, tm, D_MODEL), F32),
            pltpu.VMEM((tm, D_FF), BF16),
            pltpu.VMEM((tm, D_MODEL), F32),
            pltpu.VMEM((2 * D_FF // FFN_CHUNK, 2 * SUBLANES, FFN_CHUNK), F32),
        ],
        compiler_params=_params("arbitrary"),
        name="conv_ffn",
    )(x, *[y for y, _, _ in mixed], *[wo for _, wo, _ in mixed], g, w_up, conv_w.astype(F32),
      conv_b.astype(F32).reshape(1, -1), w_down, final_g, jnp.zeros((SUBLANES, LANES), jnp.int32),
      jnp.zeros((1,), jnp.int32), *[a for a, _ in riders])


HGRN_GROUP = 128
HGRN_TM = 256
HGRN_PROJ_COLS = 256
HGRN_EMIT = 2


def _hgrn_groups(proj_ref, row_slices, lower_all, gain, causal, tri, st, fill):
    c = HGRN_CHUNK
    heads = range(HGRN_HEADS)
    groups = range(len(row_slices))
    hk = [slice(h * HGRN_DK, (h + 1) * HGRN_DK) for h in heads]
    hv = [slice(h * HGRN_DV, (h + 1) * HGRN_DV) for h in heads]
    chunks = [slice(n * c, (n + 1) * c) for n in range(HGRN_GROUP // c)]

    q = [_silu(proj_ref[rows, 0:C_K]) for rows in row_slices]
    fg = [lower_all + (1.0 - lower_all) * jax.nn.sigmoid(proj_ref[rows, C_K:2 * C_K])
          for rows in row_slices]
    k = [1.0 - f for f in fg]
    split = []
    for f in fg:
        log_f = jnp.log(f)
        hi = log_f.astype(BF16)
        rem = log_f - hi.astype(F32)
        mid = rem.astype(BF16)
        split.append(jnp.concatenate([hi, mid, (rem - mid.astype(F32)).astype(BF16)], axis=1))
    _emit(fill, HGRN_EMIT)
    parts = [jnp.dot(tri, s, preferred_element_type=F32) for s in split]
    b = [p[:, 0:C_K] + p[:, C_K:2 * C_K] + p[:, 2 * C_K:3 * C_K] for p in parts]
    v = [proj_ref[rows, 2 * C_K:2 * C_K + C_V].astype(BF16) for rows in row_slices]
    q_t = [(q[g] * jnp.exp(b[g])).astype(BF16) for g in groups]
    k_t = [(k[g] * jnp.exp(-b[g])).astype(BF16) for g in groups]
    _emit(fill, HGRN_EMIT)
    scores = [[lax.dot_general(q_t[g][:, hk[h]], k_t[g][:, hk[h]], NT_DIMS, preferred_element_type=F32)
               for h in heads] for g in groups]
    scores = [[jnp.where(causal, s, 0.0).astype(BF16) for s in per_group] for per_group in scores]
    intra = [[jnp.dot(scores[g][h], v[g][:, hv[h]], preferred_element_type=F32) for h in heads]
             for g in groups]
    _emit(fill, HGRN_EMIT)
    b_last = [[b[g][cs.stop - 1:cs.stop, :] for cs in chunks] for g in groups]
    k_end = [[(k[g][cs, :] * jnp.exp(bl - b[g][cs, :])).astype(BF16) for cs, bl in zip(chunks, b_last[g])]
             for g in groups]
    decay = [[jnp.exp(bl) for bl in b_last[g]] for g in groups]
    gate = [gain * _silu(proj_ref[rows, 2 * C_K + C_V:2 * C_K + 2 * C_V]) for rows in row_slices]
    _emit(fill, HGRN_EMIT)
    st = list(st)
    outs = []
    for g in groups:
        inter = [[] for _ in heads]
        for n, cs in enumerate(chunks):
            for h in heads:
                inter[h].append(lax.dot_general(q_t[g][cs, hk[h]], st[h].astype(BF16), NT_DIMS,
                                                preferred_element_type=F32))
                st[h] = st[h] * decay[g][n][:, hk[h]] + lax.dot_general(
                    v[g][cs, hv[h]], k_end[g][n][:, hk[h]], TN_DIMS, preferred_element_type=F32)
        _emit(fill, HGRN_EMIT)
        per_head = []
        for h in heads:
            y = intra[g][h] + jnp.concatenate(inter[h], axis=0)
            yn = y * lax.rsqrt(jnp.mean(y * y, axis=-1, keepdims=True) + EPS)
            per_head.append((yn * gate[g][:, hv[h]]).astype(BF16))
        outs.append(per_head)
        _emit(fill, HGRN_EMIT)
    return outs, st


def _hgrn_layer_kernel(layer, tiles_per_seq, h0_ref, hnext_ref, g_ref, w_ref, lbp_ref, on_ref, o_ref,
                       proj_a, proj_b, st_ref):
    t = pl.program_id(0)
    tm = o_ref.shape[0]
    grp = HGRN_GROUP

    def projection(h_ref, dst):
        x = h_ref[...]
        ms = jnp.mean(x * x, axis=-1, keepdims=True)
        hn = (x * lax.rsqrt(ms + EPS) * g_ref[...]).astype(BF16)

        def piece(c):
            cols = slice(c * HGRN_PROJ_COLS, (c + 1) * HGRN_PROJ_COLS)

            def run():
                dst[:, cols] = jnp.dot(hn, w_ref[:, cols], preferred_element_type=F32)
            return run

        return [piece(c) for c in range(ODD_IN // HGRN_PROJ_COLS)]

    @pl.when(t == 0)
    def _():
        for run in projection(h0_ref, proj_a):
            run()

    @pl.when(t % tiles_per_seq == 0)
    def _():
        st_ref[...] = jnp.zeros_like(st_ref)

    def step(src, dst):
        fill = iter(projection(hnext_ref, dst))
        lbp = lbp_ref[...]
        ex = jnp.exp(lbp - jnp.max(lbp, axis=0, keepdims=True))
        sm = ex / jnp.sum(ex, axis=0, keepdims=True)
        lower_all = jnp.sum(sm[0:layer + 1, :], axis=0, keepdims=True) - sm[0:1, :]
        row = lax.broadcasted_iota(jnp.int32, (grp, grp), 0)
        colm = lax.broadcasted_iota(jnp.int32, (grp, grp), 1)
        causal = jnp.logical_and(row // HGRN_CHUNK == colm // HGRN_CHUNK, row >= colm)
        tri = jnp.where(causal, 1.0, 0.0).astype(BF16)
        gain = on_ref[...]
        st = [st_ref[h] for h in range(HGRN_HEADS)]
        row_slices = [slice(r0, r0 + grp) for r0 in range(0, tm, grp)]
        outs, st = _hgrn_groups(src, row_slices, lower_all, gain, causal, tri, st, fill)
        for rows, per_head in zip(row_slices, outs):
            for h, y in enumerate(per_head):
                o_ref[rows, h * HGRN_DV:(h + 1) * HGRN_DV] = y
        for run in fill:
            run()
        for h in range(HGRN_HEADS):
            st_ref[h] = st[h]

    @pl.when(t % 2 == 0)
    def _():
        step(proj_a, proj_b)

    @pl.when(t % 2 == 1)
    def _():
        step(proj_b, proj_a)


def _hgrn_layer(h, g, w, hgrn_lb, out_norm, layer, seq):
    tokens, d = h.shape
    tm = HGRN_TM
    ntiles = tokens // tm
    const = lambda shape: pl.BlockSpec(shape, lambda t: (0, 0), pipeline_mode=pl.Buffered(1))
    return pl.pallas_call(
        functools.partial(_hgrn_layer_kernel, layer, seq // tm),
        grid=(ntiles,),
        in_specs=[
            const((tm, d)),
            pl.BlockSpec((tm, d), lambda t: (jnp.minimum(t + 1, ntiles - 1), 0)),
            const((1, d)), const(w.shape), const(hgrn_lb.shape), const((1, C_V)),
        ],
        out_specs=pl.BlockSpec((tm, C_V), lambda t: (t, 0)),
        out_shape=jax.ShapeDtypeStruct((tokens, C_V), BF16),
        scratch_shapes=[pltpu.VMEM((tm, ODD_IN), F32), pltpu.VMEM((tm, ODD_IN), F32),
                        pltpu.VMEM((HGRN_HEADS, HGRN_DV, HGRN_DK), F32)],
        compiler_params=_params("arbitrary"),
        name="hgrn_layer",
    )(h, h, g, w, hgrn_lb, out_norm)


def kernel(x, even_w_in, even_w_out, ret_norm, rel_bias, odd_w_in, odd_w_out, hgrn_lb, hgrn_norm,
           mix_norm, ffn_norm, ffn_w_up, ffn_conv_w, ffn_conv_b, ffn_w_down, final_norm):
    bn, seq, d = x.shape
    tokens = bn * seq
    row = lambda a: a.reshape(1, -1).astype(F32)
    h = x.reshape(tokens, d)
    final_g = row(final_norm)

    outs = _even_in(h, row(mix_norm[0]), even_w_in[0].astype(BF16), row(ret_norm[0]), seq,
                    riders=[(even_w_out, 0), (ffn_w_up, 0), (ffn_w_down, 0)])
    ya, q4, q16, k4, k16, v4, v16, w_out0, w_up0, w_down0 = outs
    yb = _dilated((q4, k4, v4, q16, k16, v16), rel_bias, bn, seq).reshape(tokens, B_W)
    h, w_in1, w_out1, w_up1, w_down1 = _ffn(
        h, [(ya, w_out0, 0), (yb, w_out0, A_V)], row(ffn_norm[0]), w_up0, ffn_conv_w[0],
        ffn_conv_b[0], w_down0, final_g, seq, final=False,
        riders=[(odd_w_in, 0), (odd_w_out, 0), (ffn_w_up, 1), (ffn_w_down, 1)])

    y = _hgrn_layer(h, row(mix_norm[1]), w_in1, hgrn_lb.astype(F32), row(hgrn_norm[0]), 1, seq)
    h, = _ffn(h, [(y, w_out1, 0)], row(ffn_norm[1]), w_up1, ffn_conv_w[1], ffn_conv_b[1], w_down1,
              final_g, seq, final=True)
    return h.reshape(bn, seq, d)
```

```python
import functools
import math

import jax
import jax.numpy as jnp
from jax import lax
from jax.experimental import pallas as pl
from jax.experimental.pallas import tpu as pltpu

F32 = jnp.float32
BF16 = jnp.bfloat16

D_MODEL = 1024
RET_HEADS = 8
RET_DK = 64
RET_DV = 128
RET_CHUNK = 128
ROPE_BASE = 10000.0
DIL_HEADS = 8
DIL_DH = 64
DIL_BRANCHES = ((128, 1), (512, 4), (2048, 16))
DIL_BLOCK = 128
DIL_MID, DIL_WIDE = DIL_BRANCHES[1][1], DIL_BRANCHES[2][1]
HGRN_HEADS = 8
HGRN_DK = 128
HGRN_DV = 128
HGRN_CHUNK = 32
REL_BUCKETS = 32
REL_MAX_DIST = 2048
D_FF = 2816
CONV_WIDTH = 3
EPS = 1e-6

A_QK = RET_HEADS * RET_DK
A_V = RET_HEADS * RET_DV
B_W = DIL_HEADS * DIL_DH
EVEN_IN = 2 * A_QK + 2 * A_V + 3 * B_W
C_K = HGRN_HEADS * HGRN_DK
C_V = HGRN_HEADS * HGRN_DV
ODD_IN = 2 * C_K + 2 * C_V

LANES = 128
SUBLANES = 8
VMEM_LIMIT = 56 * 1024 * 1024
MASK_VALUE = -1e30

NT_DIMS = (((1,), (1,)), ((), ()))
TN_DIMS = (((0,), (0,)), ((), ()))


def _params(*semantics):
    return pltpu.CompilerParams(dimension_semantics=semantics, vmem_limit_bytes=VMEM_LIMIT)


def _silu(x):
    return x * jax.nn.sigmoid(x)


def _emit(fill, count):
    tokens = []
    for _ in range(count):
        thunk = next(fill, None)
        if thunk is not None:
            token = thunk()
            if token is not None:
                tokens.append(token)
    return tokens


def _anchor(tokens, zero_bits):
    acc = zero_bits
    for t in tokens:
        acc = acc & pltpu.bitcast(t, jnp.int32)
    return pltpu.bitcast(acc, F32)


BF16_SUBLANES = 2 * SUBLANES


def _rider_specs(weights, steps):
    in_specs, out_specs, out_shapes = [], [], []
    for arr, layer in weights:
        _, rows, cols = arr.shape
        rep = 1
        while (rows * rep) % (steps * BF16_SUBLANES):
            rep *= 2
        slab = rows * rep // steps
        slab_index = lambda i, rep=rep: jnp.minimum(i, steps - 1) // rep
        in_specs.append(pl.BlockSpec((None, slab, cols),
                                     lambda i, layer=layer, at=slab_index: (layer, at(i), 0)))
        out_specs.append(pl.BlockSpec((slab, cols), lambda i, at=slab_index: (at(i), 0)))
        out_shapes.append(jax.ShapeDtypeStruct((rows, cols), BF16))
    return in_specs, out_specs, out_shapes


def _with_riders(body, n_in, n_out, n_riders):
    def kernel_fn(*refs):
        ins, refs = refs[:n_in], refs[n_in:]
        rider_in, refs = refs[:n_riders], refs[n_riders:]
        outs, refs = refs[:n_out], refs[n_out:]
        rider_out, scratch = refs[:n_riders], refs[n_riders:]
        for src, dst in zip(rider_in, rider_out):
            dst[...] = src[...].astype(dst.dtype)
        body(*ins, *outs, *scratch)
    return kernel_fn


def _retention_tables(seq):
    h = jnp.arange(RET_HEADS, dtype=F32)
    log_g = jnp.log1p(-jnp.exp2(-5.0 - h))
    pos = jnp.arange(RET_CHUNK, dtype=F32)
    diff = pos[:, None] - pos[None, :]
    decay = jnp.where(diff >= 0, jnp.exp(jnp.maximum(diff, 0.0)[None] * log_g[:, None, None]), 0.0)
    q_scale = jnp.repeat(jnp.exp((pos + 1.0)[:, None] * log_g[None, :]), RET_DK, axis=1)
    k_scale = jnp.repeat(jnp.exp((RET_CHUNK - 1.0 - pos)[:, None] * log_g[None, :]), RET_DK, axis=1)
    chunk_decay = jnp.broadcast_to(jnp.exp(RET_CHUNK * log_g)[:, None, None], (RET_HEADS, 1, RET_DV))
    inv = ROPE_BASE ** (-jnp.arange(0, RET_DK, 2, dtype=F32) / RET_DK)
    ang = jnp.arange(seq, dtype=F32)[:, None] * inv[None, :]
    cos, sin = jnp.cos(ang), jnp.sin(ang)
    reps = LANES // RET_DK
    cos_t = jnp.tile(jnp.concatenate([cos, cos], axis=1), (1, reps))
    sin_t = jnp.tile(jnp.concatenate([-sin, sin], axis=1), (1, reps))
    return decay, q_scale, k_scale, chunk_decay, cos_t, sin_t


EVEN_TM = 512


def _even_in_kernel(tiles_per_seq, x_ref, g_ref, w_ref, cos_ref, sin_ref, qs_ref, ks_ref, dec_ref,
                    cd_ref, rn_ref, ya_ref, q4_ref, q16_ref, k4_ref, k16_ref, v4_ref, v16_ref,
                    slab_ref, slab4_ref, st_ref):
    tm = x_ref.shape[0]
    chunk = RET_CHUNK

    @pl.when(pl.program_id(0) % tiles_per_seq == 0)
    def _():
        st_ref[...] = jnp.zeros_like(st_ref)

    x = x_ref[...]
    ms = jnp.mean(x * x, axis=-1, keepdims=True)
    hn = (x * lax.rsqrt(ms + EPS) * g_ref[...]).astype(BF16)

    def proj(c):
        return jnp.dot(hn, w_ref[:, c * B_W:(c + 1) * B_W], preferred_element_type=F32)

    cos = cos_ref[...]
    sin = sin_ref[...]
    lane = lax.broadcasted_iota(jnp.int32, (tm, LANES), 1)
    half = RET_DK // 2
    first_half = (lane % RET_DK) < half

    def rot(t):
        partner = jnp.where(first_half, pltpu.roll(t, LANES - half, 1), pltpu.roll(t, half, 1))
        return t * cos + partner * sin

    groups = [slice(s * LANES, (s + 1) * LANES) for s in range(B_W // LANES)]

    r = proj(0)
    q_rot = [rot(r[:, sl]) for sl in groups]
    q = [t.astype(BF16) for t in q_rot]
    q_in = [(t * qs_ref[:, sl]).astype(BF16) for t, sl in zip(q_rot, groups)]
    r = proj(1)
    k_rot = [rot(r[:, sl]) * (RET_DK ** -0.5) for sl in groups]
    k = [t.astype(BF16) for t in k_rot]
    k_out = [(t * ks_ref[:, sl]).astype(BF16) for t, sl in zip(k_rot, groups)]
    v = jnp.concatenate([proj(2), proj(3)], axis=1).astype(BF16)
    gate = rn_ref[...] * _silu(jnp.concatenate([proj(4), proj(5)], axis=1))

    def dilated_pieces(c, slot, r4_ref, r16_ref, scale):
        def project():
            r = proj(c)
            if scale is not None:
                r = r * scale
            for s, sl in enumerate(groups):
                slab_ref[slot, s] = r[:, sl]

        mid, wide = DIL_MID, DIL_WIDE
        sub = wide // mid
        per = tm // mid

        def by4():
            for s, sl in enumerate(groups):
                for rho in range(mid):
                    t = slab_ref[slot, s, pl.ds(rho, per, stride=mid), :]
                    r4_ref[rho, :, sl] = t.astype(BF16)
                    slab4_ref[slot, s, rho * per:(rho + 1) * per, :] = t

        def by16():
            for s, sl in enumerate(groups):
                for rho in range(mid):
                    for c in range(sub):
                        r16_ref[rho + mid * c, :, sl] = slab4_ref[
                            slot, s, pl.ds(rho * per + c, tm // wide, stride=sub), :].astype(BF16)

        return [project, by4, by16]

    fill = iter(dilated_pieces(6, 0, q4_ref, q16_ref, DIL_DH ** -0.5)
                + dilated_pieces(7, 1, k4_ref, k16_ref, None)
                + dilated_pieces(8, 2, v4_ref, v16_ref, None))

    head0 = lax.broadcasted_iota(jnp.int32, (chunk, LANES), 1) < RET_DK
    heads = range(RET_HEADS)
    hs = [slice(h * RET_DV, (h + 1) * RET_DV) for h in heads]
    st = [st_ref[h] for h in heads]
    for ci in range(tm // chunk):
        rows = slice(ci * chunk, (ci + 1) * chunk)
        mask = [head0 if h % 2 == 0 else jnp.logical_not(head0) for h in heads]
        scores = [lax.dot_general(jnp.where(mask[h], q[h // 2][rows], 0), k[h // 2][rows], NT_DIMS,
                                  preferred_element_type=F32) * dec_ref[h] for h in heads]
        _emit(fill, 1)
        intra = [jnp.dot(scores[h].astype(BF16), v[rows, hs[h]], preferred_element_type=F32)
                 for h in heads]
        inter = [jnp.dot(jnp.where(mask[h], q_in[h // 2][rows], 0), st[h].astype(BF16),
                         preferred_element_type=F32) for h in heads]
        st = [st[h] * cd_ref[h] + lax.dot_general(k_out[h // 2][rows], v[rows, hs[h]], TN_DIMS,
                                                  preferred_element_type=F32) for h in heads]
        _emit(fill, 1)
        for h in heads:
            y = intra[h] + inter[h]
            yc = y - jnp.mean(y, axis=-1, keepdims=True)
            yn = yc * lax.rsqrt(jnp.mean(yc * yc, axis=-1, keepdims=True) + EPS)
            ya_ref[rows, hs[h]] = (yn * gate[rows, hs[h]]).astype(BF16)
        _emit(fill, 1)
    for run in fill:
        run()
    for h in heads:
        st_ref[h] = st[h]


def _even_in(x, g, w, ret_norm, seq, riders):
    tokens, d = x.shape
    bn = tokens // seq
    tm = EVEN_TM
    tps = seq // tm
    decay, q_scale, k_scale, chunk_decay, cos_t, sin_t = _retention_tables(seq)
    q_scale = jnp.tile(q_scale, (tm // RET_CHUNK, 1))
    k_scale = jnp.tile(k_scale, (tm // RET_CHUNK, 1))
    const = lambda shape: pl.BlockSpec(shape, lambda i: (0,) * len(shape), pipeline_mode=pl.Buffered(1))
    flat = lambda w_: pl.BlockSpec((tm, w_), lambda i: (i, 0))
    by = lambda r: pl.BlockSpec((None, r, tm // r, B_W), lambda i: (i // tps, 0, i % tps, 0))
    sds = jax.ShapeDtypeStruct
    lay = [sds((bn, r, seq // r, B_W), BF16) for r in (DIL_MID, DIL_WIDE)]
    steps = tokens // tm
    in_specs = [
        flat(d), const((1, d)), const(w.shape),
        pl.BlockSpec((tm, LANES), lambda i: (i % tps, 0)),
        pl.BlockSpec((tm, LANES), lambda i: (i % tps, 0)),
        const((tm, A_QK)), const((tm, A_QK)),
        const(decay.shape), const(chunk_decay.shape), const((1, A_V)),
    ]
    out_specs = [flat(A_V)] + [by(DIL_MID), by(DIL_WIDE)] * 3
    rider_in, rider_out, rider_shapes = _rider_specs(riders, steps)
    return pl.pallas_call(
        _with_riders(functools.partial(_even_in_kernel, tps), len(in_specs), len(out_specs), len(riders)),
        grid=(steps,),
        in_specs=in_specs + rider_in,
        out_specs=out_specs + rider_out,
        out_shape=[sds((tokens, A_V), BF16)] + lay * 3 + rider_shapes,
        scratch_shapes=[pltpu.VMEM((3, B_W // LANES, tm, LANES), F32),
                        pltpu.VMEM((3, B_W // LANES, tm, LANES), F32),
                        pltpu.VMEM((RET_HEADS, 2 * RET_DK, RET_DV), F32)],
        compiler_params=_params("arbitrary"),
        name="even_in",
    )(x, g, w, cos_t, sin_t, q_scale, k_scale, decay, chunk_decay, ret_norm, *[a for a, _ in riders])


def _rel_bucket(dist):
    max_exact = REL_BUCKETS // 2
    d = dist.astype(F32)
    large = max_exact + (jnp.log(jnp.maximum(d, 1.0) / max_exact) / math.log(REL_MAX_DIST / max_exact)
                         * (REL_BUCKETS - max_exact)).astype(jnp.int32)
    large = jnp.minimum(large, REL_BUCKETS - 1)
    return jnp.where(dist < max_exact, dist, large)


DIL_BASE = DIL_MID
DIL_UNROLL = 8


def _dilated_buckets():
    lb = DIL_BLOCK
    slot = jnp.arange(lb)
    per = lb // DIL_BASE
    regrouped = DIL_BASE * (slot % per) + slot // per
    tables = []
    for (window, dilation), token in zip(DIL_BRANCHES, (regrouped, slot, slot)):
        a_idx = token[:, None]
        c_idx = jnp.concatenate([token, lb + token])[None, :]
        dist = lb + a_idx - c_idx
        valid = (dist >= 0) & (dist <= window // dilation)
        tables.append(jnp.where(valid, _rel_bucket(jnp.maximum(dist, 0) * dilation), -1))
    return jnp.stack(tables).astype(jnp.int32)


def _dilated_kernel(rb_ref, bucket_ref, q4_ref, k4_ref, v4_ref, q16_ref, k16_ref, v16_ref, y_ref,
                    bias_ref, acc_ref, m_ref, l_ref):
    lb = DIL_BLOCK
    seq = y_ref.shape[0]
    pair = pl.program_id(1)

    @pl.when(pl.program_id(0) == 0)
    def _():
        for br in range(len(DIL_BRANCHES)):
            bucket = bucket_ref[br]
            for a in range(2):
                h = 2 * pair + a
                t = jnp.full((lb, 2 * lb), MASK_VALUE, F32)
                for j in range(REL_BUCKETS):
                    t = jnp.where(bucket == j, rb_ref[j, h], t)
                bias_ref[br * DIL_HEADS + h] = t

    lane = lax.broadcasted_iota(jnp.int32, (lb, LANES), 1)
    head0 = lane < DIL_DH

    def attend(br, blocks):
        masks = (head0, jnp.logical_not(head0))
        both = [bias_ref[br * DIL_HEADS + 2 * pair + a] for a in range(2)]
        cur_only = [t[:, lb:2 * lb] for t in both]
        bias = [cur_only if kp is None else both for _, kp, _, _, _ in blocks]
        ks = [kc if kp is None else jnp.concatenate([kp, kc], axis=0) for _, kp, kc, _, _ in blocks]
        vs = [vc if vp is None else jnp.concatenate([vp, vc], axis=0) for _, _, _, vp, vc in blocks]
        scores = [[lax.dot_general(jnp.where(masks[a], blk[0], 0), k, NT_DIMS,
                                   preferred_element_type=F32) + b[a] for a in range(2)]
                  for blk, k, b in zip(blocks, ks, bias)]
        maxes = [[jnp.max(s, axis=-1, keepdims=True) for s in pair_s] for pair_s in scores]
        exps = [[jnp.exp(s - m) for s, m in zip(pair_s, pair_m)] for pair_s, pair_m in zip(scores, maxes)]
        sums = [[jnp.sum(e, axis=-1, keepdims=True) for e in pair_e] for pair_e in exps]
        outs = [[jnp.dot(e.astype(BF16), v, preferred_element_type=F32) for e in pair_e]
                for pair_e, v in zip(exps, vs)]
        return [tuple(jnp.where(head0, x[0], x[1]) for x in (o, m, l))
                for o, m, l in zip(outs, maxes, sums)]

    def aligned(x, m):
        return x if isinstance(x, int) else pl.multiple_of(x, m)

    def is_first(n):
        return isinstance(n, int) and n == 0

    def sweep(count, item_of, run, extra=()):
        full = count // DIL_UNROLL
        if full == 1:
            run([item_of(j) for j in range(DIL_UNROLL)])
        elif full > 1:
            def body(it, carry):
                run([item_of(it * DIL_UNROLL + j) for j in range(DIL_UNROLL)])
                return carry
            lax.fori_loop(0, full, body, 0)
        rest = [item_of(j) for j in range(full * DIL_UNROLL, count)] + list(extra)
        for at in range(0, len(rest), DIL_UNROLL):
            run(rest[at:at + DIL_UNROLL])

    base = DIL_BASE
    per = lb // base
    class_rows = seq // base

    def unit_blocks(ns):
        blocks, dests = [], []
        for n in ns:
            r0 = aligned(n * per, per)
            gather = lambda ref, at: jnp.concatenate([ref[rho, pl.ds(at, per), :] for rho in range(base)],
                                                     axis=0)
            if is_first(n):
                blocks.append((gather(q4_ref, r0), None, gather(k4_ref, r0), None, gather(v4_ref, r0)))
            else:
                rp = aligned((n - 1) * per, per)
                blocks.append((gather(q4_ref, r0), gather(k4_ref, rp), gather(k4_ref, r0),
                               gather(v4_ref, rp), gather(v4_ref, r0)))
            dests.append(r0)
        for r0, (acc, m, l) in zip(dests, attend(0, blocks)):
            for rho in range(base):
                rows = pl.ds(rho * class_rows + r0, per)
                piece = slice(rho * per, (rho + 1) * per)
                acc_ref[rows, :] = acc[piece]
                m_ref[rows, :] = m[piece]
                l_ref[rows, :] = l[piece]

    sweep(seq // lb - 1, lambda j: 1 + j, unit_blocks, extra=[0])

    def dilated_blocks(br, r, q_ref, k_ref, v_ref, ids):
        sub = r // base
        blocks, dests = [], []
        for rho, n in ids:
            cur = pl.ds(aligned(n * lb, lb), lb)
            if is_first(n):
                blocks.append((q_ref[rho, cur, :], None, k_ref[rho, cur, :], None, v_ref[rho, cur, :]))
            else:
                prev = pl.ds(aligned((n - 1) * lb, lb), lb)
                blocks.append((q_ref[rho, cur, :], k_ref[rho, prev, :], k_ref[rho, cur, :],
                               v_ref[rho, prev, :], v_ref[rho, cur, :]))
            start = (rho % base) * class_rows + n * (lb * sub) + rho // base
            dests.append(pl.ds(start, lb, stride=sub) if sub > 1 else pl.ds(aligned(start, lb), lb))
        for rows, (acc, m, l) in zip(dests, attend(br, blocks)):
            m_old = m_ref[rows, :]
            m_new = jnp.maximum(m_old, m)
            c_old = jnp.exp(m_old - m_new)
            c_cur = jnp.exp(m - m_new)
            acc_ref[rows, :] = acc_ref[rows, :] * c_old + acc * c_cur
            l_ref[rows, :] = l_ref[rows, :] * c_old + l * c_cur
            m_ref[rows, :] = m_new

    for br, refs in ((1, (q4_ref, k4_ref, v4_ref)), (2, (q16_ref, k16_ref, v16_ref))):
        r = DIL_BRANCHES[br][1]
        nb = seq // (r * lb)
        run = lambda ids, br=br, r=r, refs=refs: dilated_blocks(br, r, *refs, ids)
        firsts = [(rho, 0) for rho in range(r)]
        later = lambda j, r=r: (j % r, 1 + j // r)
        if r < DIL_UNROLL:
            sweep(r * (nb - 1), later, run, extra=firsts)
        else:
            sweep(r, lambda j: (j, 0), run)
            sweep(r * (nb - 1), later, run)

    def finish(n, carry):
        for rho in range(base):
            rows = pl.ds(pl.multiple_of(rho * class_rows + n * lb, lb), lb)
            m_ref[pl.ds(n * (lb * base) + rho, lb, stride=base), :] = acc_ref[rows, :] / l_ref[rows, :]
        rows = pl.ds(pl.multiple_of(n * (lb * base), lb * base), lb * base)
        y_ref[rows, :] = m_ref[rows, :].astype(y_ref.dtype)
        return carry

    lax.fori_loop(0, class_rows // lb, finish, 0, unroll=2)


def _dilated(layouts, rel_bias, bn, seq):
    lb = DIL_BLOCK
    assert all(w // r <= lb and seq % (r * lb) == 0 for w, r in DIL_BRANCHES)
    assert DIL_BRANCHES[0][1] == 1 and DIL_WIDE % DIL_MID == 0
    nat = pl.BlockSpec((None, seq, LANES), lambda b, p: (b, 0, p))
    by = lambda r: pl.BlockSpec((None, r, seq // r, LANES), lambda b, p: (b, 0, 0, p))
    nbr = len(DIL_BRANCHES)
    return pl.pallas_call(
        _dilated_kernel,
        grid=(bn, DIL_HEADS // 2),
        in_specs=[pl.BlockSpec(memory_space=pltpu.SMEM),
                  pl.BlockSpec((nbr, lb, 2 * lb), lambda b, p: (0, 0, 0))]
        + [by(DIL_MID)] * 3 + [by(DIL_WIDE)] * 3,
        out_specs=nat,
        out_shape=jax.ShapeDtypeStruct((bn, seq, B_W), BF16),
        scratch_shapes=[pltpu.VMEM((nbr * DIL_HEADS, lb, 2 * lb), F32),
                        pltpu.VMEM((seq, LANES), F32), pltpu.VMEM((seq, LANES), F32),
                        pltpu.VMEM((seq, LANES), F32)],
        compiler_params=_params("arbitrary", "arbitrary"),
        name="dilated",
    )(rel_bias.astype(F32), _dilated_buckets(), *layouts)


FFN_TM = 512
FFN_R = 8
FFN_CHUNK = 256
FFN_HEAD_COLS = 512
FFN_DENSE_CHUNKS = 5


def _ffn_kernel(ntiles, tiles_per_seq, final, nparts, *refs):
    x_ref, y_refs, wo_refs = refs[0], refs[1:1 + nparts], refs[1 + nparts:1 + 2 * nparts]
    (g_ref, wup_ref, cw_ref, cb_ref, wd_ref, fg_ref, zero_ref, where_ref, o_ref, slab_in, slab_out,
     hn_cur, hn_next, res_ref, act_ref, down_ref, halo_ref) = refs[1 + 2 * nparts:]
    i = pl.program_id(0)
    tm = o_ref.shape[0]
    rows = tm // FFN_R
    nslab = D_MODEL // LANES
    ck = FFN_CHUNK
    group = lambda g: slice(g * rows, (g + 1) * rows)
    lanes = lambda s: slice(s * LANES, (s + 1) * LANES)
    anywhere = pl.multiple_of(where_ref[0], BF16_SUBLANES)

    def head_pieces():
        wide = FFN_HEAD_COLS
        per = wide // LANES

        def project(p):
            def run():
                cols = slice(p * wide, (p + 1) * wide)
                h = x_ref[:, cols]
                for y_ref, wo_ref in zip(y_refs, wo_refs):
                    h = h + jnp.dot(y_ref[...], wo_ref[:, cols], preferred_element_type=F32)
                for s in range(per):
                    slab_in[per * p + s] = h[:, s * LANES:(s + 1) * LANES]
                return slab_in[per * p + per - 1, pl.ds(anywhere, SUBLANES), :]
            return run

        def normalise(g):
            def run():
                x = jnp.concatenate([slab_in[s, pl.ds(g, rows, stride=FFN_R), :] for s in range(nslab)],
                                    axis=1)
                res_ref[i % 2, group(g), :] = x
                ms = jnp.mean(x * x, axis=-1, keepdims=True)
                hn_next[group(g), :] = (x * lax.rsqrt(ms + EPS) * g_ref[...]).astype(BF16)
                return hn_next[pl.ds(anywhere, BF16_SUBLANES), 0:LANES].astype(F32)[0:SUBLANES]
            return run

        return [project(p) for p in range(D_MODEL // wide)] + [normalise(g) for g in range(FFN_R)]

    def tail_pieces():
        def unpermute(g):
            def run():
                for s in range(nslab):
                    slab_out[s, pl.ds(g, rows, stride=FFN_R), :] = down_ref[group(g), lanes(s)]
                return slab_out[nslab - 1, pl.ds(anywhere, SUBLANES), :]
            return run

        def store(r0, r1):
            def run():
                y = jnp.concatenate([slab_out[s, r0:r1, :] for s in range(nslab)], axis=1)
                if final:
                    ms = jnp.mean(y * y, axis=-1, keepdims=True)
                    y = y * lax.rsqrt(ms + EPS) * fg_ref[...]
                o_ref[r0:r1, :] = y
                return o_ref[pl.ds(anywhere, SUBLANES), 0:LANES]
            return run

        return ([unpermute(g) for g in range(FFN_R)]
                + [store(g * rows, (g + 1) * rows) for g in range(FFN_R)])

    @pl.when(i == 0)
    def _():
        for run in head_pieces():
            run()
        down_ref[...] = jnp.zeros_like(down_ref)
        halo_ref[...] = jnp.zeros_like(halo_ref)

    @pl.when(jnp.logical_and(i >= 1, i <= ntiles))
    def _():
        hn_cur[...] = hn_next[...]
        fill = iter(tail_pieces() + head_pieces())
        seq_start = (i - 1) % tiles_per_seq == 0
        first_row = lax.broadcasted_iota(jnp.int32, (rows, ck), 0) == 0

        def shift_down(block, fill_row):
            return jnp.where(first_row, fill_row, pltpu.roll(block, 1, 0))

        zero_bits = zero_ref[...]

        def conv(cols, slot, tokens):
            u = jnp.dot(hn_cur[...], wup_ref[:, cols], preferred_element_type=F32)
            ug = [u[group(g), :] for g in range(FFN_R)]
            tails = jnp.where(seq_start, 0.0, halo_ref[slot])
            prev2 = shift_down(ug[FFN_R - 2], tails[SUBLANES - 1:SUBLANES, :])
            prev1 = shift_down(ug[FFN_R - 1], tails[2 * SUBLANES - 1:2 * SUBLANES, :])
            halo_ref[slot] = jnp.concatenate(
                [ug[FFN_R - 2][rows - SUBLANES:, :], ug[FFN_R - 1][rows - SUBLANES:, :]], axis=0)
            back1 = [prev1] + ug[:FFN_R - 1]
            back2 = [prev2, prev1] + ug[:FFN_R - 2]
            cw = cw_ref[:, cols]
            cb = cb_ref[:, cols]
            if tokens:
                cb = cb + jnp.tile(_anchor(tokens, zero_bits)[0:1, :], (1, ck // LANES))
            return [cw[0:1, :] * back2[g] + cw[1:2, :] * back1[g] + cw[2:3, :] * ug[g] + cb
                    for g in range(FFN_R)]

        tokens = []
        nchunk = D_FF // ck
        for c in range(nchunk):
            per_point = 2 if c < FFN_DENSE_CHUNKS else 1
            gate = conv(slice(c * ck, (c + 1) * ck), 2 * c, tokens)
            tokens = _emit(fill, per_point)
            value = conv(slice(D_FF + c * ck, D_FF + (c + 1) * ck), 2 * c + 1, tokens)
            tokens = _emit(fill, per_point)
            for g in range(FFN_R):
                act_ref[group(g), c * ck:(c + 1) * ck] = (_silu(gate[g]) * value[g]).astype(BF16)
        down_ref[...] = res_ref[(i - 1) % 2] + jnp.dot(act_ref[...], wd_ref[...],
                                                       preferred_element_type=F32)
        for run in fill:
            run()

    @pl.when(i == ntiles + 1)
    def _():
        for run in tail_pieces():
            run()


def _ffn(x, mixed, g, w_up, conv_w, conv_b, w_down, final_g, seq, final, riders=()):
    m = x.shape[0]
    tm = FFN_TM
    assert D_FF % FFN_CHUNK == 0 and tm % (FFN_R * 2 * SUBLANES) == 0 and seq % tm == 0
    const = lambda shape, at=(0, 0): pl.BlockSpec(shape, lambda i: at, pipeline_mode=pl.Buffered(1))
    ntiles = m // tm
    tile = lambda width, at: pl.BlockSpec((tm, width), lambda i: (at(i), 0))
    head_tile = lambda i: jnp.minimum(i, ntiles - 1)
    in_specs = (
        [tile(D_MODEL, head_tile)]
        + [tile(y.shape[1], head_tile) for y, _, _ in mixed]
        + [const((y.shape[1], D_MODEL), (row0 // y.shape[1], 0)) for y, _, row0 in mixed]
        + [const((1, D_MODEL)),
           const((D_MODEL, 2 * D_FF)), const((CONV_WIDTH, 2 * D_FF)), const((1, 2 * D_FF)),
           const((D_FF, D_MODEL)),
           const((1, D_MODEL)),
           const((SUBLANES, LANES)),
           pl.BlockSpec(memory_space=pltpu.SMEM)])
    rider_in, rider_out, rider_shapes = _rider_specs(riders, ntiles)
    return pl.pallas_call(
        _with_riders(functools.partial(_ffn_kernel, ntiles, seq // tm, final, len(mixed)),
                     len(in_specs), 1, len(riders)),
        grid=(ntiles + 2,),
        in_specs=in_specs + rider_in,
        out_specs=[tile(D_MODEL, lambda i: jnp.maximum(i - 2, 0))] + rider_out,
        out_shape=[jax.ShapeDtypeStruct((m, D_MODEL), F32)] + rider_shapes,
        scratch_shapes=[
            pltpu.VMEM((D_MODEL // LANES, tm, LANES), F32),
            pltpu.VMEM((D_MODEL // LANES, tm, LANES), F32),
            pltpu.VMEM((tm, D_MODEL), BF16),
            pltpu.VMEM((tm, D_MODEL), BF16),
            pltpu.VMEM((2, tm, D_MODEL), F32),
            pltpu.VMEM((tm, D_FF), BF16),
            pltpu.VMEM((tm, D_MODEL), F32),
            pltpu.VMEM((2 * D_FF // FFN_CHUNK, 2 * SUBLANES, FFN_CHUNK), F32),
        ],
        compiler_params=_params("arbitrary"),
        name="conv_ffn",
    )(x, *[y for y, _, _ in mixed], *[wo for _, wo, _ in mixed], g, w_up, conv_w.astype(F32),
      conv_b.astype(F32).reshape(1, -1), w_down, final_g, jnp.zeros((SUBLANES, LANES), jnp.int32),
      jnp.zeros((1,), jnp.int32), *[a for a, _ in riders])


HGRN_GROUP = 128
HGRN_TM = 256
HGRN_PROJ_COLS = 256
HGRN_EMIT = 2


def _hgrn_groups(proj_ref, row_slices, lower_all, gain, causal, tri, st, fill):
    c = HGRN_CHUNK
    heads = range(HGRN_HEADS)
    groups = range(len(row_slices))
    hk = [slice(h * HGRN_DK, (h + 1) * HGRN_DK) for h in heads]
    hv = [slice(h * HGRN_DV, (h + 1) * HGRN_DV) for h in heads]
    chunks = [slice(n * c, (n + 1) * c) for n in range(HGRN_GROUP // c)]

    q = [_silu(proj_ref[rows, 0:C_K]) for rows in row_slices]
    fg = [lower_all + (1.0 - lower_all) * jax.nn.sigmoid(proj_ref[rows, C_K:2 * C_K])
          for rows in row_slices]
    k = [1.0 - f for f in fg]
    split = []
    for f in fg:
        log_f = jnp.log(f)
        hi = log_f.astype(BF16)
        rem = log_f - hi.astype(F32)
        mid = rem.astype(BF16)
        split.append(jnp.concatenate([hi, mid, (rem - mid.astype(F32)).astype(BF16)], axis=1))
    _emit(fill, HGRN_EMIT)
    parts = [jnp.dot(tri, s, preferred_element_type=F32) for s in split]
    b = [p[:, 0:C_K] + p[:, C_K:2 * C_K] + p[:, 2 * C_K:3 * C_K] for p in parts]
    v = [proj_ref[rows, 2 * C_K:2 * C_K + C_V].astype(BF16) for rows in row_slices]
    q_t = [(q[g] * jnp.exp(b[g])).astype(BF16) for g in groups]
    k_t = [(k[g] * jnp.exp(-b[g])).astype(BF16) for g in groups]
    _emit(fill, HGRN_EMIT)
    scores = [[lax.dot_general(q_t[g][:, hk[h]], k_t[g][:, hk[h]], NT_DIMS, preferred_element_type=F32)
               for h in heads] for g in groups]
    scores = [[jnp.where(causal, s, 0.0).astype(BF16) for s in per_group] for per_group in scores]
    intra = [[jnp.dot(scores[g][h], v[g][:, hv[h]], preferred_element_type=F32) for h in heads]
             for g in groups]
    _emit(fill, HGRN_EMIT)
    b_last = [[b[g][cs.stop - 1:cs.stop, :] for cs in chunks] for g in groups]
    k_end = [[(k[g][cs, :] * jnp.exp(bl - b[g][cs, :])).astype(BF16) for cs, bl in zip(chunks, b_last[g])]
             for g in groups]
    decay = [[jnp.exp(bl) for bl in b_last[g]] for g in groups]
    gate = [gain * _silu(proj_ref[rows, 2 * C_K + C_V:2 * C_K + 2 * C_V]) for rows in row_slices]
    _emit(fill, HGRN_EMIT)
    st = list(st)
    outs = []
    for g in groups:
        inter = [[] for _ in heads]
        for n, cs in enumerate(chunks):
            for h in heads:
                inter[h].append(lax.dot_general(q_t[g][cs, hk[h]], st[h].astype(BF16), NT_DIMS,
                                                preferred_element_type=F32))
                st[h] = st[h] * decay[g][n][:, hk[h]] + lax.dot_general(
                    v[g][cs, hv[h]], k_end[g][n][:, hk[h]], TN_DIMS, preferred_element_type=F32)
        _emit(fill, HGRN_EMIT)
        per_head = []
        for h in heads:
            y = intra[g][h] + jnp.concatenate(inter[h], axis=0)
            yn = y * lax.rsqrt(jnp.mean(y * y, axis=-1, keepdims=True) + EPS)
            per_head.append((yn * gate[g][:, hv[h]]).astype(BF16))
        outs.append(per_head)
        _emit(fill, HGRN_EMIT)
    return outs, st


def _hgrn_layer_kernel(layer, tiles_per_seq, h0_ref, hnext_ref, g_ref, w_ref, lbp_ref, on_ref, o_ref,
                       proj_a, proj_b, st_ref):
    t = pl.program_id(0)
    tm = o_ref.shape[0]
    grp = HGRN_GROUP

    def projection(h_ref, dst):
        x = h_ref[...]
        ms = jnp.mean(x * x, axis=-1, keepdims=True)
        hn = (x * lax.rsqrt(ms + EPS) * g_ref[...]).astype(BF16)

        def piece(c):
            cols = slice(c * HGRN_PROJ_COLS, (c + 1) * HGRN_PROJ_COLS)

            def run():
                dst[:, cols] = jnp.dot(hn, w_ref[:, cols], preferred_element_type=F32)
            return run

        return [piece(c) for c in range(ODD_IN // HGRN_PROJ_COLS)]

    @pl.when(t == 0)
    def _():
        for run in projection(h0_ref, proj_a):
            run()

    @pl.when(t % tiles_per_seq == 0)
    def _():
        st_ref[...] = jnp.zeros_like(st_ref)

    def step(src, dst):
        fill = iter(projection(hnext_ref, dst))
        lbp = lbp_ref[...]
        ex = jnp.exp(lbp - jnp.max(lbp, axis=0, keepdims=True))
        sm = ex / jnp.sum(ex, axis=0, keepdims=True)
        lower_all = jnp.sum(sm[0:layer + 1, :], axis=0, keepdims=True) - sm[0:1, :]
        row = lax.broadcasted_iota(jnp.int32, (grp, grp), 0)
        colm = lax.broadcasted_iota(jnp.int32, (grp, grp), 1)
        causal = jnp.logical_and(row // HGRN_CHUNK == colm // HGRN_CHUNK, row >= colm)
        tri = jnp.where(causal, 1.0, 0.0).astype(BF16)
        gain = on_ref[...]
        st = [st_ref[h] for h in range(HGRN_HEADS)]
        row_slices = [slice(r0, r0 + grp) for r0 in range(0, tm, grp)]
        outs, st = _hgrn_groups(src, row_slices, lower_all, gain, causal, tri, st, fill)
        for rows, per_head in zip(row_slices, outs):
            for h, y in enumerate(per_head):
                o_ref[rows, h * HGRN_DV:(h + 1) * HGRN_DV] = y
        for run in fill:
            run()
        for h in range(HGRN_HEADS):
            st_ref[h] = st[h]

    @pl.when(t % 2 == 0)
    def _():
        step(proj_a, proj_b)

    @pl.when(t % 2 == 1)
    def _():
        step(proj_b, proj_a)


def _hgrn_layer(h, g, w, hgrn_lb, out_norm, layer, seq):
    tokens, d = h.shape
    tm = HGRN_TM
    ntiles = tokens // tm
    const = lambda shape: pl.BlockSpec(shape, lambda t: (0, 0), pipeline_mode=pl.Buffered(1))
    return pl.pallas_call(
        functools.partial(_hgrn_layer_kernel, layer, seq // tm),
        grid=(ntiles,),
        in_specs=[
            const((tm, d)),
            pl.BlockSpec((tm, d), lambda t: (jnp.minimum(t + 1, ntiles - 1), 0)),
            const((1, d)), const(w.shape), const(hgrn_lb.shape), const((1, C_V)),
        ],
        out_specs=pl.BlockSpec((tm, C_V), lambda t: (t, 0)),
        out_shape=jax.ShapeDtypeStruct((tokens, C_V), BF16),
        scratch_shapes=[pltpu.VMEM((tm, ODD_IN), F32), pltpu.VMEM((tm, ODD_IN), F32),
                        pltpu.VMEM((HGRN_HEADS, HGRN_DV, HGRN_DK), F32)],
        compiler_params=_params("arbitrary"),
        name="hgrn_layer",
    )(h, h, g, w, hgrn_lb, out_norm)


def kernel(x, even_w_in, even_w_out, ret_norm, rel_bias, odd_w_in, odd_w_out, hgrn_lb, hgrn_norm,
           mix_norm, ffn_norm, ffn_w_up, ffn_conv_w, ffn_conv_b, ffn_w_down, final_norm):
    bn, seq, d = x.shape
    tokens = bn * seq
    row = lambda a: a.reshape(1, -1).astype(F32)
    h = x.reshape(tokens, d)
    final_g = row(final_norm)

    outs = _even_in(h, row(mix_norm[0]), even_w_in[0].astype(BF16), row(ret_norm[0]), seq,
                    riders=[(even_w_out, 0), (ffn_w_up, 0), (ffn_w_down, 0)])
    ya, q4, q16, k4, k16, v4, v16, w_out0, w_up0, w_down0 = outs
    yb = _dilated((q4, k4, v4, q16, k16, v16), rel_bias, bn, seq).reshape(tokens, B_W)
    h, w_in1, w_out1, w_up1, w_down1 = _ffn(
        h, [(ya, w_out0, 0), (yb, w_out0, A_V)], row(ffn_norm[0]), w_up0, ffn_conv_w[0],
        ffn_conv_b[0], w_down0, final_g, seq, final=False,
        riders=[(odd_w_in, 0), (odd_w_out, 0), (ffn_w_up, 1), (ffn_w_down, 1)])

    y = _hgrn_layer(h, row(mix_norm[1]), w_in1, hgrn_lb.astype(F32), row(hgrn_norm[0]), 1, seq)
    h, = _ffn(h, [(y, w_out1, 0)], row(ffn_norm[1]), w_up1, ffn_conv_w[1], ffn_conv_b[1], w_down1,
              final_g, seq, final=True)
    return h.reshape(bn, seq, d)
```

```python
import functools
import math

import jax
import jax.numpy as jnp
from jax import lax
from jax.experimental import pallas as pl
from jax.experimental.pallas import tpu as pltpu

F32 = jnp.float32
BF16 = jnp.bfloat16

D_MODEL = 1024
RET_HEADS = 8
RET_DK = 64
RET_DV = 128
RET_CHUNK = 128
ROPE_BASE = 10000.0
DIL_HEADS = 8
DIL_DH = 64
DIL_BRANCHES = ((128, 1), (512, 4), (2048, 16))
DIL_BLOCK = 128
DIL_MID, DIL_WIDE = DIL_BRANCHES[1][1], DIL_BRANCHES[2][1]
HGRN_HEADS = 8
HGRN_DK = 128
HGRN_DV = 128
HGRN_CHUNK = 32
REL_BUCKETS = 32
REL_MAX_DIST = 2048
D_FF = 2816
CONV_WIDTH = 3
EPS = 1e-6

A_QK = RET_HEADS * RET_DK
A_V = RET_HEADS * RET_DV
B_W = DIL_HEADS * DIL_DH
EVEN_IN = 2 * A_QK + 2 * A_V + 3 * B_W
C_K = HGRN_HEADS * HGRN_DK
C_V = HGRN_HEADS * HGRN_DV
ODD_IN = 2 * C_K + 2 * C_V

LANES = 128
SUBLANES = 8
VMEM_LIMIT = 56 * 1024 * 1024
MASK_VALUE = -1e30

NT_DIMS = (((1,), (1,)), ((), ()))
TN_DIMS = (((0,), (0,)), ((), ()))


def _params(*semantics, vmem=VMEM_LIMIT):
    return pltpu.CompilerParams(dimension_semantics=semantics, vmem_limit_bytes=vmem)


def _silu(x):
    return x * jax.nn.sigmoid(x)


def _emit(fill, count):
    tokens = []
    for _ in range(count):
        thunk = next(fill, None)
        if thunk is not None:
            token = thunk()
            if token is not None:
                tokens.append(token)
    return tokens


def _anchor(tokens, zero_bits):
    acc = zero_bits
    for t in tokens:
        acc = acc & pltpu.bitcast(t, jnp.int32)
    return pltpu.bitcast(acc, F32)


BF16_SUBLANES = 2 * SUBLANES


def _rider_specs(weights, steps):
    in_specs, out_specs, out_shapes = [], [], []
    for arr, layer in weights:
        _, rows, cols = arr.shape
        rep = 1
        while (rows * rep) % (steps * BF16_SUBLANES):
            rep *= 2
        slab = rows * rep // steps
        slab_index = lambda i, rep=rep: jnp.minimum(i, steps - 1) // rep
        in_specs.append(pl.BlockSpec((None, slab, cols),
                                     lambda i, layer=layer, at=slab_index: (layer, at(i), 0)))
        out_specs.append(pl.BlockSpec((slab, cols), lambda i, at=slab_index: (at(i), 0)))
        out_shapes.append(jax.ShapeDtypeStruct((rows, cols), BF16))
    return in_specs, out_specs, out_shapes


def _with_riders(body, n_in, n_out, n_riders):
    def kernel_fn(*refs):
        ins, refs = refs[:n_in], refs[n_in:]
        rider_in, refs = refs[:n_riders], refs[n_riders:]
        outs, refs = refs[:n_out], refs[n_out:]
        rider_out, scratch = refs[:n_riders], refs[n_riders:]
        for src, dst in zip(rider_in, rider_out):
            dst[...] = src[...].astype(dst.dtype)
        body(*ins, *outs, *scratch)
    return kernel_fn


def _retention_tables(seq):
    h = jnp.arange(RET_HEADS, dtype=F32)
    log_g = jnp.log1p(-jnp.exp2(-5.0 - h))
    pos = jnp.arange(RET_CHUNK, dtype=F32)
    diff = pos[:, None] - pos[None, :]
    decay = jnp.where(diff >= 0, jnp.exp(jnp.maximum(diff, 0.0)[None] * log_g[:, None, None]), 0.0)
    q_scale = jnp.repeat(jnp.exp((pos + 1.0)[:, None] * log_g[None, :]), RET_DK, axis=1)
    k_scale = jnp.repeat(jnp.exp((RET_CHUNK - 1.0 - pos)[:, None] * log_g[None, :]), RET_DK, axis=1)
    chunk_decay = jnp.broadcast_to(jnp.exp(RET_CHUNK * log_g)[:, None, None], (RET_HEADS, 1, RET_DV))
    inv = ROPE_BASE ** (-jnp.arange(0, RET_DK, 2, dtype=F32) / RET_DK)
    ang = jnp.arange(seq, dtype=F32)[:, None] * inv[None, :]
    cos, sin = jnp.cos(ang), jnp.sin(ang)
    reps = LANES // RET_DK
    cos_t = jnp.tile(jnp.concatenate([cos, cos], axis=1), (1, reps))
    sin_t = jnp.tile(jnp.concatenate([-sin, sin], axis=1), (1, reps))
    return decay, q_scale, k_scale, chunk_decay, cos_t, sin_t


EVEN_TM = 512


def _even_in_kernel(tiles_per_seq, x_ref, g_ref, w_ref, cos_ref, sin_ref, qs_ref, ks_ref, dec_ref,
                    cd_ref, rn_ref, ya_ref, q4_ref, q16_ref, k4_ref, k16_ref, v4_ref, v16_ref,
                    slab_ref, slab4_ref, st_ref, wbf_ref):
    tm = x_ref.shape[0]
    chunk = RET_CHUNK

    @pl.when(pl.program_id(0) % tiles_per_seq == 0)
    def _():
        st_ref[...] = jnp.zeros_like(st_ref)

    @pl.when(pl.program_id(0) == 0)
    def _():
        for c in range(w_ref.shape[1] // B_W):
            cols = slice(c * B_W, (c + 1) * B_W)
            wbf_ref[:, cols] = w_ref[:, cols].astype(BF16)

    x = x_ref[...]
    ms = jnp.mean(x * x, axis=-1, keepdims=True)
    hn = (x * lax.rsqrt(ms + EPS) * g_ref[...]).astype(BF16)

    def proj(c):
        return jnp.dot(hn, wbf_ref[:, c * B_W:(c + 1) * B_W], preferred_element_type=F32)

    cos = cos_ref[...]
    sin = sin_ref[...]
    lane = lax.broadcasted_iota(jnp.int32, (tm, LANES), 1)
    half = RET_DK // 2
    first_half = (lane % RET_DK) < half

    def rot(t):
        partner = jnp.where(first_half, pltpu.roll(t, LANES - half, 1), pltpu.roll(t, half, 1))
        return t * cos + partner * sin

    groups = [slice(s * LANES, (s + 1) * LANES) for s in range(B_W // LANES)]

    r = proj(0)
    q_rot = [rot(r[:, sl]) for sl in groups]
    q = [t.astype(BF16) for t in q_rot]
    q_in = [(t * qs_ref[:, sl]).astype(BF16) for t, sl in zip(q_rot, groups)]
    r = proj(1)
    k_rot = [rot(r[:, sl]) * (RET_DK ** -0.5) for sl in groups]
    k = [t.astype(BF16) for t in k_rot]
    k_out = [(t * ks_ref[:, sl]).astype(BF16) for t, sl in zip(k_rot, groups)]
    v = jnp.concatenate([proj(2), proj(3)], axis=1).astype(BF16)
    gate = rn_ref[...] * _silu(jnp.concatenate([proj(4), proj(5)], axis=1))

    def dilated_pieces(c, slot, r4_ref, r16_ref, scale):
        def project():
            r = proj(c)
            if scale is not None:
                r = r * scale
            for s, sl in enumerate(groups):
                slab_ref[slot, s] = r[:, sl]

        mid, wide = DIL_MID, DIL_WIDE
        sub = wide // mid
        per = tm // mid

        def by4():
            for s, sl in enumerate(groups):
                for rho in range(mid):
                    t = slab_ref[slot, s, pl.ds(rho, per, stride=mid), :]
                    r4_ref[rho, :, sl] = t.astype(BF16)
                    slab4_ref[slot, s, rho * per:(rho + 1) * per, :] = t

        def by16():
            for s, sl in enumerate(groups):
                for rho in range(mid):
                    for c in range(sub):
                        r16_ref[rho + mid * c, :, sl] = slab4_ref[
                            slot, s, pl.ds(rho * per + c, tm // wide, stride=sub), :].astype(BF16)

        return [project, by4, by16]

    fill = iter(dilated_pieces(6, 0, q4_ref, q16_ref, DIL_DH ** -0.5)
                + dilated_pieces(7, 1, k4_ref, k16_ref, None)
                + dilated_pieces(8, 2, v4_ref, v16_ref, None))

    head0 = lax.broadcasted_iota(jnp.int32, (chunk, LANES), 1) < RET_DK
    heads = range(RET_HEADS)
    hs = [slice(h * RET_DV, (h + 1) * RET_DV) for h in heads]
    st = [st_ref[h] for h in heads]
    for ci in range(tm // chunk):
        rows = slice(ci * chunk, (ci + 1) * chunk)
        mask = [head0 if h % 2 == 0 else jnp.logical_not(head0) for h in heads]
        scores = [lax.dot_general(jnp.where(mask[h], q[h // 2][rows], 0), k[h // 2][rows], NT_DIMS,
                                  preferred_element_type=F32) * dec_ref[h] for h in heads]
        _emit(fill, 1)
        intra = [jnp.dot(scores[h].astype(BF16), v[rows, hs[h]], preferred_element_type=F32)
                 for h in heads]
        inter = [jnp.dot(jnp.where(mask[h], q_in[h // 2][rows], 0), st[h].astype(BF16),
                         preferred_element_type=F32) for h in heads]
        st = [st[h] * cd_ref[h] + lax.dot_general(k_out[h // 2][rows], v[rows, hs[h]], TN_DIMS,
                                                  preferred_element_type=F32) for h in heads]
        _emit(fill, 1)
        for h in heads:
            y = intra[h] + inter[h]
            yc = y - jnp.mean(y, axis=-1, keepdims=True)
            yn = yc * lax.rsqrt(jnp.mean(yc * yc, axis=-1, keepdims=True) + EPS)
            ya_ref[rows, hs[h]] = (yn * gate[rows, hs[h]]).astype(BF16)
        _emit(fill, 1)
    for run in fill:
        run()
    for h in heads:
        st_ref[h] = st[h]


def _even_in(x, g, w, ret_norm, seq, riders):
    tokens, d = x.shape
    bn = tokens // seq
    tm = EVEN_TM
    tps = seq // tm
    decay, q_scale, k_scale, chunk_decay, cos_t, sin_t = _retention_tables(seq)
    q_scale = jnp.tile(q_scale, (tm // RET_CHUNK, 1))
    k_scale = jnp.tile(k_scale, (tm // RET_CHUNK, 1))
    const = lambda shape: pl.BlockSpec(shape, lambda i: (0,) * len(shape), pipeline_mode=pl.Buffered(1))
    flat = lambda w_: pl.BlockSpec((tm, w_), lambda i: (i, 0))
    by = lambda r: pl.BlockSpec((None, r, tm // r, B_W), lambda i: (i // tps, 0, i % tps, 0))
    sds = jax.ShapeDtypeStruct
    lay = [sds((bn, r, seq // r, B_W), BF16) for r in (DIL_MID, DIL_WIDE)]
    steps = tokens // tm
    in_specs = [
        flat(d), const((1, d)), const(w.shape),
        pl.BlockSpec((tm, LANES), lambda i: (i % tps, 0)),
        pl.BlockSpec((tm, LANES), lambda i: (i % tps, 0)),
        const((tm, A_QK)), const((tm, A_QK)),
        const(decay.shape), const(chunk_decay.shape), const((1, A_V)),
    ]
    out_specs = [flat(A_V)] + [by(DIL_MID), by(DIL_WIDE)] * 3
    rider_in, rider_out, rider_shapes = _rider_specs(riders, steps)
    return pl.pallas_call(
        _with_riders(functools.partial(_even_in_kernel, tps), len(in_specs), len(out_specs), len(riders)),
        grid=(steps,),
        in_specs=in_specs + rider_in,
        out_specs=out_specs + rider_out,
        out_shape=[sds((tokens, A_V), BF16)] + lay * 3 + rider_shapes,
        scratch_shapes=[pltpu.VMEM((3, B_W // LANES, tm, LANES), F32),
                        pltpu.VMEM((3, B_W // LANES, tm, LANES), F32),
                        pltpu.VMEM((RET_HEADS, 2 * RET_DK, RET_DV), F32),
                        pltpu.VMEM(w.shape, BF16)],
        compiler_params=_params("arbitrary", vmem=62 * 1024 * 1024),
        name="even_in",
    )(x, g, w, cos_t, sin_t, q_scale, k_scale, decay, chunk_decay, ret_norm, *[a for a, _ in riders])


def _rel_bucket(dist):
    max_exact = REL_BUCKETS // 2
    d = dist.astype(F32)
    large = max_exact + (jnp.log(jnp.maximum(d, 1.0) / max_exact) / math.log(REL_MAX_DIST / max_exact)
                         * (REL_BUCKETS - max_exact)).astype(jnp.int32)
    large = jnp.minimum(large, REL_BUCKETS - 1)
    return jnp.where(dist < max_exact, dist, large)


DIL_BASE = DIL_MID
DIL_UNROLL = 8


def _dilated_buckets():
    lb = DIL_BLOCK
    slot = jnp.arange(lb)
    per = lb // DIL_BASE
    regrouped = DIL_BASE * (slot % per) + slot // per
    tables = []
    for (window, dilation), token in zip(DIL_BRANCHES, (regrouped, slot, slot)):
        a_idx = token[:, None]
        c_idx = jnp.concatenate([token, lb + token])[None, :]
        dist = lb + a_idx - c_idx
        valid = (dist >= 0) & (dist <= window // dilation)
        tables.append(jnp.where(valid, _rel_bucket(jnp.maximum(dist, 0) * dilation), -1))
    return jnp.stack(tables).astype(jnp.int32)


def _dilated_kernel(rb_ref, bucket_ref, q4_ref, k4_ref, v4_ref, q16_ref, k16_ref, v16_ref, y_ref,
                    bias_ref, acc_ref, m_ref, l_ref):
    lb = DIL_BLOCK
    seq = y_ref.shape[0]
    pair = pl.program_id(1)

    @pl.when(pl.program_id(0) == 0)
    def _():
        for br in range(len(DIL_BRANCHES)):
            bucket = bucket_ref[br]
            for a in range(2):
                h = 2 * pair + a
                t = jnp.full((lb, 2 * lb), MASK_VALUE, F32)
                for j in range(REL_BUCKETS):
                    t = jnp.where(bucket == j, rb_ref[j, h], t)
                bias_ref[br * DIL_HEADS + h] = t

    lane = lax.broadcasted_iota(jnp.int32, (lb, LANES), 1)
    head0 = lane < DIL_DH

    def attend(br, blocks):
        masks = (head0, jnp.logical_not(head0))
        both = [bias_ref[br * DIL_HEADS + 2 * pair + a] for a in range(2)]
        cur_only = [t[:, lb:2 * lb] for t in both]
        bias = [cur_only if kp is None else both for _, kp, _, _, _ in blocks]
        ks = [kc if kp is None else jnp.concatenate([kp, kc], axis=0) for _, kp, kc, _, _ in blocks]
        vs = [vc if vp is None else jnp.concatenate([vp, vc], axis=0) for _, _, _, vp, vc in blocks]
        scores = [[lax.dot_general(jnp.where(masks[a], blk[0], 0), k, NT_DIMS,
                                   preferred_element_type=F32) + b[a] for a in range(2)]
                  for blk, k, b in zip(blocks, ks, bias)]
        maxes = [[jnp.max(s, axis=-1, keepdims=True) for s in pair_s] for pair_s in scores]
        exps = [[jnp.exp(s - m) for s, m in zip(pair_s, pair_m)] for pair_s, pair_m in zip(scores, maxes)]
        sums = [[jnp.sum(e, axis=-1, keepdims=True) for e in pair_e] for pair_e in exps]
        outs = [[jnp.dot(e.astype(BF16), v, preferred_element_type=F32) for e in pair_e]
                for pair_e, v in zip(exps, vs)]
        return [tuple(jnp.where(head0, x[0], x[1]) for x in (o, m, l))
                for o, m, l in zip(outs, maxes, sums)]

    def aligned(x, m):
        return x if isinstance(x, int) else pl.multiple_of(x, m)

    def is_first(n):
        return isinstance(n, int) and n == 0

    def sweep(count, item_of, run, extra=()):
        full = count // DIL_UNROLL
        if full == 1:
            run([item_of(j) for j in range(DIL_UNROLL)])
        elif full > 1:
            def body(it, carry):
                run([item_of(it * DIL_UNROLL + j) for j in range(DIL_UNROLL)])
                return carry
            lax.fori_loop(0, full, body, 0)
        rest = [item_of(j) for j in range(full * DIL_UNROLL, count)] + list(extra)
        for at in range(0, len(rest), DIL_UNROLL):
            run(rest[at:at + DIL_UNROLL])

    base = DIL_BASE
    per = lb // base
    class_rows = seq // base

    def unit_blocks(ns):
        blocks, dests = [], []
        for n in ns:
            r0 = aligned(n * per, per)
            gather = lambda ref, at: jnp.concatenate([ref[rho, pl.ds(at, per), :] for rho in range(base)],
                                                     axis=0)
            if is_first(n):
                blocks.append((gather(q4_ref, r0), None, gather(k4_ref, r0), None, gather(v4_ref, r0)))
            else:
                rp = aligned((n - 1) * per, per)
                blocks.append((gather(q4_ref, r0), gather(k4_ref, rp), gather(k4_ref, r0),
                               gather(v4_ref, rp), gather(v4_ref, r0)))
            dests.append(r0)
        for r0, (acc, m, l) in zip(dests, attend(0, blocks)):
            for rho in range(base):
                rows = pl.ds(rho * class_rows + r0, per)
                piece = slice(rho * per, (rho + 1) * per)
                acc_ref[rows, :] = acc[piece]
                m_ref[rows, :] = m[piece]
                l_ref[rows, :] = l[piece]

    sweep(seq // lb - 1, lambda j: 1 + j, unit_blocks, extra=[0])

    def dilated_blocks(br, r, q_ref, k_ref, v_ref, ids):
        sub = r // base
        blocks, dests = [], []
        for rho, n in ids:
            cur = pl.ds(aligned(n * lb, lb), lb)
            if is_first(n):
                blocks.append((q_ref[rho, cur, :], None, k_ref[rho, cur, :], None, v_ref[rho, cur, :]))
            else:
                prev = pl.ds(aligned((n - 1) * lb, lb), lb)
                blocks.append((q_ref[rho, cur, :], k_ref[rho, prev, :], k_ref[rho, cur, :],
                               v_ref[rho, prev, :], v_ref[rho, cur, :]))
            start = (rho % base) * class_rows + n * (lb * sub) + rho // base
            dests.append(pl.ds(start, lb, stride=sub) if sub > 1 else pl.ds(aligned(start, lb), lb))
        for rows, (acc, m, l) in zip(dests, attend(br, blocks)):
            m_old = m_ref[rows, :]
            m_new = jnp.maximum(m_old, m)
            c_old = jnp.exp(m_old - m_new)
            c_cur = jnp.exp(m - m_new)
            acc_ref[rows, :] = acc_ref[rows, :] * c_old + acc * c_cur
            l_ref[rows, :] = l_ref[rows, :] * c_old + l * c_cur
            m_ref[rows, :] = m_new

    for br, refs in ((1, (q4_ref, k4_ref, v4_ref)), (2, (q16_ref, k16_ref, v16_ref))):
        r = DIL_BRANCHES[br][1]
        nb = seq // (r * lb)
        run = lambda ids, br=br, r=r, refs=refs: dilated_blocks(br, r, *refs, ids)
        firsts = [(rho, 0) for rho in range(r)]
        later = lambda j, r=r: (j % r, 1 + j // r)
        if r < DIL_UNROLL:
            sweep(r * (nb - 1), later, run, extra=firsts)
        else:
            sweep(r, lambda j: (j, 0), run)
            sweep(r * (nb - 1), later, run)

    def finish(n, carry):
        for rho in range(base):
            rows = pl.ds(pl.multiple_of(rho * class_rows + n * lb, lb), lb)
            m_ref[pl.ds(n * (lb * base) + rho, lb, stride=base), :] = acc_ref[rows, :] / l_ref[rows, :]
        rows = pl.ds(pl.multiple_of(n * (lb * base), lb * base), lb * base)
        y_ref[rows, :] = m_ref[rows, :].astype(y_ref.dtype)
        return carry

    lax.fori_loop(0, class_rows // lb, finish, 0, unroll=2)


def _dilated(layouts, rel_bias, bn, seq):
    lb = DIL_BLOCK
    assert all(w // r <= lb and seq % (r * lb) == 0 for w, r in DIL_BRANCHES)
    assert DIL_BRANCHES[0][1] == 1 and DIL_WIDE % DIL_MID == 0
    nat = pl.BlockSpec((None, seq, LANES), lambda b, p: (b, 0, p))
    by = lambda r: pl.BlockSpec((None, r, seq // r, LANES), lambda b, p: (b, 0, 0, p))
    nbr = len(DIL_BRANCHES)
    return pl.pallas_call(
        _dilated_kernel,
        grid=(bn, DIL_HEADS // 2),
        in_specs=[pl.BlockSpec(memory_space=pltpu.SMEM),
                  pl.BlockSpec((nbr, lb, 2 * lb), lambda b, p: (0, 0, 0))]
        + [by(DIL_MID)] * 3 + [by(DIL_WIDE)] * 3,
        out_specs=nat,
        out_shape=jax.ShapeDtypeStruct((bn, seq, B_W), BF16),
        scratch_shapes=[pltpu.VMEM((nbr * DIL_HEADS, lb, 2 * lb), F32),
                        pltpu.VMEM((seq, LANES), F32), pltpu.VMEM((seq, LANES), F32),
                        pltpu.VMEM((seq, LANES), F32)],
        compiler_params=_params("arbitrary", "arbitrary"),
        name="dilated",
    )(rel_bias.astype(F32), _dilated_buckets(), *layouts)


FFN_TM = 512
FFN_R = 8
FFN_CHUNK = 256
FFN_HEAD_COLS = 512
FFN_DENSE_CHUNKS = 5


def _ffn_kernel(ntiles, tiles_per_seq, final, nparts, *refs):
    x_ref, y_refs, wo_refs = refs[0], refs[1:1 + nparts], refs[1 + nparts:1 + 2 * nparts]
    (g_ref, wup_ref, cw_ref, cb_ref, wd_ref, fg_ref, zero_ref, where_ref, o_ref, slab_in, slab_out,
     hn_cur, hn_next, res_ref, act_ref, down_ref, halo_ref) = refs[1 + 2 * nparts:]
    i = pl.program_id(0)
    tm = o_ref.shape[0]
    rows = tm // FFN_R
    nslab = D_MODEL // LANES
    ck = FFN_CHUNK
    group = lambda g: slice(g * rows, (g + 1) * rows)
    lanes = lambda s: slice(s * LANES, (s + 1) * LANES)
    anywhere = pl.multiple_of(where_ref[0], BF16_SUBLANES)

    def head_pieces():
        wide = FFN_HEAD_COLS
        per = wide // LANES

        def project(p):
            def run():
                cols = slice(p * wide, (p + 1) * wide)
                h = x_ref[:, cols]
                for y_ref, wo_ref in zip(y_refs, wo_refs):
                    h = h + jnp.dot(y_ref[...], wo_ref[:, cols], preferred_element_type=F32)
                for s in range(per):
                    slab_in[per * p + s] = h[:, s * LANES:(s + 1) * LANES]
                return slab_in[per * p + per - 1, pl.ds(anywhere, SUBLANES), :]
            return run

        def normalise(g):
            def run():
                x = jnp.concatenate([slab_in[s, pl.ds(g, rows, stride=FFN_R), :] for s in range(nslab)],
                                    axis=1)
                res_ref[i % 2, group(g), :] = x
                ms = jnp.mean(x * x, axis=-1, keepdims=True)
                hn_next[group(g), :] = (x * lax.rsqrt(ms + EPS) * g_ref[...]).astype(BF16)
                return hn_next[pl.ds(anywhere, BF16_SUBLANES), 0:LANES].astype(F32)[0:SUBLANES]
            return run

        return [project(p) for p in range(D_MODEL // wide)] + [normalise(g) for g in range(FFN_R)]

    def tail_pieces():
        def unpermute(g):
            def run():
                for s in range(nslab):
                    slab_out[s, pl.ds(g, rows, stride=FFN_R), :] = down_ref[group(g), lanes(s)]
                return slab_out[nslab - 1, pl.ds(anywhere, SUBLANES), :]
            return run

        def store(r0, r1):
            def run():
                y = jnp.concatenate([slab_out[s, r0:r1, :] for s in range(nslab)], axis=1)
                if final:
                    ms = jnp.mean(y * y, axis=-1, keepdims=True)
                    y = y * lax.rsqrt(ms + EPS) * fg_ref[...]
                o_ref[r0:r1, :] = y
                return o_ref[pl.ds(anywhere, SUBLANES), 0:LANES]
            return run

        return ([unpermute(g) for g in range(FFN_R)]
                + [store(g * rows, (g + 1) * rows) for g in range(FFN_R)])

    @pl.when(i == 0)
    def _():
        for run in head_pieces():
            run()
        down_ref[...] = jnp.zeros_like(down_ref)
        halo_ref[...] = jnp.zeros_like(halo_ref)

    @pl.when(jnp.logical_and(i >= 1, i <= ntiles))
    def _():
        hn_cur[...] = hn_next[...]
        fill = iter(tail_pieces() + head_pieces())
        seq_start = (i - 1) % tiles_per_seq == 0
        first_row = lax.broadcasted_iota(jnp.int32, (rows, ck), 0) == 0

        def shift_down(block, fill_row):
            return jnp.where(first_row, fill_row, pltpu.roll(block, 1, 0))

        zero_bits = zero_ref[...]

        def conv(cols, slot, tokens):
            u = jnp.dot(hn_cur[...], wup_ref[:, cols], preferred_element_type=F32)
            ug = [u[group(g), :] for g in range(FFN_R)]
            tails = jnp.where(seq_start, 0.0, halo_ref[slot])
            prev2 = shift_down(ug[FFN_R - 2], tails[SUBLANES - 1:SUBLANES, :])
            prev1 = shift_down(ug[FFN_R - 1], tails[2 * SUBLANES - 1:2 * SUBLANES, :])
            halo_ref[slot] = jnp.concatenate(
                [ug[FFN_R - 2][rows - SUBLANES:, :], ug[FFN_R - 1][rows - SUBLANES:, :]], axis=0)
            back1 = [prev1] + ug[:FFN_R - 1]
            back2 = [prev2, prev1] + ug[:FFN_R - 2]
            cw = cw_ref[:, cols]
            cb = cb_ref[:, cols]
            if tokens:
                cb = cb + jnp.tile(_anchor(tokens, zero_bits)[0:1, :], (1, ck // LANES))
            return [cw[0:1, :] * back2[g] + cw[1:2, :] * back1[g] + cw[2:3, :] * ug[g] + cb
                    for g in range(FFN_R)]

        tokens = []
        nchunk = D_FF // ck
        for c in range(nchunk):
            per_point = 2 if c < FFN_DENSE_CHUNKS else 1
            gate = conv(slice(c * ck, (c + 1) * ck), 2 * c, tokens)
            tokens = _emit(fill, per_point)
            value = conv(slice(D_FF + c * ck, D_FF + (c + 1) * ck), 2 * c + 1, tokens)
            tokens = _emit(fill, per_point)
            for g in range(FFN_R):
                act_ref[group(g), c * ck:(c + 1) * ck] = (_silu(gate[g]) * value[g]).astype(BF16)
        down_ref[...] = res_ref[(i - 1) % 2] + jnp.dot(act_ref[...], wd_ref[...],
                                                       preferred_element_type=F32)
        for run in fill:
            run()

    @pl.when(i == ntiles + 1)
    def _():
        for run in tail_pieces():
            run()


def _ffn(x, mixed, g, w_up, conv_w, conv_b, w_down, final_g, seq, final, riders=()):
    m = x.shape[0]
    tm = FFN_TM
    assert D_FF % FFN_CHUNK == 0 and tm % (FFN_R * 2 * SUBLANES) == 0 and seq % tm == 0
    const = lambda shape, at=(0, 0): pl.BlockSpec(shape, lambda i: at, pipeline_mode=pl.Buffered(1))
    ntiles = m // tm
    tile = lambda width, at: pl.BlockSpec((tm, width), lambda i: (at(i), 0))
    head_tile = lambda i: jnp.minimum(i, ntiles - 1)
    in_specs = (
        [tile(D_MODEL, head_tile)]
        + [tile(y.shape[1], head_tile) for y, _, _ in mixed]
        + [const((y.shape[1], D_MODEL), (row0 // y.shape[1], 0)) for y, _, row0 in mixed]
        + [const((1, D_MODEL)),
           const((D_MODEL, 2 * D_FF)), const((CONV_WIDTH, 2 * D_FF)), const((1, 2 * D_FF)),
           const((D_FF, D_MODEL)),
           const((1, D_MODEL)),
           const((SUBLANES, LANES)),
           pl.BlockSpec(memory_space=pltpu.SMEM)])
    rider_in, rider_out, rider_shapes = _rider_specs(riders, ntiles)
    return pl.pallas_call(
        _with_riders(functools.partial(_ffn_kernel, ntiles, seq // tm, final, len(mixed)),
                     len(in_specs), 1, len(riders)),
        grid=(ntiles + 2,),
        in_specs=in_specs + rider_in,
        out_specs=[tile(D_MODEL, lambda i: jnp.maximum(i - 2, 0))] + rider_out,
        out_shape=[jax.ShapeDtypeStruct((m, D_MODEL), F32)] + rider_shapes,
        scratch_shapes=[
            pltpu.VMEM((D_MODEL // LANES, tm, LANES), F32),
            pltpu.VMEM((D_MODEL // LANES, tm, LANES), F32),
            pltpu.VMEM((tm, D_MODEL), BF16),
            pltpu.VMEM((tm, D_MODEL), BF16),
            pltpu.VMEM((2, tm, D_MODEL), F32),
            pltpu.VMEM((tm, D_FF), BF16),
            pltpu.VMEM((tm, D_MODEL), F32),
            pltpu.VMEM((2 * D_FF // FFN_CHUNK, 2 * SUBLANES, FFN_CHUNK), F32),
        ],
        compiler_params=_params("arbitrary"),
        name="conv_ffn",
    )(x, *[y for y, _, _ in mixed], *[wo for _, wo, _ in mixed], g, w_up, conv_w.astype(F32),
      conv_b.astype(F32).reshape(1, -1), w_down, final_g, jnp.zeros((SUBLANES, LANES), jnp.int32),
      jnp.zeros((1,), jnp.int32), *[a for a, _ in riders])


HGRN_GROUP = 128
HGRN_TM = 256
HGRN_PROJ_COLS = 256
HGRN_EMIT = 2


def _hgrn_groups(proj_ref, row_slices, lower_all, gain, causal, tri, st, fill):
    c = HGRN_CHUNK
    heads = range(HGRN_HEADS)
    groups = range(len(row_slices))
    hk = [slice(h * HGRN_DK, (h + 1) * HGRN_DK) for h in heads]
    hv = [slice(h * HGRN_DV, (h + 1) * HGRN_DV) for h in heads]
    chunks = [slice(n * c, (n + 1) * c) for n in range(HGRN_GROUP // c)]

    q = [_silu(proj_ref[rows, 0:C_K]) for rows in row_slices]
    fg = [lower_all + (1.0 - lower_all) * jax.nn.sigmoid(proj_ref[rows, C_K:2 * C_K])
          for rows in row_slices]
    k = [1.0 - f for f in fg]
    split = []
    for f in fg:
        log_f = jnp.log(f)
        hi = log_f.astype(BF16)
        rem = log_f - hi.astype(F32)
        mid = rem.astype(BF16)
        split.append(jnp.concatenate([hi, mid, (rem - mid.astype(F32)).astype(BF16)], axis=1))
    _emit(fill, HGRN_EMIT)
    parts = [jnp.dot(tri, s, preferred_element_type=F32) for s in split]
    b = [p[:, 0:C_K] + p[:, C_K:2 * C_K] + p[:, 2 * C_K:3 * C_K] for p in parts]
    v = [proj_ref[rows, 2 * C_K:2 * C_K + C_V].astype(BF16) for rows in row_slices]
    q_t = [(q[g] * jnp.exp(b[g])).astype(BF16) for g in groups]
    k_t = [(k[g] * jnp.exp(-b[g])).astype(BF16) for g in groups]
    _emit(fill, HGRN_EMIT)
    scores = [[lax.dot_general(q_t[g][:, hk[h]], k_t[g][:, hk[h]], NT_DIMS, preferred_element_type=F32)
               for h in heads] for g in groups]
    scores = [[jnp.where(causal, s, 0.0).astype(BF16) for s in per_group] for per_group in scores]
    intra = [[jnp.dot(scores[g][h], v[g][:, hv[h]], preferred_element_type=F32) for h in heads]
             for g in groups]
    _emit(fill, HGRN_EMIT)
    b_last = [[b[g][cs.stop - 1:cs.stop, :] for cs in chunks] for g in groups]
    k_end = [[(k[g][cs, :] * jnp.exp(bl - b[g][cs, :])).astype(BF16) for cs, bl in zip(chunks, b_last[g])]
             for g in groups]
    decay = [[jnp.exp(bl) for bl in b_last[g]] for g in groups]
    gate = [gain * _silu(proj_ref[rows, 2 * C_K + C_V:2 * C_K + 2 * C_V]) for rows in row_slices]
    _emit(fill, HGRN_EMIT)
    st = list(st)
    outs = []
    for g in groups:
        inter = [[] for _ in heads]
        for n, cs in enumerate(chunks):
            for h in heads:
                inter[h].append(lax.dot_general(q_t[g][cs, hk[h]], st[h].astype(BF16), NT_DIMS,
                                                preferred_element_type=F32))
                st[h] = st[h] * decay[g][n][:, hk[h]] + lax.dot_general(
                    v[g][cs, hv[h]], k_end[g][n][:, hk[h]], TN_DIMS, preferred_element_type=F32)
        _emit(fill, HGRN_EMIT)
        per_head = []
        for h in heads:
            y = intra[g][h] + jnp.concatenate(inter[h], axis=0)
            yn = y * lax.rsqrt(jnp.mean(y * y, axis=-1, keepdims=True) + EPS)
            per_head.append((yn * gate[g][:, hv[h]]).astype(BF16))
        outs.append(per_head)
        _emit(fill, HGRN_EMIT)
    return outs, st


def _hgrn_layer_kernel(layer, tiles_per_seq, h0_ref, hnext_ref, g_ref, w_ref, lbp_ref, on_ref, o_ref,
                       proj_a, proj_b, st_ref):
    t = pl.program_id(0)
    tm = o_ref.shape[0]
    grp = HGRN_GROUP

    def projection(h_ref, dst):
        x = h_ref[...]
        ms = jnp.mean(x * x, axis=-1, keepdims=True)
        hn = (x * lax.rsqrt(ms + EPS) * g_ref[...]).astype(BF16)

        def piece(c):
            cols = slice(c * HGRN_PROJ_COLS, (c + 1) * HGRN_PROJ_COLS)

            def run():
                dst[:, cols] = jnp.dot(hn, w_ref[:, cols], preferred_element_type=F32)
            return run

        return [piece(c) for c in range(ODD_IN // HGRN_PROJ_COLS)]

    @pl.when(t == 0)
    def _():
        for run in projection(h0_ref, proj_a):
            run()

    @pl.when(t % tiles_per_seq == 0)
    def _():
        st_ref[...] = jnp.zeros_like(st_ref)

    def step(src, dst):
        fill = iter(projection(hnext_ref, dst))
        lbp = lbp_ref[...]
        ex = jnp.exp(lbp - jnp.max(lbp, axis=0, keepdims=True))
        sm = ex / jnp.sum(ex, axis=0, keepdims=True)
        lower_all = jnp.sum(sm[0:layer + 1, :], axis=0, keepdims=True) - sm[0:1, :]
        row = lax.broadcasted_iota(jnp.int32, (grp, grp), 0)
        colm = lax.broadcasted_iota(jnp.int32, (grp, grp), 1)
        causal = jnp.logical_and(row // HGRN_CHUNK == colm // HGRN_CHUNK, row >= colm)
        tri = jnp.where(causal, 1.0, 0.0).astype(BF16)
        gain = on_ref[...]
        st = [st_ref[h] for h in range(HGRN_HEADS)]
        row_slices = [slice(r0, r0 + grp) for r0 in range(0, tm, grp)]
        outs, st = _hgrn_groups(src, row_slices, lower_all, gain, causal, tri, st, fill)
        for rows, per_head in zip(row_slices, outs):
            for h, y in enumerate(per_head):
                o_ref[rows, h * HGRN_DV:(h + 1) * HGRN_DV] = y
        for run in fill:
            run()
        for h in range(HGRN_HEADS):
            st_ref[h] = st[h]

    @pl.when(t % 2 == 0)
    def _():
        step(proj_a, proj_b)

    @pl.when(t % 2 == 1)
    def _():
        step(proj_b, proj_a)


def _hgrn_layer(h, g, w, hgrn_lb, out_norm, layer, seq):
    tokens, d = h.shape
    tm = HGRN_TM
    ntiles = tokens // tm
    const = lambda shape: pl.BlockSpec(shape, lambda t: (0, 0), pipeline_mode=pl.Buffered(1))
    return pl.pallas_call(
        functools.partial(_hgrn_layer_kernel, layer, seq // tm),
        grid=(ntiles,),
        in_specs=[
            const((tm, d)),
            pl.BlockSpec((tm, d), lambda t: (jnp.minimum(t + 1, ntiles - 1), 0)),
            const((1, d)), const(w.shape), const(hgrn_lb.shape), const((1, C_V)),
        ],
        out_specs=pl.BlockSpec((tm, C_V), lambda t: (t, 0)),
        out_shape=jax.ShapeDtypeStruct((tokens, C_V), BF16),
        scratch_shapes=[pltpu.VMEM((tm, ODD_IN), F32), pltpu.VMEM((tm, ODD_IN), F32),
                        pltpu.VMEM((HGRN_HEADS, HGRN_DV, HGRN_DK), F32)],
        compiler_params=_params("arbitrary"),
        name="hgrn_layer",
    )(h, h, g, w, hgrn_lb, out_norm)


def kernel(x, even_w_in, even_w_out, ret_norm, rel_bias, odd_w_in, odd_w_out, hgrn_lb, hgrn_norm,
           mix_norm, ffn_norm, ffn_w_up, ffn_conv_w, ffn_conv_b, ffn_w_down, final_norm):
    bn, seq, d = x.shape
    tokens = bn * seq
    row = lambda a: a.reshape(1, -1).astype(F32)
    h = x.reshape(tokens, d)
    final_g = row(final_norm)

    outs = _even_in(h, row(mix_norm[0]), even_w_in[0], row(ret_norm[0]), seq,
                    riders=[(even_w_out, 0), (ffn_w_up, 0), (ffn_w_down, 0)])
    ya, q4, q16, k4, k16, v4, v16, w_out0, w_up0, w_down0 = outs
    yb = _dilated((q4, k4, v4, q16, k16, v16), rel_bias, bn, seq).reshape(tokens, B_W)
    h, w_in1, w_out1, w_up1, w_down1 = _ffn(
        h, [(ya, w_out0, 0), (yb, w_out0, A_V)], row(ffn_norm[0]), w_up0, ffn_conv_w[0],
        ffn_conv_b[0], w_down0, final_g, seq, final=False,
        riders=[(odd_w_in, 0), (odd_w_out, 0), (ffn_w_up, 1), (ffn_w_down, 1)])

    y = _hgrn_layer(h, row(mix_norm[1]), w_in1, hgrn_lb.astype(F32), row(hgrn_norm[0]), 1, seq)
    h, = _ffn(h, [(y, w_out1, 0)], row(ffn_norm[1]), w_up1, ffn_conv_w[1], ffn_conv_b[1], w_down1,
              final_g, seq, final=True)
    return h.reshape(bn, seq, d)
```
